```python
import math
import jax, jax.numpy as jnp
from jax import lax
import numpy as np

D_MODEL = 1024
BATCH = 8
SEQ = 8192
DEPTH = 2

HEAD_DIM = 64
MEM_LEN = 256
MEM_HEADS = 4
MEM_WIDTH = MEM_HEADS * HEAD_DIM
MIX_WIDTH = D_MODEL
TOK_WIDTH = MIX_WIDTH - MEM_WIDTH
FOX_HEADS = TOK_WIDTH // HEAD_DIM
GMLP_GROUPS = TOK_WIDTH // HEAD_DIM
CHUNK = 128
Q_BLOCK = 128
D_FF = 2816
N_MIXERS = 2
N_FOX = (DEPTH + 1) // 2
N_GMLP = DEPTH // 2
FOX_IN = 3 * TOK_WIDTH + FOX_HEADS + MEM_WIDTH
GMLP_IN = 2 * TOK_WIDTH + MEM_WIDTH
EPS = 1e-6

kernel_name = "hybrid_fox_gmlp_macaron_memxattn"


def rms_norm(x, g):
    xf = x.astype(jnp.float32)
    y = xf * lax.rsqrt(jnp.mean(xf * xf, axis=-1, keepdims=True) + EPS)
    return (y * g.astype(jnp.float32)).astype(x.dtype)


def swiglu(h, w_in, w_out):
    a, b = jnp.split(h @ w_in, 2, axis=-1)
    return (jax.nn.silu(a) * b) @ w_out


def memory_attention(mq, mem_n, w_kv, g_q, g_k):
    b, s, _ = mq.shape
    q = rms_norm(mq.reshape(b, s, MEM_HEADS, HEAD_DIM), g_q)
    kv = mem_n @ w_kv
    k, v = jnp.split(kv, 2, axis=-1)
    k = rms_norm(k.reshape(b, MEM_LEN, MEM_HEADS, HEAD_DIM), g_k)
    v = v.reshape(b, MEM_LEN, MEM_HEADS, HEAD_DIM)
    logits = jnp.einsum('bshd,bmhd->bhsm', q, k).astype(jnp.float32) / math.sqrt(HEAD_DIM)
    p = jax.nn.softmax(logits, axis=-1).astype(v.dtype)
    o = jnp.einsum('bhsm,bmhd->bshd', p, v)
    return o.reshape(b, s, MEM_WIDTH)


def forgetting_attention(q, k, v, c):
    b, s, h, d = q.shape
    nblk = s // Q_BLOCK
    qb = q.reshape(b, nblk, Q_BLOCK, h, d).transpose(1, 0, 2, 3, 4)
    cq = c.reshape(b, h, nblk, Q_BLOCK).transpose(2, 0, 1, 3)
    starts = jnp.arange(nblk, dtype=jnp.int32) * Q_BLOCK
    key_pos = jnp.arange(s, dtype=jnp.int32)
    scale = 1.0 / math.sqrt(d)

    def block(args):
        q_i, c_i, start = args
        logits = jnp.einsum('bqhd,bkhd->bhqk', q_i, k).astype(jnp.float32) * scale
        logits = logits + c_i[:, :, :, None] - c[:, :, None, :]
        q_pos = start + jnp.arange(Q_BLOCK, dtype=jnp.int32)
        mask = key_pos[None, :] <= q_pos[:, None]
        logits = jnp.where(mask[None, None], logits, -jnp.inf)
        p = jax.nn.softmax(logits, axis=-1).astype(v.dtype)
        return jnp.einsum('bhqk,bkhd->bqhd', p, v)

    out = lax.map(block, (qb, cq, starts))
    return out.transpose(1, 0, 2, 3, 4).reshape(b, s, h * d)


def fox_token_mixer(proj, b_f, g_q, g_k):
    b, s, _ = proj.shape
    t = TOK_WIDTH
    q = rms_norm(proj[..., :t].reshape(b, s, FOX_HEADS, HEAD_DIM), g_q)
    k = rms_norm(proj[..., t:2 * t].reshape(b, s, FOX_HEADS, HEAD_DIM), g_k)
    v = proj[..., 2 * t:3 * t].reshape(b, s, FOX_HEADS, HEAD_DIM)
    f_logit = proj[..., 3 * t:3 * t + FOX_HEADS].astype(jnp.float32) + b_f.astype(jnp.float32)
    log_f = jax.nn.log_sigmoid(f_logit)
    c = jnp.cumsum(log_f, axis=1).transpose(0, 2, 1)
    mq = proj[..., 3 * t + FOX_HEADS:]
    return forgetting_attention(q, k, v, c), mq


def gmlp_token_mixer(proj, v_gain, w_s, b_s):
    b, s, _ = proj.shape
    t = TOK_WIDTH
    z = jax.nn.gelu(proj[..., :2 * t])
    u, v = z[..., :t], z[..., t:]
    v = rms_norm(v.reshape(b, s, GMLP_GROUPS, HEAD_DIM), v_gain.reshape(GMLP_GROUPS, HEAD_DIM))
    n_chunk = s // CHUNK
    vc = v.reshape(b, n_chunk, CHUNK, GMLP_GROUPS, HEAD_DIM)
    w = jnp.tril(w_s)
    gate = jnp.einsum('gts,bcsgd->bctgd', w, vc) + b_s.T[None, None, :, :, None]
    out = u.reshape(b, n_chunk, CHUNK, GMLP_GROUPS, HEAD_DIM) * gate
    return out.reshape(b, s, t), proj[..., 2 * t:]


def _fwd_setup_inputs(seed: int = 0) -> dict:
    key = jax.random.key(seed)
    ks = jax.random.split(key, 32)
    f32 = jnp.float32
    D, F = D_MODEL, D_FF

    def nrm(k, shape, scale):
        return jax.random.normal(k, shape, f32) * scale

    def gain(k, shape):
        return 1.0 + 0.02 * jax.random.normal(k, shape, f32)

    return {
        "x": jax.random.normal(ks[0], (BATCH, SEQ, D), f32),
        "mem": jax.random.normal(ks[1], (BATCH, MEM_LEN, D), f32),
        "norm_ffn1": gain(ks[2], (DEPTH, D)),
        "ffn1_w_in": nrm(ks[3], (DEPTH, D, 2 * F), D ** -0.5),
        "ffn1_w_out": nrm(ks[4], (DEPTH, F, D), F ** -0.5),
        "norm_mix": gain(ks[5], (DEPTH, D)),
        "norm_ffn2": gain(ks[6], (DEPTH, D)),
        "ffn2_w_in": nrm(ks[7], (DEPTH, D, 2 * F), D ** -0.5),
        "ffn2_w_out": nrm(ks[8], (DEPTH, F, D), F ** -0.5),
        "w_out": nrm(ks[9], (DEPTH, MIX_WIDTH, D), MIX_WIDTH ** -0.5),
        "mem_norm": gain(ks[10], (D,)),
        "mem_w_kv": nrm(ks[11], (DEPTH, D, 2 * MEM_WIDTH), D ** -0.5),
        "mem_q_norm": gain(ks[12], (DEPTH, HEAD_DIM)),
        "mem_k_norm": gain(ks[13], (DEPTH, HEAD_DIM)),
        "fox_w_in": nrm(ks[14], (N_FOX, D, FOX_IN), D ** -0.5),
        "fox_b_f": 2.0 + 4.0 * jax.random.uniform(ks[15], (N_FOX, FOX_HEADS), f32),
        "fox_q_norm": gain(ks[16], (N_FOX, HEAD_DIM)),
        "fox_k_norm": gain(ks[17], (N_FOX, HEAD_DIM)),
        "gmlp_w_in": nrm(ks[18], (N_GMLP, D, GMLP_IN), D ** -0.5),
        "gmlp_v_norm": gain(ks[19], (N_GMLP, TOK_WIDTH)),
        "gmlp_w_s": nrm(ks[20], (N_GMLP, GMLP_GROUPS, CHUNK, CHUNK), 0.5 * CHUNK ** -0.5),
        "gmlp_b_s": 1.0 + 0.02 * jax.random.normal(ks[21], (N_GMLP, GMLP_GROUPS, CHUNK), f32),
    }


def _fwd_reference(x, mem, norm_ffn1, ffn1_w_in, ffn1_w_out, norm_mix, norm_ffn2, ffn2_w_in,
              ffn2_w_out, w_out, mem_norm, mem_w_kv, mem_q_norm, mem_k_norm, fox_w_in,
              fox_b_f, fox_q_norm, fox_k_norm, gmlp_w_in, gmlp_v_norm, gmlp_w_s, gmlp_b_s):
    mem_n = rms_norm(mem, mem_norm)
    for i in range(DEPTH):
        kind, j = i % N_MIXERS, i // N_MIXERS
        x = x + 0.5 * swiglu(rms_norm(x, norm_ffn1[i]), ffn1_w_in[i], ffn1_w_out[i])
        h = rms_norm(x, norm_mix[i])
        if kind == 0:
            tok, mq = fox_token_mixer(h @ fox_w_in[j], fox_b_f[j], fox_q_norm[j], fox_k_norm[j])
        else:
            tok, mq = gmlp_token_mixer(h @ gmlp_w_in[j], gmlp_v_norm[j], gmlp_w_s[j], gmlp_b_s[j])
        mo = memory_attention(mq, mem_n, mem_w_kv[i], mem_q_norm[i], mem_k_norm[i])
        x = x + jnp.concatenate([tok, mo], axis=-1) @ w_out[i]
        x = x + 0.5 * swiglu(rms_norm(x, norm_ffn2[i]), ffn2_w_in[i], ffn2_w_out[i])
    return x


import jax as _jax
import jax.numpy as _jnp

TWIN_FORMAT = 'train_step'
FWD_PARAMS = ['x', 'mem', 'norm_ffn1', 'ffn1_w_in', 'ffn1_w_out', 'norm_mix', 'norm_ffn2', 'ffn2_w_in', 'ffn2_w_out', 'w_out', 'mem_norm', 'mem_w_kv', 'mem_q_norm', 'mem_k_norm', 'fox_w_in', 'fox_b_f', 'fox_q_norm', 'fox_k_norm', 'gmlp_w_in', 'gmlp_v_norm', 'gmlp_w_s', 'gmlp_b_s']
TWIN_WEIGHTS = ['norm_ffn1', 'ffn1_w_in', 'ffn1_w_out', 'norm_mix', 'norm_ffn2', 'ffn2_w_in', 'ffn2_w_out', 'w_out', 'mem_norm', 'mem_w_kv', 'mem_q_norm', 'mem_k_norm', 'fox_w_in', 'fox_b_f', 'fox_q_norm', 'fox_k_norm', 'gmlp_w_in', 'gmlp_v_norm', 'gmlp_w_s', 'gmlp_b_s']
TWIN_DIFF_INPUT = 'x'
TWIN_INPUTS = ['x', 'mem', 'norm_ffn1', 'ffn1_w_in', 'ffn1_w_out', 'norm_mix', 'norm_ffn2', 'ffn2_w_in', 'ffn2_w_out', 'w_out', 'mem_norm', 'mem_w_kv', 'mem_q_norm', 'mem_k_norm', 'fox_w_in', 'fox_b_f', 'fox_q_norm', 'fox_k_norm', 'gmlp_w_in', 'gmlp_v_norm', 'gmlp_w_s', 'gmlp_b_s', 'loss_target', 'm_norm_ffn1', 'm_ffn1_w_in', 'm_ffn1_w_out', 'm_norm_mix', 'm_norm_ffn2', 'm_ffn2_w_in', 'm_ffn2_w_out', 'm_w_out', 'm_mem_norm', 'm_mem_w_kv', 'm_mem_q_norm', 'm_mem_k_norm', 'm_fox_w_in', 'm_fox_b_f', 'm_fox_q_norm', 'm_fox_k_norm', 'm_gmlp_w_in', 'm_gmlp_v_norm', 'm_gmlp_w_s', 'm_gmlp_b_s', 'v_norm_ffn1', 'v_ffn1_w_in', 'v_ffn1_w_out', 'v_norm_mix', 'v_norm_ffn2', 'v_ffn2_w_in', 'v_ffn2_w_out', 'v_w_out', 'v_mem_norm', 'v_mem_w_kv', 'v_mem_q_norm', 'v_mem_k_norm', 'v_fox_w_in', 'v_fox_b_f', 'v_fox_q_norm', 'v_fox_k_norm', 'v_gmlp_w_in', 'v_gmlp_v_norm', 'v_gmlp_w_s', 'v_gmlp_b_s']
TWIN_OUTPUTS = ['loss', 'grad_x', 'grad_norm_ffn1', 'grad_ffn1_w_in', 'grad_ffn1_w_out', 'grad_norm_mix', 'grad_norm_ffn2', 'grad_ffn2_w_in', 'grad_ffn2_w_out', 'grad_w_out', 'grad_mem_norm', 'grad_mem_w_kv', 'grad_mem_q_norm', 'grad_mem_k_norm', 'grad_fox_w_in', 'grad_fox_b_f', 'grad_fox_q_norm', 'grad_fox_k_norm', 'grad_gmlp_w_in', 'grad_gmlp_v_norm', 'grad_gmlp_w_s', 'grad_gmlp_b_s', 'delta_norm_ffn1', 'delta_ffn1_w_in', 'delta_ffn1_w_out', 'delta_norm_mix', 'delta_norm_ffn2', 'delta_ffn2_w_in', 'delta_ffn2_w_out', 'delta_w_out', 'delta_mem_norm', 'delta_mem_w_kv', 'delta_mem_q_norm', 'delta_mem_k_norm', 'delta_fox_w_in', 'delta_fox_b_f', 'delta_fox_q_norm', 'delta_fox_k_norm', 'delta_gmlp_w_in', 'delta_gmlp_v_norm', 'delta_gmlp_w_s', 'delta_gmlp_b_s', 'new_m_norm_ffn1', 'new_m_ffn1_w_in', 'new_m_ffn1_w_out', 'new_m_norm_mix', 'new_m_norm_ffn2', 'new_m_ffn2_w_in', 'new_m_ffn2_w_out', 'new_m_w_out', 'new_m_mem_norm', 'new_m_mem_w_kv', 'new_m_mem_q_norm', 'new_m_mem_k_norm', 'new_m_fox_w_in', 'new_m_fox_b_f', 'new_m_fox_q_norm', 'new_m_fox_k_norm', 'new_m_gmlp_w_in', 'new_m_gmlp_v_norm', 'new_m_gmlp_w_s', 'new_m_gmlp_b_s', 'new_v_norm_ffn1', 'new_v_ffn1_w_in', 'new_v_ffn1_w_out', 'new_v_norm_mix', 'new_v_norm_ffn2', 'new_v_ffn2_w_in', 'new_v_ffn2_w_out', 'new_v_w_out', 'new_v_mem_norm', 'new_v_mem_w_kv', 'new_v_mem_q_norm', 'new_v_mem_k_norm', 'new_v_fox_w_in', 'new_v_fox_b_f', 'new_v_fox_q_norm', 'new_v_fox_k_norm', 'new_v_gmlp_w_in', 'new_v_gmlp_v_norm', 'new_v_gmlp_w_s', 'new_v_gmlp_b_s']
TWIN_LEAF_KINDS = {'loss': 'loss', 'grad_x': 'grad_x', 'grad_norm_ffn1': 'grad_w', 'grad_ffn1_w_in': 'grad_w', 'grad_ffn1_w_out': 'grad_w', 'grad_norm_mix': 'grad_w', 'grad_norm_ffn2': 'grad_w', 'grad_ffn2_w_in': 'grad_w', 'grad_ffn2_w_out': 'grad_w', 'grad_w_out': 'grad_w', 'grad_mem_norm': 'grad_w', 'grad_mem_w_kv': 'grad_w', 'grad_mem_q_norm': 'grad_w', 'grad_mem_k_norm': 'grad_w', 'grad_fox_w_in': 'grad_w', 'grad_fox_b_f': 'grad_w', 'grad_fox_q_norm': 'grad_w', 'grad_fox_k_norm': 'grad_w', 'grad_gmlp_w_in': 'grad_w', 'grad_gmlp_v_norm': 'grad_w', 'grad_gmlp_w_s': 'grad_w', 'grad_gmlp_b_s': 'grad_w', 'delta_norm_ffn1': 'delta_w', 'delta_ffn1_w_in': 'delta_w', 'delta_ffn1_w_out': 'delta_w', 'delta_norm_mix': 'delta_w', 'delta_norm_ffn2': 'delta_w', 'delta_ffn2_w_in': 'delta_w', 'delta_ffn2_w_out': 'delta_w', 'delta_w_out': 'delta_w', 'delta_mem_norm': 'delta_w', 'delta_mem_w_kv': 'delta_w', 'delta_mem_q_norm': 'delta_w', 'delta_mem_k_norm': 'delta_w', 'delta_fox_w_in': 'delta_w', 'delta_fox_b_f': 'delta_w', 'delta_fox_q_norm': 'delta_w', 'delta_fox_k_norm': 'delta_w', 'delta_gmlp_w_in': 'delta_w', 'delta_gmlp_v_norm': 'delta_w', 'delta_gmlp_w_s': 'delta_w', 'delta_gmlp_b_s': 'delta_w', 'new_m_norm_ffn1': 'new_m', 'new_m_ffn1_w_in': 'new_m', 'new_m_ffn1_w_out': 'new_m', 'new_m_norm_mix': 'new_m', 'new_m_norm_ffn2': 'new_m', 'new_m_ffn2_w_in': 'new_m', 'new_m_ffn2_w_out': 'new_m', 'new_m_w_out': 'new_m', 'new_m_mem_norm': 'new_m', 'new_m_mem_w_kv': 'new_m', 'new_m_mem_q_norm': 'new_m', 'new_m_mem_k_norm': 'new_m', 'new_m_fox_w_in': 'new_m', 'new_m_fox_b_f': 'new_m', 'new_m_fox_q_norm': 'new_m', 'new_m_fox_k_norm': 'new_m', 'new_m_gmlp_w_in': 'new_m', 'new_m_gmlp_v_norm': 'new_m', 'new_m_gmlp_w_s': 'new_m', 'new_m_gmlp_b_s': 'new_m', 'new_v_norm_ffn1': 'new_v', 'new_v_ffn1_w_in': 'new_v', 'new_v_ffn1_w_out': 'new_v', 'new_v_norm_mix': 'new_v', 'new_v_norm_ffn2': 'new_v', 'new_v_ffn2_w_in': 'new_v', 'new_v_ffn2_w_out': 'new_v', 'new_v_w_out': 'new_v', 'new_v_mem_norm': 'new_v', 'new_v_mem_w_kv': 'new_v', 'new_v_mem_q_norm': 'new_v', 'new_v_mem_k_norm': 'new_v', 'new_v_fox_w_in': 'new_v', 'new_v_fox_b_f': 'new_v', 'new_v_fox_q_norm': 'new_v', 'new_v_fox_k_norm': 'new_v', 'new_v_gmlp_w_in': 'new_v', 'new_v_gmlp_v_norm': 'new_v', 'new_v_gmlp_w_s': 'new_v', 'new_v_gmlp_b_s': 'new_v'}


def _forward(args):
    return _fwd_reference(*[args[k] for k in FWD_PARAMS])


def _output_shape():
    def fwd():
        inp = _fwd_setup_inputs(0)
        return _fwd_reference(*[inp[k] for k in FWD_PARAMS])
    out = _jax.eval_shape(fwd)
    return out.shape, out.dtype

N_MICROBATCH = 1
ADAM_LR = 0.001
ADAM_B1 = 0.9
ADAM_B2 = 0.999
ADAM_EPS = 1e-08
ADAM_WD = 0.01
ADAM_STEP = 10
PER_EXAMPLE_BATCH_AXIS = {'x': 0, 'mem': 0, 'loss_target': 0}
SHARED_INPUTS = []
_WEIGHT_DTYPES = {'norm_ffn1': _jnp.float32, 'ffn1_w_in': _jnp.float32, 'ffn1_w_out': _jnp.float32, 'norm_mix': _jnp.float32, 'norm_ffn2': _jnp.float32, 'ffn2_w_in': _jnp.float32, 'ffn2_w_out': _jnp.float32, 'w_out': _jnp.float32, 'mem_norm': _jnp.float32, 'mem_w_kv': _jnp.float32, 'mem_q_norm': _jnp.float32, 'mem_k_norm': _jnp.float32, 'fox_w_in': _jnp.float32, 'fox_b_f': _jnp.float32, 'fox_q_norm': _jnp.float32, 'fox_k_norm': _jnp.float32, 'gmlp_w_in': _jnp.float32, 'gmlp_v_norm': _jnp.float32, 'gmlp_w_s': _jnp.float32, 'gmlp_b_s': _jnp.float32}
MOMENT_SCALE = {'norm_ffn1': 1.236720e+01, 'ffn1_w_in': 1.605579e-01, 'ffn1_w_out': 2.911374e-01, 'norm_mix': 1.898364e+01, 'norm_ffn2': 1.232600e+01, 'ffn2_w_in': 2.110253e-01, 'ffn2_w_out': 3.354886e-01, 'w_out': 2.873952e+00, 'mem_norm': 4.752460e-01, 'mem_w_kv': 3.094777e-01, 'mem_q_norm': 2.456588e+00, 'mem_k_norm': 2.460065e+00, 'fox_w_in': 1.662958e-01, 'fox_b_f': 5.384941e+01, 'fox_q_norm': 1.325808e+01, 'fox_k_norm': 1.324962e+01, 'gmlp_w_in': 8.565325e-01, 'gmlp_v_norm': 3.433979e+00, 'gmlp_w_s': 4.229056e+00, 'gmlp_b_s': 1.394574e+01}


def _to_microbatches(a, axis):
    t = _jnp.moveaxis(a, axis, 0)
    t = t.reshape((N_MICROBATCH, t.shape[0] // N_MICROBATCH) + t.shape[1:])
    return _jnp.moveaxis(t, 1, axis + 1)


def setup_inputs(seed: int = 0) -> dict:
    inp = _fwd_setup_inputs(seed)
    key = _jax.random.fold_in(_jax.random.key(seed), 7919)
    shape, _ = _output_shape()
    out = dict(inp)
    out["loss_target"] = _jax.random.normal(_jax.random.fold_in(key, 0), shape, _jnp.float32)
    for i, name in enumerate(TWIN_WEIGHTS):
        w = inp[name].astype(_jnp.float32)
        if MOMENT_SCALE is None:
            s = _jnp.sqrt(_jnp.mean(_jnp.square(w)) + 1e-30)
        else:
            s = MOMENT_SCALE[name]
        km, kv = _jax.random.split(_jax.random.fold_in(key, i + 1))
        out[name] = w
        out["m_" + name] = s * _jax.random.normal(km, w.shape, _jnp.float32)
        out["v_" + name] = (s * s) * _jax.random.uniform(kv, w.shape, _jnp.float32, 0.5, 1.5)
    if N_MICROBATCH > 1:
        for name, axis in PER_EXAMPLE_BATCH_AXIS.items():
            out[name] = _to_microbatches(out[name], axis)
    return {'x': out['x'], 'mem': out['mem'], 'norm_ffn1': out['norm_ffn1'], 'ffn1_w_in': out['ffn1_w_in'], 'ffn1_w_out': out['ffn1_w_out'], 'norm_mix': out['norm_mix'], 'norm_ffn2': out['norm_ffn2'], 'ffn2_w_in': out['ffn2_w_in'], 'ffn2_w_out': out['ffn2_w_out'], 'w_out': out['w_out'], 'mem_norm': out['mem_norm'], 'mem_w_kv': out['mem_w_kv'], 'mem_q_norm': out['mem_q_norm'], 'mem_k_norm': out['mem_k_norm'], 'fox_w_in': out['fox_w_in'], 'fox_b_f': out['fox_b_f'], 'fox_q_norm': out['fox_q_norm'], 'fox_k_norm': out['fox_k_norm'], 'gmlp_w_in': out['gmlp_w_in'], 'gmlp_v_norm': out['gmlp_v_norm'], 'gmlp_w_s': out['gmlp_w_s'], 'gmlp_b_s': out['gmlp_b_s'], 'loss_target': out['loss_target'], 'm_norm_ffn1': out['m_norm_ffn1'], 'm_ffn1_w_in': out['m_ffn1_w_in'], 'm_ffn1_w_out': out['m_ffn1_w_out'], 'm_norm_mix': out['m_norm_mix'], 'm_norm_ffn2': out['m_norm_ffn2'], 'm_ffn2_w_in': out['m_ffn2_w_in'], 'm_ffn2_w_out': out['m_ffn2_w_out'], 'm_w_out': out['m_w_out'], 'm_mem_norm': out['m_mem_norm'], 'm_mem_w_kv': out['m_mem_w_kv'], 'm_mem_q_norm': out['m_mem_q_norm'], 'm_mem_k_norm': out['m_mem_k_norm'], 'm_fox_w_in': out['m_fox_w_in'], 'm_fox_b_f': out['m_fox_b_f'], 'm_fox_q_norm': out['m_fox_q_norm'], 'm_fox_k_norm': out['m_fox_k_norm'], 'm_gmlp_w_in': out['m_gmlp_w_in'], 'm_gmlp_v_norm': out['m_gmlp_v_norm'], 'm_gmlp_w_s': out['m_gmlp_w_s'], 'm_gmlp_b_s': out['m_gmlp_b_s'], 'v_norm_ffn1': out['v_norm_ffn1'], 'v_ffn1_w_in': out['v_ffn1_w_in'], 'v_ffn1_w_out': out['v_ffn1_w_out'], 'v_norm_mix': out['v_norm_mix'], 'v_norm_ffn2': out['v_norm_ffn2'], 'v_ffn2_w_in': out['v_ffn2_w_in'], 'v_ffn2_w_out': out['v_ffn2_w_out'], 'v_w_out': out['v_w_out'], 'v_mem_norm': out['v_mem_norm'], 'v_mem_w_kv': out['v_mem_w_kv'], 'v_mem_q_norm': out['v_mem_q_norm'], 'v_mem_k_norm': out['v_mem_k_norm'], 'v_fox_w_in': out['v_fox_w_in'], 'v_fox_b_f': out['v_fox_b_f'], 'v_fox_q_norm': out['v_fox_q_norm'], 'v_fox_k_norm': out['v_fox_k_norm'], 'v_gmlp_w_in': out['v_gmlp_w_in'], 'v_gmlp_v_norm': out['v_gmlp_v_norm'], 'v_gmlp_w_s': out['v_gmlp_w_s'], 'v_gmlp_b_s': out['v_gmlp_b_s']}


def _loss(weights, diff, rest, loss_target):
    with _jax.named_scope("forward"):
        args = {**rest, TWIN_DIFF_INPUT: diff, **{k: w.astype(_WEIGHT_DTYPES[k]) for k, w in weights.items()}}
        y = _forward(args)
    with _jax.named_scope("loss_head"):
        err = _jnp.square(y.astype(_jnp.float32) - loss_target)
        return 0.5 * _jnp.sum(_jnp.mean(err, axis=-1)) if err.ndim else 0.5 * err


def _adamw(w, g, m, v):
    m = ADAM_B1 * m + (1.0 - ADAM_B1) * g
    v = ADAM_B2 * v + (1.0 - ADAM_B2) * _jnp.square(g)
    m_hat = m / (1.0 - ADAM_B1 ** ADAM_STEP)
    v_hat = v / (1.0 - ADAM_B2 ** ADAM_STEP)
    delta = -ADAM_LR * (m_hat / (_jnp.sqrt(v_hat) + ADAM_EPS) + ADAM_WD * w)
    return delta, m, v


def reference(x, mem, norm_ffn1, ffn1_w_in, ffn1_w_out, norm_mix, norm_ffn2, ffn2_w_in, ffn2_w_out, w_out, mem_norm, mem_w_kv, mem_q_norm, mem_k_norm, fox_w_in, fox_b_f, fox_q_norm, fox_k_norm, gmlp_w_in, gmlp_v_norm, gmlp_w_s, gmlp_b_s, loss_target, m_norm_ffn1, m_ffn1_w_in, m_ffn1_w_out, m_norm_mix, m_norm_ffn2, m_ffn2_w_in, m_ffn2_w_out, m_w_out, m_mem_norm, m_mem_w_kv, m_mem_q_norm, m_mem_k_norm, m_fox_w_in, m_fox_b_f, m_fox_q_norm, m_fox_k_norm, m_gmlp_w_in, m_gmlp_v_norm, m_gmlp_w_s, m_gmlp_b_s, v_norm_ffn1, v_ffn1_w_in, v_ffn1_w_out, v_norm_mix, v_norm_ffn2, v_ffn2_w_in, v_ffn2_w_out, v_w_out, v_mem_norm, v_mem_w_kv, v_mem_q_norm, v_mem_k_norm, v_fox_w_in, v_fox_b_f, v_fox_q_norm, v_fox_k_norm, v_gmlp_w_in, v_gmlp_v_norm, v_gmlp_w_s, v_gmlp_b_s):
    given = dict(x=x, mem=mem, norm_ffn1=norm_ffn1, ffn1_w_in=ffn1_w_in, ffn1_w_out=ffn1_w_out, norm_mix=norm_mix, norm_ffn2=norm_ffn2, ffn2_w_in=ffn2_w_in, ffn2_w_out=ffn2_w_out, w_out=w_out, mem_norm=mem_norm, mem_w_kv=mem_w_kv, mem_q_norm=mem_q_norm, mem_k_norm=mem_k_norm, fox_w_in=fox_w_in, fox_b_f=fox_b_f, fox_q_norm=fox_q_norm, fox_k_norm=fox_k_norm, gmlp_w_in=gmlp_w_in, gmlp_v_norm=gmlp_v_norm, gmlp_w_s=gmlp_w_s, gmlp_b_s=gmlp_b_s, loss_target=loss_target, m_norm_ffn1=m_norm_ffn1, m_ffn1_w_in=m_ffn1_w_in, m_ffn1_w_out=m_ffn1_w_out, m_norm_mix=m_norm_mix, m_norm_ffn2=m_norm_ffn2, m_ffn2_w_in=m_ffn2_w_in, m_ffn2_w_out=m_ffn2_w_out, m_w_out=m_w_out, m_mem_norm=m_mem_norm, m_mem_w_kv=m_mem_w_kv, m_mem_q_norm=m_mem_q_norm, m_mem_k_norm=m_mem_k_norm, m_fox_w_in=m_fox_w_in, m_fox_b_f=m_fox_b_f, m_fox_q_norm=m_fox_q_norm, m_fox_k_norm=m_fox_k_norm, m_gmlp_w_in=m_gmlp_w_in, m_gmlp_v_norm=m_gmlp_v_norm, m_gmlp_w_s=m_gmlp_w_s, m_gmlp_b_s=m_gmlp_b_s, v_norm_ffn1=v_norm_ffn1, v_ffn1_w_in=v_ffn1_w_in, v_ffn1_w_out=v_ffn1_w_out, v_norm_mix=v_norm_mix, v_norm_ffn2=v_norm_ffn2, v_ffn2_w_in=v_ffn2_w_in, v_ffn2_w_out=v_ffn2_w_out, v_w_out=v_w_out, v_mem_norm=v_mem_norm, v_mem_w_kv=v_mem_w_kv, v_mem_q_norm=v_mem_q_norm, v_mem_k_norm=v_mem_k_norm, v_fox_w_in=v_fox_w_in, v_fox_b_f=v_fox_b_f, v_fox_q_norm=v_fox_q_norm, v_fox_k_norm=v_fox_k_norm, v_gmlp_w_in=v_gmlp_w_in, v_gmlp_v_norm=v_gmlp_v_norm, v_gmlp_w_s=v_gmlp_w_s, v_gmlp_b_s=v_gmlp_b_s)
    weights = {n: given[n] for n in TWIN_WEIGHTS}
    shared = {n: given[n] for n in SHARED_INPUTS}
    per_example = {n: given[n] for n in ['x', 'mem']}
    grad_fn = _jax.value_and_grad(_loss, argnums=(0, 1))

    def one_microbatch(ex, loss_target):
        ex = dict(ex)
        diff = ex.pop(TWIN_DIFF_INPUT)
        return grad_fn(weights, diff, {**shared, **ex}, loss_target)

    if N_MICROBATCH == 1:
        loss, (grad_w, grad_x) = one_microbatch(per_example, given["loss_target"])
    else:
        def body(carry, xs):
            loss_sum, grad_sum = carry
            l_k, (gw_k, gx_k) = one_microbatch(xs[0], xs[1])
            with _jax.named_scope("update"):
                return (loss_sum + l_k, _jax.tree.map(_jnp.add, grad_sum, gw_k)), gx_k

        init = (_jnp.zeros((), _jnp.float32), _jax.tree.map(_jnp.zeros_like, weights))
        (loss, grad_w), grad_x = _jax.lax.scan(body, init, (per_example, given["loss_target"]))
    with _jax.named_scope("update"):
        delta_w, new_m, new_v = {}, {}, {}
        for n in TWIN_WEIGHTS:
            delta_w[n], new_m[n], new_v[n] = _adamw(weights[n], grad_w[n], given["m_" + n], given["v_" + n])
    return (loss, grad_x, *[grad_w[n] for n in TWIN_WEIGHTS], *[delta_w[n] for n in TWIN_WEIGHTS],
            *[new_m[n] for n in TWIN_WEIGHTS], *[new_v[n] for n in TWIN_WEIGHTS])
```

```python
import functools
import math

import jax
import jax.numpy as jnp
from jax import lax
from jax.experimental import pallas as pl
from jax.experimental.pallas import tpu as pltpu

F32 = jnp.float32
BF16 = jnp.bfloat16

EPS = 1e-6
HEAD_DIM = 64
FOX_HEADS = 12
MEM_HEADS = 4
TOK_WIDTH = FOX_HEADS * HEAD_DIM
MEM_WIDTH = MEM_HEADS * HEAD_DIM
CHUNK = 128
LANES = 128
N_DEV = 8
SEG_ALIGN = 16 * LANES
PACK_ROWS = 1024
PACK_ALIGN = PACK_ROWS * LANES

ADAM_LR = 0.001
ADAM_B1 = 0.9
ADAM_B2 = 0.999
ADAM_EPS = 1e-08
ADAM_WD = 0.01
ADAM_STEP = 10

VMEM_LIMIT_BYTES = 48 * 1024 * 1024
MESH = pl.DeviceIdType.MESH
CONTRACT_0 = (((0,), (0,)), ((), ()))
CONTRACT_1 = (((1,), (1,)), ((), ()))


def _params(*semantics):
    return pltpu.CompilerParams(dimension_semantics=semantics, vmem_limit_bytes=VMEM_LIMIT_BYTES)


def _pick(n, candidates):
    for c in candidates:
        if c <= n and n % c == 0:
            return c
    return n


def _sigmoid(x):
    return 1.0 / (1.0 + jnp.exp(-x))


def _row_tile(r, w):
    return _pick(r, (1024,) if w >= 512 else (2048, 1024, 512, 256))


def rmsnorm_fwd(x, gain, out_dtype, name):
    g_, r_, w_ = x.shape
    tr = _row_tile(r_, w_)

    def body(x_ref, g_ref, y_ref):
        xv = x_ref[0].astype(F32)
        r = lax.rsqrt(jnp.mean(xv * xv, axis=-1, keepdims=True) + EPS)
        y_ref[0] = (xv * r * g_ref[0]).astype(y_ref.dtype)

    return pl.pallas_call(
        body, name=name, grid=(g_, r_ // tr),
        in_specs=[pl.BlockSpec((1, tr, w_), lambda g, i: (g, i, 0)),
                  pl.BlockSpec((1, 1, w_), lambda g, i: (g, 0, 0))],
        out_specs=pl.BlockSpec((1, tr, w_), lambda g, i: (g, i, 0)),
        out_shape=jax.ShapeDtypeStruct((g_, r_, w_), out_dtype),
        compiler_params=_params("parallel", "parallel"),
    )(x, gain)


def rmsnorm_bwd(x, gain, dy, name, residual=None):
    g_, r_, w_ = x.shape
    tr = _row_tile(r_, w_)
    has_res = residual is not None

    def body(*refs):
        if has_res:
            x_ref, g_ref, dy_ref, res_ref, dx_ref, dg_ref = refs
        else:
            x_ref, g_ref, dy_ref, dx_ref, dg_ref = refs
        xv = x_ref[0].astype(F32)
        dyv = dy_ref[0].astype(F32)
        r = lax.rsqrt(jnp.mean(xv * xv, axis=-1, keepdims=True) + EPS)
        n = xv * r
        dn = dyv * g_ref[0]
        dx = r * (dn - n * jnp.mean(dn * n, axis=-1, keepdims=True))
        if has_res:
            dx = dx + res_ref[0]
        dx_ref[0] = dx
        part = jnp.sum(dyv * n, axis=0, keepdims=True)

        @pl.when(pl.program_id(1) == 0)
        def _():
            dg_ref[0] = part

        @pl.when(pl.program_id(1) != 0)
        def _():
            dg_ref[0] += part

    row = pl.BlockSpec((1, tr, w_), lambda g, i: (g, i, 0))
    vec = pl.BlockSpec((1, 1, w_), lambda g, i: (g, 0, 0))
    operands = (x, gain, dy) + ((residual,) if has_res else ())
    return pl.pallas_call(
        body, name=name, grid=(g_, r_ // tr),
        in_specs=[row, vec, row] + ([row] if has_res else []),
        out_specs=(row, vec),
        out_shape=(jax.ShapeDtypeStruct((g_, r_, w_), F32), jax.ShapeDtypeStruct((g_, 1, w_), F32)),
        compiler_params=_params("parallel", "arbitrary"),
    )(*operands)


def matmul(a, b, name, out_dtype=F32, residual=None, scale=None):
    m_, k_ = a.shape
    n_ = b.shape[1]
    tm = _pick(m_, (512, 256, 128))
    tn = _pick(n_, (1408, 1024, 896, 512, 256, 128))
    has_res = residual is not None

    def body(*refs):
        if has_res:
            a_ref, b_ref, res_ref, o_ref = refs
        else:
            a_ref, b_ref, o_ref = refs
        acc = jnp.dot(a_ref[...].astype(BF16), b_ref[...].astype(BF16), preferred_element_type=F32)
        if scale is not None:
            acc = acc * scale
        if has_res:
            acc = acc + res_ref[...]
        o_ref[...] = acc.astype(o_ref.dtype)

    out_spec = pl.BlockSpec((tm, tn), lambda j, i: (i, j))
    operands = (a, b) + ((residual,) if has_res else ())
    return pl.pallas_call(
        body, name=name, grid=(n_ // tn, m_ // tm),
        in_specs=[pl.BlockSpec((tm, k_), lambda j, i: (i, 0)),
                  pl.BlockSpec((k_, tn), lambda j, i: (0, j))] + ([out_spec] if has_res else []),
        out_specs=out_spec,
        out_shape=jax.ShapeDtypeStruct((m_, n_), out_dtype),
        compiler_params=_params("parallel", "parallel"),
    )(*operands)


def matmul_tn(a, b, name, scale=None):
    s_, k_ = a.shape
    n_ = b.shape[1]
    tk = _pick(k_, (1024, 1408, 512, 256, 128))
    tn = _pick(n_, (1408, 1024, 896, 512, 256, 128))
    ts = _pick(s_, (512, 256, 128))

    def body(a_ref, b_ref, o_ref):
        part = lax.dot_general(a_ref[...].astype(BF16), b_ref[...].astype(BF16), CONTRACT_0,
                               preferred_element_type=F32)
        if scale is not None:
            part = part * scale

        @pl.when(pl.program_id(2) == 0)
        def _():
            o_ref[...] = part

        @pl.when(pl.program_id(2) != 0)
        def _():
            o_ref[...] += part

    return pl.pallas_call(
        body, name=name, grid=(k_ // tk, n_ // tn, s_ // ts),
        in_specs=[pl.BlockSpec((ts, tk), lambda i, j, s: (s, i)),
                  pl.BlockSpec((ts, tn), lambda i, j, s: (s, j))],
        out_specs=pl.BlockSpec((tk, tn), lambda i, j, s: (i, j)),
        out_shape=jax.ShapeDtypeStruct((k_, n_), F32),
        compiler_params=_params("parallel", "parallel", "arbitrary"),
    )(a, b)


def ffn_in(h, w_a, w_b, name):
    s_, d_ = h.shape
    f_ = w_a.shape[1]
    tm = _pick(s_, (512, 256, 128))
    tn = _pick(f_, (1408, 1024, 512, 256, 128))

    def body(h_ref, wa_ref, wb_ref, a_ref, b_ref, u_ref):
        hv = h_ref[...]
        a = jnp.dot(hv, wa_ref[...], preferred_element_type=F32)
        b = jnp.dot(hv, wb_ref[...], preferred_element_type=F32)
        a_ref[...] = a.astype(BF16)
        b_ref[...] = b.astype(BF16)
        u_ref[...] = (a * _sigmoid(a) * b).astype(BF16)

    w_spec = pl.BlockSpec((d_, tn), lambda j, i: (0, j))
    o_spec = pl.BlockSpec((tm, tn), lambda j, i: (i, j))
    out = jax.ShapeDtypeStruct((s_, f_), BF16)
    return pl.pallas_call(
        body, name=name, grid=(f_ // tn, s_ // tm),
        in_specs=[pl.BlockSpec((tm, d_), lambda j, i: (i, 0)), w_spec, w_spec],
        out_specs=(o_spec, o_spec, o_spec), out_shape=(out, out, out),
        compiler_params=_params("parallel", "parallel"),
    )(h, w_a, w_b)


def ffn_bwd_act(dy, w_out_t, a, b, name):
    s_, d_ = dy.shape
    f_ = w_out_t.shape[1]
    tm = _pick(s_, (512, 256, 128))
    tn = _pick(f_, (1408, 1024, 512, 256, 128))

    def body(dy_ref, w_ref, a_ref, b_ref, da_ref, db_ref):
        du = 0.5 * jnp.dot(dy_ref[...].astype(BF16), w_ref[...], preferred_element_type=F32)
        av = a_ref[...].astype(F32)
        bv = b_ref[...].astype(F32)
        sig = _sigmoid(av)
        da_ref[...] = (du * bv * (sig * (1.0 + av * (1.0 - sig)))).astype(BF16)
        db_ref[...] = (du * (av * sig)).astype(BF16)

    t_spec = pl.BlockSpec((tm, tn), lambda j, i: (i, j))
    out = jax.ShapeDtypeStruct((s_, f_), BF16)
    return pl.pallas_call(
        body, name=name, grid=(f_ // tn, s_ // tm),
        in_specs=[pl.BlockSpec((tm, d_), lambda j, i: (i, 0)),
                  pl.BlockSpec((d_, tn), lambda j, i: (0, j)), t_spec, t_spec],
        out_specs=(t_spec, t_spec), out_shape=(out, out),
        compiler_params=_params("parallel", "parallel"),
    )(dy, w_out_t, a, b)


def _scores(q, k, cq_ref, ck_ref, masked):
    s = lax.dot_general(q, k, CONTRACT_1, preferred_element_type=F32) * (1.0 / math.sqrt(HEAD_DIM))
    if cq_ref is not None:
        s = s + (cq_ref[0] - ck_ref[0])
    if masked:
        row = lax.broadcasted_iota(jnp.int32, s.shape, 0)
        col = lax.broadcasted_iota(jnp.int32, s.shape, 1)
        s = jnp.where(col <= row, s, -jnp.inf)
    return s


def attn_fwd(q, k, v, cq, ck, causal, name):
    h_, sq, d_ = q.shape
    sk = k.shape[1]
    tq = _pick(sq, (512, 256, 128))
    tk = tq if causal else _pick(sk, (512, 256, 128))
    nk = sk // tk
    bias = cq is not None

    def body(*refs):
        if bias:
            q_ref, k_ref, v_ref, cq_ref, ck_ref, o_ref, lse_ref, m_sc, l_sc, acc_sc = refs
        else:
            q_ref, k_ref, v_ref, o_ref, lse_ref, m_sc, l_sc, acc_sc = refs
            cq_ref = ck_ref = None
        i, j = pl.program_id(1), pl.program_id(2)

        @pl.when(j == 0)
        def _():
            m_sc[...] = jnp.full(m_sc.shape, -jnp.inf, F32)
            l_sc[...] = jnp.zeros(l_sc.shape, F32)
            acc_sc[...] = jnp.zeros(acc_sc.shape, F32)

        def step(masked):
            s = _scores(q_ref[0].astype(BF16), k_ref[0].astype(BF16), cq_ref, ck_ref, masked)
            m_prev = m_sc[...]
            m_new = jnp.maximum(m_prev, jnp.max(s, axis=1, keepdims=True))
            alpha = jnp.exp(m_prev - m_new)
            p = jnp.exp(s - m_new)
            l_sc[...] = alpha * l_sc[...] + jnp.sum(p, axis=1, keepdims=True)
            acc_sc[...] = alpha * acc_sc[...] + jnp.dot(p.astype(BF16), v_ref[0].astype(BF16),
                                                        preferred_element_type=F32)
            m_sc[...] = m_new

        if causal:
            pl.when(j < i)(functools.partial(step, False))
            pl.when(j == i)(functools.partial(step, True))
        else:
            step(False)

        @pl.when(j == (i if causal else nk - 1))
        def _():
            o_ref[0] = acc_sc[...] / l_sc[...]
            lse_ref[0] = m_sc[...] + jnp.log(l_sc[...])

    kj = (lambda j, i: jnp.minimum(j, i)) if causal else (lambda j, i: j)
    q_spec = pl.BlockSpec((1, tq, d_), lambda h, i, j: (h, i, 0))
    k_spec = pl.BlockSpec((1, tk, d_), lambda h, i, j: (h, kj(j, i), 0))
    in_specs = [q_spec, k_spec, k_spec]
    operands = [q, k, v]
    if bias:
        in_specs += [pl.BlockSpec((1, tq, 1), lambda h, i, j: (h, i, 0)),
                     pl.BlockSpec((1, 1, tk), lambda h, i, j: (h, 0, kj(j, i)))]
        operands += [cq, ck]
    return pl.pallas_call(
        body, name=name, grid=(h_, sq // tq, nk),
        in_specs=in_specs,
        out_specs=(q_spec, pl.BlockSpec((1, tq, 1), lambda h, i, j: (h, i, 0))),
        out_shape=(jax.ShapeDtypeStruct((h_, sq, d_), F32), jax.ShapeDtypeStruct((h_, sq, 1), F32)),
        scratch_shapes=[pltpu.VMEM((tq, 1), F32), pltpu.VMEM((tq, 1), F32), pltpu.VMEM((tq, d_), F32)],
        compiler_params=_params("parallel", "parallel", "arbitrary"),
    )(*operands)


def attn_bwd(q, k, v, cq, ck, o, lse, do, causal, name):
    h_, sq, d_ = q.shape
    sk = k.shape[1]
    tq = _pick(sq, (512, 256, 128))
    tk = tq if causal else _pick(sk, (512, 256, 128))
    nq = sq // tq
    bias = cq is not None
    scale = 1.0 / math.sqrt(HEAD_DIM)

    def body(*refs):
        if bias:
            (q_ref, k_ref, v_ref, cq_ref, ck_ref, o_ref, lse_ref, do_ref,
             dq_ref, dk_ref, dv_ref, dc_ref, dk_sc, dv_sc, dc_sc) = refs
        else:
            (q_ref, k_ref, v_ref, o_ref, lse_ref, do_ref,
             dq_ref, dk_ref, dv_ref, dk_sc, dv_sc) = refs
            cq_ref = ck_ref = dc_ref = dc_sc = None
        j, i = pl.program_id(1), pl.program_id(2)
        first_i = j if causal else 0

        @pl.when((j == 0) & (i == 0))
        def _():
            dq_ref[...] = jnp.zeros(dq_ref.shape, F32)

        @pl.when(i == first_i)
        def _():
            dk_sc[...] = jnp.zeros(dk_sc.shape, F32)
            dv_sc[...] = jnp.zeros(dv_sc.shape, F32)
            if bias:
                dc_sc[...] = jnp.zeros(dc_sc.shape, F32)

        def step(masked):
            qb = q_ref[0].astype(BF16)
            kb = k_ref[0].astype(BF16)
            dof = do_ref[0].astype(F32)
            dob = dof.astype(BF16)
            s = _scores(qb, kb, cq_ref, ck_ref, masked)
            p = jnp.exp(s - lse_ref[0])
            dp = lax.dot_general(dob, v_ref[0].astype(BF16), CONTRACT_1, preferred_element_type=F32)
            delta = jnp.sum(dof * o_ref[0], axis=1, keepdims=True)
            ds = p * (dp - delta)
            dsb = ds.astype(BF16)
            dv_sc[...] += lax.dot_general(p.astype(BF16), dob, CONTRACT_0, preferred_element_type=F32)
            dk_sc[...] += lax.dot_general(dsb, qb, CONTRACT_0, preferred_element_type=F32) * scale
            rows = pl.ds(pl.multiple_of(i * tq, tq), tq)
            dq_ref[0, rows, :] += jnp.dot(dsb, kb, preferred_element_type=F32) * scale
            if bias:
                dc_sc[...] -= jnp.sum(ds, axis=0, keepdims=True)

        if causal:
            pl.when(i > j)(functools.partial(step, False))
            pl.when(i == j)(functools.partial(step, True))
        else:
            step(False)

        @pl.when(i == nq - 1)
        def _():
            dk_ref[0] = dk_sc[...]
            dv_ref[0] = dv_sc[...]
            if bias:
                dc_ref[0] = dc_sc[...]

    qi = (lambda j, i: jnp.maximum(i, j)) if causal else (lambda j, i: i)
    q_spec = pl.BlockSpec((1, tq, d_), lambda h, j, i: (h, qi(j, i), 0))
    q1_spec = pl.BlockSpec((1, tq, 1), lambda h, j, i: (h, qi(j, i), 0))
    k_spec = pl.BlockSpec((1, tk, d_), lambda h, j, i: (h, j, 0))
    c_spec = pl.BlockSpec((1, 1, tk), lambda h, j, i: (h, 0, j))
    in_specs = [q_spec, k_spec, k_spec] + ([q1_spec, c_spec] if bias else []) + [q_spec, q1_spec, q_spec]
    operands = [q, k, v] + ([cq, ck] if bias else []) + [o, lse, do]
    out_specs = [pl.BlockSpec((1, sq, d_), lambda h, j, i: (h, 0, 0)), k_spec, k_spec]
    out_shape = [jax.ShapeDtypeStruct((h_, sq, d_), F32), jax.ShapeDtypeStruct((h_, sk, d_), F32),
                 jax.ShapeDtypeStruct((h_, sk, d_), F32)]
    scratch = [pltpu.VMEM((tk, d_), F32), pltpu.VMEM((tk, d_), F32)]
    if bias:
        out_specs.append(c_spec)
        out_shape.append(jax.ShapeDtypeStruct((h_, 1, sk), F32))
        scratch.append(pltpu.VMEM((1, tk), F32))
    return pl.pallas_call(
        body, name=name, grid=(h_, sk // tk, nq),
        in_specs=in_specs, out_specs=tuple(out_specs), out_shape=tuple(out_shape),
        scratch_shapes=scratch,
        compiler_params=_params("parallel", "arbitrary", "arbitrary"),
    )(*operands)


def _tri(lower):
    row = lax.broadcasted_iota(jnp.int32, (CHUNK, CHUNK), 0)
    col = lax.broadcasted_iota(jnp.int32, (CHUNK, CHUNK), 1)
    return jnp.where((col <= row) if lower else (col >= row), 1.0, 0.0).astype(F32)


def fox_gate_fwd(f, b, name):
    s_ = f.shape[0]

    def body(f_ref, b_ref, c_ref, carry):
        @pl.when(pl.program_id(0) == 0)
        def _():
            carry[...] = jnp.zeros(carry.shape, F32)

        xv = f_ref[...] + b_ref[...]
        log_f = jnp.minimum(xv, 0.0) - jnp.log(1.0 + jnp.exp(-jnp.abs(xv)))
        c = jnp.dot(_tri(True), log_f, precision=lax.Precision.HIGHEST, preferred_element_type=F32) + carry[...]
        c_ref[...] = c
        carry[...] = c[CHUNK - 1:CHUNK, :]

    blk = pl.BlockSpec((CHUNK, LANES), lambda i: (i, 0))
    return pl.pallas_call(
        body, name=name, grid=(s_ // CHUNK,),
        in_specs=[blk, pl.BlockSpec((1, LANES), lambda i: (0, 0))], out_specs=blk,
        out_shape=jax.ShapeDtypeStruct((s_, LANES), F32),
        scratch_shapes=[pltpu.VMEM((1, LANES), F32)],
        compiler_params=_params("arbitrary"),
    )(f, b)


def fox_gate_bwd(dc, f, b, name):
    s_ = f.shape[0]
    n = s_ // CHUNK

    def body(dc_ref, f_ref, b_ref, df_ref, db_ref, carry):
        @pl.when(pl.program_id(0) == 0)
        def _():
            carry[...] = jnp.zeros(carry.shape, F32)
            db_ref[...] = jnp.zeros(db_ref.shape, F32)

        dlog = jnp.dot(_tri(False), dc_ref[...], precision=lax.Precision.HIGHEST,
                       preferred_element_type=F32) + carry[...]
        df = dlog * _sigmoid(-(f_ref[...] + b_ref[...]))
        df_ref[...] = df
        db_ref[...] += jnp.sum(df, axis=0, keepdims=True)
        carry[...] = dlog[0:1, :]

    blk = pl.BlockSpec((CHUNK, LANES), lambda i: (n - 1 - i, 0))
    vec = pl.BlockSpec((1, LANES), lambda i: (0, 0))
    return pl.pallas_call(
        body, name=name, grid=(n,),
        in_specs=[blk, blk, vec], out_specs=(blk, vec),
        out_shape=(jax.ShapeDtypeStruct((s_, LANES), F32), jax.ShapeDtypeStruct((1, LANES), F32)),
        scratch_shapes=[pltpu.VMEM((1, LANES), F32)],
        compiler_params=_params("arbitrary"),
    )(dc, f, b)


GELU_K = math.sqrt(2.0 / math.pi)
GELU_C = 0.044715


def _gelu(x):
    return 0.5 * x * (1.0 + jnp.tanh(GELU_K * (x + GELU_C * (x * x * x))))


def _gelu_grad(x):
    t = jnp.tanh(GELU_K * (x + GELU_C * (x * x * x)))
    return 0.5 * (1.0 + t) + 0.5 * x * (1.0 - t * t) * (GELU_K * (1.0 + 3.0 * GELU_C * (x * x)))


def _tril_mask():
    row = lax.broadcasted_iota(jnp.int32, (CHUNK, CHUNK), 0)
    col = lax.broadcasted_iota(jnp.int32, (CHUNK, CHUNK), 1)
    return col <= row


def _gmlp_specs(s_, ts):
    row = pl.BlockSpec((1, ts, HEAD_DIM), lambda g, i: (g, i, 0))
    gain = pl.BlockSpec((1, 1, HEAD_DIM), lambda g, i: (g, 0, 0))
    w = pl.BlockSpec((1, CHUNK, CHUNK), lambda g, i: (g, 0, 0))
    b = pl.BlockSpec((1, CHUNK, 1), lambda g, i: (g, 0, 0))
    return row, gain, w, b


def gmlp_fwd(pu, pv, gain, w, b, name):
    g_, s_, d_ = pu.shape
    ts = _pick(s_, (1024, 512, 256, 128))

    def body(pu_ref, pv_ref, g_ref, w_ref, b_ref, o_ref):
        v = _gelu(pv_ref[0])
        r = lax.rsqrt(jnp.mean(v * v, axis=-1, keepdims=True) + EPS)
        vn = (v * r * g_ref[0]).astype(BF16)
        wt = jnp.where(_tril_mask(), w_ref[0], 0.0).astype(BF16)
        for c in range(ts // CHUNK):
            rows = pl.ds(c * CHUNK, CHUNK)
            gate = jnp.dot(wt, vn[c * CHUNK:(c + 1) * CHUNK], preferred_element_type=F32) + b_ref[0]
            o_ref[0, rows, :] = _gelu(pu_ref[0, rows, :]) * gate

    row, gspec, wspec, bspec = _gmlp_specs(s_, ts)
    return pl.pallas_call(
        body, name=name, grid=(g_, s_ // ts),
        in_specs=[row, row, gspec, wspec, bspec], out_specs=row,
        out_shape=jax.ShapeDtypeStruct((g_, s_, d_), F32),
        compiler_params=_params("parallel", "parallel"),
    )(pu, pv, gain, w, b)


def gmlp_bwd(pu, pv, gain, w, b, dout, name):
    g_, s_, d_ = pu.shape
    ts = _pick(s_, (1024, 512, 256, 128))

    def body(pu_ref, pv_ref, g_ref, w_ref, b_ref, do_ref, dpu_ref, dpv_ref, dw_ref, db_ref, dg_ref):
        @pl.when(pl.program_id(1) == 0)
        def _():
            dw_ref[...] = jnp.zeros(dw_ref.shape, F32)
            db_ref[...] = jnp.zeros(db_ref.shape, F32)
            dg_ref[...] = jnp.zeros(dg_ref.shape, F32)

        gain_v = g_ref[0]
        mask = _tril_mask()
        wt = jnp.where(mask, w_ref[0], 0.0).astype(BF16)
        dw = jnp.zeros((CHUNK, CHUNK), F32)
        db = jnp.zeros((CHUNK, 1), F32)
        dg = jnp.zeros((1, d_), F32)
        for c in range(ts // CHUNK):
            rows = pl.ds(c * CHUNK, CHUNK)
            pu_c = pu_ref[0, rows, :]
            pv_c = pv_ref[0, rows, :]
            do_c = do_ref[0, rows, :]
            u = _gelu(pu_c)
            v = _gelu(pv_c)
            r = lax.rsqrt(jnp.mean(v * v, axis=-1, keepdims=True) + EPS)
            n = v * r
            vn = (n * gain_v).astype(BF16)
            gate = jnp.dot(wt, vn, preferred_element_type=F32) + b_ref[0]
            dgate = do_c * u
            dgate_b = dgate.astype(BF16)
            dpu_ref[0, rows, :] = do_c * gate * _gelu_grad(pu_c)
            db = db + jnp.sum(dgate, axis=1, keepdims=True)
            dw = dw + lax.dot_general(dgate_b, vn, CONTRACT_1, preferred_element_type=F32)
            dvn = lax.dot_general(wt, dgate_b, CONTRACT_0, preferred_element_type=F32)
            dg = dg + jnp.sum(dvn * n, axis=0, keepdims=True)
            dn = dvn * gain_v
            dv = r * (dn - n * jnp.mean(dn * n, axis=-1, keepdims=True))
            dpv_ref[0, rows, :] = dv * _gelu_grad(pv_c)
        dw_ref[0] += jnp.where(mask, dw, 0.0)
        db_ref[0] += db
        dg_ref[0] += dg

    row, gspec, wspec, bspec = _gmlp_specs(s_, ts)
    return pl.pallas_call(
        body, name=name, grid=(g_, s_ // ts),
        in_specs=[row, row, gspec, wspec, bspec, row],
        out_specs=(row, row, wspec, bspec, gspec),
        out_shape=(jax.ShapeDtypeStruct((g_, s_, d_), F32), jax.ShapeDtypeStruct((g_, s_, d_), F32),
                   jax.ShapeDtypeStruct((g_, CHUNK, CHUNK), F32), jax.ShapeDtypeStruct((g_, CHUNK, 1), F32),
                   jax.ShapeDtypeStruct((g_, 1, d_), F32)),
        compiler_params=_params("parallel", "arbitrary"),
    )(pu, pv, gain, w, b, dout)


def loss_head(y, t, name):
    s_, d_ = y.shape
    tr = _pick(s_, (1024, 512, 256, 128))

    def body(y_ref, t_ref, dy_ref, l_ref):
        err = y_ref[...] - t_ref[...]
        dy_ref[...] = err * (1.0 / d_)
        part = jnp.full(l_ref.shape, jnp.sum(err * err) * (0.5 / d_), F32)

        @pl.when(pl.program_id(0) == 0)
        def _():
            l_ref[...] = part

        @pl.when(pl.program_id(0) != 0)
        def _():
            l_ref[...] += part

    blk = pl.BlockSpec((tr, d_), lambda i: (i, 0))
    dy, l = pl.pallas_call(
        body, name=name, grid=(s_ // tr,),
        in_specs=[blk, blk], out_specs=(blk, pl.BlockSpec((8, LANES), lambda i: (0, 0))),
        out_shape=(jax.ShapeDtypeStruct((s_, d_), F32), jax.ShapeDtypeStruct((8, LANES), F32)),
        compiler_params=_params("arbitrary"),
    )(y, t)
    return dy, l[0, 0]


def adamw_packed(w, g_slots, m, v, name):
    r_ = w.shape[0]
    tr = PACK_ROWS

    def body(w_ref, gs_ref, m_ref, v_ref, g_ref, d_ref, nm_ref, nv_ref):
        g = gs_ref[0].astype(F32)
        for k in range(1, N_DEV):
            g = g + gs_ref[k].astype(F32)
        m_new = ADAM_B1 * m_ref[...] + (1.0 - ADAM_B1) * g
        v_new = ADAM_B2 * v_ref[...] + (1.0 - ADAM_B2) * (g * g)
        m_hat = m_new / (1.0 - ADAM_B1 ** ADAM_STEP)
        v_hat = v_new / (1.0 - ADAM_B2 ** ADAM_STEP)
        g_ref[...] = g
        d_ref[...] = -ADAM_LR * (m_hat / (jnp.sqrt(v_hat) + ADAM_EPS) + ADAM_WD * w_ref[...])
        nm_ref[...] = m_new
        nv_ref[...] = v_new

    blk = pl.BlockSpec((tr, LANES), lambda i: (i, 0))
    out = jax.ShapeDtypeStruct((r_, LANES), F32)
    return pl.pallas_call(
        body, name=name, grid=(r_ // tr,),
        in_specs=[blk, pl.BlockSpec((N_DEV, tr, LANES), lambda i: (0, i, 0)), blk, blk],
        out_specs=(blk, blk, blk, blk), out_shape=(out, out, out, out),
        compiler_params=_params("parallel"),
    )(w, g_slots, m, v)


def _position():
    x, y, c = lax.axis_index("x"), lax.axis_index("y"), lax.axis_index("c")
    return x, y, c


def _slot(px, py, pc):
    return 4 * px + 2 * py + pc


def all_gather(block, name):
    r_, l_ = block.shape

    def body(x_ref, out_ref, send_sems, recv_sems, local_sem):
        x, y, c = _position()
        me, sibling = (x, y, c), (x, y, 1 - c)
        chips = [(1 - x, y), (x, 1 - y), (1 - x, 1 - y)]

        def copy(k, owner, to, src=None):
            dst = out_ref.at[_slot(*owner)]
            return pltpu.make_async_remote_copy(
                src_ref=dst if src is None else src, dst_ref=dst,
                send_sem=send_sems.at[k], recv_sem=recv_sems.at[k], device_id=to, device_id_type=MESH)

        mine = pltpu.make_async_copy(x_ref, out_ref.at[_slot(*me)], local_sem)
        mine.start()
        first = [copy(0, me, sibling, src=x_ref)]
        first += [copy(1 + j, me, (*chip, c), src=x_ref) for j, chip in enumerate(chips)]
        for cp in first:
            cp.start()
        passed = [copy(4 + j, (*chip, c), sibling) for j, chip in enumerate(chips)]
        for j, chip in enumerate(chips):
            copy(1 + j, (*chip, c), me).wait_recv()
            passed[j].start()
        copy(0, sibling, me).wait_recv()
        for j, chip in enumerate(chips):
            copy(4 + j, (*chip, 1 - c), me).wait_recv()
        for cp in first + passed:
            cp.wait_send()
        mine.wait()

    return pl.pallas_call(
        body, name=name,
        in_specs=[pl.BlockSpec(memory_space=pl.ANY)], out_specs=pl.BlockSpec(memory_space=pl.ANY),
        out_shape=jax.ShapeDtypeStruct((N_DEV, r_, l_), block.dtype),
        scratch_shapes=[pltpu.SemaphoreType.DMA((7,)), pltpu.SemaphoreType.DMA((7,)), pltpu.SemaphoreType.DMA(())],
    )(block)


def all_to_all(slots, name):
    _, r_, l_ = slots.shape

    def body(in_ref, out_ref, send_sems, recv_sems, local_sem):
        x, y, c = _position()
        me = _slot(x, y, c)
        mine = pltpu.make_async_copy(in_ref.at[me], out_ref.at[me], local_sem)
        mine.start()
        copies = []
        for k in range(1, N_DEV):
            px = 1 - x if k & 4 else x
            py = 1 - y if k & 2 else y
            pc = 1 - c if k & 1 else c
            cp = pltpu.make_async_remote_copy(
                src_ref=in_ref.at[_slot(px, py, pc)], dst_ref=out_ref.at[me],
                send_sem=send_sems.at[k - 1], recv_sem=recv_sems.at[k - 1],
                device_id=(px, py, pc), device_id_type=MESH)
            cp.start()
            copies.append(cp)
        for cp in copies:
            cp.wait()
        mine.wait()

    return pl.pallas_call(
        body, name=name,
        in_specs=[pl.BlockSpec(memory_space=pl.ANY)], out_specs=pl.BlockSpec(memory_space=pl.ANY),
        out_shape=jax.ShapeDtypeStruct((N_DEV, r_, l_), slots.dtype),
        scratch_shapes=[pltpu.SemaphoreType.DMA((7,)), pltpu.SemaphoreType.DMA((7,)), pltpu.SemaphoreType.DMA(())],
    )(slots)


def _seg_len(n):
    return -(-n // SEG_ALIGN) * SEG_ALIGN


def _pack(arrays, lead=()):
    parts, total = [], 0
    for a in arrays:
        flat = a.reshape(lead + (-1,))
        pad = _seg_len(flat.shape[-1]) - flat.shape[-1]
        parts.append(jnp.pad(flat, [(0, 0)] * len(lead) + [(0, pad)]) if pad else flat)
        total += flat.shape[-1] + pad
    tail = -(-total // PACK_ALIGN) * PACK_ALIGN - total
    if tail:
        parts.append(jnp.zeros(lead + (tail,), parts[0].dtype))
    return jnp.concatenate(parts, axis=-1).reshape(lead + (-1, LANES))


def _unpack(packed, shapes, lead=()):
    flat = packed.reshape(lead + (-1,))
    out, off = [], 0
    for shp in shapes:
        n = math.prod(shp)
        out.append(flat[..., off:off + n].reshape(lead + tuple(shp)))
        off += _seg_len(n)
    return out


def _heads(a, n_heads):
    return a.reshape(a.shape[0], n_heads, HEAD_DIM).transpose(1, 0, 2)


def _unheads(a):
    return a.transpose(1, 0, 2).reshape(a.shape[1], a.shape[0] * HEAD_DIM)


def _head_gain(g, n_heads):
    return jnp.broadcast_to(g.reshape(1, 1, HEAD_DIM), (n_heads, 1, HEAD_DIM))


def _ffn_forward(x, gain, w_in, w_out, tag):
    f_ = w_out.shape[0]
    h = rmsnorm_fwd(x[None], gain.reshape(1, 1, -1), BF16, f"{tag}_norm")[0]
    a, b, u = ffn_in(h, w_in[:, :f_], w_in[:, f_:], f"{tag}_in")
    y = matmul(u, w_out, f"{tag}_out", residual=x, scale=0.5)
    return y, (x, h, a, b, u)


def _ffn_backward(dy, saved, gain, w_in, w_out, tag):
    x, h, a, b, u = saved
    f_ = w_out.shape[0]
    dw_out = matmul_tn(u, dy, f"{tag}_dwout", scale=0.5)
    da, db = ffn_bwd_act(dy, w_out.T, a, b, f"{tag}_dact")
    w_in_t = w_in.T
    dh = matmul(da, w_in_t[:f_], f"{tag}_dh_a")
    dh = matmul(db, w_in_t[f_:], f"{tag}_dh_b", residual=dh)
    dw_in = jnp.concatenate([matmul_tn(h, da, f"{tag}_dwin_a"), matmul_tn(h, db, f"{tag}_dwin_b")], axis=1)
    dx, dgain = rmsnorm_bwd(x[None], gain.reshape(1, 1, -1), dh[None], f"{tag}_dnorm", residual=dy[None])
    return dx[0], dgain.reshape(-1), dw_in, dw_out


def _mem_forward(mq, mem_n, w_kv, g_q, g_k, tag):
    gq = _head_gain(g_q, MEM_HEADS)
    gk = _head_gain(g_k, MEM_HEADS)
    qn = rmsnorm_fwd(mq, gq, BF16, f"{tag}_qnorm")
    kv = matmul(mem_n, w_kv, f"{tag}_kv")
    k = _heads(kv[:, :MEM_WIDTH], MEM_HEADS)
    v = _heads(kv[:, MEM_WIDTH:], MEM_HEADS)
    kn = rmsnorm_fwd(k, gk, BF16, f"{tag}_knorm")
    o, lse = attn_fwd(qn, kn, v, None, None, False, f"{tag}_attn")
    return o, (mq, gq, gk, qn, k, kn, v, o, lse)


def _mem_backward(do, saved, mem_n, w_kv, tag):
    mq, gq, gk, qn, k, kn, v, o, lse = saved
    dqn, dkn, dv = attn_bwd(qn, kn, v, None, None, o, lse, do, False, f"{tag}_dattn")
    dmq, dgq = rmsnorm_bwd(mq, gq, dqn, f"{tag}_dqnorm")
    dk, dgk = rmsnorm_bwd(k, gk, dkn, f"{tag}_dknorm")
    dkv = jnp.concatenate([_unheads(dk), _unheads(dv)], axis=1)
    dw_kv = matmul_tn(mem_n, dkv, f"{tag}_dwkv")
    dmem_n = matmul(dkv, w_kv.T, f"{tag}_dmem")
    return dmq, dgq.sum(axis=0).reshape(-1), dgk.sum(axis=0).reshape(-1), dw_kv, dmem_n


def _fox_split(w_in):
    t3 = 3 * TOK_WIDTH
    pad = jnp.zeros((w_in.shape[0], LANES - FOX_HEADS), w_in.dtype)
    return jnp.concatenate([w_in[:, :t3], w_in[:, t3 + FOX_HEADS:], w_in[:, t3:t3 + FOX_HEADS], pad], axis=1)


def _fox_unsplit(dw):
    t3 = 3 * TOK_WIDTH
    return jnp.concatenate([dw[:, :t3], dw[:, t3 + MEM_WIDTH:t3 + MEM_WIDTH + FOX_HEADS], dw[:, t3:t3 + MEM_WIDTH]], axis=1)


def _fox_forward(h, w_split, b_f, g_q, g_k, tag):
    t3 = 3 * TOK_WIDTH
    proj = matmul(h, w_split, f"{tag}_proj")
    qkv = _heads(proj[:, :t3], 3 * FOX_HEADS)
    mq = _heads(proj[:, t3:t3 + MEM_WIDTH], MEM_HEADS)
    f_pad = proj[:, t3 + MEM_WIDTH:]
    b_pad = jnp.pad(b_f.reshape(1, -1), ((0, 0), (0, LANES - FOX_HEADS)))
    gains = jnp.concatenate([_head_gain(g_q, FOX_HEADS), _head_gain(g_k, FOX_HEADS)], axis=0)
    qk = qkv[:2 * FOX_HEADS]
    qkn = rmsnorm_fwd(qk, gains, BF16, f"{tag}_qknorm")
    v = qkv[2 * FOX_HEADS:]
    c = fox_gate_fwd(f_pad, b_pad, f"{tag}_gate")
    c_t = c[:, :FOX_HEADS].T
    cq, ck = c_t[:, :, None], c_t[:, None, :]
    qn, kn = qkn[:FOX_HEADS], qkn[FOX_HEADS:]
    o, lse = attn_fwd(qn, kn, v, cq, ck, True, f"{tag}_attn")
    return o, mq, (qk, gains, qn, kn, v, cq, ck, o, lse, f_pad, b_pad)


def _fox_backward(do, dmq, saved, tag):
    qk, gains, qn, kn, v, cq, ck, o, lse, f_pad, b_pad = saved
    dqn, dkn, dv, dck = attn_bwd(qn, kn, v, cq, ck, o, lse, do, True, f"{tag}_dattn")
    dqk, dgains = rmsnorm_bwd(qk, gains, jnp.concatenate([dqn, dkn], axis=0), f"{tag}_dqknorm")
    dc = jnp.pad(dck[:, 0, :].T, ((0, 0), (0, LANES - FOX_HEADS)))
    df, db = fox_gate_bwd(dc, f_pad, b_pad, f"{tag}_dgate")
    dproj = jnp.concatenate([_unheads(dqk), _unheads(dv), _unheads(dmq), df], axis=1)
    dgains = dgains.reshape(2, FOX_HEADS, HEAD_DIM).sum(axis=1)
    return dproj, db[0, :FOX_HEADS], dgains[0], dgains[1]


def _gmlp_forward(h, w_in, v_gain, w_s, b_s, tag):
    proj = matmul(h, w_in, f"{tag}_proj")
    pu = _heads(proj[:, :TOK_WIDTH], FOX_HEADS)
    pv = _heads(proj[:, TOK_WIDTH:2 * TOK_WIDTH], FOX_HEADS)
    mq = _heads(proj[:, 2 * TOK_WIDTH:], MEM_HEADS)
    gain = v_gain.reshape(FOX_HEADS, 1, HEAD_DIM)
    b = b_s[:, :, None]
    o = gmlp_fwd(pu, pv, gain, w_s, b, f"{tag}_sgu")
    return o, mq, (pu, pv, gain, w_s, b)


def _gmlp_backward(do, dmq, saved, tag):
    pu, pv, gain, w_s, b = saved
    dpu, dpv, dw, db, dg = gmlp_bwd(pu, pv, gain, w_s, b, do, f"{tag}_dsgu")
    dproj = jnp.concatenate([_unheads(dpu), _unheads(dpv), _unheads(dmq)], axis=1)
    return dproj, dg.reshape(-1), dw, db[:, :, 0]


SHARDED = ("ffn1_w_in", "ffn1_w_out", "ffn2_w_in", "ffn2_w_out", "w_out", "mem_w_kv", "fox_w_in",
           "gmlp_w_in", "gmlp_v_norm")
SHARD_AXIS = {"ffn1_w_in": 2, "ffn1_w_out": 1, "ffn2_w_in": 2, "ffn2_w_out": 1, "w_out": 1, "mem_w_kv": 1,
              "fox_w_in": 1, "gmlp_w_in": 2, "gmlp_v_norm": 1}
REPLICATED = ("norm_ffn1", "norm_mix", "norm_ffn2", "mem_norm", "mem_q_norm", "mem_k_norm", "fox_b_f",
              "fox_q_norm", "fox_k_norm", "gmlp_w_s", "gmlp_b_s")
WEIGHTS = ("norm_ffn1", "ffn1_w_in", "ffn1_w_out", "norm_mix", "norm_ffn2", "ffn2_w_in", "ffn2_w_out", "w_out",
           "mem_norm", "mem_w_kv", "mem_q_norm", "mem_k_norm", "fox_w_in", "fox_b_f", "fox_q_norm", "fox_k_norm",
           "gmlp_w_in", "gmlp_v_norm", "gmlp_w_s", "gmlp_b_s")


def _join_shards(stacked, axis):
    moved = jnp.moveaxis(stacked, 0, axis)
    shp = list(moved.shape)
    shp[axis:axis + 2] = [shp[axis] * shp[axis + 1]]
    return moved.reshape(shp)


def _split_shards(whole, axis):
    shp = list(whole.shape)
    shp[axis:axis + 1] = [N_DEV, shp[axis] // N_DEV]
    return jnp.moveaxis(whole.reshape(shp), axis, 0)


def kernel(x, mem, norm_ffn1, ffn1_w_in, ffn1_w_out, norm_mix, norm_ffn2, ffn2_w_in, ffn2_w_out, w_out, mem_norm, mem_w_kv, mem_q_norm, mem_k_norm, fox_w_in, fox_b_f, fox_q_norm, fox_k_norm, gmlp_w_in, gmlp_v_norm, gmlp_w_s, gmlp_b_s, loss_target, m_norm_ffn1, m_ffn1_w_in, m_ffn1_w_out, m_norm_mix, m_norm_ffn2, m_ffn2_w_in, m_ffn2_w_out, m_w_out, m_mem_norm, m_mem_w_kv, m_mem_q_norm, m_mem_k_norm, m_fox_w_in, m_fox_b_f, m_fox_q_norm, m_fox_k_norm, m_gmlp_w_in, m_gmlp_v_norm, m_gmlp_w_s, m_gmlp_b_s, v_norm_ffn1, v_ffn1_w_in, v_ffn1_w_out, v_norm_mix, v_norm_ffn2, v_ffn2_w_in, v_ffn2_w_out, v_w_out, v_mem_norm, v_mem_w_kv, v_mem_q_norm, v_mem_k_norm, v_fox_w_in, v_fox_b_f, v_fox_q_norm, v_fox_k_norm, v_gmlp_w_in, v_gmlp_v_norm, v_gmlp_w_s, v_gmlp_b_s):
    w = dict(norm_ffn1=norm_ffn1, ffn1_w_in=ffn1_w_in, ffn1_w_out=ffn1_w_out, norm_mix=norm_mix, norm_ffn2=norm_ffn2, ffn2_w_in=ffn2_w_in, ffn2_w_out=ffn2_w_out, w_out=w_out, mem_norm=mem_norm, mem_w_kv=mem_w_kv, mem_q_norm=mem_q_norm, mem_k_norm=mem_k_norm, fox_w_in=fox_w_in, fox_b_f=fox_b_f, fox_q_norm=fox_q_norm, fox_k_norm=fox_k_norm, gmlp_w_in=gmlp_w_in, gmlp_v_norm=gmlp_v_norm, gmlp_w_s=gmlp_w_s, gmlp_b_s=gmlp_b_s)
    m = dict(norm_ffn1=m_norm_ffn1, ffn1_w_in=m_ffn1_w_in, ffn1_w_out=m_ffn1_w_out, norm_mix=m_norm_mix, norm_ffn2=m_norm_ffn2, ffn2_w_in=m_ffn2_w_in, ffn2_w_out=m_ffn2_w_out, w_out=m_w_out, mem_norm=m_mem_norm, mem_w_kv=m_mem_w_kv, mem_q_norm=m_mem_q_norm, mem_k_norm=m_mem_k_norm, fox_w_in=m_fox_w_in, fox_b_f=m_fox_b_f, fox_q_norm=m_fox_q_norm, fox_k_norm=m_fox_k_norm, gmlp_w_in=m_gmlp_w_in, gmlp_v_norm=m_gmlp_v_norm, gmlp_w_s=m_gmlp_w_s, gmlp_b_s=m_gmlp_b_s)
    v = dict(norm_ffn1=v_norm_ffn1, ffn1_w_in=v_ffn1_w_in, ffn1_w_out=v_ffn1_w_out, norm_mix=v_norm_mix, norm_ffn2=v_norm_ffn2, ffn2_w_in=v_ffn2_w_in, ffn2_w_out=v_ffn2_w_out, w_out=v_w_out, mem_norm=v_mem_norm, mem_w_kv=v_mem_w_kv, mem_q_norm=v_mem_q_norm, mem_k_norm=v_mem_k_norm, fox_w_in=v_fox_w_in, fox_b_f=v_fox_b_f, fox_q_norm=v_fox_q_norm, fox_k_norm=v_fox_k_norm, gmlp_w_in=v_gmlp_w_in, gmlp_v_norm=v_gmlp_v_norm, gmlp_w_s=v_gmlp_w_s, gmlp_b_s=v_gmlp_b_s)

    depth = norm_ffn1.shape[0]
    x0 = x[0]
    mem0 = mem[0]
    target = loss_target[0]

    shard_shapes = [w[n].shape for n in SHARDED]
    to_send = [w[n].astype(BF16) for n in SHARDED[:-1]]
    to_send.append(lax.bitcast_convert_type(w["gmlp_v_norm"], BF16))
    gathered = all_gather(_pack(to_send), "gather_weights")
    got = _unpack(gathered, [a.shape for a in to_send], lead=(N_DEV,))
    full = {n: _join_shards(g, SHARD_AXIS[n] ) for n, g in zip(SHARDED[:-1], got[:-1])}
    full["gmlp_v_norm"] = _join_shards(lax.bitcast_convert_type(got[-1], F32), 1)

    mem_n = rmsnorm_fwd(mem0[None], mem_norm.reshape(1, 1, -1), BF16, "mem_norm")[0]
    fox_split = [_fox_split(full["fox_w_in"][j]) for j in range(fox_w_in.shape[0])]
    saved = []
    xi = x0
    for i in range(depth):
        kind, j = i % 2, i // 2
        x1, ffn1_saved = _ffn_forward(xi, norm_ffn1[i], full["ffn1_w_in"][i], full["ffn1_w_out"][i], f"l{i}_ffn1")
        h = rmsnorm_fwd(x1[None], norm_mix[i].reshape(1, 1, -1), BF16, f"l{i}_mixnorm")[0]
        if kind == 0:
            tok, mq, mix_saved = _fox_forward(h, fox_split[j], fox_b_f[j], fox_q_norm[j], fox_k_norm[j], f"l{i}_fox")
        else:
            tok, mq, mix_saved = _gmlp_forward(h, full["gmlp_w_in"][j], full["gmlp_v_norm"][j], gmlp_w_s[j],
                                               gmlp_b_s[j], f"l{i}_gmlp")
        mo, mem_saved = _mem_forward(mq, mem_n, full["mem_w_kv"][i], mem_q_norm[i], mem_k_norm[i], f"l{i}_mem")
        cat = _unheads(jnp.concatenate([tok, mo], axis=0)).astype(BF16)
        x2 = matmul(cat, full["w_out"][i], f"l{i}_wout", residual=x1)
        x3, ffn2_saved = _ffn_forward(x2, norm_ffn2[i], full["ffn2_w_in"][i], full["ffn2_w_out"][i], f"l{i}_ffn2")
        saved.append((ffn1_saved, x1, h, mix_saved, mem_saved, cat, ffn2_saved))
        xi = x3

    dy, loss_part = loss_head(xi, target, "loss_head")
    loss = lax.psum(loss_part, ("x", "y", "c"))

    grads = {n: [None] * w[n].shape[0] for n in WEIGHTS if n != "mem_norm"}
    dmem_n = None
    for i in reversed(range(depth)):
        kind, j = i % 2, i // 2
        ffn1_saved, x1, h, mix_saved, mem_saved, cat, ffn2_saved = saved[i]
        dy, grads["norm_ffn2"][i], grads["ffn2_w_in"][i], grads["ffn2_w_out"][i] = _ffn_backward(
            dy, ffn2_saved, norm_ffn2[i], full["ffn2_w_in"][i], full["ffn2_w_out"][i], f"l{i}_ffn2")
        grads["w_out"][i] = matmul_tn(cat, dy, f"l{i}_dwout")
        dcat = _heads(matmul(dy, full["w_out"][i].T, f"l{i}_dcat"), FOX_HEADS + MEM_HEADS)
        dmq, grads["mem_q_norm"][i], grads["mem_k_norm"][i], grads["mem_w_kv"][i], dmem_i = _mem_backward(
            dcat[FOX_HEADS:], mem_saved, mem_n, full["mem_w_kv"][i], f"l{i}_mem")
        dmem_n = dmem_i if dmem_n is None else dmem_n + dmem_i
        if kind == 0:
            dproj, grads["fox_b_f"][j], grads["fox_q_norm"][j], grads["fox_k_norm"][j] = _fox_backward(
                dcat[:FOX_HEADS], dmq, mix_saved, f"l{i}_fox")
            grads["fox_w_in"][j] = _fox_unsplit(matmul_tn(h, dproj, f"l{i}_fox_dwin"))
            dh = matmul(dproj, fox_split[j].T, f"l{i}_fox_dh")
        else:
            dproj, grads["gmlp_v_norm"][j], grads["gmlp_w_s"][j], grads["gmlp_b_s"][j] = _gmlp_backward(
                dcat[:FOX_HEADS], dmq, mix_saved, f"l{i}_gmlp")
            grads["gmlp_w_in"][j] = matmul_tn(h, dproj, f"l{i}_gmlp_dwin")
            dh = matmul(dproj, full["gmlp_w_in"][j].T, f"l{i}_gmlp_dh")
        dy, dg_mix = rmsnorm_bwd(x1[None], norm_mix[i].reshape(1, 1, -1), dh[None], f"l{i}_dmixnorm", residual=dy[None])
        dy, grads["norm_mix"][i] = dy[0], dg_mix.reshape(-1)
        dy, grads["norm_ffn1"][i], grads["ffn1_w_in"][i], grads["ffn1_w_out"][i] = _ffn_backward(
            dy, ffn1_saved, norm_ffn1[i], full["ffn1_w_in"][i], full["ffn1_w_out"][i], f"l{i}_ffn1")
    grad_x = dy[None]
    _, dg_mem = rmsnorm_bwd(mem0[None], mem_norm.reshape(1, 1, -1), dmem_n[None], "dmem_norm")
    grad = {n: jnp.stack(g) for n, g in grads.items()}
    grad["mem_norm"] = dg_mem.reshape(-1)

    def update(names, g_slots, tag):
        shapes = [w[n].shape for n in names]
        outs = adamw_packed(_pack([w[n] for n in names]), g_slots, _pack([m[n] for n in names]),
                            _pack([v[n] for n in names]), tag)
        return [dict(zip(names, _unpack(o, shapes))) for o in outs]

    send = _pack([_split_shards(grad[n], SHARD_AXIS[n]) for n in SHARDED], lead=(N_DEV,))
    sharded = update(SHARDED, all_to_all(send, "exchange_grads"), "adamw_sharded")
    partial = _pack([grad[n] for n in REPLICATED])
    replicated = update(REPLICATED, all_gather(partial, "gather_small_grads"), "adamw_replicated")

    out = [loss, grad_x]
    for k in range(4):
        out += [(sharded[k][n] if n in SHARDED else replicated[k][n]) for n in WEIGHTS]
    return tuple(out)
```

```python
import functools
import math

import jax
import jax.numpy as jnp
from jax import lax
from jax.experimental import pallas as pl
from jax.experimental.pallas import tpu as pltpu

F32 = jnp.float32
BF16 = jnp.bfloat16

EPS = 1e-6
HEAD_DIM = 64
FOX_HEADS = 12
MEM_HEADS = 4
TOK_WIDTH = FOX_HEADS * HEAD_DIM
MEM_WIDTH = MEM_HEADS * HEAD_DIM
CHUNK = 128
LANES = 128
N_DEV = 8
SEG_ALIGN = 16 * LANES
PACK_ROWS = 1024
PACK_ALIGN = PACK_ROWS * LANES

ADAM_LR = 0.001
ADAM_B1 = 0.9
ADAM_B2 = 0.999
ADAM_EPS = 1e-08
ADAM_WD = 0.01
ADAM_STEP = 10

VMEM_LIMIT_BYTES = 48 * 1024 * 1024
MESH = pl.DeviceIdType.MESH
CONTRACT_0 = (((0,), (0,)), ((), ()))
CONTRACT_1 = (((1,), (1,)), ((), ()))


def _params(*semantics):
    return pltpu.CompilerParams(dimension_semantics=semantics, vmem_limit_bytes=VMEM_LIMIT_BYTES)


def _pick(n, candidates):
    for c in candidates:
        if c <= n and n % c == 0:
            return c
    return n


def _sigmoid(x):
    return 1.0 / (1.0 + jnp.exp(-x))


def _row_tile(r, w):
    return _pick(r, (1024,) if w >= 512 else (2048, 1024, 512, 256))


def rmsnorm_fwd(x, gain, out_dtype, name):
    g_, r_, w_ = x.shape
    tr = _row_tile(r_, w_)

    def body(x_ref, g_ref, y_ref):
        xv = x_ref[0].astype(F32)
        r = lax.rsqrt(jnp.mean(xv * xv, axis=-1, keepdims=True) + EPS)
        y_ref[0] = (xv * r * g_ref[0]).astype(y_ref.dtype)

    return pl.pallas_call(
        body, name=name, grid=(g_, r_ // tr),
        in_specs=[pl.BlockSpec((1, tr, w_), lambda g, i: (g, i, 0)),
                  pl.BlockSpec((1, 1, w_), lambda g, i: (g, 0, 0))],
        out_specs=pl.BlockSpec((1, tr, w_), lambda g, i: (g, i, 0)),
        out_shape=jax.ShapeDtypeStruct((g_, r_, w_), out_dtype),
        compiler_params=_params("parallel", "parallel"),
    )(x, gain)


def rmsnorm_bwd(x, gain, dy, name, residual=None):
    g_, r_, w_ = x.shape
    tr = _row_tile(r_, w_)
    has_res = residual is not None

    def body(*refs):
        if has_res:
            x_ref, g_ref, dy_ref, res_ref, dx_ref, dg_ref = refs
        else:
            x_ref, g_ref, dy_ref, dx_ref, dg_ref = refs
        xv = x_ref[0].astype(F32)
        dyv = dy_ref[0].astype(F32)
        r = lax.rsqrt(jnp.mean(xv * xv, axis=-1, keepdims=True) + EPS)
        n = xv * r
        dn = dyv * g_ref[0]
        dx = r * (dn - n * jnp.mean(dn * n, axis=-1, keepdims=True))
        if has_res:
            dx = dx + res_ref[0]
        dx_ref[0] = dx
        part = jnp.sum(dyv * n, axis=0, keepdims=True)

        @pl.when(pl.program_id(1) == 0)
        def _():
            dg_ref[0] = part

        @pl.when(pl.program_id(1) != 0)
        def _():
            dg_ref[0] += part

    row = pl.BlockSpec((1, tr, w_), lambda g, i: (g, i, 0))
    vec = pl.BlockSpec((1, 1, w_), lambda g, i: (g, 0, 0))
    operands = (x, gain, dy) + ((residual,) if has_res else ())
    return pl.pallas_call(
        body, name=name, grid=(g_, r_ // tr),
        in_specs=[row, vec, row] + ([row] if has_res else []),
        out_specs=(row, vec),
        out_shape=(jax.ShapeDtypeStruct((g_, r_, w_), F32), jax.ShapeDtypeStruct((g_, 1, w_), F32)),
        compiler_params=_params("parallel", "arbitrary"),
    )(*operands)


def matmul(a, b, name, out_dtype=F32, residual=None, scale=None, transpose_b=False):
    m_, k_ = a.shape
    n_ = b.shape[0] if transpose_b else b.shape[1]
    tm = _pick(m_, (512, 256, 128))
    tn = _pick(n_, (1408, 1024, 896, 512, 256, 128))
    has_res = residual is not None

    def body(*refs):
        if has_res:
            a_ref, b_ref, res_ref, o_ref = refs
        else:
            a_ref, b_ref, o_ref = refs
        av, bv = a_ref[...].astype(BF16), b_ref[...].astype(BF16)
        if transpose_b:
            acc = lax.dot_general(av, bv, CONTRACT_1, preferred_element_type=F32)
        else:
            acc = jnp.dot(av, bv, preferred_element_type=F32)
        if scale is not None:
            acc = acc * scale
        if has_res:
            acc = acc + res_ref[...]
        o_ref[...] = acc.astype(o_ref.dtype)

    out_spec = pl.BlockSpec((tm, tn), lambda j, i: (i, j))
    b_spec = pl.BlockSpec((tn, k_), lambda j, i: (j, 0)) if transpose_b else pl.BlockSpec((k_, tn), lambda j, i: (0, j))
    operands = (a, b) + ((residual,) if has_res else ())
    return pl.pallas_call(
        body, name=name, grid=(n_ // tn, m_ // tm),
        in_specs=[pl.BlockSpec((tm, k_), lambda j, i: (i, 0)), b_spec] + ([out_spec] if has_res else []),
        out_specs=out_spec,
        out_shape=jax.ShapeDtypeStruct((m_, n_), out_dtype),
        compiler_params=_params("parallel", "parallel"),
    )(*operands)


def matmul_tn(a, b, name, scale=None, out_dtype=F32):
    s_, k_ = a.shape
    n_ = b.shape[1]
    tk = _pick(k_, (1024, 1408, 896, 512, 256, 128))
    tn = _pick(n_, (1408, 1024, 896, 512, 256, 128))
    ts = _pick(s_, (512, 256, 128))
    ns = s_ // ts

    def body(a_ref, b_ref, o_ref, acc_ref):
        part = lax.dot_general(a_ref[...].astype(BF16), b_ref[...].astype(BF16), CONTRACT_0,
                               preferred_element_type=F32)
        step = pl.program_id(2)

        @pl.when(step == 0)
        def _():
            acc_ref[...] = part

        @pl.when(step != 0)
        def _():
            acc_ref[...] += part

        @pl.when(step == ns - 1)
        def _():
            acc = acc_ref[...]
            o_ref[...] = (acc if scale is None else acc * scale).astype(o_ref.dtype)

    return pl.pallas_call(
        body, name=name, grid=(k_ // tk, n_ // tn, ns),
        in_specs=[pl.BlockSpec((ts, tk), lambda i, j, s: (s, i)),
                  pl.BlockSpec((ts, tn), lambda i, j, s: (s, j))],
        out_specs=pl.BlockSpec((tk, tn), lambda i, j, s: (i, j)),
        out_shape=jax.ShapeDtypeStruct((k_, n_), out_dtype),
        scratch_shapes=[pltpu.VMEM((tk, tn), F32)],
        compiler_params=_params("parallel", "parallel", "arbitrary"),
    )(a, b)


def ffn_in(h, w_in_t, name):
    s_, d_ = h.shape
    f_ = w_in_t.shape[0] // 2
    tm = _pick(s_, (512, 256, 128))
    tn = _pick(f_, (1408, 1024, 512, 256, 128))
    nb = f_ // tn

    def body(h_ref, wa_ref, wb_ref, a_ref, b_ref, u_ref):
        hv = h_ref[...]
        a = lax.dot_general(hv, wa_ref[...], CONTRACT_1, preferred_element_type=F32)
        b = lax.dot_general(hv, wb_ref[...], CONTRACT_1, preferred_element_type=F32)
        a_ref[...] = a.astype(BF16)
        b_ref[...] = b.astype(BF16)
        u_ref[...] = (a * _sigmoid(a) * b).astype(BF16)

    o_spec = pl.BlockSpec((tm, tn), lambda j, i: (i, j))
    out = jax.ShapeDtypeStruct((s_, f_), BF16)
    return pl.pallas_call(
        body, name=name, grid=(nb, s_ // tm),
        in_specs=[pl.BlockSpec((tm, d_), lambda j, i: (i, 0)),
                  pl.BlockSpec((tn, d_), lambda j, i: (j, 0)),
                  pl.BlockSpec((tn, d_), lambda j, i: (j + nb, 0))],
        out_specs=(o_spec, o_spec, o_spec), out_shape=(out, out, out),
        compiler_params=_params("parallel", "parallel"),
    )(h, w_in_t, w_in_t)


def ffn_bwd_act(dy, w_out, a, b, name):
    s_, d_ = dy.shape
    f_ = w_out.shape[0]
    tm = _pick(s_, (512, 256, 128))
    tn = _pick(f_, (1408, 1024, 512, 256, 128))

    def body(dy_ref, w_ref, a_ref, b_ref, da_ref, db_ref):
        du = 0.5 * lax.dot_general(dy_ref[...].astype(BF16), w_ref[...], CONTRACT_1, preferred_element_type=F32)
        av = a_ref[...].astype(F32)
        bv = b_ref[...].astype(F32)
        sig = _sigmoid(av)
        da_ref[...] = (du * bv * (sig * (1.0 + av * (1.0 - sig)))).astype(BF16)
        db_ref[...] = (du * (av * sig)).astype(BF16)

    t_spec = pl.BlockSpec((tm, tn), lambda j, i: (i, j))
    out = jax.ShapeDtypeStruct((s_, f_), BF16)
    return pl.pallas_call(
        body, name=name, grid=(f_ // tn, s_ // tm),
        in_specs=[pl.BlockSpec((tm, d_), lambda j, i: (i, 0)),
                  pl.BlockSpec((tn, d_), lambda j, i: (j, 0)), t_spec, t_spec],
        out_specs=(t_spec, t_spec), out_shape=(out, out),
        compiler_params=_params("parallel", "parallel"),
    )(dy, w_out, a, b)


def _scores(q, k, cq_ref, ck_ref, masked):
    s = lax.dot_general(q, k, CONTRACT_1, preferred_element_type=F32) * (1.0 / math.sqrt(HEAD_DIM))
    if cq_ref is not None:
        s = s + (cq_ref[0] - ck_ref[0])
    if masked:
        row = lax.broadcasted_iota(jnp.int32, s.shape, 0)
        col = lax.broadcasted_iota(jnp.int32, s.shape, 1)
        s = jnp.where(col <= row, s, -jnp.inf)
    return s


def attn_fwd(q, k, v, cq, ck, causal, name):
    h_, sq, d_ = q.shape
    sk = k.shape[1]
    tq = _pick(sq, (512, 256, 128))
    tk = tq if causal else _pick(sk, (512, 256, 128))
    nk = sk // tk
    bias = cq is not None

    def body(*refs):
        if bias:
            q_ref, k_ref, v_ref, cq_ref, ck_ref, o_ref, lse_ref, m_sc, l_sc, acc_sc = refs
        else:
            q_ref, k_ref, v_ref, o_ref, lse_ref, m_sc, l_sc, acc_sc = refs
            cq_ref = ck_ref = None
        i, j = pl.program_id(1), pl.program_id(2)

        @pl.when(j == 0)
        def _():
            m_sc[...] = jnp.full(m_sc.shape, -jnp.inf, F32)
            l_sc[...] = jnp.zeros(l_sc.shape, F32)
            acc_sc[...] = jnp.zeros(acc_sc.shape, F32)

        def step(masked):
            s = _scores(q_ref[0].astype(BF16), k_ref[0].astype(BF16), cq_ref, ck_ref, masked)
            m_prev = m_sc[...]
            m_new = jnp.maximum(m_prev, jnp.max(s, axis=1, keepdims=True))
            alpha = jnp.exp(m_prev - m_new)
            p = jnp.exp(s - m_new)
            l_sc[...] = alpha * l_sc[...] + jnp.sum(p, axis=1, keepdims=True)
            acc_sc[...] = alpha * acc_sc[...] + jnp.dot(p.astype(BF16), v_ref[0].astype(BF16),
                                                        preferred_element_type=F32)
            m_sc[...] = m_new

        if causal:
            pl.when(j < i)(functools.partial(step, False))
            pl.when(j == i)(functools.partial(step, True))
        else:
            step(False)

        @pl.when(j == (i if causal else nk - 1))
        def _():
            o_ref[0] = acc_sc[...] / l_sc[...]
            lse_ref[0] = m_sc[...] + jnp.log(l_sc[...])

    kj = (lambda j, i: jnp.minimum(j, i)) if causal else (lambda j, i: j)
    q_spec = pl.BlockSpec((1, tq, d_), lambda h, i, j: (h, i, 0))
    k_spec = pl.BlockSpec((1, tk, d_), lambda h, i, j: (h, kj(j, i), 0))
    in_specs = [q_spec, k_spec, k_spec]
    operands = [q, k, v]
    if bias:
        in_specs += [pl.BlockSpec((1, tq, 1), lambda h, i, j: (h, i, 0)),
                     pl.BlockSpec((1, 1, tk), lambda h, i, j: (h, 0, kj(j, i)))]
        operands += [cq, ck]
    return pl.pallas_call(
        body, name=name, grid=(h_, sq // tq, nk),
        in_specs=in_specs,
        out_specs=(q_spec, pl.BlockSpec((1, tq, 1), lambda h, i, j: (h, i, 0))),
        out_shape=(jax.ShapeDtypeStruct((h_, sq, d_), F32), jax.ShapeDtypeStruct((h_, sq, 1), F32)),
        scratch_shapes=[pltpu.VMEM((tq, 1), F32), pltpu.VMEM((tq, 1), F32), pltpu.VMEM((tq, d_), F32)],
        compiler_params=_params("parallel", "parallel", "arbitrary"),
    )(*operands)


def attn_bwd(q, k, v, cq, ck, o, lse, do, causal, name):
    h_, sq, d_ = q.shape
    sk = k.shape[1]
    tq = _pick(sq, (512, 256, 128))
    tk = tq if causal else _pick(sk, (512, 256, 128))
    nq = sq // tq
    bias = cq is not None
    scale = 1.0 / math.sqrt(HEAD_DIM)

    def body(*refs):
        if bias:
            (q_ref, k_ref, v_ref, cq_ref, ck_ref, o_ref, lse_ref, do_ref,
             dq_ref, dk_ref, dv_ref, dc_ref, dcq_ref, dk_sc, dv_sc, dc_sc) = refs
        else:
            (q_ref, k_ref, v_ref, o_ref, lse_ref, do_ref,
             dq_ref, dk_ref, dv_ref, dk_sc, dv_sc) = refs
            cq_ref = ck_ref = dc_ref = dcq_ref = dc_sc = None
        j, i = pl.program_id(1), pl.program_id(2)
        first_i = j if causal else 0

        @pl.when((j == 0) & (i == 0))
        def _():
            dq_ref[...] = jnp.zeros(dq_ref.shape, F32)
            if bias:
                dcq_ref[...] = jnp.zeros(dcq_ref.shape, F32)

        @pl.when(i == first_i)
        def _():
            dk_sc[...] = jnp.zeros(dk_sc.shape, F32)
            dv_sc[...] = jnp.zeros(dv_sc.shape, F32)
            if bias:
                dc_sc[...] = jnp.zeros(dc_sc.shape, F32)

        def step(masked):
            qb = q_ref[0].astype(BF16)
            kb = k_ref[0].astype(BF16)
            dof = do_ref[0].astype(F32)
            dob = dof.astype(BF16)
            s = _scores(qb, kb, cq_ref, ck_ref, masked)
            p = jnp.exp(s - lse_ref[0])
            dp = lax.dot_general(dob, v_ref[0].astype(BF16), CONTRACT_1, preferred_element_type=F32)
            delta = jnp.sum(dof * o_ref[0], axis=1, keepdims=True)
            ds = p * (dp - delta)
            dsb = ds.astype(BF16)
            dv_sc[...] += lax.dot_general(p.astype(BF16), dob, CONTRACT_0, preferred_element_type=F32)
            dk_sc[...] += lax.dot_general(dsb, qb, CONTRACT_0, preferred_element_type=F32) * scale
            rows = pl.ds(pl.multiple_of(i * tq, tq), tq)
            dq_ref[0, rows, :] += jnp.dot(dsb, kb, preferred_element_type=F32) * scale
            if bias:
                dc_sc[...] -= jnp.sum(ds, axis=0, keepdims=True)
                dcq_ref[0, rows, :] += jnp.sum(ds, axis=1, keepdims=True)

        if causal:
            pl.when(i > j)(functools.partial(step, False))
            pl.when(i == j)(functools.partial(step, True))
        else:
            step(False)

        @pl.when(i == nq - 1)
        def _():
            dk_ref[0] = dk_sc[...]
            dv_ref[0] = dv_sc[...]
            if bias:
                dc_ref[0] = dc_sc[...]

    qi = (lambda j, i: jnp.maximum(i, j)) if causal else (lambda j, i: i)
    q_spec = pl.BlockSpec((1, tq, d_), lambda h, j, i: (h, qi(j, i), 0))
    q1_spec = pl.BlockSpec((1, tq, 1), lambda h, j, i: (h, qi(j, i), 0))
    k_spec = pl.BlockSpec((1, tk, d_), lambda h, j, i: (h, j, 0))
    c_spec = pl.BlockSpec((1, 1, tk), lambda h, j, i: (h, 0, j))
    in_specs = [q_spec, k_spec, k_spec] + ([q1_spec, c_spec] if bias else []) + [q_spec, q1_spec, q_spec]
    operands = [q, k, v] + ([cq, ck] if bias else []) + [o, lse, do]
    out_specs = [pl.BlockSpec((1, sq, d_), lambda h, j, i: (h, 0, 0)), k_spec, k_spec]
    out_shape = [jax.ShapeDtypeStruct((h_, sq, d_), F32), jax.ShapeDtypeStruct((h_, sk, d_), F32),
                 jax.ShapeDtypeStruct((h_, sk, d_), F32)]
    scratch = [pltpu.VMEM((tk, d_), F32), pltpu.VMEM((tk, d_), F32)]
    if bias:
        out_specs += [c_spec, pl.BlockSpec((1, sq, 1), lambda h, j, i: (h, 0, 0))]
        out_shape += [jax.ShapeDtypeStruct((h_, 1, sk), F32), jax.ShapeDtypeStruct((h_, sq, 1), F32)]
        scratch.append(pltpu.VMEM((1, tk), F32))
    return pl.pallas_call(
        body, name=name, grid=(h_, sk // tk, nq),
        in_specs=in_specs, out_specs=tuple(out_specs), out_shape=tuple(out_shape),
        scratch_shapes=scratch,
        compiler_params=_params("parallel", "arbitrary", "arbitrary"),
    )(*operands)


def _tri(lower):
    row = lax.broadcasted_iota(jnp.int32, (CHUNK, CHUNK), 0)
    col = lax.broadcasted_iota(jnp.int32, (CHUNK, CHUNK), 1)
    return jnp.where((col <= row) if lower else (col >= row), 1.0, 0.0).astype(F32)


def fox_gate_fwd(f, b, name):
    s_ = f.shape[0]

    def body(f_ref, b_ref, c_ref, carry):
        @pl.when(pl.program_id(0) == 0)
        def _():
            carry[...] = jnp.zeros(carry.shape, F32)

        xv = f_ref[...] + b_ref[...]
        log_f = jnp.minimum(xv, 0.0) - jnp.log(1.0 + jnp.exp(-jnp.abs(xv)))
        c = jnp.dot(_tri(True), log_f, precision=lax.Precision.HIGHEST, preferred_element_type=F32) + carry[...]
        c_ref[...] = c
        carry[...] = c[CHUNK - 1:CHUNK, :]

    blk = pl.BlockSpec((CHUNK, LANES), lambda i: (i, 0))
    return pl.pallas_call(
        body, name=name, grid=(s_ // CHUNK,),
        in_specs=[blk, pl.BlockSpec((1, LANES), lambda i: (0, 0))], out_specs=blk,
        out_shape=jax.ShapeDtypeStruct((s_, LANES), F32),
        scratch_shapes=[pltpu.VMEM((1, LANES), F32)],
        compiler_params=_params("arbitrary"),
    )(f, b)


def fox_gate_bwd(dc, f, b, name):
    s_ = f.shape[0]
    n = s_ // CHUNK

    def body(dc_ref, f_ref, b_ref, df_ref, db_ref, carry):
        @pl.when(pl.program_id(0) == 0)
        def _():
            carry[...] = jnp.zeros(carry.shape, F32)
            db_ref[...] = jnp.zeros(db_ref.shape, F32)

        dlog = jnp.dot(_tri(False), dc_ref[...], precision=lax.Precision.HIGHEST,
                       preferred_element_type=F32) + carry[...]
        df = dlog * _sigmoid(-(f_ref[...] + b_ref[...]))
        df_ref[...] = df
        db_ref[...] += jnp.sum(df, axis=0, keepdims=True)
        carry[...] = dlog[0:1, :]

    blk = pl.BlockSpec((CHUNK, LANES), lambda i: (n - 1 - i, 0))
    vec = pl.BlockSpec((1, LANES), lambda i: (0, 0))
    return pl.pallas_call(
        body, name=name, grid=(n,),
        in_specs=[blk, blk, vec], out_specs=(blk, vec),
        out_shape=(jax.ShapeDtypeStruct((s_, LANES), F32), jax.ShapeDtypeStruct((1, LANES), F32)),
        scratch_shapes=[pltpu.VMEM((1, LANES), F32)],
        compiler_params=_params("arbitrary"),
    )(dc, f, b)


GELU_K = math.sqrt(2.0 / math.pi)
GELU_C = 0.044715


def _gelu(x):
    return 0.5 * x * (1.0 + jnp.tanh(GELU_K * (x + GELU_C * (x * x * x))))


def _gelu_grad(x):
    t = jnp.tanh(GELU_K * (x + GELU_C * (x * x * x)))
    return 0.5 * (1.0 + t) + 0.5 * x * (1.0 - t * t) * (GELU_K * (1.0 + 3.0 * GELU_C * (x * x)))


def _tril_mask():
    row = lax.broadcasted_iota(jnp.int32, (CHUNK, CHUNK), 0)
    col = lax.broadcasted_iota(jnp.int32, (CHUNK, CHUNK), 1)
    return col <= row


def _gmlp_specs(s_, ts):
    row = pl.BlockSpec((1, ts, HEAD_DIM), lambda g, i: (g, i, 0))
    gain = pl.BlockSpec((1, 1, HEAD_DIM), lambda g, i: (g, 0, 0))
    w = pl.BlockSpec((1, CHUNK, CHUNK), lambda g, i: (g, 0, 0))
    b = pl.BlockSpec((1, CHUNK, 1), lambda g, i: (g, 0, 0))
    return row, gain, w, b


def gmlp_fwd(pu, pv, gain, w, b, name):
    g_, s_, d_ = pu.shape
    ts = _pick(s_, (1024, 512, 256, 128))

    def body(pu_ref, pv_ref, g_ref, w_ref, b_ref, o_ref):
        v = _gelu(pv_ref[0])
        r = lax.rsqrt(jnp.mean(v * v, axis=-1, keepdims=True) + EPS)
        vn = (v * r * g_ref[0]).astype(BF16)
        wt = jnp.where(_tril_mask(), w_ref[0], 0.0).astype(BF16)
        for c in range(ts // CHUNK):
            rows = pl.ds(c * CHUNK, CHUNK)
            gate = jnp.dot(wt, vn[c * CHUNK:(c + 1) * CHUNK], preferred_element_type=F32) + b_ref[0]
            o_ref[0, rows, :] = _gelu(pu_ref[0, rows, :]) * gate

    row, gspec, wspec, bspec = _gmlp_specs(s_, ts)
    return pl.pallas_call(
        body, name=name, grid=(g_, s_ // ts),
        in_specs=[row, row, gspec, wspec, bspec], out_specs=row,
        out_shape=jax.ShapeDtypeStruct((g_, s_, d_), F32),
        compiler_params=_params("parallel", "parallel"),
    )(pu, pv, gain, w, b)


def gmlp_bwd(pu, pv, gain, w, b, dout, name):
    g_, s_, d_ = pu.shape
    ts = _pick(s_, (1024, 512, 256, 128))

    def body(pu_ref, pv_ref, g_ref, w_ref, b_ref, do_ref, dpu_ref, dpv_ref, dw_ref, db_ref, dg_ref):
        @pl.when(pl.program_id(1) == 0)
        def _():
            dw_ref[...] = jnp.zeros(dw_ref.shape, F32)
            db_ref[...] = jnp.zeros(db_ref.shape, F32)
            dg_ref[...] = jnp.zeros(dg_ref.shape, F32)

        gain_v = g_ref[0]
        mask = _tril_mask()
        wt = jnp.where(mask, w_ref[0], 0.0).astype(BF16)
        dw = jnp.zeros((CHUNK, CHUNK), F32)
        db = jnp.zeros((CHUNK, 1), F32)
        dg = jnp.zeros((1, d_), F32)
        for c in range(ts // CHUNK):
            rows = pl.ds(c * CHUNK, CHUNK)
            pu_c = pu_ref[0, rows, :]
            pv_c = pv_ref[0, rows, :]
            do_c = do_ref[0, rows, :]
            u = _gelu(pu_c)
            v = _gelu(pv_c)
            r = lax.rsqrt(jnp.mean(v * v, axis=-1, keepdims=True) + EPS)
            n = v * r
            vn = (n * gain_v).astype(BF16)
            gate = jnp.dot(wt, vn, preferred_element_type=F32) + b_ref[0]
            dgate = do_c * u
            dgate_b = dgate.astype(BF16)
            dpu_ref[0, rows, :] = do_c * gate * _gelu_grad(pu_c)
            db = db + jnp.sum(dgate, axis=1, keepdims=True)
            dw = dw + lax.dot_general(dgate_b, vn, CONTRACT_1, preferred_element_type=F32)
            dvn = lax.dot_general(wt, dgate_b, CONTRACT_0, preferred_element_type=F32)
            dg = dg + jnp.sum(dvn * n, axis=0, keepdims=True)
            dn = dvn * gain_v
            dv = r * (dn - n * jnp.mean(dn * n, axis=-1, keepdims=True))
            dpv_ref[0, rows, :] = dv * _gelu_grad(pv_c)
        dw_ref[0] += jnp.where(mask, dw, 0.0)
        db_ref[0] += db
        dg_ref[0] += dg

    row, gspec, wspec, bspec = _gmlp_specs(s_, ts)
    return pl.pallas_call(
        body, name=name, grid=(g_, s_ // ts),
        in_specs=[row, row, gspec, wspec, bspec, row],
        out_specs=(row, row, wspec, bspec, gspec),
        out_shape=(jax.ShapeDtypeStruct((g_, s_, d_), F32), jax.ShapeDtypeStruct((g_, s_, d_), F32),
                   jax.ShapeDtypeStruct((g_, CHUNK, CHUNK), F32), jax.ShapeDtypeStruct((g_, CHUNK, 1), F32),
                   jax.ShapeDtypeStruct((g_, 1, d_), F32)),
        compiler_params=_params("parallel", "arbitrary"),
    )(pu, pv, gain, w, b, dout)


def loss_head(y, t, name):
    s_, d_ = y.shape
    tr = _pick(s_, (1024, 512, 256, 128))

    def body(y_ref, t_ref, dy_ref, l_ref):
        err = y_ref[...] - t_ref[...]
        dy_ref[...] = err * (1.0 / d_)
        part = jnp.full(l_ref.shape, jnp.sum(err * err) * (0.5 / d_), F32)

        @pl.when(pl.program_id(0) == 0)
        def _():
            l_ref[...] = part

        @pl.when(pl.program_id(0) != 0)
        def _():
            l_ref[...] += part

    blk = pl.BlockSpec((tr, d_), lambda i: (i, 0))
    dy, l = pl.pallas_call(
        body, name=name, grid=(s_ // tr,),
        in_specs=[blk, blk], out_specs=(blk, pl.BlockSpec((8, LANES), lambda i: (0, 0))),
        out_shape=(jax.ShapeDtypeStruct((s_, d_), F32), jax.ShapeDtypeStruct((8, LANES), F32)),
        compiler_params=_params("arbitrary"),
    )(y, t)
    return dy, l[0, 0]


def adamw(w, g_slots, m, v, name):
    r_, c_ = w.shape
    tr = _pick(r_, (1024, 512, 352, 256, 224, 128, 64, 32, 16, 8))

    def body(w_ref, gs_ref, m_ref, v_ref, g_ref, d_ref, nm_ref, nv_ref):
        g = gs_ref[0].astype(F32)
        for k in range(1, N_DEV):
            g = g + gs_ref[k].astype(F32)
        m_new = ADAM_B1 * m_ref[...] + (1.0 - ADAM_B1) * g
        v_new = ADAM_B2 * v_ref[...] + (1.0 - ADAM_B2) * (g * g)
        m_hat = m_new / (1.0 - ADAM_B1 ** ADAM_STEP)
        v_hat = v_new / (1.0 - ADAM_B2 ** ADAM_STEP)
        g_ref[...] = g
        d_ref[...] = -ADAM_LR * (m_hat / (jnp.sqrt(v_hat) + ADAM_EPS) + ADAM_WD * w_ref[...])
        nm_ref[...] = m_new
        nv_ref[...] = v_new

    blk = pl.BlockSpec((tr, c_), lambda i: (i, 0))
    out = jax.ShapeDtypeStruct((r_, c_), F32)
    return pl.pallas_call(
        body, name=name, grid=(r_ // tr,),
        in_specs=[blk, pl.BlockSpec((N_DEV, tr, c_), lambda i: (0, i, 0)), blk, blk],
        out_specs=(blk, blk, blk, blk), out_shape=(out, out, out, out),
        compiler_params=_params("parallel"),
    )(w, g_slots, m, v)


def _position():
    x, y, c = lax.axis_index("x"), lax.axis_index("y"), lax.axis_index("c")
    return x, y, c


def _slot(px, py, pc):
    return 4 * px + 2 * py + pc


def all_gather_multi(blocks, name):
    n = len(blocks)

    def body(*refs):
        x_refs, out_refs = refs[:n], refs[n:2 * n]
        send_sems, recv_sems, local_sems = refs[2 * n:]
        x, y, c = _position()
        me, sibling = (x, y, c), (x, y, 1 - c)
        chips = [(1 - x, y), (x, 1 - y), (1 - x, 1 - y)]

        def copy(b, k, owner, to, src=None):
            dst = out_refs[b].at[_slot(*owner)]
            return pltpu.make_async_remote_copy(
                src_ref=dst if src is None else src, dst_ref=dst,
                send_sem=send_sems.at[b, k], recv_sem=recv_sems.at[b, k], device_id=to, device_id_type=MESH)

        mine = [pltpu.make_async_copy(x_refs[b], out_refs[b].at[_slot(*me)], local_sems.at[b]) for b in range(n)]
        for cp in mine:
            cp.start()
        first = [copy(b, 1 + j, me, (*chip, c), src=x_refs[b]) for j, chip in enumerate(chips) for b in range(n)]
        first += [copy(b, 0, me, sibling, src=x_refs[b]) for b in range(n)]
        for cp in first:
            cp.start()
        passed = []
        for j, chip in enumerate(chips):
            for b in range(n):
                copy(b, 1 + j, (*chip, c), me).wait_recv()
                cp = copy(b, 4 + j, (*chip, c), sibling)
                cp.start()
                passed.append(cp)
        for b in range(n):
            copy(b, 0, sibling, me).wait_recv()
        for j, chip in enumerate(chips):
            for b in range(n):
                copy(b, 4 + j, (*chip, 1 - c), me).wait_recv()
        for cp in first + passed:
            cp.wait_send()
        for cp in mine:
            cp.wait()

    any_spec = pl.BlockSpec(memory_space=pl.ANY)
    return pl.pallas_call(
        body, name=name,
        in_specs=[any_spec] * n, out_specs=tuple([any_spec] * n),
        out_shape=tuple(jax.ShapeDtypeStruct((N_DEV,) + blk.shape, blk.dtype) for blk in blocks),
        scratch_shapes=[pltpu.SemaphoreType.DMA((n, 7)), pltpu.SemaphoreType.DMA((n, 7)), pltpu.SemaphoreType.DMA((n,))],
    )(*blocks)


def all_to_all_multi(slot_bufs, name):
    n = len(slot_bufs)

    def body(*refs):
        in_refs, out_refs = refs[:n], refs[n:2 * n]
        send_sems, recv_sems, local_sems = refs[2 * n:]
        x, y, c = _position()
        me = _slot(x, y, c)
        mine = [pltpu.make_async_copy(in_refs[b].at[me], out_refs[b].at[me], local_sems.at[b]) for b in range(n)]
        for cp in mine:
            cp.start()
        copies = []
        for k in range(1, N_DEV):
            px = 1 - x if k & 4 else x
            py = 1 - y if k & 2 else y
            pc = 1 - c if k & 1 else c
            for b in range(n):
                cp = pltpu.make_async_remote_copy(
                    src_ref=in_refs[b].at[_slot(px, py, pc)], dst_ref=out_refs[b].at[me],
                    send_sem=send_sems.at[b, k - 1], recv_sem=recv_sems.at[b, k - 1],
                    device_id=(px, py, pc), device_id_type=MESH)
                cp.start()
                copies.append(cp)
        for cp in copies:
            cp.wait()
        for cp in mine:
            cp.wait()

    any_spec = pl.BlockSpec(memory_space=pl.ANY)
    return pl.pallas_call(
        body, name=name,
        in_specs=[any_spec] * n, out_specs=tuple([any_spec] * n),
        out_shape=tuple(jax.ShapeDtypeStruct(buf.shape, buf.dtype) for buf in slot_bufs),
        scratch_shapes=[pltpu.SemaphoreType.DMA((n, 7)), pltpu.SemaphoreType.DMA((n, 7)), pltpu.SemaphoreType.DMA((n,))],
    )(*slot_bufs)


def _seg_len(n):
    return -(-n // SEG_ALIGN) * SEG_ALIGN


def _pack(arrays, lead=()):
    parts, total = [], 0
    for a in arrays:
        flat = a.reshape(lead + (-1,))
        pad = _seg_len(flat.shape[-1]) - flat.shape[-1]
        parts.append(jnp.pad(flat, [(0, 0)] * len(lead) + [(0, pad)]) if pad else flat)
        total += flat.shape[-1] + pad
    tail = -(-total // PACK_ALIGN) * PACK_ALIGN - total
    if tail:
        parts.append(jnp.zeros(lead + (tail,), parts[0].dtype))
    return jnp.concatenate(parts, axis=-1).reshape(lead + (-1, LANES))


def _unpack(packed, shapes, lead=()):
    flat = packed.reshape(lead + (-1,))
    out, off = [], 0
    for shp in shapes:
        n = math.prod(shp)
        out.append(flat[..., off:off + n].reshape(lead + tuple(shp)))
        off += _seg_len(n)
    return out


def _heads(a, n_heads):
    return a.reshape(a.shape[0], n_heads, HEAD_DIM).transpose(1, 0, 2)


def _unheads(a):
    return a.transpose(1, 0, 2).reshape(a.shape[1], a.shape[0] * HEAD_DIM)


def _head_gain(g, n_heads):
    return jnp.broadcast_to(g.reshape(1, 1, HEAD_DIM), (n_heads, 1, HEAD_DIM))


def _ffn_forward(x, gain, w_in_t, w_out, tag):
    h = rmsnorm_fwd(x[None], gain.reshape(1, 1, -1), BF16, f"{tag}_norm")[0]
    a, b, u = ffn_in(h, w_in_t, f"{tag}_in")
    y = matmul(u, w_out, f"{tag}_out", residual=x, scale=0.5)
    return y, (x, h, a, b, u)


def _ffn_backward(dy, saved, gain, w_in_t, w_out, tag):
    x, h, a, b, u = saved
    f_ = w_out.shape[0]
    dw_out = matmul_tn(u, dy, f"{tag}_dwout", scale=0.5, out_dtype=BF16)
    da, db = ffn_bwd_act(dy, w_out, a, b, f"{tag}_dact")
    dh = matmul(da, w_in_t[:f_], f"{tag}_dh_a")
    dh = matmul(db, w_in_t[f_:], f"{tag}_dh_b", residual=dh)
    dw_in_t = jnp.concatenate([matmul_tn(da, h, f"{tag}_dwin_a", out_dtype=BF16),
                               matmul_tn(db, h, f"{tag}_dwin_b", out_dtype=BF16)], axis=0)
    dx, dgain = rmsnorm_bwd(x[None], gain.reshape(1, 1, -1), dh[None], f"{tag}_dnorm", residual=dy[None])
    return dx[0], dgain.reshape(-1), dw_in_t, dw_out


def _mem_forward(mq, mem_n, w_kv, g_q, g_k, tag):
    gq = _head_gain(g_q, MEM_HEADS)
    gk = _head_gain(g_k, MEM_HEADS)
    qn = rmsnorm_fwd(mq, gq, BF16, f"{tag}_qnorm")
    kv = matmul(mem_n, w_kv, f"{tag}_kv")
    k = _heads(kv[:, :MEM_WIDTH], MEM_HEADS)
    v = _heads(kv[:, MEM_WIDTH:], MEM_HEADS)
    kn = rmsnorm_fwd(k, gk, BF16, f"{tag}_knorm")
    o, lse = attn_fwd(qn, kn, v, None, None, False, f"{tag}_attn")
    return o, (mq, gq, gk, qn, k, kn, v, o, lse)


def _mem_backward(do, saved, mem_n, w_kv, tag):
    mq, gq, gk, qn, k, kn, v, o, lse = saved
    dqn, dkn, dv = attn_bwd(qn, kn, v, None, None, o, lse, do, False, f"{tag}_dattn")
    dmq, dgq = rmsnorm_bwd(mq, gq, dqn, f"{tag}_dqnorm")
    dk, dgk = rmsnorm_bwd(k, gk, dkn, f"{tag}_dknorm")
    dkv = jnp.concatenate([_unheads(dk), _unheads(dv)], axis=1)
    dw_kv = matmul_tn(mem_n, dkv, f"{tag}_dwkv", out_dtype=BF16)
    dmem_n = matmul(dkv, w_kv, f"{tag}_dmem", transpose_b=True)
    return dmq, dgq.sum(axis=0).reshape(-1), dgk.sum(axis=0).reshape(-1), dw_kv, dmem_n


def _fox_split(w_in):
    t3 = 3 * TOK_WIDTH
    pad = jnp.zeros(w_in.shape[:-1] + (LANES - FOX_HEADS,), w_in.dtype)
    return jnp.concatenate([w_in[..., :t3], w_in[..., t3 + FOX_HEADS:], w_in[..., t3:t3 + FOX_HEADS], pad], axis=-1)


def _fox_unsplit(w):
    t3 = 3 * TOK_WIDTH
    return jnp.concatenate([w[..., :t3], w[..., t3 + MEM_WIDTH:t3 + MEM_WIDTH + FOX_HEADS], w[..., t3:t3 + MEM_WIDTH]],
                           axis=-1)


def _fox_forward(h, w_split, b_f, g_q, g_k, tag):
    t3 = 3 * TOK_WIDTH
    proj = matmul(h, w_split, f"{tag}_proj")
    qkv = _heads(proj[:, :t3], 3 * FOX_HEADS)
    mq = _heads(proj[:, t3:t3 + MEM_WIDTH], MEM_HEADS)
    f_pad = proj[:, t3 + MEM_WIDTH:]
    b_pad = jnp.pad(b_f.reshape(1, -1), ((0, 0), (0, LANES - FOX_HEADS)))
    gains = jnp.concatenate([_head_gain(g_q, FOX_HEADS), _head_gain(g_k, FOX_HEADS)], axis=0)
    qk = qkv[:2 * FOX_HEADS]
    qkn = rmsnorm_fwd(qk, gains, BF16, f"{tag}_qknorm")
    v = qkv[2 * FOX_HEADS:]
    c = fox_gate_fwd(f_pad, b_pad, f"{tag}_gate")
    c_t = c[:, :FOX_HEADS].T
    cq, ck = c_t[:, :, None], c_t[:, None, :]
    qn, kn = qkn[:FOX_HEADS], qkn[FOX_HEADS:]
    o, lse = attn_fwd(qn, kn, v, cq, ck, True, f"{tag}_attn")
    return o, mq, (qk, gains, qn, kn, v, cq, ck, o, lse, f_pad, b_pad)


def _fox_backward(do, dmq, saved, tag):
    qk, gains, qn, kn, v, cq, ck, o, lse, f_pad, b_pad = saved
    dqn, dkn, dv, dck, dcq = attn_bwd(qn, kn, v, cq, ck, o, lse, do, True, f"{tag}_dattn")
    dqk, dgains = rmsnorm_bwd(qk, gains, jnp.concatenate([dqn, dkn], axis=0), f"{tag}_dqknorm")
    dc = jnp.pad((dck[:, 0, :] + dcq[:, :, 0]).T, ((0, 0), (0, LANES - FOX_HEADS)))
    df, db = fox_gate_bwd(dc, f_pad, b_pad, f"{tag}_dgate")
    dproj = jnp.concatenate([_unheads(dqk), _unheads(dv), _unheads(dmq), df], axis=1)
    dgains = dgains.reshape(2, FOX_HEADS, HEAD_DIM).sum(axis=1)
    return dproj, db[0, :FOX_HEADS], dgains[0], dgains[1]


def _gmlp_forward(h, w_in_t, v_gain, w_s, b_s, tag):
    proj = matmul(h, w_in_t, f"{tag}_proj", transpose_b=True)
    pu = _heads(proj[:, :TOK_WIDTH], FOX_HEADS)
    pv = _heads(proj[:, TOK_WIDTH:2 * TOK_WIDTH], FOX_HEADS)
    mq = _heads(proj[:, 2 * TOK_WIDTH:], MEM_HEADS)
    gain = v_gain.reshape(FOX_HEADS, 1, HEAD_DIM)
    b = b_s[:, :, None]
    o = gmlp_fwd(pu, pv, gain, w_s, b, f"{tag}_sgu")
    return o, mq, (pu, pv, gain, w_s, b)


def _gmlp_backward(do, dmq, saved, tag):
    pu, pv, gain, w_s, b = saved
    dpu, dpv, dw, db, dg = gmlp_bwd(pu, pv, gain, w_s, b, do, f"{tag}_dsgu")
    dproj = jnp.concatenate([_unheads(dpu), _unheads(dpv), _unheads(dmq)], axis=1)
    return dproj, dg.reshape(-1), dw, db[:, :, 0]


BIG = ("ffn1_w_in", "ffn1_w_out", "ffn2_w_in", "ffn2_w_out", "w_out", "mem_w_kv", "fox_w_in", "gmlp_w_in")
COLUMN_SHARDED = ("ffn1_w_in", "ffn2_w_in", "gmlp_w_in")
REPLICATED =("norm_ffn1", "norm_mix", "norm_ffn2", "mem_norm", "mem_q_norm", "mem_k_norm", "fox_b_f",
              "fox_q_norm", "fox_k_norm", "gmlp_w_s", "gmlp_b_s")
WEIGHTS = ("norm_ffn1", "ffn1_w_in", "ffn1_w_out", "norm_mix", "norm_ffn2", "ffn2_w_in", "ffn2_w_out", "w_out",
           "mem_norm", "mem_w_kv", "mem_q_norm", "mem_k_norm", "fox_w_in", "fox_b_f", "fox_q_norm", "fox_k_norm",
           "gmlp_w_in", "gmlp_v_norm", "gmlp_w_s", "gmlp_b_s")


def _to_transport(name, a):
    if name in COLUMN_SHARDED:
        return jnp.swapaxes(a, -1, -2)
    return _fox_split(a) if name == "fox_w_in" else a


def _from_transport(name, a):
    if name in COLUMN_SHARDED:
        return jnp.swapaxes(a, -1, -2)
    return _fox_unsplit(a) if name == "fox_w_in" else a


def kernel(x, mem, norm_ffn1, ffn1_w_in, ffn1_w_out, norm_mix, norm_ffn2, ffn2_w_in, ffn2_w_out, w_out, mem_norm, mem_w_kv, mem_q_norm, mem_k_norm, fox_w_in, fox_b_f, fox_q_norm, fox_k_norm, gmlp_w_in, gmlp_v_norm, gmlp_w_s, gmlp_b_s, loss_target, m_norm_ffn1, m_ffn1_w_in, m_ffn1_w_out, m_norm_mix, m_norm_ffn2, m_ffn2_w_in, m_ffn2_w_out, m_w_out, m_mem_norm, m_mem_w_kv, m_mem_q_norm, m_mem_k_norm, m_fox_w_in, m_fox_b_f, m_fox_q_norm, m_fox_k_norm, m_gmlp_w_in, m_gmlp_v_norm, m_gmlp_w_s, m_gmlp_b_s, v_norm_ffn1, v_ffn1_w_in, v_ffn1_w_out, v_norm_mix, v_norm_ffn2, v_ffn2_w_in, v_ffn2_w_out, v_w_out, v_mem_norm, v_mem_w_kv, v_mem_q_norm, v_mem_k_norm, v_fox_w_in, v_fox_b_f, v_fox_q_norm, v_fox_k_norm, v_gmlp_w_in, v_gmlp_v_norm, v_gmlp_w_s, v_gmlp_b_s):
    w = dict(norm_ffn1=norm_ffn1, ffn1_w_in=ffn1_w_in, ffn1_w_out=ffn1_w_out, norm_mix=norm_mix, norm_ffn2=norm_ffn2, ffn2_w_in=ffn2_w_in, ffn2_w_out=ffn2_w_out, w_out=w_out, mem_norm=mem_norm, mem_w_kv=mem_w_kv, mem_q_norm=mem_q_norm, mem_k_norm=mem_k_norm, fox_w_in=fox_w_in, fox_b_f=fox_b_f, fox_q_norm=fox_q_norm, fox_k_norm=fox_k_norm, gmlp_w_in=gmlp_w_in, gmlp_v_norm=gmlp_v_norm, gmlp_w_s=gmlp_w_s, gmlp_b_s=gmlp_b_s)
    m = dict(norm_ffn1=m_norm_ffn1, ffn1_w_in=m_ffn1_w_in, ffn1_w_out=m_ffn1_w_out, norm_mix=m_norm_mix, norm_ffn2=m_norm_ffn2, ffn2_w_in=m_ffn2_w_in, ffn2_w_out=m_ffn2_w_out, w_out=m_w_out, mem_norm=m_mem_norm, mem_w_kv=m_mem_w_kv, mem_q_norm=m_mem_q_norm, mem_k_norm=m_mem_k_norm, fox_w_in=m_fox_w_in, fox_b_f=m_fox_b_f, fox_q_norm=m_fox_q_norm, fox_k_norm=m_fox_k_norm, gmlp_w_in=m_gmlp_w_in, gmlp_v_norm=m_gmlp_v_norm, gmlp_w_s=m_gmlp_w_s, gmlp_b_s=m_gmlp_b_s)
    v = dict(norm_ffn1=v_norm_ffn1, ffn1_w_in=v_ffn1_w_in, ffn1_w_out=v_ffn1_w_out, norm_mix=v_norm_mix, norm_ffn2=v_norm_ffn2, ffn2_w_in=v_ffn2_w_in, ffn2_w_out=v_ffn2_w_out, w_out=v_w_out, mem_norm=v_mem_norm, mem_w_kv=v_mem_w_kv, mem_q_norm=v_mem_q_norm, mem_k_norm=v_mem_k_norm, fox_w_in=v_fox_w_in, fox_b_f=v_fox_b_f, fox_q_norm=v_fox_q_norm, fox_k_norm=v_fox_k_norm, gmlp_w_in=v_gmlp_w_in, gmlp_v_norm=v_gmlp_v_norm, gmlp_w_s=v_gmlp_w_s, gmlp_b_s=v_gmlp_b_s)

    depth = norm_ffn1.shape[0]
    x0 = x[0]
    mem0 = mem[0]
    target = loss_target[0]
    me = _slot(*_position())

    keys = [(n, i) for n in BIG for i in range(w[n].shape[0])]
    local = {k: _to_transport(k[0], w[k[0]][k[1]]) for k in keys}
    n_gain, gain_len = gmlp_v_norm.shape
    pad_gain = lambda a: jnp.pad(a, ((0, 8 - n_gain), (0, LANES - gain_len)))
    gathered = all_gather_multi([local[k].astype(BF16) for k in keys] + [pad_gain(gmlp_v_norm)], "gather_weights")
    full = {k: g.reshape(-1, g.shape[-1]) for k, g in zip(keys, gathered)}
    v_gain_full = gathered[-1][:, :n_gain, :gain_len]

    mem_n = rmsnorm_fwd(mem0[None], mem_norm.reshape(1, 1, -1), BF16, "mem_norm")[0]
    saved = []
    xi = x0
    for i in range(depth):
        kind, j = i % 2, i // 2
        x1, ffn1_saved = _ffn_forward(xi, norm_ffn1[i], full["ffn1_w_in", i], full["ffn1_w_out", i], f"l{i}_ffn1")
        h = rmsnorm_fwd(x1[None], norm_mix[i].reshape(1, 1, -1), BF16, f"l{i}_mixnorm")[0]
        if kind == 0:
            tok, mq, mix_saved = _fox_forward(h, full["fox_w_in", j], fox_b_f[j], fox_q_norm[j], fox_k_norm[j],
                                              f"l{i}_fox")
        else:
            tok, mq, mix_saved = _gmlp_forward(h, full["gmlp_w_in", j], v_gain_full[:, j, :].reshape(-1), gmlp_w_s[j],
                                               gmlp_b_s[j], f"l{i}_gmlp")
        mo, mem_saved = _mem_forward(mq, mem_n, full["mem_w_kv", i], mem_q_norm[i], mem_k_norm[i], f"l{i}_mem")
        cat = _unheads(jnp.concatenate([tok, mo], axis=0)).astype(BF16)
        x2 = matmul(cat, full["w_out", i], f"l{i}_wout", residual=x1)
        x3, ffn2_saved = _ffn_forward(x2, norm_ffn2[i], full["ffn2_w_in", i], full["ffn2_w_out", i], f"l{i}_ffn2")
        saved.append((ffn1_saved, x1, h, mix_saved, mem_saved, cat, ffn2_saved))
        xi = x3

    dy, loss_part = loss_head(xi, target, "loss_head")
    loss = lax.psum(loss_part, ("x", "y", "c"))

    small = {n: [None] * w[n].shape[0] for n in REPLICATED + ("gmlp_v_norm",) if n != "mem_norm"}
    big = {}
    dmem_n = None
    for i in reversed(range(depth)):
        kind, j = i % 2, i // 2
        ffn1_saved, x1, h, mix_saved, mem_saved, cat, ffn2_saved = saved[i]
        dy, small["norm_ffn2"][i], big["ffn2_w_in", i], big["ffn2_w_out", i] = _ffn_backward(
            dy, ffn2_saved, norm_ffn2[i], full["ffn2_w_in", i], full["ffn2_w_out", i], f"l{i}_ffn2")
        big["w_out", i] = matmul_tn(cat, dy, f"l{i}_dwout", out_dtype=BF16)
        dcat = _heads(matmul(dy, full["w_out", i], f"l{i}_dcat", transpose_b=True), FOX_HEADS + MEM_HEADS)
        dmq, small["mem_q_norm"][i], small["mem_k_norm"][i], big["mem_w_kv", i], dmem_i = _mem_backward(
            dcat[FOX_HEADS:], mem_saved, mem_n, full["mem_w_kv", i], f"l{i}_mem")
        dmem_n = dmem_i if dmem_n is None else dmem_n + dmem_i
        if kind == 0:
            dproj, small["fox_b_f"][j], small["fox_q_norm"][j], small["fox_k_norm"][j] = _fox_backward(
                dcat[:FOX_HEADS], dmq, mix_saved, f"l{i}_fox")
            big["fox_w_in", j] = matmul_tn(h, dproj, f"l{i}_fox_dwin", out_dtype=BF16)
            dh = matmul(dproj, full["fox_w_in", j], f"l{i}_fox_dh", transpose_b=True)
        else:
            dproj, small["gmlp_v_norm"][j], small["gmlp_w_s"][j], small["gmlp_b_s"][j] = _gmlp_backward(
                dcat[:FOX_HEADS], dmq, mix_saved, f"l{i}_gmlp")
            big["gmlp_w_in", j] = matmul_tn(dproj, h, f"l{i}_gmlp_dwin", out_dtype=BF16)
            dh = matmul(dproj, full["gmlp_w_in", j], f"l{i}_gmlp_dh")
        dy, dg_mix = rmsnorm_bwd(x1[None], norm_mix[i].reshape(1, 1, -1), dh[None], f"l{i}_dmixnorm", residual=dy[None])
        dy, small["norm_mix"][i] = dy[0], dg_mix.reshape(-1)
        dy, small["norm_ffn1"][i], big["ffn1_w_in", i], big["ffn1_w_out", i] = _ffn_backward(
            dy, ffn1_saved, norm_ffn1[i], full["ffn1_w_in", i], full["ffn1_w_out", i], f"l{i}_ffn1")
    grad_x = dy[None]
    _, dg_mem = rmsnorm_bwd(mem0[None], mem_norm.reshape(1, 1, -1), dmem_n[None], "dmem_norm")
    small = {n: jnp.stack(g) for n, g in small.items()}
    small["mem_norm"] = dg_mem.reshape(-1)

    got = all_to_all_multi([big[k].reshape((N_DEV, -1, big[k].shape[-1])) for k in keys], "exchange_grads")
    results = {n: [[None] * w[n].shape[0] for _ in range(4)] for n in BIG}
    for k, slots in zip(keys, got):
        n, i = k
        outs = adamw(local[k], slots, _to_transport(n, m[n][i]), _to_transport(n, v[n][i]), f"adamw_{n}_{i}")
        for q in range(4):
            results[n][q][i] = _from_transport(n, outs[q])
    sharded = {n: [jnp.stack(r) for r in results[n]] for n in BIG}

    small_names = REPLICATED + ("gmlp_v_norm",)
    (small_got,) = all_gather_multi([_pack([small[n] for n in small_names])], "gather_small_grads")
    rep_shapes = [w[n].shape for n in REPLICATED]
    gain_seg = jnp.zeros((n_gain, N_DEV * gain_len), F32)
    pack_rep = lambda d: _pack([d[n] for n in REPLICATED] + [gain_seg])
    outs = adamw(pack_rep(w), small_got, pack_rep(m), pack_rep(v), "adamw_replicated")
    replicated = [dict(zip(REPLICATED, _unpack(o, rep_shapes))) for o in outs]
    gain_parts = _unpack(small_got, rep_shapes + [(n_gain, N_DEV * gain_len)], lead=(N_DEV,))[-1]
    gain_slots = lax.dynamic_slice_in_dim(gain_parts, me * gain_len, gain_len, axis=2)
    gain_slots = jnp.pad(gain_slots, ((0, 0), (0, 8 - n_gain), (0, LANES - gain_len)))
    outs = adamw(pad_gain(gmlp_v_norm), gain_slots, pad_gain(m["gmlp_v_norm"]), pad_gain(v["gmlp_v_norm"]),
                 "adamw_gmlp_v_norm")
    sharded["gmlp_v_norm"] = [o[:n_gain, :gain_len] for o in outs]

    out = [loss, grad_x]
    for q in range(4):
        out += [(replicated[q][n] if n in REPLICATED else sharded[n][q]) for n in WEIGHTS]
    return tuple(out)
```

```python
import functools
import math

import jax
import jax.numpy as jnp
from jax import lax
from jax.experimental import pallas as pl
from jax.experimental.pallas import tpu as pltpu

F32 = jnp.float32
BF16 = jnp.bfloat16

EPS = 1e-6
HEAD_DIM = 64
FOX_HEADS = 12
MEM_HEADS = 4
TOK_WIDTH = FOX_HEADS * HEAD_DIM
MEM_WIDTH = MEM_HEADS * HEAD_DIM
CHUNK = 128
LANES = 128
N_DEV = 8
SEG_ALIGN = 16 * LANES
PACK_ROWS = 1024
PACK_ALIGN = PACK_ROWS * LANES

ADAM_LR = 0.001
ADAM_B1 = 0.9
ADAM_B2 = 0.999
ADAM_EPS = 1e-08
ADAM_WD = 0.01
ADAM_STEP = 10

VMEM_LIMIT_BYTES = 48 * 1024 * 1024
MESH = pl.DeviceIdType.MESH
CONTRACT_0 = (((0,), (0,)), ((), ()))
CONTRACT_1 = (((1,), (1,)), ((), ()))


def _params(*semantics):
    return pltpu.CompilerParams(dimension_semantics=semantics, vmem_limit_bytes=VMEM_LIMIT_BYTES)


def _pick(n, candidates):
    for c in candidates:
        if c <= n and n % c == 0:
            return c
    return n


def _sigmoid(x):
    return 1.0 / (1.0 + jnp.exp(-x))


def _row_tile(r, w):
    return _pick(r, (1024,) if w >= 512 else (2048, 1024, 512, 256))


def rmsnorm_fwd(x, gain, out_dtype, name):
    g_, r_, w_ = x.shape
    tr = _row_tile(r_, w_)

    def body(x_ref, g_ref, y_ref):
        xv = x_ref[0].astype(F32)
        r = lax.rsqrt(jnp.mean(xv * xv, axis=-1, keepdims=True) + EPS)
        y_ref[0] = (xv * r * g_ref[0]).astype(y_ref.dtype)

    return pl.pallas_call(
        body, name=name, grid=(g_, r_ // tr),
        in_specs=[pl.BlockSpec((1, tr, w_), lambda g, i: (g, i, 0)),
                  pl.BlockSpec((1, 1, w_), lambda g, i: (g, 0, 0))],
        out_specs=pl.BlockSpec((1, tr, w_), lambda g, i: (g, i, 0)),
        out_shape=jax.ShapeDtypeStruct((g_, r_, w_), out_dtype),
        compiler_params=_params("parallel", "parallel"),
    )(x, gain)


def rmsnorm_bwd(x, gain, dy, name, residual=None):
    g_, r_, w_ = x.shape
    tr = _row_tile(r_, w_)
    has_res = residual is not None

    def body(*refs):
        if has_res:
            x_ref, g_ref, dy_ref, res_ref, dx_ref, dg_ref = refs
        else:
            x_ref, g_ref, dy_ref, dx_ref, dg_ref = refs
        xv = x_ref[0].astype(F32)
        dyv = dy_ref[0].astype(F32)
        r = lax.rsqrt(jnp.mean(xv * xv, axis=-1, keepdims=True) + EPS)
        n = xv * r
        dn = dyv * g_ref[0]
        dx = r * (dn - n * jnp.mean(dn * n, axis=-1, keepdims=True))
        if has_res:
            dx = dx + res_ref[0]
        dx_ref[0] = dx
        part = jnp.sum(dyv * n, axis=0, keepdims=True)

        @pl.when(pl.program_id(1) == 0)
        def _():
            dg_ref[0] = part

        @pl.when(pl.program_id(1) != 0)
        def _():
            dg_ref[0] += part

    row = pl.BlockSpec((1, tr, w_), lambda g, i: (g, i, 0))
    vec = pl.BlockSpec((1, 1, w_), lambda g, i: (g, 0, 0))
    operands = (x, gain, dy) + ((residual,) if has_res else ())
    return pl.pallas_call(
        body, name=name, grid=(g_, r_ // tr),
        in_specs=[row, vec, row] + ([row] if has_res else []),
        out_specs=(row, vec),
        out_shape=(jax.ShapeDtypeStruct((g_, r_, w_), F32), jax.ShapeDtypeStruct((g_, 1, w_), F32)),
        compiler_params=_params("parallel", "arbitrary"),
    )(*operands)


def matmul(a, b, name, out_dtype=F32, residual=None, scale=None, transpose_b=False):
    m_, k_ = a.shape
    n_ = b.shape[0] if transpose_b else b.shape[1]
    tm = _pick(m_, (512, 256, 128))
    tn = _pick(n_, (1408, 1024, 896, 512, 256, 128))
    has_res = residual is not None

    def body(*refs):
        if has_res:
            a_ref, b_ref, res_ref, o_ref = refs
        else:
            a_ref, b_ref, o_ref = refs
        av, bv = a_ref[...].astype(BF16), b_ref[...].astype(BF16)
        if transpose_b:
            acc = lax.dot_general(av, bv, CONTRACT_1, preferred_element_type=F32)
        else:
            acc = jnp.dot(av, bv, preferred_element_type=F32)
        if scale is not None:
            acc = acc * scale
        if has_res:
            acc = acc + res_ref[...]
        o_ref[...] = acc.astype(o_ref.dtype)

    out_spec = pl.BlockSpec((tm, tn), lambda j, i: (i, j))
    b_spec = pl.BlockSpec((tn, k_), lambda j, i: (j, 0)) if transpose_b else pl.BlockSpec((k_, tn), lambda j, i: (0, j))
    operands = (a, b) + ((residual,) if has_res else ())
    return pl.pallas_call(
        body, name=name, grid=(n_ // tn, m_ // tm),
        in_specs=[pl.BlockSpec((tm, k_), lambda j, i: (i, 0)), b_spec] + ([out_spec] if has_res else []),
        out_specs=out_spec,
        out_shape=jax.ShapeDtypeStruct((m_, n_), out_dtype),
        compiler_params=_params("parallel", "parallel"),
    )(*operands)


def matmul_tn(a, b, name, scale=None, out_dtype=F32):
    s_, k_ = a.shape
    n_ = b.shape[1]
    tk = _pick(k_, (1024, 1408, 896, 512, 256, 128))
    tn = _pick(n_, (1408, 1024, 896, 512, 256, 128))
    ts = _pick(s_, (512, 256, 128))
    ns = s_ // ts

    def body(a_ref, b_ref, o_ref, acc_ref):
        part = lax.dot_general(a_ref[...].astype(BF16), b_ref[...].astype(BF16), CONTRACT_0,
                               preferred_element_type=F32)
        step = pl.program_id(2)

        @pl.when(step == 0)
        def _():
            acc_ref[...] = part

        @pl.when(step != 0)
        def _():
            acc_ref[...] += part

        @pl.when(step == ns - 1)
        def _():
            acc = acc_ref[...]
            o_ref[...] = (acc if scale is None else acc * scale).astype(o_ref.dtype)

    return pl.pallas_call(
        body, name=name, grid=(k_ // tk, n_ // tn, ns),
        in_specs=[pl.BlockSpec((ts, tk), lambda i, j, s: (s, i)),
                  pl.BlockSpec((ts, tn), lambda i, j, s: (s, j))],
        out_specs=pl.BlockSpec((tk, tn), lambda i, j, s: (i, j)),
        out_shape=jax.ShapeDtypeStruct((k_, n_), out_dtype),
        scratch_shapes=[pltpu.VMEM((tk, tn), F32)],
        compiler_params=_params("parallel", "parallel", "arbitrary"),
    )(a, b)


def ffn_in(h, w_in_t, name):
    s_, d_ = h.shape
    f_ = w_in_t.shape[0] // 2
    tm = _pick(s_, (512, 256, 128))
    tn = _pick(f_, (1408, 1024, 512, 256, 128))
    nb = f_ // tn

    def body(h_ref, wa_ref, wb_ref, a_ref, b_ref, u_ref):
        hv = h_ref[...]
        a = lax.dot_general(hv, wa_ref[...], CONTRACT_1, preferred_element_type=F32)
        b = lax.dot_general(hv, wb_ref[...], CONTRACT_1, preferred_element_type=F32)
        a_ref[...] = a.astype(BF16)
        b_ref[...] = b.astype(BF16)
        u_ref[...] = (a * _sigmoid(a) * b).astype(BF16)

    o_spec = pl.BlockSpec((tm, tn), lambda j, i: (i, j))
    out = jax.ShapeDtypeStruct((s_, f_), BF16)
    return pl.pallas_call(
        body, name=name, grid=(nb, s_ // tm),
        in_specs=[pl.BlockSpec((tm, d_), lambda j, i: (i, 0)),
                  pl.BlockSpec((tn, d_), lambda j, i: (j, 0)),
                  pl.BlockSpec((tn, d_), lambda j, i: (j + nb, 0))],
        out_specs=(o_spec, o_spec, o_spec), out_shape=(out, out, out),
        compiler_params=_params("parallel", "parallel"),
    )(h, w_in_t, w_in_t)


def ffn_bwd_act(dy, w_out, a, b, name):
    s_, d_ = dy.shape
    f_ = w_out.shape[0]
    tm = _pick(s_, (512, 256, 128))
    tn = _pick(f_, (1408, 1024, 512, 256, 128))

    def body(dy_ref, w_ref, a_ref, b_ref, da_ref, db_ref):
        du = 0.5 * lax.dot_general(dy_ref[...].astype(BF16), w_ref[...], CONTRACT_1, preferred_element_type=F32)
        av = a_ref[...].astype(F32)
        bv = b_ref[...].astype(F32)
        sig = _sigmoid(av)
        da_ref[...] = (du * bv * (sig * (1.0 + av * (1.0 - sig)))).astype(BF16)
        db_ref[...] = (du * (av * sig)).astype(BF16)

    t_spec = pl.BlockSpec((tm, tn), lambda j, i: (i, j))
    out = jax.ShapeDtypeStruct((s_, f_), BF16)
    return pl.pallas_call(
        body, name=name, grid=(f_ // tn, s_ // tm),
        in_specs=[pl.BlockSpec((tm, d_), lambda j, i: (i, 0)),
                  pl.BlockSpec((tn, d_), lambda j, i: (j, 0)), t_spec, t_spec],
        out_specs=(t_spec, t_spec), out_shape=(out, out),
        compiler_params=_params("parallel", "parallel"),
    )(dy, w_out, a, b)


ROW_BLOCK = 128
QK_SCALE = 1.0 / math.sqrt(HEAD_DIM)
SUM_LANE = HEAD_DIM
CQ_LANE, CK_LANE = HEAD_DIM, HEAD_DIM + 3


def _split3(c):
    hi = lax.reduce_precision(c, 8, 7)
    mid = lax.reduce_precision(c - hi, 8, 7)
    lo = lax.reduce_precision(c - hi - mid, 8, 7)
    return [hi.astype(BF16), mid.astype(BF16), lo.astype(BF16)]


def _augment(x, extra):
    cols = jnp.stack(extra, axis=-1) if extra else jnp.zeros(x.shape[:2] + (0,), BF16)
    pad = jnp.zeros(x.shape[:2] + (LANES - HEAD_DIM - cols.shape[-1],), BF16)
    return jnp.concatenate([x, cols, pad], axis=-1)


def _masked_scores(q, k, masked, row0):
    s = lax.dot_general(q, k, CONTRACT_1, preferred_element_type=F32)
    if masked:
        row = lax.broadcasted_iota(jnp.int32, s.shape, 0) + row0
        col = lax.broadcasted_iota(jnp.int32, s.shape, 1)
        s = jnp.where(col <= row, s, -jnp.inf)
    return s


def _lane(x, lane):
    idx = lax.broadcasted_iota(jnp.int32, x.shape, 1)
    return jnp.sum(jnp.where(idx == lane, x, 0.0), axis=1, keepdims=True)


def attn_fwd(q, k, v, causal, name):
    h_, sq, d_ = q.shape
    sk = k.shape[1]
    tq = _pick(sq, (512, 256, 128))
    tk = tq if causal else _pick(sk, (512, 256, 128))
    nk = sk // tk

    def body(q_ref, k_ref, v_ref, o_ref, lse_ref, m_sc, acc_sc):
        i, j = pl.program_id(1), pl.program_id(2)

        @pl.when(j == 0)
        def _():
            m_sc[...] = jnp.full(m_sc.shape, -jnp.inf, F32)
            acc_sc[...] = jnp.zeros(acc_sc.shape, F32)

        def step(masked):
            for r in range(tq // ROW_BLOCK):
                rows = pl.ds(r * ROW_BLOCK, ROW_BLOCK)
                nkeys = (r + 1) * ROW_BLOCK if masked else tk
                s = _masked_scores(q_ref[0, rows, :], k_ref[0, :nkeys, :], masked, r * ROW_BLOCK)
                m_prev = m_sc[rows, :]
                m_new = jnp.maximum(m_prev, jnp.max(s, axis=1, keepdims=True))
                p = jnp.exp(s - jnp.tile(m_new, (1, nkeys // LANES)))
                acc_sc[rows, :] = jnp.exp(m_prev - m_new) * acc_sc[rows, :] + jnp.dot(
                    p.astype(BF16), v_ref[0, :nkeys, :], preferred_element_type=F32)
                m_sc[rows, :] = m_new

        if causal:
            pl.when(j < i)(functools.partial(step, False))
            pl.when(j == i)(functools.partial(step, True))
        else:
            step(False)

        @pl.when(j == (i if causal else nk - 1))
        def _():
            acc = acc_sc[...]
            l = _lane(acc, SUM_LANE)
            o_ref[0] = acc / l
            lse_ref[0] = m_sc[...] + jnp.log(l)

    kj = (lambda j, i: jnp.minimum(j, i)) if causal else (lambda j, i: j)
    q_spec = pl.BlockSpec((1, tq, d_), lambda h, i, j: (h, i, 0))
    k_spec = pl.BlockSpec((1, tk, d_), lambda h, i, j: (h, kj(j, i), 0))
    out = jax.ShapeDtypeStruct((h_, sq, d_), F32)
    return pl.pallas_call(
        body, name=name, grid=(h_, sq // tq, nk),
        in_specs=[q_spec, k_spec, k_spec], out_specs=(q_spec, q_spec), out_shape=(out, out),
        scratch_shapes=[pltpu.VMEM((tq, d_), F32), pltpu.VMEM((tq, d_), F32)],
        compiler_params=_params("parallel", "parallel", "arbitrary"),
    )(q, k, v)


def attn_bwd(q, k, v, o, lse, do, causal, name):
    h_, sq, d_ = q.shape
    sk = k.shape[1]
    tq = _pick(sq, (512, 256, 128))
    tk = tq if causal else _pick(sk, (512, 256, 128))
    nq = sq // tq

    def body(q_ref, k_ref, v_ref, o_ref, lse_ref, do_ref, dq_ref, dk_ref, dv_ref, dk_sc, dv_sc):
        j, i = pl.program_id(1), pl.program_id(2)
        first_i = j if causal else 0

        @pl.when((j == 0) & (i == 0))
        def _():
            dq_ref[...] = jnp.zeros(dq_ref.shape, F32)

        @pl.when(i == first_i)
        def _():
            dk_sc[...] = jnp.zeros(dk_sc.shape, F32)
            dv_sc[...] = jnp.zeros(dv_sc.shape, F32)

        def step(masked):
            base = pl.multiple_of(i * tq, tq)
            for r in range(tq // ROW_BLOCK):
                rows = pl.ds(r * ROW_BLOCK, ROW_BLOCK)
                nkeys = (r + 1) * ROW_BLOCK if masked else tk
                qb, kb, vb = q_ref[0, rows, :], k_ref[0, :nkeys, :], v_ref[0, :nkeys, :]
                dof = do_ref[0, rows, :]
                dob = dof.astype(BF16)
                s = _masked_scores(qb, kb, masked, r * ROW_BLOCK)
                p = jnp.exp(s - jnp.tile(lse_ref[0, rows, :], (1, nkeys // LANES)))
                dp = lax.dot_general(dob, vb, CONTRACT_1, preferred_element_type=F32)
                delta = jnp.sum(dof * o_ref[0, rows, :], axis=1, keepdims=True)
                dsb = (p * (dp - delta)).astype(BF16)
                dv_sc[:nkeys, :] += lax.dot_general(p.astype(BF16), dob, CONTRACT_0, preferred_element_type=F32)
                dk_sc[:nkeys, :] += lax.dot_general(dsb, qb, CONTRACT_0, preferred_element_type=F32)
                dq_ref[0, pl.ds(base + r * ROW_BLOCK, ROW_BLOCK), :] += jnp.dot(dsb, kb, preferred_element_type=F32)

        if causal:
            pl.when(i > j)(functools.partial(step, False))
            pl.when(i == j)(functools.partial(step, True))
        else:
            step(False)

        @pl.when(i == nq - 1)
        def _():
            dk_ref[0] = dk_sc[...]
            dv_ref[0] = dv_sc[...]

    qi = (lambda j, i: jnp.maximum(i, j)) if causal else (lambda j, i: i)
    q_spec = pl.BlockSpec((1, tq, d_), lambda h, j, i: (h, qi(j, i), 0))
    k_spec = pl.BlockSpec((1, tk, d_), lambda h, j, i: (h, j, 0))
    return pl.pallas_call(
        body, name=name, grid=(h_, sk // tk, nq),
        in_specs=[q_spec, k_spec, k_spec, q_spec, q_spec, q_spec],
        out_specs=(pl.BlockSpec((1, sq, d_), lambda h, j, i: (h, 0, 0)), k_spec, k_spec),
        out_shape=(jax.ShapeDtypeStruct((h_, sq, d_), F32), jax.ShapeDtypeStruct((h_, sk, d_), F32),
                   jax.ShapeDtypeStruct((h_, sk, d_), F32)),
        scratch_shapes=[pltpu.VMEM((tk, d_), F32), pltpu.VMEM((tk, d_), F32)],
        compiler_params=_params("parallel", "arbitrary", "arbitrary"),
    )(q, k, v, o, lse, do)


def _tri(lower):
    row = lax.broadcasted_iota(jnp.int32, (CHUNK, CHUNK), 0)
    col = lax.broadcasted_iota(jnp.int32, (CHUNK, CHUNK), 1)
    return jnp.where((col <= row) if lower else (col >= row), 1.0, 0.0).astype(F32)


def fox_gate_fwd(f, b, name):
    s_ = f.shape[0]

    def body(f_ref, b_ref, c_ref, carry):
        @pl.when(pl.program_id(0) == 0)
        def _():
            carry[...] = jnp.zeros(carry.shape, F32)

        xv = f_ref[...] + b_ref[...]
        log_f = jnp.minimum(xv, 0.0) - jnp.log(1.0 + jnp.exp(-jnp.abs(xv)))
        c = jnp.dot(_tri(True), log_f, precision=lax.Precision.HIGHEST, preferred_element_type=F32) + carry[...]
        c_ref[...] = c
        carry[...] = c[CHUNK - 1:CHUNK, :]

    blk = pl.BlockSpec((CHUNK, LANES), lambda i: (i, 0))
    return pl.pallas_call(
        body, name=name, grid=(s_ // CHUNK,),
        in_specs=[blk, pl.BlockSpec((1, LANES), lambda i: (0, 0))], out_specs=blk,
        out_shape=jax.ShapeDtypeStruct((s_, LANES), F32),
        scratch_shapes=[pltpu.VMEM((1, LANES), F32)],
        compiler_params=_params("arbitrary"),
    )(f, b)


def fox_gate_bwd(dc, f, b, name):
    s_ = f.shape[0]
    n = s_ // CHUNK

    def body(dc_ref, f_ref, b_ref, df_ref, db_ref, carry):
        @pl.when(pl.program_id(0) == 0)
        def _():
            carry[...] = jnp.zeros(carry.shape, F32)
            db_ref[...] = jnp.zeros(db_ref.shape, F32)

        dlog = jnp.dot(_tri(False), dc_ref[...], precision=lax.Precision.HIGHEST,
                       preferred_element_type=F32) + carry[...]
        df = dlog * _sigmoid(-(f_ref[...] + b_ref[...]))
        df_ref[...] = df
        db_ref[...] += jnp.sum(df, axis=0, keepdims=True)
        carry[...] = dlog[0:1, :]

    blk = pl.BlockSpec((CHUNK, LANES), lambda i: (n - 1 - i, 0))
    vec = pl.BlockSpec((1, LANES), lambda i: (0, 0))
    return pl.pallas_call(
        body, name=name, grid=(n,),
        in_specs=[blk, blk, vec], out_specs=(blk, vec),
        out_shape=(jax.ShapeDtypeStruct((s_, LANES), F32), jax.ShapeDtypeStruct((1, LANES), F32)),
        scratch_shapes=[pltpu.VMEM((1, LANES), F32)],
        compiler_params=_params("arbitrary"),
    )(dc, f, b)


GELU_K = math.sqrt(2.0 / math.pi)
GELU_C = 0.044715


def _gelu(x):
    return 0.5 * x * (1.0 + jnp.tanh(GELU_K * (x + GELU_C * (x * x * x))))


def _gelu_grad(x):
    t = jnp.tanh(GELU_K * (x + GELU_C * (x * x * x)))
    return 0.5 * (1.0 + t) + 0.5 * x * (1.0 - t * t) * (GELU_K * (1.0 + 3.0 * GELU_C * (x * x)))


def _tril_mask():
    row = lax.broadcasted_iota(jnp.int32, (CHUNK, CHUNK), 0)
    col = lax.broadcasted_iota(jnp.int32, (CHUNK, CHUNK), 1)
    return col <= row


def _gmlp_specs(s_, ts):
    row = pl.BlockSpec((1, ts, HEAD_DIM), lambda g, i: (g, i, 0))
    gain = pl.BlockSpec((1, 1, HEAD_DIM), lambda g, i: (g, 0, 0))
    w = pl.BlockSpec((1, CHUNK, CHUNK), lambda g, i: (g, 0, 0))
    b = pl.BlockSpec((1, CHUNK, 1), lambda g, i: (g, 0, 0))
    return row, gain, w, b


def gmlp_fwd(pu, pv, gain, w, b, name):
    g_, s_, d_ = pu.shape
    ts = _pick(s_, (1024, 512, 256, 128))

    def body(pu_ref, pv_ref, g_ref, w_ref, b_ref, o_ref):
        v = _gelu(pv_ref[0])
        r = lax.rsqrt(jnp.mean(v * v, axis=-1, keepdims=True) + EPS)
        vn = (v * r * g_ref[0]).astype(BF16)
        wt = jnp.where(_tril_mask(), w_ref[0], 0.0).astype(BF16)
        for c in range(ts // CHUNK):
            rows = pl.ds(c * CHUNK, CHUNK)
            gate = jnp.dot(wt, vn[c * CHUNK:(c + 1) * CHUNK], preferred_element_type=F32) + b_ref[0]
            o_ref[0, rows, :] = _gelu(pu_ref[0, rows, :]) * gate

    row, gspec, wspec, bspec = _gmlp_specs(s_, ts)
    return pl.pallas_call(
        body, name=name, grid=(g_, s_ // ts),
        in_specs=[row, row, gspec, wspec, bspec], out_specs=row,
        out_shape=jax.ShapeDtypeStruct((g_, s_, d_), F32),
        compiler_params=_params("parallel", "parallel"),
    )(pu, pv, gain, w, b)


def gmlp_bwd(pu, pv, gain, w, b, dout, name):
    g_, s_, d_ = pu.shape
    ts = _pick(s_, (1024, 512, 256, 128))

    def body(pu_ref, pv_ref, g_ref, w_ref, b_ref, do_ref, dpu_ref, dpv_ref, dw_ref, db_ref, dg_ref):
        @pl.when(pl.program_id(1) == 0)
        def _():
            dw_ref[...] = jnp.zeros(dw_ref.shape, F32)
            db_ref[...] = jnp.zeros(db_ref.shape, F32)
            dg_ref[...] = jnp.zeros(dg_ref.shape, F32)

        gain_v = g_ref[0]
        mask = _tril_mask()
        wt = jnp.where(mask, w_ref[0], 0.0).astype(BF16)
        dw = jnp.zeros((CHUNK, CHUNK), F32)
        db = jnp.zeros((CHUNK, 1), F32)
        dg = jnp.zeros((1, d_), F32)
        for c in range(ts // CHUNK):
            rows = pl.ds(c * CHUNK, CHUNK)
            pu_c = pu_ref[0, rows, :]
            pv_c = pv_ref[0, rows, :]
            do_c = do_ref[0, rows, :]
            u = _gelu(pu_c)
            v = _gelu(pv_c)
            r = lax.rsqrt(jnp.mean(v * v, axis=-1, keepdims=True) + EPS)
            n = v * r
            vn = (n * gain_v).astype(BF16)
            gate = jnp.dot(wt, vn, preferred_element_type=F32) + b_ref[0]
            dgate = do_c * u
            dgate_b = dgate.astype(BF16)
            dpu_ref[0, rows, :] = do_c * gate * _gelu_grad(pu_c)
            db = db + jnp.sum(dgate, axis=1, keepdims=True)
            dw = dw + lax.dot_general(dgate_b, vn, CONTRACT_1, preferred_element_type=F32)
            dvn = lax.dot_general(wt, dgate_b, CONTRACT_0, preferred_element_type=F32)
            dg = dg + jnp.sum(dvn * n, axis=0, keepdims=True)
            dn = dvn * gain_v
            dv = r * (dn - n * jnp.mean(dn * n, axis=-1, keepdims=True))
            dpv_ref[0, rows, :] = dv * _gelu_grad(pv_c)
        dw_ref[0] += jnp.where(mask, dw, 0.0)
        db_ref[0] += db
        dg_ref[0] += dg

    row, gspec, wspec, bspec = _gmlp_specs(s_, ts)
    return pl.pallas_call(
        body, name=name, grid=(g_, s_ // ts),
        in_specs=[row, row, gspec, wspec, bspec, row],
        out_specs=(row, row, wspec, bspec, gspec),
        out_shape=(jax.ShapeDtypeStruct((g_, s_, d_), F32), jax.ShapeDtypeStruct((g_, s_, d_), F32),
                   jax.ShapeDtypeStruct((g_, CHUNK, CHUNK), F32), jax.ShapeDtypeStruct((g_, CHUNK, 1), F32),
                   jax.ShapeDtypeStruct((g_, 1, d_), F32)),
        compiler_params=_params("parallel", "arbitrary"),
    )(pu, pv, gain, w, b, dout)


def loss_head(y, t, name):
    s_, d_ = y.shape
    tr = _pick(s_, (1024, 512, 256, 128))

    def body(y_ref, t_ref, dy_ref, l_ref):
        err = y_ref[...] - t_ref[...]
        dy_ref[...] = err * (1.0 / d_)
        part = jnp.full(l_ref.shape, jnp.sum(err * err) * (0.5 / d_), F32)

        @pl.when(pl.program_id(0) == 0)
        def _():
            l_ref[...] = part

        @pl.when(pl.program_id(0) != 0)
        def _():
            l_ref[...] += part

    blk = pl.BlockSpec((tr, d_), lambda i: (i, 0))
    dy, l = pl.pallas_call(
        body, name=name, grid=(s_ // tr,),
        in_specs=[blk, blk], out_specs=(blk, pl.BlockSpec((8, LANES), lambda i: (0, 0))),
        out_shape=(jax.ShapeDtypeStruct((s_, d_), F32), jax.ShapeDtypeStruct((8, LANES), F32)),
        compiler_params=_params("arbitrary"),
    )(y, t)
    return dy, l[0, 0]


def adamw(w, g_slots, m, v, name):
    r_, c_ = w.shape
    tr = _pick(r_, (1024, 512, 352, 256, 224, 128, 64, 32, 16, 8))

    def body(w_ref, gs_ref, m_ref, v_ref, g_ref, d_ref, nm_ref, nv_ref):
        g = gs_ref[0].astype(F32)
        for k in range(1, N_DEV):
            g = g + gs_ref[k].astype(F32)
        m_new = ADAM_B1 * m_ref[...] + (1.0 - ADAM_B1) * g
        v_new = ADAM_B2 * v_ref[...] + (1.0 - ADAM_B2) * (g * g)
        m_hat = m_new / (1.0 - ADAM_B1 ** ADAM_STEP)
        v_hat = v_new / (1.0 - ADAM_B2 ** ADAM_STEP)
        g_ref[...] = g
        d_ref[...] = -ADAM_LR * (m_hat / (jnp.sqrt(v_hat) + ADAM_EPS) + ADAM_WD * w_ref[...])
        nm_ref[...] = m_new
        nv_ref[...] = v_new

    blk = pl.BlockSpec((tr, c_), lambda i: (i, 0))
    out = jax.ShapeDtypeStruct((r_, c_), F32)
    return pl.pallas_call(
        body, name=name, grid=(r_ // tr,),
        in_specs=[blk, pl.BlockSpec((N_DEV, tr, c_), lambda i: (0, i, 0)), blk, blk],
        out_specs=(blk, blk, blk, blk), out_shape=(out, out, out, out),
        compiler_params=_params("parallel"),
    )(w, g_slots, m, v)


def _position():
    x, y, c = lax.axis_index("x"), lax.axis_index("y"), lax.axis_index("c")
    return x, y, c


def _slot(px, py, pc):
    return 4 * px + 2 * py + pc


def all_gather_multi(blocks, name):
    n = len(blocks)

    def body(*refs):
        x_refs, out_refs = refs[:n], refs[n:2 * n]
        send_sems, recv_sems, local_sems = refs[2 * n:]
        x, y, c = _position()
        me, sibling = (x, y, c), (x, y, 1 - c)
        chips = [(1 - x, y), (x, 1 - y), (1 - x, 1 - y)]

        def copy(b, k, owner, to, src=None):
            dst = out_refs[b].at[_slot(*owner)]
            return pltpu.make_async_remote_copy(
                src_ref=dst if src is None else src, dst_ref=dst,
                send_sem=send_sems.at[b, k], recv_sem=recv_sems.at[b, k], device_id=to, device_id_type=MESH)

        mine = [pltpu.make_async_copy(x_refs[b], out_refs[b].at[_slot(*me)], local_sems.at[b]) for b in range(n)]
        for cp in mine:
            cp.start()
        first = [copy(b, 1 + j, me, (*chip, c), src=x_refs[b]) for j, chip in enumerate(chips) for b in range(n)]
        first += [copy(b, 0, me, sibling, src=x_refs[b]) for b in range(n)]
        for cp in first:
            cp.start()
        passed = []
        for j, chip in enumerate(chips):
            for b in range(n):
                copy(b, 1 + j, (*chip, c), me).wait_recv()
                cp = copy(b, 4 + j, (*chip, c), sibling)
                cp.start()
                passed.append(cp)
        for b in range(n):
            copy(b, 0, sibling, me).wait_recv()
        for j, chip in enumerate(chips):
            for b in range(n):
                copy(b, 4 + j, (*chip, 1 - c), me).wait_recv()
        for cp in first + passed:
            cp.wait_send()
        for cp in mine:
            cp.wait()

    any_spec = pl.BlockSpec(memory_space=pl.ANY)
    return pl.pallas_call(
        body, name=name,
        in_specs=[any_spec] * n, out_specs=tuple([any_spec] * n),
        out_shape=tuple(jax.ShapeDtypeStruct((N_DEV,) + blk.shape, blk.dtype) for blk in blocks),
        scratch_shapes=[pltpu.SemaphoreType.DMA((n, 7)), pltpu.SemaphoreType.DMA((n, 7)), pltpu.SemaphoreType.DMA((n,))],
    )(*blocks)


def all_to_all_multi(slot_bufs, name):
    n = len(slot_bufs)

    def body(*refs):
        in_refs, out_refs = refs[:n], refs[n:2 * n]
        send_sems, recv_sems, local_sems = refs[2 * n:]
        x, y, c = _position()
        me = _slot(x, y, c)
        mine = [pltpu.make_async_copy(in_refs[b].at[me], out_refs[b].at[me], local_sems.at[b]) for b in range(n)]
        for cp in mine:
            cp.start()
        copies = []
        for k in range(1, N_DEV):
            px = 1 - x if k & 4 else x
            py = 1 - y if k & 2 else y
            pc = 1 - c if k & 1 else c
            for b in range(n):
                cp = pltpu.make_async_remote_copy(
                    src_ref=in_refs[b].at[_slot(px, py, pc)], dst_ref=out_refs[b].at[me],
                    send_sem=send_sems.at[b, k - 1], recv_sem=recv_sems.at[b, k - 1],
                    device_id=(px, py, pc), device_id_type=MESH)
                cp.start()
                copies.append(cp)
        for cp in copies:
            cp.wait()
        for cp in mine:
            cp.wait()

    any_spec = pl.BlockSpec(memory_space=pl.ANY)
    return pl.pallas_call(
        body, name=name,
        in_specs=[any_spec] * n, out_specs=tuple([any_spec] * n),
        out_shape=tuple(jax.ShapeDtypeStruct(buf.shape, buf.dtype) for buf in slot_bufs),
        scratch_shapes=[pltpu.SemaphoreType.DMA((n, 7)), pltpu.SemaphoreType.DMA((n, 7)), pltpu.SemaphoreType.DMA((n,))],
    )(*slot_bufs)


def _seg_len(n):
    return -(-n // SEG_ALIGN) * SEG_ALIGN


def _pack(arrays, lead=()):
    parts, total = [], 0
    for a in arrays:
        flat = a.reshape(lead + (-1,))
        pad = _seg_len(flat.shape[-1]) - flat.shape[-1]
        parts.append(jnp.pad(flat, [(0, 0)] * len(lead) + [(0, pad)]) if pad else flat)
        total += flat.shape[-1] + pad
    tail = -(-total // PACK_ALIGN) * PACK_ALIGN - total
    if tail:
        parts.append(jnp.zeros(lead + (tail,), parts[0].dtype))
    return jnp.concatenate(parts, axis=-1).reshape(lead + (-1, LANES))


def _unpack(packed, shapes, lead=()):
    flat = packed.reshape(lead + (-1,))
    out, off = [], 0
    for shp in shapes:
        n = math.prod(shp)
        out.append(flat[..., off:off + n].reshape(lead + tuple(shp)))
        off += _seg_len(n)
    return out


def _heads(a, n_heads):
    return a.reshape(a.shape[0], n_heads, HEAD_DIM).transpose(1, 0, 2)


def _unheads(a):
    return a.transpose(1, 0, 2).reshape(a.shape[1], a.shape[0] * HEAD_DIM)


def _head_gain(g, n_heads):
    return jnp.broadcast_to(g.reshape(1, 1, HEAD_DIM), (n_heads, 1, HEAD_DIM))


def _pad_lanes(a):
    return jnp.pad(a, ((0, 0), (0, 0), (0, LANES - HEAD_DIM)))


def _ffn_forward(x, gain, w_in_t, w_out, tag):
    h = rmsnorm_fwd(x[None], gain.reshape(1, 1, -1), BF16, f"{tag}_norm")[0]
    a, b, u = ffn_in(h, w_in_t, f"{tag}_in")
    y = matmul(u, w_out, f"{tag}_out", residual=x, scale=0.5)
    return y, (x, h, a, b, u)


def _ffn_backward(dy, saved, gain, w_in_t, w_out, tag):
    x, h, a, b, u = saved
    f_ = w_out.shape[0]
    dw_out = matmul_tn(u, dy, f"{tag}_dwout", scale=0.5, out_dtype=BF16)
    da, db = ffn_bwd_act(dy, w_out, a, b, f"{tag}_dact")
    dh = matmul(da, w_in_t[:f_], f"{tag}_dh_a")
    dh = matmul(db, w_in_t[f_:], f"{tag}_dh_b", residual=dh)
    dw_in_t = jnp.concatenate([matmul_tn(da, h, f"{tag}_dwin_a", out_dtype=BF16),
                               matmul_tn(db, h, f"{tag}_dwin_b", out_dtype=BF16)], axis=0)
    dx, dgain = rmsnorm_bwd(x[None], gain.reshape(1, 1, -1), dh[None], f"{tag}_dnorm", residual=dy[None])
    return dx[0], dgain.reshape(-1), dw_in_t, dw_out


def _mem_forward(mq, mem_n, w_kv, g_q, g_k, tag):
    gq = _head_gain(g_q, MEM_HEADS)
    gk = _head_gain(g_k, MEM_HEADS)
    qn = _augment(rmsnorm_fwd(mq, gq * QK_SCALE, BF16, f"{tag}_qnorm"), [])
    kv = matmul(mem_n, w_kv, f"{tag}_kv")
    k = _heads(kv[:, :MEM_WIDTH], MEM_HEADS)
    v = _heads(kv[:, MEM_WIDTH:], MEM_HEADS).astype(BF16)
    kn = _augment(rmsnorm_fwd(k, gk, BF16, f"{tag}_knorm"), [])
    v = _augment(v, [jnp.ones(v.shape[:2], BF16)])
    o, lse = attn_fwd(qn, kn, v, False, f"{tag}_attn")
    return o[:, :, :HEAD_DIM], (mq, gq, gk, qn, k, kn, v, o, lse)


def _mem_backward(do, saved, mem_n, w_kv, tag):
    mq, gq, gk, qn, k, kn, v, o, lse = saved
    dqn, dkn, dv = attn_bwd(qn, kn, v, o, lse, _pad_lanes(do), False, f"{tag}_dattn")
    dmq, dgq = rmsnorm_bwd(mq, gq, dqn[:, :, :HEAD_DIM] * QK_SCALE, f"{tag}_dqnorm")
    dk, dgk = rmsnorm_bwd(k, gk, dkn[:, :, :HEAD_DIM], f"{tag}_dknorm")
    dkv = jnp.concatenate([_unheads(dk), _unheads(dv[:, :, :HEAD_DIM])], axis=1)
    dw_kv = matmul_tn(mem_n, dkv, f"{tag}_dwkv", out_dtype=BF16)
    dmem_n = matmul(dkv, w_kv, f"{tag}_dmem", transpose_b=True)
    return dmq, dgq.sum(axis=0).reshape(-1), dgk.sum(axis=0).reshape(-1), dw_kv, dmem_n


def _fox_split(w_in):
    t3 = 3 * TOK_WIDTH
    pad = jnp.zeros(w_in.shape[:-1] + (LANES - FOX_HEADS,), w_in.dtype)
    return jnp.concatenate([w_in[..., :t3], w_in[..., t3 + FOX_HEADS:], w_in[..., t3:t3 + FOX_HEADS], pad], axis=-1)


def _fox_unsplit(w):
    t3 = 3 * TOK_WIDTH
    return jnp.concatenate([w[..., :t3], w[..., t3 + MEM_WIDTH:t3 + MEM_WIDTH + FOX_HEADS], w[..., t3:t3 + MEM_WIDTH]],
                           axis=-1)


def _fox_forward(h, w_split, b_f, g_q, g_k, tag):
    t3 = 3 * TOK_WIDTH
    proj = matmul(h, w_split, f"{tag}_proj")
    qkv = _heads(proj[:, :t3], 3 * FOX_HEADS)
    mq = _heads(proj[:, t3:t3 + MEM_WIDTH], MEM_HEADS)
    f_pad = proj[:, t3 + MEM_WIDTH:]
    b_pad = jnp.pad(b_f.reshape(1, -1), ((0, 0), (0, LANES - FOX_HEADS)))
    gains = jnp.concatenate([_head_gain(g_q, FOX_HEADS), _head_gain(g_k, FOX_HEADS)], axis=0)
    qk = qkv[:2 * FOX_HEADS]
    scaled = jnp.concatenate([gains[:FOX_HEADS] * QK_SCALE, gains[FOX_HEADS:]], axis=0)
    qkn = rmsnorm_fwd(qk, scaled, BF16, f"{tag}_qknorm")
    c = fox_gate_fwd(f_pad, b_pad, f"{tag}_gate")
    parts = _split3(c[:, :FOX_HEADS].T)
    ones = jnp.ones_like(parts[0])
    qn = _augment(qkn[:FOX_HEADS], parts + [ones] * 3)
    kn = _augment(qkn[FOX_HEADS:], [ones] * 3 + [-part for part in parts])
    v = _augment(qkv[2 * FOX_HEADS:].astype(BF16), [ones])
    o, lse = attn_fwd(qn, kn, v, True, f"{tag}_attn")
    return o[:, :, :HEAD_DIM], mq, (qk, gains, qn, kn, v, o, lse, f_pad, b_pad)


def _fox_backward(do, dmq, saved, tag):
    qk, gains, qn, kn, v, o, lse, f_pad, b_pad = saved
    dqn, dkn, dv = attn_bwd(qn, kn, v, o, lse, _pad_lanes(do), True, f"{tag}_dattn")
    dqkn = jnp.concatenate([dqn[:, :, :HEAD_DIM] * QK_SCALE, dkn[:, :, :HEAD_DIM]], axis=0)
    dqk, dgains = rmsnorm_bwd(qk, gains, dqkn, f"{tag}_dqknorm")
    dc = jnp.pad((dqn[:, :, CQ_LANE] - dkn[:, :, CK_LANE]).T, ((0, 0), (0, LANES - FOX_HEADS)))
    df, db = fox_gate_bwd(dc, f_pad, b_pad, f"{tag}_dgate")
    dproj = jnp.concatenate([_unheads(dqk), _unheads(dv[:, :, :HEAD_DIM]), _unheads(dmq), df], axis=1)
    dgains = dgains.reshape(2, FOX_HEADS, HEAD_DIM).sum(axis=1)
    return dproj, db[0, :FOX_HEADS], dgains[0], dgains[1]


def _gmlp_forward(h, w_in_t, v_gain, w_s, b_s, tag):
    proj = matmul(h, w_in_t, f"{tag}_proj", transpose_b=True)
    pu = _heads(proj[:, :TOK_WIDTH], FOX_HEADS)
    pv = _heads(proj[:, TOK_WIDTH:2 * TOK_WIDTH], FOX_HEADS)
    mq = _heads(proj[:, 2 * TOK_WIDTH:], MEM_HEADS)
    gain = v_gain.reshape(FOX_HEADS, 1, HEAD_DIM)
    b = b_s[:, :, None]
    o = gmlp_fwd(pu, pv, gain, w_s, b, f"{tag}_sgu")
    return o, mq, (pu, pv, gain, w_s, b)


def _gmlp_backward(do, dmq, saved, tag):
    pu, pv, gain, w_s, b = saved
    dpu, dpv, dw, db, dg = gmlp_bwd(pu, pv, gain, w_s, b, do, f"{tag}_dsgu")
    dproj = jnp.concatenate([_unheads(dpu), _unheads(dpv), _unheads(dmq)], axis=1)
    return dproj, dg.reshape(-1), dw, db[:, :, 0]


BIG = ("ffn1_w_in", "ffn1_w_out", "ffn2_w_in", "ffn2_w_out", "w_out", "mem_w_kv", "fox_w_in", "gmlp_w_in")
COLUMN_SHARDED = ("ffn1_w_in", "ffn2_w_in", "gmlp_w_in")
REPLICATED =("norm_ffn1", "norm_mix", "norm_ffn2", "mem_norm", "mem_q_norm", "mem_k_norm", "fox_b_f",
              "fox_q_norm", "fox_k_norm", "gmlp_w_s", "gmlp_b_s")
WEIGHTS = ("norm_ffn1", "ffn1_w_in", "ffn1_w_out", "norm_mix", "norm_ffn2", "ffn2_w_in", "ffn2_w_out", "w_out",
           "mem_norm", "mem_w_kv", "mem_q_norm", "mem_k_norm", "fox_w_in", "fox_b_f", "fox_q_norm", "fox_k_norm",
           "gmlp_w_in", "gmlp_v_norm", "gmlp_w_s", "gmlp_b_s")


def _to_transport(name, a):
    if name in COLUMN_SHARDED:
        return jnp.swapaxes(a, -1, -2)
    return _fox_split(a) if name == "fox_w_in" else a


def _from_transport(name, a):
    if name in COLUMN_SHARDED:
        return jnp.swapaxes(a, -1, -2)
    return _fox_unsplit(a) if name == "fox_w_in" else a


def kernel(x, mem, norm_ffn1, ffn1_w_in, ffn1_w_out, norm_mix, norm_ffn2, ffn2_w_in, ffn2_w_out, w_out, mem_norm, mem_w_kv, mem_q_norm, mem_k_norm, fox_w_in, fox_b_f, fox_q_norm, fox_k_norm, gmlp_w_in, gmlp_v_norm, gmlp_w_s, gmlp_b_s, loss_target, m_norm_ffn1, m_ffn1_w_in, m_ffn1_w_out, m_norm_mix, m_norm_ffn2, m_ffn2_w_in, m_ffn2_w_out, m_w_out, m_mem_norm, m_mem_w_kv, m_mem_q_norm, m_mem_k_norm, m_fox_w_in, m_fox_b_f, m_fox_q_norm, m_fox_k_norm, m_gmlp_w_in, m_gmlp_v_norm, m_gmlp_w_s, m_gmlp_b_s, v_norm_ffn1, v_ffn1_w_in, v_ffn1_w_out, v_norm_mix, v_norm_ffn2, v_ffn2_w_in, v_ffn2_w_out, v_w_out, v_mem_norm, v_mem_w_kv, v_mem_q_norm, v_mem_k_norm, v_fox_w_in, v_fox_b_f, v_fox_q_norm, v_fox_k_norm, v_gmlp_w_in, v_gmlp_v_norm, v_gmlp_w_s, v_gmlp_b_s):
    w = dict(norm_ffn1=norm_ffn1, ffn1_w_in=ffn1_w_in, ffn1_w_out=ffn1_w_out, norm_mix=norm_mix, norm_ffn2=norm_ffn2, ffn2_w_in=ffn2_w_in, ffn2_w_out=ffn2_w_out, w_out=w_out, mem_norm=mem_norm, mem_w_kv=mem_w_kv, mem_q_norm=mem_q_norm, mem_k_norm=mem_k_norm, fox_w_in=fox_w_in, fox_b_f=fox_b_f, fox_q_norm=fox_q_norm, fox_k_norm=fox_k_norm, gmlp_w_in=gmlp_w_in, gmlp_v_norm=gmlp_v_norm, gmlp_w_s=gmlp_w_s, gmlp_b_s=gmlp_b_s)
    m = dict(norm_ffn1=m_norm_ffn1, ffn1_w_in=m_ffn1_w_in, ffn1_w_out=m_ffn1_w_out, norm_mix=m_norm_mix, norm_ffn2=m_norm_ffn2, ffn2_w_in=m_ffn2_w_in, ffn2_w_out=m_ffn2_w_out, w_out=m_w_out, mem_norm=m_mem_norm, mem_w_kv=m_mem_w_kv, mem_q_norm=m_mem_q_norm, mem_k_norm=m_mem_k_norm, fox_w_in=m_fox_w_in, fox_b_f=m_fox_b_f, fox_q_norm=m_fox_q_norm, fox_k_norm=m_fox_k_norm, gmlp_w_in=m_gmlp_w_in, gmlp_v_norm=m_gmlp_v_norm, gmlp_w_s=m_gmlp_w_s, gmlp_b_s=m_gmlp_b_s)
    v = dict(norm_ffn1=v_norm_ffn1, ffn1_w_in=v_ffn1_w_in, ffn1_w_out=v_ffn1_w_out, norm_mix=v_norm_mix, norm_ffn2=v_norm_ffn2, ffn2_w_in=v_ffn2_w_in, ffn2_w_out=v_ffn2_w_out, w_out=v_w_out, mem_norm=v_mem_norm, mem_w_kv=v_mem_w_kv, mem_q_norm=v_mem_q_norm, mem_k_norm=v_mem_k_norm, fox_w_in=v_fox_w_in, fox_b_f=v_fox_b_f, fox_q_norm=v_fox_q_norm, fox_k_norm=v_fox_k_norm, gmlp_w_in=v_gmlp_w_in, gmlp_v_norm=v_gmlp_v_norm, gmlp_w_s=v_gmlp_w_s, gmlp_b_s=v_gmlp_b_s)

    depth = norm_ffn1.shape[0]
    x0 = x[0]
    mem0 = mem[0]
    target = loss_target[0]
    me = _slot(*_position())

    keys = [(n, i) for n in BIG for i in range(w[n].shape[0])]
    local = {k: _to_transport(k[0], w[k[0]][k[1]]) for k in keys}
    n_gain, gain_len = gmlp_v_norm.shape
    pad_gain = lambda a: jnp.pad(a, ((0, 8 - n_gain), (0, LANES - gain_len)))
    gathered = all_gather_multi([local[k].astype(BF16) for k in keys] + [pad_gain(gmlp_v_norm)], "gather_weights")
    full = {k: g.reshape(-1, g.shape[-1]) for k, g in zip(keys, gathered)}
    v_gain_full = gathered[-1][:, :n_gain, :gain_len]

    mem_n = rmsnorm_fwd(mem0[None], mem_norm.reshape(1, 1, -1), BF16, "mem_norm")[0]
    saved = []
    xi = x0
    for i in range(depth):
        kind, j = i % 2, i // 2
        x1, ffn1_saved = _ffn_forward(xi, norm_ffn1[i], full["ffn1_w_in", i], full["ffn1_w_out", i], f"l{i}_ffn1")
        h = rmsnorm_fwd(x1[None], norm_mix[i].reshape(1, 1, -1), BF16, f"l{i}_mixnorm")[0]
        if kind == 0:
            tok, mq, mix_saved = _fox_forward(h, full["fox_w_in", j], fox_b_f[j], fox_q_norm[j], fox_k_norm[j],
                                              f"l{i}_fox")
        else:
            tok, mq, mix_saved = _gmlp_forward(h, full["gmlp_w_in", j], v_gain_full[:, j, :].reshape(-1), gmlp_w_s[j],
                                               gmlp_b_s[j], f"l{i}_gmlp")
        mo, mem_saved = _mem_forward(mq, mem_n, full["mem_w_kv", i], mem_q_norm[i], mem_k_norm[i], f"l{i}_mem")
        cat = _unheads(jnp.concatenate([tok, mo], axis=0)).astype(BF16)
        x2 = matmul(cat, full["w_out", i], f"l{i}_wout", residual=x1)
        x3, ffn2_saved = _ffn_forward(x2, norm_ffn2[i], full["ffn2_w_in", i], full["ffn2_w_out", i], f"l{i}_ffn2")
        saved.append((ffn1_saved, x1, h, mix_saved, mem_saved, cat, ffn2_saved))
        xi = x3

    dy, loss_part = loss_head(xi, target, "loss_head")
    loss = lax.psum(loss_part, ("x", "y", "c"))

    small = {n: [None] * w[n].shape[0] for n in REPLICATED + ("gmlp_v_norm",) if n != "mem_norm"}
    big = {}
    dmem_n = None
    for i in reversed(range(depth)):
        kind, j = i % 2, i // 2
        ffn1_saved, x1, h, mix_saved, mem_saved, cat, ffn2_saved = saved[i]
        dy, small["norm_ffn2"][i], big["ffn2_w_in", i], big["ffn2_w_out", i] = _ffn_backward(
            dy, ffn2_saved, norm_ffn2[i], full["ffn2_w_in", i], full["ffn2_w_out", i], f"l{i}_ffn2")
        big["w_out", i] = matmul_tn(cat, dy, f"l{i}_dwout", out_dtype=BF16)
        dcat = _heads(matmul(dy, full["w_out", i], f"l{i}_dcat", transpose_b=True), FOX_HEADS + MEM_HEADS)
        dmq, small["mem_q_norm"][i], small["mem_k_norm"][i], big["mem_w_kv", i], dmem_i = _mem_backward(
            dcat[FOX_HEADS:], mem_saved, mem_n, full["mem_w_kv", i], f"l{i}_mem")
        dmem_n = dmem_i if dmem_n is None else dmem_n + dmem_i
        if kind == 0:
            dproj, small["fox_b_f"][j], small["fox_q_norm"][j], small["fox_k_norm"][j] = _fox_backward(
                dcat[:FOX_HEADS], dmq, mix_saved, f"l{i}_fox")
            big["fox_w_in", j] = matmul_tn(h, dproj, f"l{i}_fox_dwin", out_dtype=BF16)
            dh = matmul(dproj, full["fox_w_in", j], f"l{i}_fox_dh", transpose_b=True)
        else:
            dproj, small["gmlp_v_norm"][j], small["gmlp_w_s"][j], small["gmlp_b_s"][j] = _gmlp_backward(
                dcat[:FOX_HEADS], dmq, mix_saved, f"l{i}_gmlp")
            big["gmlp_w_in", j] = matmul_tn(dproj, h, f"l{i}_gmlp_dwin", out_dtype=BF16)
            dh = matmul(dproj, full["gmlp_w_in", j], f"l{i}_gmlp_dh")
        dy, dg_mix = rmsnorm_bwd(x1[None], norm_mix[i].reshape(1, 1, -1), dh[None], f"l{i}_dmixnorm", residual=dy[None])
        dy, small["norm_mix"][i] = dy[0], dg_mix.reshape(-1)
        dy, small["norm_ffn1"][i], big["ffn1_w_in", i], big["ffn1_w_out", i] = _ffn_backward(
            dy, ffn1_saved, norm_ffn1[i], full["ffn1_w_in", i], full["ffn1_w_out", i], f"l{i}_ffn1")
    grad_x = dy[None]
    _, dg_mem = rmsnorm_bwd(mem0[None], mem_norm.reshape(1, 1, -1), dmem_n[None], "dmem_norm")
    small = {n: jnp.stack(g) for n, g in small.items()}
    small["mem_norm"] = dg_mem.reshape(-1)

    got = all_to_all_multi([big[k].reshape((N_DEV, -1, big[k].shape[-1])) for k in keys], "exchange_grads")
    results = {n: [[None] * w[n].shape[0] for _ in range(4)] for n in BIG}
    for k, slots in zip(keys, got):
        n, i = k
        outs = adamw(local[k], slots, _to_transport(n, m[n][i]), _to_transport(n, v[n][i]), f"adamw_{n}_{i}")
        for q in range(4):
            results[n][q][i] = _from_transport(n, outs[q])
    sharded = {n: [jnp.stack(r) for r in results[n]] for n in BIG}

    small_names = REPLICATED + ("gmlp_v_norm",)
    (small_got,) = all_gather_multi([_pack([small[n] for n in small_names])], "gather_small_grads")
    rep_shapes = [w[n].shape for n in REPLICATED]
    gain_seg = jnp.zeros((n_gain, N_DEV * gain_len), F32)
    pack_rep = lambda d: _pack([d[n] for n in REPLICATED] + [gain_seg])
    outs = adamw(pack_rep(w), small_got, pack_rep(m), pack_rep(v), "adamw_replicated")
    replicated = [dict(zip(REPLICATED, _unpack(o, rep_shapes))) for o in outs]
    gain_parts = _unpack(small_got, rep_shapes + [(n_gain, N_DEV * gain_len)], lead=(N_DEV,))[-1]
    gain_slots = lax.dynamic_slice_in_dim(gain_parts, me * gain_len, gain_len, axis=2)
    gain_slots = jnp.pad(gain_slots, ((0, 0), (0, 8 - n_gain), (0, LANES - gain_len)))
    outs = adamw(pad_gain(gmlp_v_norm), gain_slots, pad_gain(m["gmlp_v_norm"]), pad_gain(v["gmlp_v_norm"]),
                 "adamw_gmlp_v_norm")
    sharded["gmlp_v_norm"] = [o[:n_gain, :gain_len] for o in outs]

    out = [loss, grad_x]
    for q in range(4):
        out += [(replicated[q][n] if n in REPLICATED else sharded[n][q]) for n in WEIGHTS]
    return tuple(out)
```

```python
import functools
import math

import jax
import jax.numpy as jnp
from jax import lax
from jax.experimental import pallas as pl
from jax.experimental.pallas import tpu as pltpu

F32 = jnp.float32
BF16 = jnp.bfloat16

EPS = 1e-6
HEAD_DIM = 64
FOX_HEADS = 12
MEM_HEADS = 4
TOK_WIDTH = FOX_HEADS * HEAD_DIM
MEM_WIDTH = MEM_HEADS * HEAD_DIM
CHUNK = 128
LANES = 128
N_DEV = 8
SEG_ALIGN = 16 * LANES
PACK_ROWS = 1024
PACK_ALIGN = PACK_ROWS * LANES

ADAM_LR = 0.001
ADAM_B1 = 0.9
ADAM_B2 = 0.999
ADAM_EPS = 1e-08
ADAM_WD = 0.01
ADAM_STEP = 10

VMEM_LIMIT_BYTES = 48 * 1024 * 1024
MESH = pl.DeviceIdType.MESH
CONTRACT_0 = (((0,), (0,)), ((), ()))
CONTRACT_1 = (((1,), (1,)), ((), ()))


def _params(*semantics):
    return pltpu.CompilerParams(dimension_semantics=semantics, vmem_limit_bytes=VMEM_LIMIT_BYTES)


def _pick(n, candidates):
    for c in candidates:
        if c <= n and n % c == 0:
            return c
    return n


def _sigmoid(x):
    return 1.0 / (1.0 + jnp.exp(-x))


def _row_tile(r, w):
    return _pick(r, (1024,) if w >= 512 else (2048, 1024, 512, 256))


def rmsnorm_fwd(x, gain, out_dtype, name):
    g_, r_, w_ = x.shape
    tr = _row_tile(r_, w_)

    def body(x_ref, g_ref, y_ref):
        xv = x_ref[0].astype(F32)
        r = lax.rsqrt(jnp.mean(xv * xv, axis=-1, keepdims=True) + EPS)
        y_ref[0] = (xv * r * g_ref[0]).astype(y_ref.dtype)

    return pl.pallas_call(
        body, name=name, grid=(g_, r_ // tr),
        in_specs=[pl.BlockSpec((1, tr, w_), lambda g, i: (g, i, 0)),
                  pl.BlockSpec((1, 1, w_), lambda g, i: (g, 0, 0))],
        out_specs=pl.BlockSpec((1, tr, w_), lambda g, i: (g, i, 0)),
        out_shape=jax.ShapeDtypeStruct((g_, r_, w_), out_dtype),
        compiler_params=_params("parallel", "parallel"),
    )(x, gain)


def rmsnorm_bwd(x, gain, dy, name, residual=None):
    g_, r_, w_ = x.shape
    tr = _row_tile(r_, w_)
    has_res = residual is not None

    def body(*refs):
        if has_res:
            x_ref, g_ref, dy_ref, res_ref, dx_ref, dg_ref = refs
        else:
            x_ref, g_ref, dy_ref, dx_ref, dg_ref = refs
        xv = x_ref[0].astype(F32)
        dyv = dy_ref[0].astype(F32)
        r = lax.rsqrt(jnp.mean(xv * xv, axis=-1, keepdims=True) + EPS)
        n = xv * r
        dn = dyv * g_ref[0]
        dx = r * (dn - n * jnp.mean(dn * n, axis=-1, keepdims=True))
        if has_res:
            dx = dx + res_ref[0]
        dx_ref[0] = dx
        part = jnp.sum(dyv * n, axis=0, keepdims=True)

        @pl.when(pl.program_id(1) == 0)
        def _():
            dg_ref[0] = part

        @pl.when(pl.program_id(1) != 0)
        def _():
            dg_ref[0] += part

    row = pl.BlockSpec((1, tr, w_), lambda g, i: (g, i, 0))
    vec = pl.BlockSpec((1, 1, w_), lambda g, i: (g, 0, 0))
    operands = (x, gain, dy) + ((residual,) if has_res else ())
    return pl.pallas_call(
        body, name=name, grid=(g_, r_ // tr),
        in_specs=[row, vec, row] + ([row] if has_res else []),
        out_specs=(row, vec),
        out_shape=(jax.ShapeDtypeStruct((g_, r_, w_), F32), jax.ShapeDtypeStruct((g_, 1, w_), F32)),
        compiler_params=_params("parallel", "arbitrary"),
    )(*operands)


def matmul(a, b, name, out_dtype=F32, residual=None, scale=None, transpose_b=False):
    m_, k_ = a.shape
    n_ = b.shape[0] if transpose_b else b.shape[1]
    tm = _pick(m_, (512, 256, 128))
    tn = _pick(n_, (1408, 1024, 896, 512, 256, 128))
    has_res = residual is not None

    def body(*refs):
        if has_res:
            a_ref, b_ref, res_ref, o_ref = refs
        else:
            a_ref, b_ref, o_ref = refs
        av, bv = a_ref[...].astype(BF16), b_ref[...].astype(BF16)
        if transpose_b:
            acc = lax.dot_general(av, bv, CONTRACT_1, preferred_element_type=F32)
        else:
            acc = jnp.dot(av, bv, preferred_element_type=F32)
        if scale is not None:
            acc = acc * scale
        if has_res:
            acc = acc + res_ref[...]
        o_ref[...] = acc.astype(o_ref.dtype)

    out_spec = pl.BlockSpec((tm, tn), lambda j, i: (i, j))
    b_spec = pl.BlockSpec((tn, k_), lambda j, i: (j, 0)) if transpose_b else pl.BlockSpec((k_, tn), lambda j, i: (0, j))
    operands = (a, b) + ((residual,) if has_res else ())
    return pl.pallas_call(
        body, name=name, grid=(n_ // tn, m_ // tm),
        in_specs=[pl.BlockSpec((tm, k_), lambda j, i: (i, 0)), b_spec] + ([out_spec] if has_res else []),
        out_specs=out_spec,
        out_shape=jax.ShapeDtypeStruct((m_, n_), out_dtype),
        compiler_params=_params("parallel", "parallel"),
    )(*operands)


def matmul_tn(a, b, name, scale=None, out_dtype=F32):
    s_, k_ = a.shape
    n_ = b.shape[1]
    tk = _pick(k_, (1024, 1408, 896, 512, 256, 128))
    tn = _pick(n_, (1408, 1024, 896, 512, 256, 128))
    ts = _pick(s_, (512, 256, 128))
    ns = s_ // ts

    def body(a_ref, b_ref, o_ref, acc_ref):
        part = lax.dot_general(a_ref[...].astype(BF16), b_ref[...].astype(BF16), CONTRACT_0,
                               preferred_element_type=F32)
        step = pl.program_id(2)

        @pl.when(step == 0)
        def _():
            acc_ref[...] = part

        @pl.when(step != 0)
        def _():
            acc_ref[...] += part

        @pl.when(step == ns - 1)
        def _():
            acc = acc_ref[...]
            o_ref[...] = (acc if scale is None else acc * scale).astype(o_ref.dtype)

    return pl.pallas_call(
        body, name=name, grid=(k_ // tk, n_ // tn, ns),
        in_specs=[pl.BlockSpec((ts, tk), lambda i, j, s: (s, i)),
                  pl.BlockSpec((ts, tn), lambda i, j, s: (s, j))],
        out_specs=pl.BlockSpec((tk, tn), lambda i, j, s: (i, j)),
        out_shape=jax.ShapeDtypeStruct((k_, n_), out_dtype),
        scratch_shapes=[pltpu.VMEM((tk, tn), F32)],
        compiler_params=_params("parallel", "parallel", "arbitrary"),
    )(a, b)


def ffn_in(h, w_in_t, name):
    s_, d_ = h.shape
    f_ = w_in_t.shape[0] // 2
    tm = _pick(s_, (512, 256, 128))
    tn = _pick(f_, (1408, 1024, 512, 256, 128))
    nb = f_ // tn

    def body(h_ref, wa_ref, wb_ref, a_ref, b_ref, u_ref):
        hv = h_ref[...]
        a = lax.dot_general(hv, wa_ref[...], CONTRACT_1, preferred_element_type=F32)
        b = lax.dot_general(hv, wb_ref[...], CONTRACT_1, preferred_element_type=F32)
        a_ref[...] = a.astype(BF16)
        b_ref[...] = b.astype(BF16)
        u_ref[...] = (a * _sigmoid(a) * b).astype(BF16)

    o_spec = pl.BlockSpec((tm, tn), lambda j, i: (i, j))
    out = jax.ShapeDtypeStruct((s_, f_), BF16)
    return pl.pallas_call(
        body, name=name, grid=(nb, s_ // tm),
        in_specs=[pl.BlockSpec((tm, d_), lambda j, i: (i, 0)),
                  pl.BlockSpec((tn, d_), lambda j, i: (j, 0)),
                  pl.BlockSpec((tn, d_), lambda j, i: (j + nb, 0))],
        out_specs=(o_spec, o_spec, o_spec), out_shape=(out, out, out),
        compiler_params=_params("parallel", "parallel"),
    )(h, w_in_t, w_in_t)


def ffn_bwd_act(dy, w_out, a, b, name):
    s_, d_ = dy.shape
    f_ = w_out.shape[0]
    tm = _pick(s_, (512, 256, 128))
    tn = _pick(f_, (1408, 1024, 512, 256, 128))

    def body(dy_ref, w_ref, a_ref, b_ref, da_ref, db_ref):
        du = 0.5 * lax.dot_general(dy_ref[...].astype(BF16), w_ref[...], CONTRACT_1, preferred_element_type=F32)
        av = a_ref[...].astype(F32)
        bv = b_ref[...].astype(F32)
        sig = _sigmoid(av)
        da_ref[...] = (du * bv * (sig * (1.0 + av * (1.0 - sig)))).astype(BF16)
        db_ref[...] = (du * (av * sig)).astype(BF16)

    t_spec = pl.BlockSpec((tm, tn), lambda j, i: (i, j))
    out = jax.ShapeDtypeStruct((s_, f_), BF16)
    return pl.pallas_call(
        body, name=name, grid=(f_ // tn, s_ // tm),
        in_specs=[pl.BlockSpec((tm, d_), lambda j, i: (i, 0)),
                  pl.BlockSpec((tn, d_), lambda j, i: (j, 0)), t_spec, t_spec],
        out_specs=(t_spec, t_spec), out_shape=(out, out),
        compiler_params=_params("parallel", "parallel"),
    )(dy, w_out, a, b)


def _fold(a, b, n, forward):
    if forward:
        low = b <= a
        return jnp.where(low, a, n - 1 - a), jnp.where(low, b, b - a - 1)
    low = b < n - a
    return jnp.where(low, a, n - 1 - a), jnp.where(low, a + b, b - 1)


def _attn_grid(h_, n_outer, n_inner, causal, forward):
    if causal and n_outer % 2 == 0:
        return (h_, n_outer // 2, n_outer + 1), lambda a, b: _fold(a, b, n_outer, forward)
    return (h_, n_outer, n_inner), lambda a, b: (a, b)


def _scores(q, k, cq, ck, masked):
    s = lax.dot_general(q, k, CONTRACT_1, preferred_element_type=F32) * (1.0 / math.sqrt(HEAD_DIM))
    if cq is not None:
        s = s + (cq - ck)
    if masked:
        row = lax.broadcasted_iota(jnp.int32, s.shape, 0)
        col = lax.broadcasted_iota(jnp.int32, s.shape, 1)
        s = jnp.where(col <= row, s, -jnp.inf)
    return s


def attn_fwd(q, k, v, cq, ck, causal, name):
    h_, sq, d_ = q.shape
    sk = k.shape[1]
    tq = _pick(sq, (512, 256, 128))
    tk = tq if causal else _pick(sk, (512, 256, 128))
    nq, nk = sq // tq, sk // tk
    assert not causal or nq == 1 or nq % 2 == 0
    bias = cq is not None
    grid, blocks = _attn_grid(h_, nq, nk, causal, True)

    def body(*refs):
        if bias:
            q_ref, k_ref, v_ref, cq_ref, ck_ref, o_ref, lse_ref, m_sc, l_sc, acc_sc = refs
        else:
            q_ref, k_ref, v_ref, o_ref, lse_ref, m_sc, l_sc, acc_sc = refs
        i, j = blocks(pl.program_id(1), pl.program_id(2))

        @pl.when(j == 0)
        def _():
            m_sc[...] = jnp.full(m_sc.shape, -jnp.inf, F32)
            l_sc[...] = jnp.zeros(l_sc.shape, F32)
            acc_sc[...] = jnp.zeros(acc_sc.shape, F32)

        def step(masked):
            s = _scores(q_ref[0].astype(BF16), k_ref[0].astype(BF16), cq_ref[0] if bias else None,
                        ck_ref[0] if bias else None, masked)
            m_prev = m_sc[...]
            m_new = jnp.maximum(m_prev, jnp.max(s, axis=1, keepdims=True))
            alpha = jnp.exp(m_prev - m_new)
            p = jnp.exp(s - m_new)
            l_sc[...] = alpha * l_sc[...] + jnp.sum(p, axis=1, keepdims=True)
            acc_sc[...] = alpha * acc_sc[...] + jnp.dot(p.astype(BF16), v_ref[0].astype(BF16),
                                                        preferred_element_type=F32)
            m_sc[...] = m_new

        if causal:
            pl.when(j < i)(functools.partial(step, False))
            pl.when(j == i)(functools.partial(step, True))
        else:
            step(False)

        @pl.when(j == (i if causal else nk - 1))
        def _():
            o_ref[0] = acc_sc[...] / l_sc[...]
            lse_ref[0] = m_sc[...] + jnp.log(l_sc[...])

    q_spec = pl.BlockSpec((1, tq, d_), lambda h, a, b: (h, blocks(a, b)[0], 0))
    q1_spec = pl.BlockSpec((1, tq, 1), lambda h, a, b: (h, blocks(a, b)[0], 0))
    k_spec = pl.BlockSpec((1, tk, d_), lambda h, a, b: (h, blocks(a, b)[1], 0))
    in_specs = [q_spec, k_spec, k_spec]
    operands = [q, k, v]
    if bias:
        in_specs += [q1_spec, pl.BlockSpec((1, 1, tk), lambda h, a, b: (h, 0, blocks(a, b)[1]))]
        operands += [cq, ck]
    return pl.pallas_call(
        body, name=name, grid=grid,
        in_specs=in_specs, out_specs=(q_spec, q1_spec),
        out_shape=(jax.ShapeDtypeStruct((h_, sq, d_), F32), jax.ShapeDtypeStruct((h_, sq, 1), F32)),
        scratch_shapes=[pltpu.VMEM((tq, 1), F32), pltpu.VMEM((tq, 1), F32), pltpu.VMEM((tq, d_), F32)],
        compiler_params=_params("parallel", "arbitrary", "arbitrary"),
    )(*operands)


def attn_bwd(q, k, v, cq, ck, o, lse, do, causal, name):
    h_, sq, d_ = q.shape
    sk = k.shape[1]
    tq = _pick(sq, (512, 256, 128))
    tk = tq if causal else _pick(sk, (512, 256, 128))
    nq, nk = sq // tq, sk // tk
    assert not causal or nq == 1 or nq % 2 == 0
    bias = cq is not None
    scale = 1.0 / math.sqrt(HEAD_DIM)
    grid, blocks = _attn_grid(h_, nk, nq, causal, False)

    def body(*refs):
        if bias:
            (q_ref, k_ref, v_ref, cq_ref, ck_ref, o_ref, lse_ref, do_ref,
             dq_ref, dk_ref, dv_ref, dc_ref, dcq_ref, dk_sc, dv_sc, dc_sc) = refs
        else:
            (q_ref, k_ref, v_ref, o_ref, lse_ref, do_ref,
             dq_ref, dk_ref, dv_ref, dk_sc, dv_sc) = refs
        j, i = blocks(pl.program_id(1), pl.program_id(2))

        @pl.when((pl.program_id(1) == 0) & (pl.program_id(2) == 0))
        def _():
            dq_ref[...] = jnp.zeros(dq_ref.shape, F32)
            if bias:
                dcq_ref[...] = jnp.zeros(dcq_ref.shape, F32)

        @pl.when(i == (j if causal else 0))
        def _():
            dk_sc[...] = jnp.zeros(dk_sc.shape, F32)
            dv_sc[...] = jnp.zeros(dv_sc.shape, F32)
            if bias:
                dc_sc[...] = jnp.zeros(dc_sc.shape, F32)

        def step(masked):
            qb = q_ref[0].astype(BF16)
            kb = k_ref[0].astype(BF16)
            dof = do_ref[0].astype(F32)
            dob = dof.astype(BF16)
            s = _scores(qb, kb, cq_ref[0] if bias else None, ck_ref[0] if bias else None, masked)
            p = jnp.exp(s - lse_ref[0])
            dp = lax.dot_general(dob, v_ref[0].astype(BF16), CONTRACT_1, preferred_element_type=F32)
            delta = jnp.sum(dof * o_ref[0], axis=1, keepdims=True)
            ds = p * (dp - delta)
            dsb = ds.astype(BF16)
            dv_sc[...] += lax.dot_general(p.astype(BF16), dob, CONTRACT_0, preferred_element_type=F32)
            dk_sc[...] += lax.dot_general(dsb, qb, CONTRACT_0, preferred_element_type=F32) * scale
            rows = pl.ds(pl.multiple_of(i * tq, tq), tq)
            dq_ref[0, rows, :] += jnp.dot(dsb, kb, preferred_element_type=F32) * scale
            if bias:
                dc_sc[...] -= jnp.sum(ds, axis=0, keepdims=True)
                dcq_ref[0, rows, :] += jnp.sum(ds, axis=1, keepdims=True)

        if causal:
            pl.when(i > j)(functools.partial(step, False))
            pl.when(i == j)(functools.partial(step, True))
        else:
            step(False)

        @pl.when(i == nq - 1)
        def _():
            dk_ref[0] = dk_sc[...]
            dv_ref[0] = dv_sc[...]
            if bias:
                dc_ref[0] = dc_sc[...]

    q_spec = pl.BlockSpec((1, tq, d_), lambda h, a, b: (h, blocks(a, b)[1], 0))
    q1_spec = pl.BlockSpec((1, tq, 1), lambda h, a, b: (h, blocks(a, b)[1], 0))
    k_spec = pl.BlockSpec((1, tk, d_), lambda h, a, b: (h, blocks(a, b)[0], 0))
    c_spec = pl.BlockSpec((1, 1, tk), lambda h, a, b: (h, 0, blocks(a, b)[0]))
    in_specs = [q_spec, k_spec, k_spec] + ([q1_spec, c_spec] if bias else []) + [q_spec, q1_spec, q_spec]
    operands = [q, k, v] + ([cq, ck] if bias else []) + [o, lse, do]
    out_specs = [pl.BlockSpec((1, sq, d_), lambda h, a, b: (h, 0, 0)), k_spec, k_spec]
    out_shape = [jax.ShapeDtypeStruct((h_, sq, d_), F32), jax.ShapeDtypeStruct((h_, sk, d_), F32),
                 jax.ShapeDtypeStruct((h_, sk, d_), F32)]
    scratch = [pltpu.VMEM((tk, d_), F32), pltpu.VMEM((tk, d_), F32)]
    if bias:
        out_specs += [c_spec, pl.BlockSpec((1, sq, 1), lambda h, a, b: (h, 0, 0))]
        out_shape += [jax.ShapeDtypeStruct((h_, 1, sk), F32), jax.ShapeDtypeStruct((h_, sq, 1), F32)]
        scratch.append(pltpu.VMEM((1, tk), F32))
    return pl.pallas_call(
        body, name=name, grid=grid,
        in_specs=in_specs, out_specs=tuple(out_specs), out_shape=tuple(out_shape),
        scratch_shapes=scratch,
        compiler_params=_params("parallel", "arbitrary", "arbitrary"),
    )(*operands)


def _tri(lower):
    row = lax.broadcasted_iota(jnp.int32, (CHUNK, CHUNK), 0)
    col = lax.broadcasted_iota(jnp.int32, (CHUNK, CHUNK), 1)
    return jnp.where((col <= row) if lower else (col >= row), 1.0, 0.0).astype(F32)


def fox_gate_fwd(f, b, name):
    s_ = f.shape[0]

    def body(f_ref, b_ref, c_ref, carry):
        @pl.when(pl.program_id(0) == 0)
        def _():
            carry[...] = jnp.zeros(carry.shape, F32)

        xv = f_ref[...] + b_ref[...]
        log_f = jnp.minimum(xv, 0.0) - jnp.log(1.0 + jnp.exp(-jnp.abs(xv)))
        c = jnp.dot(_tri(True), log_f, precision=lax.Precision.HIGHEST, preferred_element_type=F32) + carry[...]
        c_ref[...] = c
        carry[...] = c[CHUNK - 1:CHUNK, :]

    blk = pl.BlockSpec((CHUNK, LANES), lambda i: (i, 0))
    return pl.pallas_call(
        body, name=name, grid=(s_ // CHUNK,),
        in_specs=[blk, pl.BlockSpec((1, LANES), lambda i: (0, 0))], out_specs=blk,
        out_shape=jax.ShapeDtypeStruct((s_, LANES), F32),
        scratch_shapes=[pltpu.VMEM((1, LANES), F32)],
        compiler_params=_params("arbitrary"),
    )(f, b)


def fox_gate_bwd(dc, f, b, name):
    s_ = f.shape[0]
    n = s_ // CHUNK

    def body(dc_ref, f_ref, b_ref, df_ref, db_ref, carry):
        @pl.when(pl.program_id(0) == 0)
        def _():
            carry[...] = jnp.zeros(carry.shape, F32)
            db_ref[...] = jnp.zeros(db_ref.shape, F32)

        dlog = jnp.dot(_tri(False), dc_ref[...], precision=lax.Precision.HIGHEST,
                       preferred_element_type=F32) + carry[...]
        df = dlog * _sigmoid(-(f_ref[...] + b_ref[...]))
        df_ref[...] = df
        db_ref[...] += jnp.sum(df, axis=0, keepdims=True)
        carry[...] = dlog[0:1, :]

    blk = pl.BlockSpec((CHUNK, LANES), lambda i: (n - 1 - i, 0))
    vec = pl.BlockSpec((1, LANES), lambda i: (0, 0))
    return pl.pallas_call(
        body, name=name, grid=(n,),
        in_specs=[blk, blk, vec], out_specs=(blk, vec),
        out_shape=(jax.ShapeDtypeStruct((s_, LANES), F32), jax.ShapeDtypeStruct((1, LANES), F32)),
        scratch_shapes=[pltpu.VMEM((1, LANES), F32)],
        compiler_params=_params("arbitrary"),
    )(dc, f, b)


GELU_K = math.sqrt(2.0 / math.pi)
GELU_C = 0.044715


def _gelu(x):
    return 0.5 * x * (1.0 + jnp.tanh(GELU_K * (x + GELU_C * (x * x * x))))


def _gelu_grad(x):
    t = jnp.tanh(GELU_K * (x + GELU_C * (x * x * x)))
    return 0.5 * (1.0 + t) + 0.5 * x * (1.0 - t * t) * (GELU_K * (1.0 + 3.0 * GELU_C * (x * x)))


def _tril_mask():
    row = lax.broadcasted_iota(jnp.int32, (CHUNK, CHUNK), 0)
    col = lax.broadcasted_iota(jnp.int32, (CHUNK, CHUNK), 1)
    return col <= row


def _gmlp_specs(s_, ts):
    row = pl.BlockSpec((1, ts, HEAD_DIM), lambda g, i: (g, i, 0))
    gain = pl.BlockSpec((1, 1, HEAD_DIM), lambda g, i: (g, 0, 0))
    w = pl.BlockSpec((1, CHUNK, CHUNK), lambda g, i: (g, 0, 0))
    b = pl.BlockSpec((1, CHUNK, 1), lambda g, i: (g, 0, 0))
    return row, gain, w, b


def gmlp_fwd(pu, pv, gain, w, b, name):
    g_, s_, d_ = pu.shape
    ts = _pick(s_, (1024, 512, 256, 128))

    def body(pu_ref, pv_ref, g_ref, w_ref, b_ref, o_ref):
        v = _gelu(pv_ref[0])
        r = lax.rsqrt(jnp.mean(v * v, axis=-1, keepdims=True) + EPS)
        vn = (v * r * g_ref[0]).astype(BF16)
        wt = jnp.where(_tril_mask(), w_ref[0], 0.0).astype(BF16)
        for c in range(ts // CHUNK):
            rows = pl.ds(c * CHUNK, CHUNK)
            gate = jnp.dot(wt, vn[c * CHUNK:(c + 1) * CHUNK], preferred_element_type=F32) + b_ref[0]
            o_ref[0, rows, :] = _gelu(pu_ref[0, rows, :]) * gate

    row, gspec, wspec, bspec = _gmlp_specs(s_, ts)
    return pl.pallas_call(
        body, name=name, grid=(g_, s_ // ts),
        in_specs=[row, row, gspec, wspec, bspec], out_specs=row,
        out_shape=jax.ShapeDtypeStruct((g_, s_, d_), F32),
        compiler_params=_params("parallel", "parallel"),
    )(pu, pv, gain, w, b)


def gmlp_bwd(pu, pv, gain, w, b, dout, name):
    g_, s_, d_ = pu.shape
    ts = _pick(s_, (1024, 512, 256, 128))

    def body(pu_ref, pv_ref, g_ref, w_ref, b_ref, do_ref, dpu_ref, dpv_ref, dw_ref, db_ref, dg_ref):
        @pl.when(pl.program_id(1) == 0)
        def _():
            dw_ref[...] = jnp.zeros(dw_ref.shape, F32)
            db_ref[...] = jnp.zeros(db_ref.shape, F32)
            dg_ref[...] = jnp.zeros(dg_ref.shape, F32)

        gain_v = g_ref[0]
        mask = _tril_mask()
        wt = jnp.where(mask, w_ref[0], 0.0).astype(BF16)
        dw = jnp.zeros((CHUNK, CHUNK), F32)
        db = jnp.zeros((CHUNK, 1), F32)
        dg = jnp.zeros((1, d_), F32)
        for c in range(ts // CHUNK):
            rows = pl.ds(c * CHUNK, CHUNK)
            pu_c = pu_ref[0, rows, :]
            pv_c = pv_ref[0, rows, :]
            do_c = do_ref[0, rows, :]
            u = _gelu(pu_c)
            v = _gelu(pv_c)
            r = lax.rsqrt(jnp.mean(v * v, axis=-1, keepdims=True) + EPS)
            n = v * r
            vn = (n * gain_v).astype(BF16)
            gate = jnp.dot(wt, vn, preferred_element_type=F32) + b_ref[0]
            dgate = do_c * u
            dgate_b = dgate.astype(BF16)
            dpu_ref[0, rows, :] = do_c * gate * _gelu_grad(pu_c)
            db = db + jnp.sum(dgate, axis=1, keepdims=True)
            dw = dw + lax.dot_general(dgate_b, vn, CONTRACT_1, preferred_element_type=F32)
            dvn = lax.dot_general(wt, dgate_b, CONTRACT_0, preferred_element_type=F32)
            dg = dg + jnp.sum(dvn * n, axis=0, keepdims=True)
            dn = dvn * gain_v
            dv = r * (dn - n * jnp.mean(dn * n, axis=-1, keepdims=True))
            dpv_ref[0, rows, :] = dv * _gelu_grad(pv_c)
        dw_ref[0] += jnp.where(mask, dw, 0.0)
        db_ref[0] += db
        dg_ref[0] += dg

    row, gspec, wspec, bspec = _gmlp_specs(s_, ts)
    return pl.pallas_call(
        body, name=name, grid=(g_, s_ // ts),
        in_specs=[row, row, gspec, wspec, bspec, row],
        out_specs=(row, row, wspec, bspec, gspec),
        out_shape=(jax.ShapeDtypeStruct((g_, s_, d_), F32), jax.ShapeDtypeStruct((g_, s_, d_), F32),
                   jax.ShapeDtypeStruct((g_, CHUNK, CHUNK), F32), jax.ShapeDtypeStruct((g_, CHUNK, 1), F32),
                   jax.ShapeDtypeStruct((g_, 1, d_), F32)),
        compiler_params=_params("parallel", "arbitrary"),
    )(pu, pv, gain, w, b, dout)


def loss_head(y, t, name):
    s_, d_ = y.shape
    tr = _pick(s_, (1024, 512, 256, 128))

    def body(y_ref, t_ref, dy_ref, l_ref):
        err = y_ref[...] - t_ref[...]
        dy_ref[...] = err * (1.0 / d_)
        part = jnp.full(l_ref.shape, jnp.sum(err * err) * (0.5 / d_), F32)

        @pl.when(pl.program_id(0) == 0)
        def _():
            l_ref[...] = part

        @pl.when(pl.program_id(0) != 0)
        def _():
            l_ref[...] += part

    blk = pl.BlockSpec((tr, d_), lambda i: (i, 0))
    dy, l = pl.pallas_call(
        body, name=name, grid=(s_ // tr,),
        in_specs=[blk, blk], out_specs=(blk, pl.BlockSpec((8, LANES), lambda i: (0, 0))),
        out_shape=(jax.ShapeDtypeStruct((s_, d_), F32), jax.ShapeDtypeStruct((8, LANES), F32)),
        compiler_params=_params("arbitrary"),
    )(y, t)
    return dy, l[0, 0]


def adamw(w, g_slots, m, v, name):
    r_, c_ = w.shape
    tr = _pick(r_, (1024, 512, 352, 256, 224, 128, 64, 32, 16, 8))

    def body(w_ref, gs_ref, m_ref, v_ref, g_ref, d_ref, nm_ref, nv_ref):
        g = gs_ref[0].astype(F32)
        for k in range(1, N_DEV):
            g = g + gs_ref[k].astype(F32)
        m_new = ADAM_B1 * m_ref[...] + (1.0 - ADAM_B1) * g
        v_new = ADAM_B2 * v_ref[...] + (1.0 - ADAM_B2) * (g * g)
        m_hat = m_new / (1.0 - ADAM_B1 ** ADAM_STEP)
        v_hat = v_new / (1.0 - ADAM_B2 ** ADAM_STEP)
        g_ref[...] = g
        d_ref[...] = -ADAM_LR * (m_hat / (jnp.sqrt(v_hat) + ADAM_EPS) + ADAM_WD * w_ref[...])
        nm_ref[...] = m_new
        nv_ref[...] = v_new

    blk = pl.BlockSpec((tr, c_), lambda i: (i, 0))
    out = jax.ShapeDtypeStruct((r_, c_), F32)
    return pl.pallas_call(
        body, name=name, grid=(r_ // tr,),
        in_specs=[blk, pl.BlockSpec((N_DEV, tr, c_), lambda i: (0, i, 0)), blk, blk],
        out_specs=(blk, blk, blk, blk), out_shape=(out, out, out, out),
        compiler_params=_params("parallel"),
    )(w, g_slots, m, v)


def _position():
    x, y, c = lax.axis_index("x"), lax.axis_index("y"), lax.axis_index("c")
    return x, y, c


def _slot(px, py, pc):
    return 4 * px + 2 * py + pc


def all_gather_multi(blocks, name):
    n = len(blocks)

    def body(*refs):
        x_refs, out_refs = refs[:n], refs[n:2 * n]
        send_sems, recv_sems, local_sems = refs[2 * n:]
        x, y, c = _position()
        me, sibling = (x, y, c), (x, y, 1 - c)
        chips = [(1 - x, y), (x, 1 - y), (1 - x, 1 - y)]

        def copy(b, k, owner, to, src=None):
            dst = out_refs[b].at[_slot(*owner)]
            return pltpu.make_async_remote_copy(
                src_ref=dst if src is None else src, dst_ref=dst,
                send_sem=send_sems.at[b, k], recv_sem=recv_sems.at[b, k], device_id=to, device_id_type=MESH)

        mine = [pltpu.make_async_copy(x_refs[b], out_refs[b].at[_slot(*me)], local_sems.at[b]) for b in range(n)]
        for cp in mine:
            cp.start()
        first = [copy(b, 1 + j, me, (*chip, c), src=x_refs[b]) for j, chip in enumerate(chips) for b in range(n)]
        first += [copy(b, 0, me, sibling, src=x_refs[b]) for b in range(n)]
        for cp in first:
            cp.start()
        passed = []
        for j, chip in enumerate(chips):
            for b in range(n):
                copy(b, 1 + j, (*chip, c), me).wait_recv()
                cp = copy(b, 4 + j, (*chip, c), sibling)
                cp.start()
                passed.append(cp)
        for b in range(n):
            copy(b, 0, sibling, me).wait_recv()
        for j, chip in enumerate(chips):
            for b in range(n):
                copy(b, 4 + j, (*chip, 1 - c), me).wait_recv()
        for cp in first + passed:
            cp.wait_send()
        for cp in mine:
            cp.wait()

    any_spec = pl.BlockSpec(memory_space=pl.ANY)
    return pl.pallas_call(
        body, name=name,
        in_specs=[any_spec] * n, out_specs=tuple([any_spec] * n),
        out_shape=tuple(jax.ShapeDtypeStruct((N_DEV,) + blk.shape, blk.dtype) for blk in blocks),
        scratch_shapes=[pltpu.SemaphoreType.DMA((n, 7)), pltpu.SemaphoreType.DMA((n, 7)), pltpu.SemaphoreType.DMA((n,))],
    )(*blocks)


def all_to_all_multi(slot_bufs, name):
    n = len(slot_bufs)

    def body(*refs):
        in_refs, out_refs = refs[:n], refs[n:2 * n]
        send_sems, recv_sems, local_sems = refs[2 * n:]
        x, y, c = _position()
        me = _slot(x, y, c)
        mine = [pltpu.make_async_copy(in_refs[b].at[me], out_refs[b].at[me], local_sems.at[b]) for b in range(n)]
        for cp in mine:
            cp.start()
        copies = []
        for k in range(1, N_DEV):
            px = 1 - x if k & 4 else x
            py = 1 - y if k & 2 else y
            pc = 1 - c if k & 1 else c
            for b in range(n):
                cp = pltpu.make_async_remote_copy(
                    src_ref=in_refs[b].at[_slot(px, py, pc)], dst_ref=out_refs[b].at[me],
                    send_sem=send_sems.at[b, k - 1], recv_sem=recv_sems.at[b, k - 1],
                    device_id=(px, py, pc), device_id_type=MESH)
                cp.start()
                copies.append(cp)
        for cp in copies:
            cp.wait()
        for cp in mine:
            cp.wait()

    any_spec = pl.BlockSpec(memory_space=pl.ANY)
    return pl.pallas_call(
        body, name=name,
        in_specs=[any_spec] * n, out_specs=tuple([any_spec] * n),
        out_shape=tuple(jax.ShapeDtypeStruct(buf.shape, buf.dtype) for buf in slot_bufs),
        scratch_shapes=[pltpu.SemaphoreType.DMA((n, 7)), pltpu.SemaphoreType.DMA((n, 7)), pltpu.SemaphoreType.DMA((n,))],
    )(*slot_bufs)


def _seg_len(n):
    return -(-n // SEG_ALIGN) * SEG_ALIGN


def _pack(arrays, lead=()):
    parts, total = [], 0
    for a in arrays:
        flat = a.reshape(lead + (-1,))
        pad = _seg_len(flat.shape[-1]) - flat.shape[-1]
        parts.append(jnp.pad(flat, [(0, 0)] * len(lead) + [(0, pad)]) if pad else flat)
        total += flat.shape[-1] + pad
    tail = -(-total // PACK_ALIGN) * PACK_ALIGN - total
    if tail:
        parts.append(jnp.zeros(lead + (tail,), parts[0].dtype))
    return jnp.concatenate(parts, axis=-1).reshape(lead + (-1, LANES))


def _unpack(packed, shapes, lead=()):
    flat = packed.reshape(lead + (-1,))
    out, off = [], 0
    for shp in shapes:
        n = math.prod(shp)
        out.append(flat[..., off:off + n].reshape(lead + tuple(shp)))
        off += _seg_len(n)
    return out


def _heads(a, n_heads):
    return a.reshape(a.shape[0], n_heads, HEAD_DIM).transpose(1, 0, 2)


def _unheads(a):
    return a.transpose(1, 0, 2).reshape(a.shape[1], a.shape[0] * HEAD_DIM)


def _head_gain(g, n_heads):
    return jnp.broadcast_to(g.reshape(1, 1, HEAD_DIM), (n_heads, 1, HEAD_DIM))


def _ffn_forward(x, gain, w_in_t, w_out, tag):
    h = rmsnorm_fwd(x[None], gain.reshape(1, 1, -1), BF16, f"{tag}_norm")[0]
    a, b, u = ffn_in(h, w_in_t, f"{tag}_in")
    y = matmul(u, w_out, f"{tag}_out", residual=x, scale=0.5)
    return y, (x, h, a, b, u)


def _ffn_backward(dy, saved, gain, w_in_t, w_out, tag):
    x, h, a, b, u = saved
    f_ = w_out.shape[0]
    dw_out = matmul_tn(u, dy, f"{tag}_dwout", scale=0.5, out_dtype=BF16)
    da, db = ffn_bwd_act(dy, w_out, a, b, f"{tag}_dact")
    dh = matmul(da, w_in_t[:f_], f"{tag}_dh_a")
    dh = matmul(db, w_in_t[f_:], f"{tag}_dh_b", residual=dh)
    dw_in_t = jnp.concatenate([matmul_tn(da, h, f"{tag}_dwin_a", out_dtype=BF16),
                               matmul_tn(db, h, f"{tag}_dwin_b", out_dtype=BF16)], axis=0)
    dx, dgain = rmsnorm_bwd(x[None], gain.reshape(1, 1, -1), dh[None], f"{tag}_dnorm", residual=dy[None])
    return dx[0], dgain.reshape(-1), dw_in_t, dw_out


def _mem_forward(mq, mem_n, w_kv, g_q, g_k, tag):
    gq = _head_gain(g_q, MEM_HEADS)
    gk = _head_gain(g_k, MEM_HEADS)
    qn = rmsnorm_fwd(mq, gq, BF16, f"{tag}_qnorm")
    kv = matmul(mem_n, w_kv, f"{tag}_kv")
    k = _heads(kv[:, :MEM_WIDTH], MEM_HEADS)
    v = _heads(kv[:, MEM_WIDTH:], MEM_HEADS)
    kn = rmsnorm_fwd(k, gk, BF16, f"{tag}_knorm")
    o, lse = attn_fwd(qn, kn, v, None, None, False, f"{tag}_attn")
    return o, (mq, gq, gk, qn, k, kn, v, o, lse)


def _mem_backward(do, saved, mem_n, w_kv, tag):
    mq, gq, gk, qn, k, kn, v, o, lse = saved
    dqn, dkn, dv = attn_bwd(qn, kn, v, None, None, o, lse, do, False, f"{tag}_dattn")
    dmq, dgq = rmsnorm_bwd(mq, gq, dqn, f"{tag}_dqnorm")
    dk, dgk = rmsnorm_bwd(k, gk, dkn, f"{tag}_dknorm")
    dkv = jnp.concatenate([_unheads(dk), _unheads(dv)], axis=1)
    dw_kv = matmul_tn(mem_n, dkv, f"{tag}_dwkv", out_dtype=BF16)
    dmem_n = matmul(dkv, w_kv, f"{tag}_dmem", transpose_b=True)
    return dmq, dgq.sum(axis=0).reshape(-1), dgk.sum(axis=0).reshape(-1), dw_kv, dmem_n


def _fox_split(w_in):
    t3 = 3 * TOK_WIDTH
    pad = jnp.zeros(w_in.shape[:-1] + (LANES - FOX_HEADS,), w_in.dtype)
    return jnp.concatenate([w_in[..., :t3], w_in[..., t3 + FOX_HEADS:], w_in[..., t3:t3 + FOX_HEADS], pad], axis=-1)


def _fox_unsplit(w):
    t3 = 3 * TOK_WIDTH
    return jnp.concatenate([w[..., :t3], w[..., t3 + MEM_WIDTH:t3 + MEM_WIDTH + FOX_HEADS], w[..., t3:t3 + MEM_WIDTH]],
                           axis=-1)


def _fox_forward(h, w_split, b_f, g_q, g_k, tag):
    t3 = 3 * TOK_WIDTH
    proj = matmul(h, w_split, f"{tag}_proj")
    qkv = _heads(proj[:, :t3], 3 * FOX_HEADS)
    mq = _heads(proj[:, t3:t3 + MEM_WIDTH], MEM_HEADS)
    f_pad = proj[:, t3 + MEM_WIDTH:]
    b_pad = jnp.pad(b_f.reshape(1, -1), ((0, 0), (0, LANES - FOX_HEADS)))
    gains = jnp.concatenate([_head_gain(g_q, FOX_HEADS), _head_gain(g_k, FOX_HEADS)], axis=0)
    qk = qkv[:2 * FOX_HEADS]
    qkn = rmsnorm_fwd(qk, gains, BF16, f"{tag}_qknorm")
    v = qkv[2 * FOX_HEADS:]
    c = fox_gate_fwd(f_pad, b_pad, f"{tag}_gate")
    c_t = c[:, :FOX_HEADS].T
    cq, ck = c_t[:, :, None], c_t[:, None, :]
    qn, kn = qkn[:FOX_HEADS], qkn[FOX_HEADS:]
    o, lse = attn_fwd(qn, kn, v, cq, ck, True, f"{tag}_attn")
    return o, mq, (qk, gains, qn, kn, v, cq, ck, o, lse, f_pad, b_pad)


def _fox_backward(do, dmq, saved, tag):
    qk, gains, qn, kn, v, cq, ck, o, lse, f_pad, b_pad = saved
    dqn, dkn, dv, dck, dcq = attn_bwd(qn, kn, v, cq, ck, o, lse, do, True, f"{tag}_dattn")
    dqk, dgains = rmsnorm_bwd(qk, gains, jnp.concatenate([dqn, dkn], axis=0), f"{tag}_dqknorm")
    dc = jnp.pad((dck[:, 0, :] + dcq[:, :, 0]).T, ((0, 0), (0, LANES - FOX_HEADS)))
    df, db = fox_gate_bwd(dc, f_pad, b_pad, f"{tag}_dgate")
    dproj = jnp.concatenate([_unheads(dqk), _unheads(dv), _unheads(dmq), df], axis=1)
    dgains = dgains.reshape(2, FOX_HEADS, HEAD_DIM).sum(axis=1)
    return dproj, db[0, :FOX_HEADS], dgains[0], dgains[1]


def _gmlp_forward(h, w_in_t, v_gain, w_s, b_s, tag):
    proj = matmul(h, w_in_t, f"{tag}_proj", transpose_b=True)
    pu = _heads(proj[:, :TOK_WIDTH], FOX_HEADS)
    pv = _heads(proj[:, TOK_WIDTH:2 * TOK_WIDTH], FOX_HEADS)
    mq = _heads(proj[:, 2 * TOK_WIDTH:], MEM_HEADS)
    gain = v_gain.reshape(FOX_HEADS, 1, HEAD_DIM)
    b = b_s[:, :, None]
    o = gmlp_fwd(pu, pv, gain, w_s, b, f"{tag}_sgu")
    return o, mq, (pu, pv, gain, w_s, b)


def _gmlp_backward(do, dmq, saved, tag):
    pu, pv, gain, w_s, b = saved
    dpu, dpv, dw, db, dg = gmlp_bwd(pu, pv, gain, w_s, b, do, f"{tag}_dsgu")
    dproj = jnp.concatenate([_unheads(dpu), _unheads(dpv), _unheads(dmq)], axis=1)
    return dproj, dg.reshape(-1), dw, db[:, :, 0]


BIG = ("ffn1_w_in", "ffn1_w_out", "ffn2_w_in", "ffn2_w_out", "w_out", "mem_w_kv", "fox_w_in", "gmlp_w_in")
COLUMN_SHARDED = ("ffn1_w_in", "ffn2_w_in", "gmlp_w_in")
REPLICATED =("norm_ffn1", "norm_mix", "norm_ffn2", "mem_norm", "mem_q_norm", "mem_k_norm", "fox_b_f",
              "fox_q_norm", "fox_k_norm", "gmlp_w_s", "gmlp_b_s")
WEIGHTS = ("norm_ffn1", "ffn1_w_in", "ffn1_w_out", "norm_mix", "norm_ffn2", "ffn2_w_in", "ffn2_w_out", "w_out",
           "mem_norm", "mem_w_kv", "mem_q_norm", "mem_k_norm", "fox_w_in", "fox_b_f", "fox_q_norm", "fox_k_norm",
           "gmlp_w_in", "gmlp_v_norm", "gmlp_w_s", "gmlp_b_s")


def _to_transport(name, a):
    if name in COLUMN_SHARDED:
        return jnp.swapaxes(a, -1, -2)
    return _fox_split(a) if name == "fox_w_in" else a


def _from_transport(name, a):
    if name in COLUMN_SHARDED:
        return jnp.swapaxes(a, -1, -2)
    return _fox_unsplit(a) if name == "fox_w_in" else a


def kernel(x, mem, norm_ffn1, ffn1_w_in, ffn1_w_out, norm_mix, norm_ffn2, ffn2_w_in, ffn2_w_out, w_out, mem_norm, mem_w_kv, mem_q_norm, mem_k_norm, fox_w_in, fox_b_f, fox_q_norm, fox_k_norm, gmlp_w_in, gmlp_v_norm, gmlp_w_s, gmlp_b_s, loss_target, m_norm_ffn1, m_ffn1_w_in, m_ffn1_w_out, m_norm_mix, m_norm_ffn2, m_ffn2_w_in, m_ffn2_w_out, m_w_out, m_mem_norm, m_mem_w_kv, m_mem_q_norm, m_mem_k_norm, m_fox_w_in, m_fox_b_f, m_fox_q_norm, m_fox_k_norm, m_gmlp_w_in, m_gmlp_v_norm, m_gmlp_w_s, m_gmlp_b_s, v_norm_ffn1, v_ffn1_w_in, v_ffn1_w_out, v_norm_mix, v_norm_ffn2, v_ffn2_w_in, v_ffn2_w_out, v_w_out, v_mem_norm, v_mem_w_kv, v_mem_q_norm, v_mem_k_norm, v_fox_w_in, v_fox_b_f, v_fox_q_norm, v_fox_k_norm, v_gmlp_w_in, v_gmlp_v_norm, v_gmlp_w_s, v_gmlp_b_s):
    w = dict(norm_ffn1=norm_ffn1, ffn1_w_in=ffn1_w_in, ffn1_w_out=ffn1_w_out, norm_mix=norm_mix, norm_ffn2=norm_ffn2, ffn2_w_in=ffn2_w_in, ffn2_w_out=ffn2_w_out, w_out=w_out, mem_norm=mem_norm, mem_w_kv=mem_w_kv, mem_q_norm=mem_q_norm, mem_k_norm=mem_k_norm, fox_w_in=fox_w_in, fox_b_f=fox_b_f, fox_q_norm=fox_q_norm, fox_k_norm=fox_k_norm, gmlp_w_in=gmlp_w_in, gmlp_v_norm=gmlp_v_norm, gmlp_w_s=gmlp_w_s, gmlp_b_s=gmlp_b_s)
    m = dict(norm_ffn1=m_norm_ffn1, ffn1_w_in=m_ffn1_w_in, ffn1_w_out=m_ffn1_w_out, norm_mix=m_norm_mix, norm_ffn2=m_norm_ffn2, ffn2_w_in=m_ffn2_w_in, ffn2_w_out=m_ffn2_w_out, w_out=m_w_out, mem_norm=m_mem_norm, mem_w_kv=m_mem_w_kv, mem_q_norm=m_mem_q_norm, mem_k_norm=m_mem_k_norm, fox_w_in=m_fox_w_in, fox_b_f=m_fox_b_f, fox_q_norm=m_fox_q_norm, fox_k_norm=m_fox_k_norm, gmlp_w_in=m_gmlp_w_in, gmlp_v_norm=m_gmlp_v_norm, gmlp_w_s=m_gmlp_w_s, gmlp_b_s=m_gmlp_b_s)
    v = dict(norm_ffn1=v_norm_ffn1, ffn1_w_in=v_ffn1_w_in, ffn1_w_out=v_ffn1_w_out, norm_mix=v_norm_mix, norm_ffn2=v_norm_ffn2, ffn2_w_in=v_ffn2_w_in, ffn2_w_out=v_ffn2_w_out, w_out=v_w_out, mem_norm=v_mem_norm, mem_w_kv=v_mem_w_kv, mem_q_norm=v_mem_q_norm, mem_k_norm=v_mem_k_norm, fox_w_in=v_fox_w_in, fox_b_f=v_fox_b_f, fox_q_norm=v_fox_q_norm, fox_k_norm=v_fox_k_norm, gmlp_w_in=v_gmlp_w_in, gmlp_v_norm=v_gmlp_v_norm, gmlp_w_s=v_gmlp_w_s, gmlp_b_s=v_gmlp_b_s)

    depth = norm_ffn1.shape[0]
    x0 = x[0]
    mem0 = mem[0]
    target = loss_target[0]
    me = _slot(*_position())

    keys = [(n, i) for n in BIG for i in range(w[n].shape[0])]
    local = {k: _to_transport(k[0], w[k[0]][k[1]]) for k in keys}
    n_gain, gain_len = gmlp_v_norm.shape
    pad_gain = lambda a: jnp.pad(a, ((0, 8 - n_gain), (0, LANES - gain_len)))
    gathered = all_gather_multi([local[k].astype(BF16) for k in keys] + [pad_gain(gmlp_v_norm)], "gather_weights")
    full = {k: g.reshape(-1, g.shape[-1]) for k, g in zip(keys, gathered)}
    v_gain_full = gathered[-1][:, :n_gain, :gain_len]

    mem_n = rmsnorm_fwd(mem0[None], mem_norm.reshape(1, 1, -1), BF16, "mem_norm")[0]
    saved = []
    xi = x0
    for i in range(depth):
        kind, j = i % 2, i // 2
        x1, ffn1_saved = _ffn_forward(xi, norm_ffn1[i], full["ffn1_w_in", i], full["ffn1_w_out", i], f"l{i}_ffn1")
        h = rmsnorm_fwd(x1[None], norm_mix[i].reshape(1, 1, -1), BF16, f"l{i}_mixnorm")[0]
        if kind == 0:
            tok, mq, mix_saved = _fox_forward(h, full["fox_w_in", j], fox_b_f[j], fox_q_norm[j], fox_k_norm[j],
                                              f"l{i}_fox")
        else:
            tok, mq, mix_saved = _gmlp_forward(h, full["gmlp_w_in", j], v_gain_full[:, j, :].reshape(-1), gmlp_w_s[j],
                                               gmlp_b_s[j], f"l{i}_gmlp")
        mo, mem_saved = _mem_forward(mq, mem_n, full["mem_w_kv", i], mem_q_norm[i], mem_k_norm[i], f"l{i}_mem")
        cat = _unheads(jnp.concatenate([tok, mo], axis=0)).astype(BF16)
        x2 = matmul(cat, full["w_out", i], f"l{i}_wout", residual=x1)
        x3, ffn2_saved = _ffn_forward(x2, norm_ffn2[i], full["ffn2_w_in", i], full["ffn2_w_out", i], f"l{i}_ffn2")
        saved.append((ffn1_saved, x1, h, mix_saved, mem_saved, cat, ffn2_saved))
        xi = x3

    dy, loss_part = loss_head(xi, target, "loss_head")
    loss = lax.psum(loss_part, ("x", "y", "c"))

    small = {n: [None] * w[n].shape[0] for n in REPLICATED + ("gmlp_v_norm",) if n != "mem_norm"}
    big = {}
    dmem_n = None
    for i in reversed(range(depth)):
        kind, j = i % 2, i // 2
        ffn1_saved, x1, h, mix_saved, mem_saved, cat, ffn2_saved = saved[i]
        dy, small["norm_ffn2"][i], big["ffn2_w_in", i], big["ffn2_w_out", i] = _ffn_backward(
            dy, ffn2_saved, norm_ffn2[i], full["ffn2_w_in", i], full["ffn2_w_out", i], f"l{i}_ffn2")
        big["w_out", i] = matmul_tn(cat, dy, f"l{i}_dwout", out_dtype=BF16)
        dcat = _heads(matmul(dy, full["w_out", i], f"l{i}_dcat", transpose_b=True), FOX_HEADS + MEM_HEADS)
        dmq, small["mem_q_norm"][i], small["mem_k_norm"][i], big["mem_w_kv", i], dmem_i = _mem_backward(
            dcat[FOX_HEADS:], mem_saved, mem_n, full["mem_w_kv", i], f"l{i}_mem")
        dmem_n = dmem_i if dmem_n is None else dmem_n + dmem_i
        if kind == 0:
            dproj, small["fox_b_f"][j], small["fox_q_norm"][j], small["fox_k_norm"][j] = _fox_backward(
                dcat[:FOX_HEADS], dmq, mix_saved, f"l{i}_fox")
            big["fox_w_in", j] = matmul_tn(h, dproj, f"l{i}_fox_dwin", out_dtype=BF16)
            dh = matmul(dproj, full["fox_w_in", j], f"l{i}_fox_dh", transpose_b=True)
        else:
            dproj, small["gmlp_v_norm"][j], small["gmlp_w_s"][j], small["gmlp_b_s"][j] = _gmlp_backward(
                dcat[:FOX_HEADS], dmq, mix_saved, f"l{i}_gmlp")
            big["gmlp_w_in", j] = matmul_tn(dproj, h, f"l{i}_gmlp_dwin", out_dtype=BF16)
            dh = matmul(dproj, full["gmlp_w_in", j], f"l{i}_gmlp_dh")
        dy, dg_mix = rmsnorm_bwd(x1[None], norm_mix[i].reshape(1, 1, -1), dh[None], f"l{i}_dmixnorm", residual=dy[None])
        dy, small["norm_mix"][i] = dy[0], dg_mix.reshape(-1)
        dy, small["norm_ffn1"][i], big["ffn1_w_in", i], big["ffn1_w_out", i] = _ffn_backward(
            dy, ffn1_saved, norm_ffn1[i], full["ffn1_w_in", i], full["ffn1_w_out", i], f"l{i}_ffn1")
    grad_x = dy[None]
    _, dg_mem = rmsnorm_bwd(mem0[None], mem_norm.reshape(1, 1, -1), dmem_n[None], "dmem_norm")
    small = {n: jnp.stack(g) for n, g in small.items()}
    small["mem_norm"] = dg_mem.reshape(-1)

    got = all_to_all_multi([big[k].reshape((N_DEV, -1, big[k].shape[-1])) for k in keys], "exchange_grads")
    results = {n: [[None] * w[n].shape[0] for _ in range(4)] for n in BIG}
    for k, slots in zip(keys, got):
        n, i = k
        outs = adamw(local[k], slots, _to_transport(n, m[n][i]), _to_transport(n, v[n][i]), f"adamw_{n}_{i}")
        for q in range(4):
            results[n][q][i] = _from_transport(n, outs[q])
    sharded = {n: [jnp.stack(r) for r in results[n]] for n in BIG}

    small_names = REPLICATED + ("gmlp_v_norm",)
    (small_got,) = all_gather_multi([_pack([small[n] for n in small_names])], "gather_small_grads")
    rep_shapes = [w[n].shape for n in REPLICATED]
    gain_seg = jnp.zeros((n_gain, N_DEV * gain_len), F32)
    pack_rep = lambda d: _pack([d[n] for n in REPLICATED] + [gain_seg])
    outs = adamw(pack_rep(w), small_got, pack_rep(m), pack_rep(v), "adamw_replicated")
    replicated = [dict(zip(REPLICATED, _unpack(o, rep_shapes))) for o in outs]
    gain_parts = _unpack(small_got, rep_shapes + [(n_gain, N_DEV * gain_len)], lead=(N_DEV,))[-1]
    gain_slots = lax.dynamic_slice_in_dim(gain_parts, me * gain_len, gain_len, axis=2)
    gain_slots = jnp.pad(gain_slots, ((0, 0), (0, 8 - n_gain), (0, LANES - gain_len)))
    outs = adamw(pad_gain(gmlp_v_norm), gain_slots, pad_gain(m["gmlp_v_norm"]), pad_gain(v["gmlp_v_norm"]),
                 "adamw_gmlp_v_norm")
    sharded["gmlp_v_norm"] = [o[:n_gain, :gain_len] for o in outs]

    out = [loss, grad_x]
    for q in range(4):
        out += [(replicated[q][n] if n in REPLICATED else sharded[n][q]) for n in WEIGHTS]
    return tuple(out)
```

```python
import functools
import math

import jax
import jax.numpy as jnp
from jax import lax
from jax.experimental import pallas as pl
from jax.experimental.pallas import tpu as pltpu

F32 = jnp.float32
BF16 = jnp.bfloat16

EPS = 1e-6
HEAD_DIM = 64
FOX_HEADS = 12
MEM_HEADS = 4
TOK_WIDTH = FOX_HEADS * HEAD_DIM
MEM_WIDTH = MEM_HEADS * HEAD_DIM
CHUNK = 128
LANES = 128
N_DEV = 8
SEG_ALIGN = 16 * LANES
PACK_ROWS = 1024
PACK_ALIGN = PACK_ROWS * LANES

ADAM_LR = 0.001
ADAM_B1 = 0.9
ADAM_B2 = 0.999
ADAM_EPS = 1e-08
ADAM_WD = 0.01
ADAM_STEP = 10

VMEM_LIMIT_BYTES = 48 * 1024 * 1024
MESH = pl.DeviceIdType.MESH
CONTRACT_0 = (((0,), (0,)), ((), ()))
CONTRACT_1 = (((1,), (1,)), ((), ()))


def _params(*semantics):
    return pltpu.CompilerParams(dimension_semantics=semantics, vmem_limit_bytes=VMEM_LIMIT_BYTES)


def _pick(n, candidates):
    for c in candidates:
        if c <= n and n % c == 0:
            return c
    return n


def _sigmoid(x):
    return 1.0 / (1.0 + jnp.exp(-x))


def _row_tile(r, w):
    return _pick(r, (1024,) if w >= 512 else (2048, 1024, 512, 256))


def rmsnorm_fwd(x, gain, out_dtype, name):
    g_, r_, w_ = x.shape
    tr = _row_tile(r_, w_)

    def body(x_ref, g_ref, y_ref):
        xv = x_ref[0].astype(F32)
        r = lax.rsqrt(jnp.mean(xv * xv, axis=-1, keepdims=True) + EPS)
        y_ref[0] = (xv * r * g_ref[0]).astype(y_ref.dtype)

    return pl.pallas_call(
        body, name=name, grid=(g_, r_ // tr),
        in_specs=[pl.BlockSpec((1, tr, w_), lambda g, i: (g, i, 0)),
                  pl.BlockSpec((1, 1, w_), lambda g, i: (g, 0, 0))],
        out_specs=pl.BlockSpec((1, tr, w_), lambda g, i: (g, i, 0)),
        out_shape=jax.ShapeDtypeStruct((g_, r_, w_), out_dtype),
        compiler_params=_params("parallel", "parallel"),
    )(x, gain)


def rmsnorm_bwd(x, gain, dy, name, residual=None):
    g_, r_, w_ = x.shape
    tr = _row_tile(r_, w_)
    has_res = residual is not None

    def body(*refs):
        if has_res:
            x_ref, g_ref, dy_ref, res_ref, dx_ref, dg_ref = refs
        else:
            x_ref, g_ref, dy_ref, dx_ref, dg_ref = refs
        xv = x_ref[0].astype(F32)
        dyv = dy_ref[0].astype(F32)
        r = lax.rsqrt(jnp.mean(xv * xv, axis=-1, keepdims=True) + EPS)
        n = xv * r
        dn = dyv * g_ref[0]
        dx = r * (dn - n * jnp.mean(dn * n, axis=-1, keepdims=True))
        if has_res:
            dx = dx + res_ref[0]
        dx_ref[0] = dx
        part = jnp.sum(dyv * n, axis=0, keepdims=True)

        @pl.when(pl.program_id(1) == 0)
        def _():
            dg_ref[0] = part

        @pl.when(pl.program_id(1) != 0)
        def _():
            dg_ref[0] += part

    row = pl.BlockSpec((1, tr, w_), lambda g, i: (g, i, 0))
    vec = pl.BlockSpec((1, 1, w_), lambda g, i: (g, 0, 0))
    operands = (x, gain, dy) + ((residual,) if has_res else ())
    return pl.pallas_call(
        body, name=name, grid=(g_, r_ // tr),
        in_specs=[row, vec, row] + ([row] if has_res else []),
        out_specs=(row, vec),
        out_shape=(jax.ShapeDtypeStruct((g_, r_, w_), F32), jax.ShapeDtypeStruct((g_, 1, w_), F32)),
        compiler_params=_params("parallel", "arbitrary"),
    )(*operands)


def matmul(a, b, name, out_dtype=F32, residual=None, scale=None, transpose_b=False):
    m_, k_ = a.shape
    n_ = b.shape[0] if transpose_b else b.shape[1]
    tm = _pick(m_, (512, 256, 128))
    tn = _pick(n_, (1408, 1024, 896, 512, 256, 128))
    has_res = residual is not None

    def body(*refs):
        if has_res:
            a_ref, b_ref, res_ref, o_ref = refs
        else:
            a_ref, b_ref, o_ref = refs
        av, bv = a_ref[...].astype(BF16), b_ref[...].astype(BF16)
        if transpose_b:
            acc = lax.dot_general(av, bv, CONTRACT_1, preferred_element_type=F32)
        else:
            acc = jnp.dot(av, bv, preferred_element_type=F32)
        if scale is not None:
            acc = acc * scale
        if has_res:
            acc = acc + res_ref[...]
        o_ref[...] = acc.astype(o_ref.dtype)

    out_spec = pl.BlockSpec((tm, tn), lambda j, i: (i, j))
    b_spec = pl.BlockSpec((tn, k_), lambda j, i: (j, 0)) if transpose_b else pl.BlockSpec((k_, tn), lambda j, i: (0, j))
    operands = (a, b) + ((residual,) if has_res else ())
    return pl.pallas_call(
        body, name=name, grid=(n_ // tn, m_ // tm),
        in_specs=[pl.BlockSpec((tm, k_), lambda j, i: (i, 0)), b_spec] + ([out_spec] if has_res else []),
        out_specs=out_spec,
        out_shape=jax.ShapeDtypeStruct((m_, n_), out_dtype),
        compiler_params=_params("parallel", "parallel"),
    )(*operands)


def matmul_tn(a, b, name, scale=None, out_dtype=F32):
    s_, k_ = a.shape
    n_ = b.shape[1]
    tk = _pick(k_, (1024, 1408, 896, 512, 256, 128))
    tn = _pick(n_, (1408, 1024, 896, 512, 256, 128))
    ts = _pick(s_, (512, 256, 128))
    ns = s_ // ts

    def body(a_ref, b_ref, o_ref, acc_ref):
        part = lax.dot_general(a_ref[...].astype(BF16), b_ref[...].astype(BF16), CONTRACT_0,
                               preferred_element_type=F32)
        step = pl.program_id(2)

        @pl.when(step == 0)
        def _():
            acc_ref[...] = part

        @pl.when(step != 0)
        def _():
            acc_ref[...] += part

        @pl.when(step == ns - 1)
        def _():
            acc = acc_ref[...]
            o_ref[...] = (acc if scale is None else acc * scale).astype(o_ref.dtype)

    return pl.pallas_call(
        body, name=name, grid=(k_ // tk, n_ // tn, ns),
        in_specs=[pl.BlockSpec((ts, tk), lambda i, j, s: (s, i)),
                  pl.BlockSpec((ts, tn), lambda i, j, s: (s, j))],
        out_specs=pl.BlockSpec((tk, tn), lambda i, j, s: (i, j)),
        out_shape=jax.ShapeDtypeStruct((k_, n_), out_dtype),
        scratch_shapes=[pltpu.VMEM((tk, tn), F32)],
        compiler_params=_params("parallel", "parallel", "arbitrary"),
    )(a, b)


def ffn_in(h, w_in_t, name):
    s_, d_ = h.shape
    f_ = w_in_t.shape[0] // 2
    tm = _pick(s_, (512, 256, 128))
    tn = _pick(f_, (1408, 1024, 512, 256, 128))
    nb = f_ // tn

    def body(h_ref, wa_ref, wb_ref, a_ref, b_ref, u_ref):
        hv = h_ref[...]
        a = lax.dot_general(hv, wa_ref[...], CONTRACT_1, preferred_element_type=F32)
        b = lax.dot_general(hv, wb_ref[...], CONTRACT_1, preferred_element_type=F32)
        a_ref[...] = a.astype(BF16)
        b_ref[...] = b.astype(BF16)
        u_ref[...] = (a * _sigmoid(a) * b).astype(BF16)

    o_spec = pl.BlockSpec((tm, tn), lambda j, i: (i, j))
    out = jax.ShapeDtypeStruct((s_, f_), BF16)
    return pl.pallas_call(
        body, name=name, grid=(nb, s_ // tm),
        in_specs=[pl.BlockSpec((tm, d_), lambda j, i: (i, 0)),
                  pl.BlockSpec((tn, d_), lambda j, i: (j, 0)),
                  pl.BlockSpec((tn, d_), lambda j, i: (j + nb, 0))],
        out_specs=(o_spec, o_spec, o_spec), out_shape=(out, out, out),
        compiler_params=_params("parallel", "parallel"),
    )(h, w_in_t, w_in_t)


def ffn_bwd_act(dy, w_out, a, b, name):
    s_, d_ = dy.shape
    f_ = w_out.shape[0]
    tm = _pick(s_, (512, 256, 128))
    tn = _pick(f_, (1408, 1024, 512, 256, 128))

    def body(dy_ref, w_ref, a_ref, b_ref, da_ref, db_ref):
        du = 0.5 * lax.dot_general(dy_ref[...].astype(BF16), w_ref[...], CONTRACT_1, preferred_element_type=F32)
        av = a_ref[...].astype(F32)
        bv = b_ref[...].astype(F32)
        sig = _sigmoid(av)
        da_ref[...] = (du * bv * (sig * (1.0 + av * (1.0 - sig)))).astype(BF16)
        db_ref[...] = (du * (av * sig)).astype(BF16)

    t_spec = pl.BlockSpec((tm, tn), lambda j, i: (i, j))
    out = jax.ShapeDtypeStruct((s_, f_), BF16)
    return pl.pallas_call(
        body, name=name, grid=(f_ // tn, s_ // tm),
        in_specs=[pl.BlockSpec((tm, d_), lambda j, i: (i, 0)),
                  pl.BlockSpec((tn, d_), lambda j, i: (j, 0)), t_spec, t_spec],
        out_specs=(t_spec, t_spec), out_shape=(out, out),
        compiler_params=_params("parallel", "parallel"),
    )(dy, w_out, a, b)


def _fold(a, b, n, forward):
    if forward:
        low = b <= a
        return jnp.where(low, a, n - 1 - a), jnp.where(low, b, b - a - 1)
    low = b < n - a
    return jnp.where(low, a, n - 1 - a), jnp.where(low, a + b, b - 1)


def _attn_grid(h_, n_outer, n_inner, causal, forward):
    if causal and n_outer % 2 == 0:
        return (h_, n_outer // 2, n_outer + 1), lambda a, b: _fold(a, b, n_outer, forward)
    return (h_, n_outer, n_inner), lambda a, b: (a, b)


QK_SCALE = 1.0 / math.sqrt(HEAD_DIM)
SUM_LANE = HEAD_DIM


def _scores(q, k, ck, masked):
    s = lax.dot_general(q, k, CONTRACT_1, preferred_element_type=F32)
    if ck is not None:
        s = s - ck
    if masked:
        row = lax.broadcasted_iota(jnp.int32, s.shape, 0)
        col = lax.broadcasted_iota(jnp.int32, s.shape, 1)
        s = jnp.where(col <= row, s, -jnp.inf)
    return s


def _widen_v(v):
    ones = jnp.ones(v.shape[:2] + (1,), BF16)
    zeros = jnp.zeros(v.shape[:2] + (LANES - HEAD_DIM - 1,), BF16)
    return jnp.concatenate([v.astype(BF16), ones, zeros], axis=-1)


def attn_fwd(q, k, v_wide, ck, causal, name):
    h_, sq, d_ = q.shape
    sk = k.shape[1]
    tq = _pick(sq, (512, 256, 128))
    tk = tq if causal else _pick(sk, (512, 256, 128))
    nq, nk = sq // tq, sk // tk
    assert not causal or nq == 1 or nq % 2 == 0
    bias = ck is not None
    grid, blocks = _attn_grid(h_, nq, nk, causal, True)

    def body(*refs):
        if bias:
            q_ref, k_ref, v_ref, ck_ref, o_ref, lse_ref, m_sc, acc_sc = refs
        else:
            q_ref, k_ref, v_ref, o_ref, lse_ref, m_sc, acc_sc = refs
        i, j = blocks(pl.program_id(1), pl.program_id(2))

        @pl.when(j == 0)
        def _():
            m_sc[...] = jnp.full(m_sc.shape, -jnp.inf, F32)
            acc_sc[...] = jnp.zeros(acc_sc.shape, F32)

        def step(masked):
            s = _scores(q_ref[0], k_ref[0], ck_ref[0] if bias else None, masked)
            m_prev = m_sc[...]
            m_new = jnp.maximum(m_prev, jnp.max(s, axis=1, keepdims=True))
            p = jnp.exp(s - m_new)
            acc_sc[...] = jnp.exp(m_prev - m_new) * acc_sc[...] + jnp.dot(p.astype(BF16), v_ref[0],
                                                                         preferred_element_type=F32)
            m_sc[...] = m_new

        if causal:
            pl.when(j < i)(functools.partial(step, False))
            pl.when(j == i)(functools.partial(step, True))
        else:
            step(False)

        @pl.when(j == (i if causal else nk - 1))
        def _():
            acc = acc_sc[...]
            lane = lax.broadcasted_iota(jnp.int32, acc.shape, 1)
            l = jnp.sum(jnp.where(lane == SUM_LANE, acc, 0.0), axis=1, keepdims=True)
            o_ref[0] = acc_sc[:, :HEAD_DIM] / l
            lse_ref[0] = m_sc[...] + jnp.log(l)

    q_spec = pl.BlockSpec((1, tq, d_), lambda h, a, b: (h, blocks(a, b)[0], 0))
    q1_spec = pl.BlockSpec((1, tq, 1), lambda h, a, b: (h, blocks(a, b)[0], 0))
    k_spec = pl.BlockSpec((1, tk, d_), lambda h, a, b: (h, blocks(a, b)[1], 0))
    in_specs = [q_spec, k_spec, pl.BlockSpec((1, tk, LANES), lambda h, a, b: (h, blocks(a, b)[1], 0))]
    operands = [q, k, v_wide]
    if bias:
        in_specs.append(pl.BlockSpec((1, 1, tk), lambda h, a, b: (h, 0, blocks(a, b)[1])))
        operands.append(ck)
    return pl.pallas_call(
        body, name=name, grid=grid,
        in_specs=in_specs, out_specs=(q_spec, q1_spec),
        out_shape=(jax.ShapeDtypeStruct((h_, sq, d_), F32), jax.ShapeDtypeStruct((h_, sq, 1), F32)),
        scratch_shapes=[pltpu.VMEM((tq, 1), F32), pltpu.VMEM((tq, LANES), F32)],
        compiler_params=_params("parallel", "arbitrary", "arbitrary"),
    )(*operands)


def attn_bwd(q, k, v, ck, o, lse, do, causal, name):
    h_, sq, d_ = q.shape
    sk = k.shape[1]
    tq = _pick(sq, (512, 256, 128))
    tk = tq if causal else _pick(sk, (512, 256, 128))
    nq, nk = sq // tq, sk // tk
    assert not causal or nq == 1 or nq % 2 == 0
    bias = ck is not None
    grid, blocks = _attn_grid(h_, nk, nq, causal, False)

    def body(*refs):
        if bias:
            (q_ref, k_ref, v_ref, ck_ref, o_ref, lse_ref, do_ref,
             dq_ref, dk_ref, dv_ref, dc_ref, dcq_ref, dk_sc, dv_sc, dc_sc) = refs
        else:
            (q_ref, k_ref, v_ref, o_ref, lse_ref, do_ref,
             dq_ref, dk_ref, dv_ref, dk_sc, dv_sc) = refs
        j, i = blocks(pl.program_id(1), pl.program_id(2))

        @pl.when((pl.program_id(1) == 0) & (pl.program_id(2) == 0))
        def _():
            dq_ref[...] = jnp.zeros(dq_ref.shape, F32)
            if bias:
                dcq_ref[...] = jnp.zeros(dcq_ref.shape, F32)

        @pl.when(i == (j if causal else 0))
        def _():
            dk_sc[...] = jnp.zeros(dk_sc.shape, F32)
            dv_sc[...] = jnp.zeros(dv_sc.shape, F32)
            if bias:
                dc_sc[...] = jnp.zeros(dc_sc.shape, F32)

        def step(masked):
            qb, kb = q_ref[0], k_ref[0]
            dof = do_ref[0]
            dob = dof.astype(BF16)
            s = _scores(qb, kb, ck_ref[0] if bias else None, masked)
            p = jnp.exp(s - lse_ref[0])
            dp = lax.dot_general(dob, v_ref[0].astype(BF16), CONTRACT_1, preferred_element_type=F32)
            delta = jnp.sum(dof * o_ref[0], axis=1, keepdims=True)
            ds = p * (dp - delta)
            dsb = ds.astype(BF16)
            dv_sc[...] += lax.dot_general(p.astype(BF16), dob, CONTRACT_0, preferred_element_type=F32)
            dk_sc[...] += lax.dot_general(dsb, qb, CONTRACT_0, preferred_element_type=F32)
            rows = pl.ds(pl.multiple_of(i * tq, tq), tq)
            dq_ref[0, rows, :] += jnp.dot(dsb, kb, preferred_element_type=F32)
            if bias:
                dc_sc[...] -= jnp.sum(ds, axis=0, keepdims=True)
                dcq_ref[0, rows, :] += jnp.sum(ds, axis=1, keepdims=True)

        if causal:
            pl.when(i > j)(functools.partial(step, False))
            pl.when(i == j)(functools.partial(step, True))
        else:
            step(False)

        @pl.when(i == nq - 1)
        def _():
            dk_ref[0] = dk_sc[...]
            dv_ref[0] = dv_sc[...]
            if bias:
                dc_ref[0] = dc_sc[...]

    q_spec = pl.BlockSpec((1, tq, d_), lambda h, a, b: (h, blocks(a, b)[1], 0))
    q1_spec = pl.BlockSpec((1, tq, 1), lambda h, a, b: (h, blocks(a, b)[1], 0))
    k_spec = pl.BlockSpec((1, tk, d_), lambda h, a, b: (h, blocks(a, b)[0], 0))
    c_spec = pl.BlockSpec((1, 1, tk), lambda h, a, b: (h, 0, blocks(a, b)[0]))
    in_specs = [q_spec, k_spec, k_spec] + ([c_spec] if bias else []) + [q_spec, q1_spec, q_spec]
    operands = [q, k, v] + ([ck] if bias else []) + [o, lse, do]
    out_specs = [pl.BlockSpec((1, sq, d_), lambda h, a, b: (h, 0, 0)), k_spec, k_spec]
    out_shape = [jax.ShapeDtypeStruct((h_, sq, d_), F32), jax.ShapeDtypeStruct((h_, sk, d_), F32),
                 jax.ShapeDtypeStruct((h_, sk, d_), F32)]
    scratch = [pltpu.VMEM((tk, d_), F32), pltpu.VMEM((tk, d_), F32)]
    if bias:
        out_specs += [c_spec, pl.BlockSpec((1, sq, 1), lambda h, a, b: (h, 0, 0))]
        out_shape += [jax.ShapeDtypeStruct((h_, 1, sk), F32), jax.ShapeDtypeStruct((h_, sq, 1), F32)]
        scratch.append(pltpu.VMEM((1, tk), F32))
    return pl.pallas_call(
        body, name=name, grid=grid,
        in_specs=in_specs, out_specs=tuple(out_specs), out_shape=tuple(out_shape),
        scratch_shapes=scratch,
        compiler_params=_params("parallel", "arbitrary", "arbitrary"),
    )(*operands)


def _tri(lower):
    row = lax.broadcasted_iota(jnp.int32, (CHUNK, CHUNK), 0)
    col = lax.broadcasted_iota(jnp.int32, (CHUNK, CHUNK), 1)
    return jnp.where((col <= row) if lower else (col >= row), 1.0, 0.0).astype(F32)


def fox_gate_fwd(f, b, name):
    s_ = f.shape[0]

    def body(f_ref, b_ref, c_ref, carry):
        @pl.when(pl.program_id(0) == 0)
        def _():
            carry[...] = jnp.zeros(carry.shape, F32)

        xv = f_ref[...] + b_ref[...]
        log_f = jnp.minimum(xv, 0.0) - jnp.log(1.0 + jnp.exp(-jnp.abs(xv)))
        c = jnp.dot(_tri(True), log_f, precision=lax.Precision.HIGHEST, preferred_element_type=F32) + carry[...]
        c_ref[...] = c
        carry[...] = c[CHUNK - 1:CHUNK, :]

    blk = pl.BlockSpec((CHUNK, LANES), lambda i: (i, 0))
    return pl.pallas_call(
        body, name=name, grid=(s_ // CHUNK,),
        in_specs=[blk, pl.BlockSpec((1, LANES), lambda i: (0, 0))], out_specs=blk,
        out_shape=jax.ShapeDtypeStruct((s_, LANES), F32),
        scratch_shapes=[pltpu.VMEM((1, LANES), F32)],
        compiler_params=_params("arbitrary"),
    )(f, b)


def fox_gate_bwd(dc, f, b, name):
    s_ = f.shape[0]
    n = s_ // CHUNK

    def body(dc_ref, f_ref, b_ref, df_ref, db_ref, carry):
        @pl.when(pl.program_id(0) == 0)
        def _():
            carry[...] = jnp.zeros(carry.shape, F32)
            db_ref[...] = jnp.zeros(db_ref.shape, F32)

        dlog = jnp.dot(_tri(False), dc_ref[...], precision=lax.Precision.HIGHEST,
                       preferred_element_type=F32) + carry[...]
        df = dlog * _sigmoid(-(f_ref[...] + b_ref[...]))
        df_ref[...] = df
        db_ref[...] += jnp.sum(df, axis=0, keepdims=True)
        carry[...] = dlog[0:1, :]

    blk = pl.BlockSpec((CHUNK, LANES), lambda i: (n - 1 - i, 0))
    vec = pl.BlockSpec((1, LANES), lambda i: (0, 0))
    return pl.pallas_call(
        body, name=name, grid=(n,),
        in_specs=[blk, blk, vec], out_specs=(blk, vec),
        out_shape=(jax.ShapeDtypeStruct((s_, LANES), F32), jax.ShapeDtypeStruct((1, LANES), F32)),
        scratch_shapes=[pltpu.VMEM((1, LANES), F32)],
        compiler_params=_params("arbitrary"),
    )(dc, f, b)


GELU_K = math.sqrt(2.0 / math.pi)
GELU_C = 0.044715


def _gelu(x):
    return 0.5 * x * (1.0 + jnp.tanh(GELU_K * (x + GELU_C * (x * x * x))))


def _gelu_grad(x):
    t = jnp.tanh(GELU_K * (x + GELU_C * (x * x * x)))
    return 0.5 * (1.0 + t) + 0.5 * x * (1.0 - t * t) * (GELU_K * (1.0 + 3.0 * GELU_C * (x * x)))


def _tril_mask():
    row = lax.broadcasted_iota(jnp.int32, (CHUNK, CHUNK), 0)
    col = lax.broadcasted_iota(jnp.int32, (CHUNK, CHUNK), 1)
    return col <= row


def _gmlp_specs(s_, ts):
    row = pl.BlockSpec((1, ts, HEAD_DIM), lambda g, i: (g, i, 0))
    gain = pl.BlockSpec((1, 1, HEAD_DIM), lambda g, i: (g, 0, 0))
    w = pl.BlockSpec((1, CHUNK, CHUNK), lambda g, i: (g, 0, 0))
    b = pl.BlockSpec((1, CHUNK, 1), lambda g, i: (g, 0, 0))
    return row, gain, w, b


def gmlp_fwd(pu, pv, gain, w, b, name):
    g_, s_, d_ = pu.shape
    ts = _pick(s_, (1024, 512, 256, 128))

    def body(pu_ref, pv_ref, g_ref, w_ref, b_ref, o_ref):
        v = _gelu(pv_ref[0])
        r = lax.rsqrt(jnp.mean(v * v, axis=-1, keepdims=True) + EPS)
        vn = (v * r * g_ref[0]).astype(BF16)
        wt = jnp.where(_tril_mask(), w_ref[0], 0.0).astype(BF16)
        for c in range(ts // CHUNK):
            rows = pl.ds(c * CHUNK, CHUNK)
            gate = jnp.dot(wt, vn[c * CHUNK:(c + 1) * CHUNK], preferred_element_type=F32) + b_ref[0]
            o_ref[0, rows, :] = _gelu(pu_ref[0, rows, :]) * gate

    row, gspec, wspec, bspec = _gmlp_specs(s_, ts)
    return pl.pallas_call(
        body, name=name, grid=(g_, s_ // ts),
        in_specs=[row, row, gspec, wspec, bspec], out_specs=row,
        out_shape=jax.ShapeDtypeStruct((g_, s_, d_), F32),
        compiler_params=_params("parallel", "parallel"),
    )(pu, pv, gain, w, b)


def gmlp_bwd(pu, pv, gain, w, b, dout, name):
    g_, s_, d_ = pu.shape
    ts = _pick(s_, (1024, 512, 256, 128))

    def body(pu_ref, pv_ref, g_ref, w_ref, b_ref, do_ref, dpu_ref, dpv_ref, dw_ref, db_ref, dg_ref):
        @pl.when(pl.program_id(1) == 0)
        def _():
            dw_ref[...] = jnp.zeros(dw_ref.shape, F32)
            db_ref[...] = jnp.zeros(db_ref.shape, F32)
            dg_ref[...] = jnp.zeros(dg_ref.shape, F32)

        gain_v = g_ref[0]
        mask = _tril_mask()
        wt = jnp.where(mask, w_ref[0], 0.0).astype(BF16)
        dw = jnp.zeros((CHUNK, CHUNK), F32)
        db = jnp.zeros((CHUNK, 1), F32)
        dg = jnp.zeros((1, d_), F32)
        for c in range(ts // CHUNK):
            rows = pl.ds(c * CHUNK, CHUNK)
            pu_c = pu_ref[0, rows, :]
            pv_c = pv_ref[0, rows, :]
            do_c = do_ref[0, rows, :]
            u = _gelu(pu_c)
            v = _gelu(pv_c)
            r = lax.rsqrt(jnp.mean(v * v, axis=-1, keepdims=True) + EPS)
            n = v * r
            vn = (n * gain_v).astype(BF16)
            gate = jnp.dot(wt, vn, preferred_element_type=F32) + b_ref[0]
            dgate = do_c * u
            dgate_b = dgate.astype(BF16)
            dpu_ref[0, rows, :] = do_c * gate * _gelu_grad(pu_c)
            db = db + jnp.sum(dgate, axis=1, keepdims=True)
            dw = dw + lax.dot_general(dgate_b, vn, CONTRACT_1, preferred_element_type=F32)
            dvn = lax.dot_general(wt, dgate_b, CONTRACT_0, preferred_element_type=F32)
            dg = dg + jnp.sum(dvn * n, axis=0, keepdims=True)
            dn = dvn * gain_v
            dv = r * (dn - n * jnp.mean(dn * n, axis=-1, keepdims=True))
            dpv_ref[0, rows, :] = dv * _gelu_grad(pv_c)
        dw_ref[0] += jnp.where(mask, dw, 0.0)
        db_ref[0] += db
        dg_ref[0] += dg

    row, gspec, wspec, bspec = _gmlp_specs(s_, ts)
    return pl.pallas_call(
        body, name=name, grid=(g_, s_ // ts),
        in_specs=[row, row, gspec, wspec, bspec, row],
        out_specs=(row, row, wspec, bspec, gspec),
        out_shape=(jax.ShapeDtypeStruct((g_, s_, d_), F32), jax.ShapeDtypeStruct((g_, s_, d_), F32),
                   jax.ShapeDtypeStruct((g_, CHUNK, CHUNK), F32), jax.ShapeDtypeStruct((g_, CHUNK, 1), F32),
                   jax.ShapeDtypeStruct((g_, 1, d_), F32)),
        compiler_params=_params("parallel", "arbitrary"),
    )(pu, pv, gain, w, b, dout)


def loss_head(y, t, name):
    s_, d_ = y.shape
    tr = _pick(s_, (1024, 512, 256, 128))

    def body(y_ref, t_ref, dy_ref, l_ref):
        err = y_ref[...] - t_ref[...]
        dy_ref[...] = err * (1.0 / d_)
        part = jnp.full(l_ref.shape, jnp.sum(err * err) * (0.5 / d_), F32)

        @pl.when(pl.program_id(0) == 0)
        def _():
            l_ref[...] = part

        @pl.when(pl.program_id(0) != 0)
        def _():
            l_ref[...] += part

    blk = pl.BlockSpec((tr, d_), lambda i: (i, 0))
    dy, l = pl.pallas_call(
        body, name=name, grid=(s_ // tr,),
        in_specs=[blk, blk], out_specs=(blk, pl.BlockSpec((8, LANES), lambda i: (0, 0))),
        out_shape=(jax.ShapeDtypeStruct((s_, d_), F32), jax.ShapeDtypeStruct((8, LANES), F32)),
        compiler_params=_params("arbitrary"),
    )(y, t)
    return dy, l[0, 0]


def adamw(w, g_slots, m, v, name):
    r_, c_ = w.shape
    tr = _pick(r_, (1024, 512, 352, 256, 224, 128, 64, 32, 16, 8))

    def body(w_ref, gs_ref, m_ref, v_ref, g_ref, d_ref, nm_ref, nv_ref):
        g = gs_ref[0].astype(F32)
        for k in range(1, N_DEV):
            g = g + gs_ref[k].astype(F32)
        m_new = ADAM_B1 * m_ref[...] + (1.0 - ADAM_B1) * g
        v_new = ADAM_B2 * v_ref[...] + (1.0 - ADAM_B2) * (g * g)
        m_hat = m_new / (1.0 - ADAM_B1 ** ADAM_STEP)
        v_hat = v_new / (1.0 - ADAM_B2 ** ADAM_STEP)
        g_ref[...] = g
        d_ref[...] = -ADAM_LR * (m_hat / (jnp.sqrt(v_hat) + ADAM_EPS) + ADAM_WD * w_ref[...])
        nm_ref[...] = m_new
        nv_ref[...] = v_new

    blk = pl.BlockSpec((tr, c_), lambda i: (i, 0))
    out = jax.ShapeDtypeStruct((r_, c_), F32)
    return pl.pallas_call(
        body, name=name, grid=(r_ // tr,),
        in_specs=[blk, pl.BlockSpec((N_DEV, tr, c_), lambda i: (0, i, 0)), blk, blk],
        out_specs=(blk, blk, blk, blk), out_shape=(out, out, out, out),
        compiler_params=_params("parallel"),
    )(w, g_slots, m, v)


def _position():
    x, y, c = lax.axis_index("x"), lax.axis_index("y"), lax.axis_index("c")
    return x, y, c


def _slot(px, py, pc):
    return 4 * px + 2 * py + pc


def all_gather_multi(blocks, name):
    n = len(blocks)

    def body(*refs):
        x_refs, out_refs = refs[:n], refs[n:2 * n]
        send_sems, recv_sems, local_sems = refs[2 * n:]
        x, y, c = _position()
        me, sibling = (x, y, c), (x, y, 1 - c)
        chips = [(1 - x, y), (x, 1 - y), (1 - x, 1 - y)]

        def copy(b, k, owner, to, src=None):
            dst = out_refs[b].at[_slot(*owner)]
            return pltpu.make_async_remote_copy(
                src_ref=dst if src is None else src, dst_ref=dst,
                send_sem=send_sems.at[b, k], recv_sem=recv_sems.at[b, k], device_id=to, device_id_type=MESH)

        mine = [pltpu.make_async_copy(x_refs[b], out_refs[b].at[_slot(*me)], local_sems.at[b]) for b in range(n)]
        for cp in mine:
            cp.start()
        first = [copy(b, 1 + j, me, (*chip, c), src=x_refs[b]) for j, chip in enumerate(chips) for b in range(n)]
        first += [copy(b, 0, me, sibling, src=x_refs[b]) for b in range(n)]
        for cp in first:
            cp.start()
        passed = []
        for j, chip in enumerate(chips):
            for b in range(n):
                copy(b, 1 + j, (*chip, c), me).wait_recv()
                cp = copy(b, 4 + j, (*chip, c), sibling)
                cp.start()
                passed.append(cp)
        for b in range(n):
            copy(b, 0, sibling, me).wait_recv()
        for j, chip in enumerate(chips):
            for b in range(n):
                copy(b, 4 + j, (*chip, 1 - c), me).wait_recv()
        for cp in first + passed:
            cp.wait_send()
        for cp in mine:
            cp.wait()

    any_spec = pl.BlockSpec(memory_space=pl.ANY)
    return pl.pallas_call(
        body, name=name,
        in_specs=[any_spec] * n, out_specs=tuple([any_spec] * n),
        out_shape=tuple(jax.ShapeDtypeStruct((N_DEV,) + blk.shape, blk.dtype) for blk in blocks),
        scratch_shapes=[pltpu.SemaphoreType.DMA((n, 7)), pltpu.SemaphoreType.DMA((n, 7)), pltpu.SemaphoreType.DMA((n,))],
    )(*blocks)


def all_to_all_multi(slot_bufs, name):
    n = len(slot_bufs)

    def body(*refs):
        in_refs, out_refs = refs[:n], refs[n:2 * n]
        send_sems, recv_sems, local_sems = refs[2 * n:]
        x, y, c = _position()
        me = _slot(x, y, c)
        mine = [pltpu.make_async_copy(in_refs[b].at[me], out_refs[b].at[me], local_sems.at[b]) for b in range(n)]
        for cp in mine:
            cp.start()
        copies = []
        for k in range(1, N_DEV):
            px = 1 - x if k & 4 else x
            py = 1 - y if k & 2 else y
            pc = 1 - c if k & 1 else c
            for b in range(n):
                cp = pltpu.make_async_remote_copy(
                    src_ref=in_refs[b].at[_slot(px, py, pc)], dst_ref=out_refs[b].at[me],
                    send_sem=send_sems.at[b, k - 1], recv_sem=recv_sems.at[b, k - 1],
                    device_id=(px, py, pc), device_id_type=MESH)
                cp.start()
                copies.append(cp)
        for cp in copies:
            cp.wait()
        for cp in mine:
            cp.wait()

    any_spec = pl.BlockSpec(memory_space=pl.ANY)
    return pl.pallas_call(
        body, name=name,
        in_specs=[any_spec] * n, out_specs=tuple([any_spec] * n),
        out_shape=tuple(jax.ShapeDtypeStruct(buf.shape, buf.dtype) for buf in slot_bufs),
        scratch_shapes=[pltpu.SemaphoreType.DMA((n, 7)), pltpu.SemaphoreType.DMA((n, 7)), pltpu.SemaphoreType.DMA((n,))],
    )(*slot_bufs)


def _seg_len(n):
    return -(-n // SEG_ALIGN) * SEG_ALIGN


def _pack(arrays, lead=()):
    parts, total = [], 0
    for a in arrays:
        flat = a.reshape(lead + (-1,))
        pad = _seg_len(flat.shape[-1]) - flat.shape[-1]
        parts.append(jnp.pad(flat, [(0, 0)] * len(lead) + [(0, pad)]) if pad else flat)
        total += flat.shape[-1] + pad
    tail = -(-total // PACK_ALIGN) * PACK_ALIGN - total
    if tail:
        parts.append(jnp.zeros(lead + (tail,), parts[0].dtype))
    return jnp.concatenate(parts, axis=-1).reshape(lead + (-1, LANES))


def _unpack(packed, shapes, lead=()):
    flat = packed.reshape(lead + (-1,))
    out, off = [], 0
    for shp in shapes:
        n = math.prod(shp)
        out.append(flat[..., off:off + n].reshape(lead + tuple(shp)))
        off += _seg_len(n)
    return out


def _heads(a, n_heads):
    return a.reshape(a.shape[0], n_heads, HEAD_DIM).transpose(1, 0, 2)


def _unheads(a):
    return a.transpose(1, 0, 2).reshape(a.shape[1], a.shape[0] * HEAD_DIM)


def _head_gain(g, n_heads):
    return jnp.broadcast_to(g.reshape(1, 1, HEAD_DIM), (n_heads, 1, HEAD_DIM))


def _ffn_forward(x, gain, w_in_t, w_out, tag):
    h = rmsnorm_fwd(x[None], gain.reshape(1, 1, -1), BF16, f"{tag}_norm")[0]
    a, b, u = ffn_in(h, w_in_t, f"{tag}_in")
    y = matmul(u, w_out, f"{tag}_out", residual=x, scale=0.5)
    return y, (x, h, a, b, u)


def _ffn_backward(dy, saved, gain, w_in_t, w_out, tag):
    x, h, a, b, u = saved
    f_ = w_out.shape[0]
    dw_out = matmul_tn(u, dy, f"{tag}_dwout", scale=0.5, out_dtype=BF16)
    da, db = ffn_bwd_act(dy, w_out, a, b, f"{tag}_dact")
    dh = matmul(da, w_in_t[:f_], f"{tag}_dh_a")
    dh = matmul(db, w_in_t[f_:], f"{tag}_dh_b", residual=dh)
    dw_in_t = jnp.concatenate([matmul_tn(da, h, f"{tag}_dwin_a", out_dtype=BF16),
                               matmul_tn(db, h, f"{tag}_dwin_b", out_dtype=BF16)], axis=0)
    dx, dgain = rmsnorm_bwd(x[None], gain.reshape(1, 1, -1), dh[None], f"{tag}_dnorm", residual=dy[None])
    return dx[0], dgain.reshape(-1), dw_in_t, dw_out


def _mem_forward(mq, mem_n, w_kv, g_q, g_k, tag):
    gq = _head_gain(g_q, MEM_HEADS)
    gk = _head_gain(g_k, MEM_HEADS)
    qn = rmsnorm_fwd(mq, gq * QK_SCALE, BF16, f"{tag}_qnorm")
    kv = matmul(mem_n, w_kv, f"{tag}_kv")
    k = _heads(kv[:, :MEM_WIDTH], MEM_HEADS)
    v = _heads(kv[:, MEM_WIDTH:], MEM_HEADS)
    kn = rmsnorm_fwd(k, gk, BF16, f"{tag}_knorm")
    o, lse = attn_fwd(qn, kn, _widen_v(v), None, False, f"{tag}_attn")
    return o, (mq, gq, gk, qn, k, kn, v, o, lse)


def _mem_backward(do, saved, mem_n, w_kv, tag):
    mq, gq, gk, qn, k, kn, v, o, lse = saved
    dqn, dkn, dv = attn_bwd(qn, kn, v, None, o, lse, do, False, f"{tag}_dattn")
    dmq, dgq = rmsnorm_bwd(mq, gq, dqn * QK_SCALE, f"{tag}_dqnorm")
    dk, dgk = rmsnorm_bwd(k, gk, dkn, f"{tag}_dknorm")
    dkv = jnp.concatenate([_unheads(dk), _unheads(dv)], axis=1)
    dw_kv = matmul_tn(mem_n, dkv, f"{tag}_dwkv", out_dtype=BF16)
    dmem_n = matmul(dkv, w_kv, f"{tag}_dmem", transpose_b=True)
    return dmq, dgq.sum(axis=0).reshape(-1), dgk.sum(axis=0).reshape(-1), dw_kv, dmem_n


def _fox_split(w_in):
    t3 = 3 * TOK_WIDTH
    pad = jnp.zeros(w_in.shape[:-1] + (LANES - FOX_HEADS,), w_in.dtype)
    return jnp.concatenate([w_in[..., :t3], w_in[..., t3 + FOX_HEADS:], w_in[..., t3:t3 + FOX_HEADS], pad], axis=-1)


def _fox_unsplit(w):
    t3 = 3 * TOK_WIDTH
    return jnp.concatenate([w[..., :t3], w[..., t3 + MEM_WIDTH:t3 + MEM_WIDTH + FOX_HEADS], w[..., t3:t3 + MEM_WIDTH]],
                           axis=-1)


def _fox_forward(h, w_split, b_f, g_q, g_k, tag):
    t3 = 3 * TOK_WIDTH
    proj = matmul(h, w_split, f"{tag}_proj")
    qkv = _heads(proj[:, :t3], 3 * FOX_HEADS)
    mq = _heads(proj[:, t3:t3 + MEM_WIDTH], MEM_HEADS)
    f_pad = proj[:, t3 + MEM_WIDTH:]
    b_pad = jnp.pad(b_f.reshape(1, -1), ((0, 0), (0, LANES - FOX_HEADS)))
    gains = jnp.concatenate([_head_gain(g_q, FOX_HEADS), _head_gain(g_k, FOX_HEADS)], axis=0)
    qk = qkv[:2 * FOX_HEADS]
    scaled = jnp.concatenate([gains[:FOX_HEADS] * QK_SCALE, gains[FOX_HEADS:]], axis=0)
    qkn = rmsnorm_fwd(qk, scaled, BF16, f"{tag}_qknorm")
    v = qkv[2 * FOX_HEADS:]
    c = fox_gate_fwd(f_pad, b_pad, f"{tag}_gate")
    ck = c[:, :FOX_HEADS].T[:, None, :]
    qn, kn = qkn[:FOX_HEADS], qkn[FOX_HEADS:]
    o, lse = attn_fwd(qn, kn, _widen_v(v), ck, True, f"{tag}_attn")
    return o, mq, (qk, gains, qn, kn, v, ck, o, lse, f_pad, b_pad)


def _fox_backward(do, dmq, saved, tag):
    qk, gains, qn, kn, v, ck, o, lse, f_pad, b_pad = saved
    dqn, dkn, dv, dck, dcq = attn_bwd(qn, kn, v, ck, o, lse, do, True, f"{tag}_dattn")
    dqk, dgains = rmsnorm_bwd(qk, gains, jnp.concatenate([dqn * QK_SCALE, dkn], axis=0), f"{tag}_dqknorm")
    dc = jnp.pad((dck[:, 0, :] + dcq[:, :, 0]).T, ((0, 0), (0, LANES - FOX_HEADS)))
    df, db = fox_gate_bwd(dc, f_pad, b_pad, f"{tag}_dgate")
    dproj = jnp.concatenate([_unheads(dqk), _unheads(dv), _unheads(dmq), df], axis=1)
    dgains = dgains.reshape(2, FOX_HEADS, HEAD_DIM).sum(axis=1)
    return dproj, db[0, :FOX_HEADS], dgains[0], dgains[1]


def _gmlp_forward(h, w_in_t, v_gain, w_s, b_s, tag):
    proj = matmul(h, w_in_t, f"{tag}_proj", transpose_b=True)
    pu = _heads(proj[:, :TOK_WIDTH], FOX_HEADS)
    pv = _heads(proj[:, TOK_WIDTH:2 * TOK_WIDTH], FOX_HEADS)
    mq = _heads(proj[:, 2 * TOK_WIDTH:], MEM_HEADS)
    gain = v_gain.reshape(FOX_HEADS, 1, HEAD_DIM)
    b = b_s[:, :, None]
    o = gmlp_fwd(pu, pv, gain, w_s, b, f"{tag}_sgu")
    return o, mq, (pu, pv, gain, w_s, b)


def _gmlp_backward(do, dmq, saved, tag):
    pu, pv, gain, w_s, b = saved
    dpu, dpv, dw, db, dg = gmlp_bwd(pu, pv, gain, w_s, b, do, f"{tag}_dsgu")
    dproj = jnp.concatenate([_unheads(dpu), _unheads(dpv), _unheads(dmq)], axis=1)
    return dproj, dg.reshape(-1), dw, db[:, :, 0]


BIG = ("ffn1_w_in", "ffn1_w_out", "ffn2_w_in", "ffn2_w_out", "w_out", "mem_w_kv", "fox_w_in", "gmlp_w_in")
COLUMN_SHARDED = ("ffn1_w_in", "ffn2_w_in", "gmlp_w_in")
REPLICATED =("norm_ffn1", "norm_mix", "norm_ffn2", "mem_norm", "mem_q_norm", "mem_k_norm", "fox_b_f",
              "fox_q_norm", "fox_k_norm", "gmlp_w_s", "gmlp_b_s")
WEIGHTS = ("norm_ffn1", "ffn1_w_in", "ffn1_w_out", "norm_mix", "norm_ffn2", "ffn2_w_in", "ffn2_w_out", "w_out",
           "mem_norm", "mem_w_kv", "mem_q_norm", "mem_k_norm", "fox_w_in", "fox_b_f", "fox_q_norm", "fox_k_norm",
           "gmlp_w_in", "gmlp_v_norm", "gmlp_w_s", "gmlp_b_s")


def _to_transport(name, a):
    if name in COLUMN_SHARDED:
        return jnp.swapaxes(a, -1, -2)
    return _fox_split(a) if name == "fox_w_in" else a


def _from_transport(name, a):
    if name in COLUMN_SHARDED:
        return jnp.swapaxes(a, -1, -2)
    return _fox_unsplit(a) if name == "fox_w_in" else a


def kernel(x, mem, norm_ffn1, ffn1_w_in, ffn1_w_out, norm_mix, norm_ffn2, ffn2_w_in, ffn2_w_out, w_out, mem_norm, mem_w_kv, mem_q_norm, mem_k_norm, fox_w_in, fox_b_f, fox_q_norm, fox_k_norm, gmlp_w_in, gmlp_v_norm, gmlp_w_s, gmlp_b_s, loss_target, m_norm_ffn1, m_ffn1_w_in, m_ffn1_w_out, m_norm_mix, m_norm_ffn2, m_ffn2_w_in, m_ffn2_w_out, m_w_out, m_mem_norm, m_mem_w_kv, m_mem_q_norm, m_mem_k_norm, m_fox_w_in, m_fox_b_f, m_fox_q_norm, m_fox_k_norm, m_gmlp_w_in, m_gmlp_v_norm, m_gmlp_w_s, m_gmlp_b_s, v_norm_ffn1, v_ffn1_w_in, v_ffn1_w_out, v_norm_mix, v_norm_ffn2, v_ffn2_w_in, v_ffn2_w_out, v_w_out, v_mem_norm, v_mem_w_kv, v_mem_q_norm, v_mem_k_norm, v_fox_w_in, v_fox_b_f, v_fox_q_norm, v_fox_k_norm, v_gmlp_w_in, v_gmlp_v_norm, v_gmlp_w_s, v_gmlp_b_s):
    w = dict(norm_ffn1=norm_ffn1, ffn1_w_in=ffn1_w_in, ffn1_w_out=ffn1_w_out, norm_mix=norm_mix, norm_ffn2=norm_ffn2, ffn2_w_in=ffn2_w_in, ffn2_w_out=ffn2_w_out, w_out=w_out, mem_norm=mem_norm, mem_w_kv=mem_w_kv, mem_q_norm=mem_q_norm, mem_k_norm=mem_k_norm, fox_w_in=fox_w_in, fox_b_f=fox_b_f, fox_q_norm=fox_q_norm, fox_k_norm=fox_k_norm, gmlp_w_in=gmlp_w_in, gmlp_v_norm=gmlp_v_norm, gmlp_w_s=gmlp_w_s, gmlp_b_s=gmlp_b_s)
    m = dict(norm_ffn1=m_norm_ffn1, ffn1_w_in=m_ffn1_w_in, ffn1_w_out=m_ffn1_w_out, norm_mix=m_norm_mix, norm_ffn2=m_norm_ffn2, ffn2_w_in=m_ffn2_w_in, ffn2_w_out=m_ffn2_w_out, w_out=m_w_out, mem_norm=m_mem_norm, mem_w_kv=m_mem_w_kv, mem_q_norm=m_mem_q_norm, mem_k_norm=m_mem_k_norm, fox_w_in=m_fox_w_in, fox_b_f=m_fox_b_f, fox_q_norm=m_fox_q_norm, fox_k_norm=m_fox_k_norm, gmlp_w_in=m_gmlp_w_in, gmlp_v_norm=m_gmlp_v_norm, gmlp_w_s=m_gmlp_w_s, gmlp_b_s=m_gmlp_b_s)
    v = dict(norm_ffn1=v_norm_ffn1, ffn1_w_in=v_ffn1_w_in, ffn1_w_out=v_ffn1_w_out, norm_mix=v_norm_mix, norm_ffn2=v_norm_ffn2, ffn2_w_in=v_ffn2_w_in, ffn2_w_out=v_ffn2_w_out, w_out=v_w_out, mem_norm=v_mem_norm, mem_w_kv=v_mem_w_kv, mem_q_norm=v_mem_q_norm, mem_k_norm=v_mem_k_norm, fox_w_in=v_fox_w_in, fox_b_f=v_fox_b_f, fox_q_norm=v_fox_q_norm, fox_k_norm=v_fox_k_norm, gmlp_w_in=v_gmlp_w_in, gmlp_v_norm=v_gmlp_v_norm, gmlp_w_s=v_gmlp_w_s, gmlp_b_s=v_gmlp_b_s)

    depth = norm_ffn1.shape[0]
    x0 = x[0]
    mem0 = mem[0]
    target = loss_target[0]
    me = _slot(*_position())

    keys = [(n, i) for n in BIG for i in range(w[n].shape[0])]
    local = {k: _to_transport(k[0], w[k[0]][k[1]]) for k in keys}
    n_gain, gain_len = gmlp_v_norm.shape
    pad_gain = lambda a: jnp.pad(a, ((0, 8 - n_gain), (0, LANES - gain_len)))
    gathered = all_gather_multi([local[k].astype(BF16) for k in keys] + [pad_gain(gmlp_v_norm)], "gather_weights")
    full = {k: g.reshape(-1, g.shape[-1]) for k, g in zip(keys, gathered)}
    v_gain_full = gathered[-1][:, :n_gain, :gain_len]

    mem_n = rmsnorm_fwd(mem0[None], mem_norm.reshape(1, 1, -1), BF16, "mem_norm")[0]
    saved = []
    xi = x0
    for i in range(depth):
        kind, j = i % 2, i // 2
        x1, ffn1_saved = _ffn_forward(xi, norm_ffn1[i], full["ffn1_w_in", i], full["ffn1_w_out", i], f"l{i}_ffn1")
        h = rmsnorm_fwd(x1[None], norm_mix[i].reshape(1, 1, -1), BF16, f"l{i}_mixnorm")[0]
        if kind == 0:
            tok, mq, mix_saved = _fox_forward(h, full["fox_w_in", j], fox_b_f[j], fox_q_norm[j], fox_k_norm[j],
                                              f"l{i}_fox")
        else:
            tok, mq, mix_saved = _gmlp_forward(h, full["gmlp_w_in", j], v_gain_full[:, j, :].reshape(-1), gmlp_w_s[j],
                                               gmlp_b_s[j], f"l{i}_gmlp")
        mo, mem_saved = _mem_forward(mq, mem_n, full["mem_w_kv", i], mem_q_norm[i], mem_k_norm[i], f"l{i}_mem")
        cat = _unheads(jnp.concatenate([tok, mo], axis=0)).astype(BF16)
        x2 = matmul(cat, full["w_out", i], f"l{i}_wout", residual=x1)
        x3, ffn2_saved = _ffn_forward(x2, norm_ffn2[i], full["ffn2_w_in", i], full["ffn2_w_out", i], f"l{i}_ffn2")
        saved.append((ffn1_saved, x1, h, mix_saved, mem_saved, cat, ffn2_saved))
        xi = x3

    dy, loss_part = loss_head(xi, target, "loss_head")
    loss = lax.psum(loss_part, ("x", "y", "c"))

    small = {n: [None] * w[n].shape[0] for n in REPLICATED + ("gmlp_v_norm",) if n != "mem_norm"}
    big = {}
    dmem_n = None
    for i in reversed(range(depth)):
        kind, j = i % 2, i // 2
        ffn1_saved, x1, h, mix_saved, mem_saved, cat, ffn2_saved = saved[i]
        dy, small["norm_ffn2"][i], big["ffn2_w_in", i], big["ffn2_w_out", i] = _ffn_backward(
            dy, ffn2_saved, norm_ffn2[i], full["ffn2_w_in", i], full["ffn2_w_out", i], f"l{i}_ffn2")
        big["w_out", i] = matmul_tn(cat, dy, f"l{i}_dwout", out_dtype=BF16)
        dcat = _heads(matmul(dy, full["w_out", i], f"l{i}_dcat", transpose_b=True), FOX_HEADS + MEM_HEADS)
        dmq, small["mem_q_norm"][i], small["mem_k_norm"][i], big["mem_w_kv", i], dmem_i = _mem_backward(
            dcat[FOX_HEADS:], mem_saved, mem_n, full["mem_w_kv", i], f"l{i}_mem")
        dmem_n = dmem_i if dmem_n is None else dmem_n + dmem_i
        if kind == 0:
            dproj, small["fox_b_f"][j], small["fox_q_norm"][j], small["fox_k_norm"][j] = _fox_backward(
                dcat[:FOX_HEADS], dmq, mix_saved, f"l{i}_fox")
            big["fox_w_in", j] = matmul_tn(h, dproj, f"l{i}_fox_dwin", out_dtype=BF16)
            dh = matmul(dproj, full["fox_w_in", j], f"l{i}_fox_dh", transpose_b=True)
        else:
            dproj, small["gmlp_v_norm"][j], small["gmlp_w_s"][j], small["gmlp_b_s"][j] = _gmlp_backward(
                dcat[:FOX_HEADS], dmq, mix_saved, f"l{i}_gmlp")
            big["gmlp_w_in", j] = matmul_tn(dproj, h, f"l{i}_gmlp_dwin", out_dtype=BF16)
            dh = matmul(dproj, full["gmlp_w_in", j], f"l{i}_gmlp_dh")
        dy, dg_mix = rmsnorm_bwd(x1[None], norm_mix[i].reshape(1, 1, -1), dh[None], f"l{i}_dmixnorm", residual=dy[None])
        dy, small["norm_mix"][i] = dy[0], dg_mix.reshape(-1)
        dy, small["norm_ffn1"][i], big["ffn1_w_in", i], big["ffn1_w_out", i] = _ffn_backward(
            dy, ffn1_saved, norm_ffn1[i], full["ffn1_w_in", i], full["ffn1_w_out", i], f"l{i}_ffn1")
    grad_x = dy[None]
    _, dg_mem = rmsnorm_bwd(mem0[None], mem_norm.reshape(1, 1, -1), dmem_n[None], "dmem_norm")
    small = {n: jnp.stack(g) for n, g in small.items()}
    small["mem_norm"] = dg_mem.reshape(-1)

    got = all_to_all_multi([big[k].reshape((N_DEV, -1, big[k].shape[-1])) for k in keys], "exchange_grads")
    results = {n: [[None] * w[n].shape[0] for _ in range(4)] for n in BIG}
    for k, slots in zip(keys, got):
        n, i = k
        outs = adamw(local[k], slots, _to_transport(n, m[n][i]), _to_transport(n, v[n][i]), f"adamw_{n}_{i}")
        for q in range(4):
            results[n][q][i] = _from_transport(n, outs[q])
    sharded = {n: [jnp.stack(r) for r in results[n]] for n in BIG}

    small_names = REPLICATED + ("gmlp_v_norm",)
    (small_got,) = all_gather_multi([_pack([small[n] for n in small_names])], "gather_small_grads")
    rep_shapes = [w[n].shape for n in REPLICATED]
    gain_seg = jnp.zeros((n_gain, N_DEV * gain_len), F32)
    pack_rep = lambda d: _pack([d[n] for n in REPLICATED] + [gain_seg])
    outs = adamw(pack_rep(w), small_got, pack_rep(m), pack_rep(v), "adamw_replicated")
    replicated = [dict(zip(REPLICATED, _unpack(o, rep_shapes))) for o in outs]
    gain_parts = _unpack(small_got, rep_shapes + [(n_gain, N_DEV * gain_len)], lead=(N_DEV,))[-1]
    gain_slots = lax.dynamic_slice_in_dim(gain_parts, me * gain_len, gain_len, axis=2)
    gain_slots = jnp.pad(gain_slots, ((0, 0), (0, 8 - n_gain), (0, LANES - gain_len)))
    outs = adamw(pad_gain(gmlp_v_norm), gain_slots, pad_gain(m["gmlp_v_norm"]), pad_gain(v["gmlp_v_norm"]),
                 "adamw_gmlp_v_norm")
    sharded["gmlp_v_norm"] = [o[:n_gain, :gain_len] for o in outs]

    out = [loss, grad_x]
    for q in range(4):
        out += [(replicated[q][n] if n in REPLICATED else sharded[n][q]) for n in WEIGHTS]
    return tuple(out)
```

```python
import functools
import math

import jax
import jax.numpy as jnp
from jax import lax
from jax.experimental import pallas as pl
from jax.experimental.pallas import tpu as pltpu

F32 = jnp.float32
BF16 = jnp.bfloat16

EPS = 1e-6
HEAD_DIM = 64
FOX_HEADS = 12
MEM_HEADS = 4
TOK_WIDTH = FOX_HEADS * HEAD_DIM
MEM_WIDTH = MEM_HEADS * HEAD_DIM
CHUNK = 128
LANES = 128
N_DEV = 8
SEG_ALIGN = 16 * LANES
PACK_ROWS = 1024
PACK_ALIGN = PACK_ROWS * LANES

ADAM_LR = 0.001
ADAM_B1 = 0.9
ADAM_B2 = 0.999
ADAM_EPS = 1e-08
ADAM_WD = 0.01
ADAM_STEP = 10

VMEM_LIMIT_BYTES = 48 * 1024 * 1024
MESH = pl.DeviceIdType.MESH
CONTRACT_0 = (((0,), (0,)), ((), ()))
CONTRACT_1 = (((1,), (1,)), ((), ()))


def _params(*semantics):
    return pltpu.CompilerParams(dimension_semantics=semantics, vmem_limit_bytes=VMEM_LIMIT_BYTES)


def _pick(n, candidates):
    for c in candidates:
        if c <= n and n % c == 0:
            return c
    return n


def _sigmoid(x):
    return 1.0 / (1.0 + jnp.exp(-x))


def _row_tile(r, w):
    return _pick(r, (1024,) if w >= 512 else (2048, 1024, 512, 256))


def rmsnorm_fwd(x, gain, out_dtype, name):
    g_, r_, w_ = x.shape
    tr = _row_tile(r_, w_)

    def body(x_ref, g_ref, y_ref):
        xv = x_ref[0].astype(F32)
        r = lax.rsqrt(jnp.mean(xv * xv, axis=-1, keepdims=True) + EPS)
        y_ref[0] = (xv * r * g_ref[0]).astype(y_ref.dtype)

    return pl.pallas_call(
        body, name=name, grid=(g_, r_ // tr),
        in_specs=[pl.BlockSpec((1, tr, w_), lambda g, i: (g, i, 0)),
                  pl.BlockSpec((1, 1, w_), lambda g, i: (g, 0, 0))],
        out_specs=pl.BlockSpec((1, tr, w_), lambda g, i: (g, i, 0)),
        out_shape=jax.ShapeDtypeStruct((g_, r_, w_), out_dtype),
        compiler_params=_params("parallel", "parallel"),
    )(x, gain)


def rmsnorm_bwd(x, gain, dy, name, residual=None):
    g_, r_, w_ = x.shape
    tr = _row_tile(r_, w_)
    has_res = residual is not None

    def body(*refs):
        if has_res:
            x_ref, g_ref, dy_ref, res_ref, dx_ref, dg_ref = refs
        else:
            x_ref, g_ref, dy_ref, dx_ref, dg_ref = refs
        xv = x_ref[0].astype(F32)
        dyv = dy_ref[0].astype(F32)
        r = lax.rsqrt(jnp.mean(xv * xv, axis=-1, keepdims=True) + EPS)
        n = xv * r
        dn = dyv * g_ref[0]
        dx = r * (dn - n * jnp.mean(dn * n, axis=-1, keepdims=True))
        if has_res:
            dx = dx + res_ref[0]
        dx_ref[0] = dx
        part = jnp.sum(dyv * n, axis=0, keepdims=True)

        @pl.when(pl.program_id(1) == 0)
        def _():
            dg_ref[0] = part

        @pl.when(pl.program_id(1) != 0)
        def _():
            dg_ref[0] += part

    row = pl.BlockSpec((1, tr, w_), lambda g, i: (g, i, 0))
    vec = pl.BlockSpec((1, 1, w_), lambda g, i: (g, 0, 0))
    operands = (x, gain, dy) + ((residual,) if has_res else ())
    return pl.pallas_call(
        body, name=name, grid=(g_, r_ // tr),
        in_specs=[row, vec, row] + ([row] if has_res else []),
        out_specs=(row, vec),
        out_shape=(jax.ShapeDtypeStruct((g_, r_, w_), F32), jax.ShapeDtypeStruct((g_, 1, w_), F32)),
        compiler_params=_params("parallel", "arbitrary"),
    )(*operands)


def matmul(a, b, name, out_dtype=F32, residual=None, scale=None, transpose_b=False):
    m_, k_ = a.shape
    n_ = b.shape[0] if transpose_b else b.shape[1]
    tm = _pick(m_, (512, 256, 128))
    tn = _pick(n_, (1408, 1024, 896, 512, 256, 128))
    has_res = residual is not None

    def body(*refs):
        if has_res:
            a_ref, b_ref, res_ref, o_ref = refs
        else:
            a_ref, b_ref, o_ref = refs
        av, bv = a_ref[...].astype(BF16), b_ref[...].astype(BF16)
        if transpose_b:
            acc = lax.dot_general(av, bv, CONTRACT_1, preferred_element_type=F32)
        else:
            acc = jnp.dot(av, bv, preferred_element_type=F32)
        if scale is not None:
            acc = acc * scale
        if has_res:
            acc = acc + res_ref[...]
        o_ref[...] = acc.astype(o_ref.dtype)

    out_spec = pl.BlockSpec((tm, tn), lambda j, i: (i, j))
    b_spec = pl.BlockSpec((tn, k_), lambda j, i: (j, 0)) if transpose_b else pl.BlockSpec((k_, tn), lambda j, i: (0, j))
    operands = (a, b) + ((residual,) if has_res else ())
    return pl.pallas_call(
        body, name=name, grid=(n_ // tn, m_ // tm),
        in_specs=[pl.BlockSpec((tm, k_), lambda j, i: (i, 0)), b_spec] + ([out_spec] if has_res else []),
        out_specs=out_spec,
        out_shape=jax.ShapeDtypeStruct((m_, n_), out_dtype),
        compiler_params=_params("parallel", "parallel"),
    )(*operands)


def matmul_tn(a, b, name, scale=None, out_dtype=F32):
    s_, k_ = a.shape
    n_ = b.shape[1]
    tk = _pick(k_, (1024, 1408, 896, 512, 256, 128))
    tn = _pick(n_, (1408, 1024, 896, 512, 256, 128))
    ts = _pick(s_, (512, 256, 128))
    ns = s_ // ts

    def body(a_ref, b_ref, o_ref, acc_ref):
        part = lax.dot_general(a_ref[...].astype(BF16), b_ref[...].astype(BF16), CONTRACT_0,
                               preferred_element_type=F32)
        step = pl.program_id(2)

        @pl.when(step == 0)
        def _():
            acc_ref[...] = part

        @pl.when(step != 0)
        def _():
            acc_ref[...] += part

        @pl.when(step == ns - 1)
        def _():
            acc = acc_ref[...]
            o_ref[...] = (acc if scale is None else acc * scale).astype(o_ref.dtype)

    return pl.pallas_call(
        body, name=name, grid=(k_ // tk, n_ // tn, ns),
        in_specs=[pl.BlockSpec((ts, tk), lambda i, j, s: (s, i)),
                  pl.BlockSpec((ts, tn), lambda i, j, s: (s, j))],
        out_specs=pl.BlockSpec((tk, tn), lambda i, j, s: (i, j)),
        out_shape=jax.ShapeDtypeStruct((k_, n_), out_dtype),
        scratch_shapes=[pltpu.VMEM((tk, tn), F32)],
        compiler_params=_params("parallel", "parallel", "arbitrary"),
    )(a, b)


def ffn_in(h, w_in_t, name):
    s_, d_ = h.shape
    f_ = w_in_t.shape[0] // 2
    tm = _pick(s_, (512, 256, 128))
    tn = _pick(f_, (1408, 1024, 512, 256, 128))
    nb = f_ // tn

    def body(h_ref, wa_ref, wb_ref, a_ref, b_ref, u_ref):
        hv = h_ref[...]
        a = lax.dot_general(hv, wa_ref[...], CONTRACT_1, preferred_element_type=F32)
        b = lax.dot_general(hv, wb_ref[...], CONTRACT_1, preferred_element_type=F32)
        a_ref[...] = a.astype(BF16)
        b_ref[...] = b.astype(BF16)
        u_ref[...] = (a * _sigmoid(a) * b).astype(BF16)

    o_spec = pl.BlockSpec((tm, tn), lambda j, i: (i, j))
    out = jax.ShapeDtypeStruct((s_, f_), BF16)
    return pl.pallas_call(
        body, name=name, grid=(nb, s_ // tm),
        in_specs=[pl.BlockSpec((tm, d_), lambda j, i: (i, 0)),
                  pl.BlockSpec((tn, d_), lambda j, i: (j, 0)),
                  pl.BlockSpec((tn, d_), lambda j, i: (j + nb, 0))],
        out_specs=(o_spec, o_spec, o_spec), out_shape=(out, out, out),
        compiler_params=_params("parallel", "parallel"),
    )(h, w_in_t, w_in_t)


def ffn_bwd_act(dy, w_out, a, b, name):
    s_, d_ = dy.shape
    f_ = w_out.shape[0]
    tm = _pick(s_, (512, 256, 128))
    tn = _pick(f_, (1408, 1024, 512, 256, 128))

    def body(dy_ref, w_ref, a_ref, b_ref, da_ref, db_ref):
        du = 0.5 * lax.dot_general(dy_ref[...].astype(BF16), w_ref[...], CONTRACT_1, preferred_element_type=F32)
        av = a_ref[...].astype(F32)
        bv = b_ref[...].astype(F32)
        sig = _sigmoid(av)
        da_ref[...] = (du * bv * (sig * (1.0 + av * (1.0 - sig)))).astype(BF16)
        db_ref[...] = (du * (av * sig)).astype(BF16)

    t_spec = pl.BlockSpec((tm, tn), lambda j, i: (i, j))
    out = jax.ShapeDtypeStruct((s_, f_), BF16)
    return pl.pallas_call(
        body, name=name, grid=(f_ // tn, s_ // tm),
        in_specs=[pl.BlockSpec((tm, d_), lambda j, i: (i, 0)),
                  pl.BlockSpec((tn, d_), lambda j, i: (j, 0)), t_spec, t_spec],
        out_specs=(t_spec, t_spec), out_shape=(out, out),
        compiler_params=_params("parallel", "parallel"),
    )(dy, w_out, a, b)


def _fold(a, b, n, forward):
    if forward:
        low = b <= a
        return jnp.where(low, a, n - 1 - a), jnp.where(low, b, b - a - 1)
    low = b < n - a
    return jnp.where(low, a, n - 1 - a), jnp.where(low, a + b, b - 1)


def _attn_grid(h_, n_outer, n_inner, causal, forward):
    if causal and n_outer % 2 == 0:
        return (h_, n_outer // 2, n_outer + 1), lambda a, b: _fold(a, b, n_outer, forward)
    return (h_, n_outer, n_inner), lambda a, b: (a, b)


QK_SCALE = 1.0 / math.sqrt(HEAD_DIM)
SUM_LANE = HEAD_DIM


def _scores(q, k, ck, masked):
    s = lax.dot_general(q, k, CONTRACT_1, preferred_element_type=F32)
    if ck is not None:
        s = s - ck
    if masked:
        row = lax.broadcasted_iota(jnp.int32, s.shape, 0)
        col = lax.broadcasted_iota(jnp.int32, s.shape, 1)
        s = jnp.where(col <= row, s, -jnp.inf)
    return s


def _widen_v(v):
    ones = jnp.ones(v.shape[:2] + (1,), BF16)
    zeros = jnp.zeros(v.shape[:2] + (LANES - HEAD_DIM - 1,), BF16)
    return jnp.concatenate([v.astype(BF16), ones, zeros], axis=-1)


def _grid_step(grid):
    step = (pl.program_id(0) * grid[1] + pl.program_id(1)) * grid[2] + pl.program_id(2)
    return step, grid[0] * grid[1] * grid[2] - 1


def _split_refs(refs, counts):
    out, at = [], 0
    for n in counts:
        out.append(refs[at:at + n])
        at += n
    return out


def attn_fwd(q, k, v_wide, ck, causal, name, carry=None):
    h_, sq, d_ = q.shape
    sk = k.shape[1]
    tq = _pick(sq, (512, 256, 128))
    tk = tq if causal else _pick(sk, (512, 256, 128))
    nq, nk = sq // tq, sk // tk
    assert not causal or nq == 1 or nq % 2 == 0
    bias = ck is not None
    grid, blocks = _attn_grid(h_, nq, nk, causal, True)
    nc = len(carry[1]) if carry else 0

    def body(*refs):
        ins, c_in, outs, c_out, scratch, sems = _split_refs(refs, (4 if bias else 3, nc, 2, nc, 2, 3 if carry else 0))
        q_ref, k_ref, v_ref = ins[:3]
        ck_ref = ins[3] if bias else None
        (o_ref, lse_ref), (m_sc, acc_sc) = outs, scratch
        i, j = blocks(pl.program_id(1), pl.program_id(2))
        if carry:
            step_no, last_no = _grid_step(grid)

            @pl.when(step_no == 0)
            def _():
                for cp in _direct_copies(carry[0], c_in, c_out, *sems):
                    cp.start()

        @pl.when(j == 0)
        def _():
            m_sc[...] = jnp.full(m_sc.shape, -jnp.inf, F32)
            acc_sc[...] = jnp.zeros(acc_sc.shape, F32)

        def step(masked):
            s = _scores(q_ref[0], k_ref[0], ck_ref[0] if bias else None, masked)
            m_prev = m_sc[...]
            m_new = jnp.maximum(m_prev, jnp.max(s, axis=1, keepdims=True))
            p = jnp.exp(s - m_new)
            acc_sc[...] = jnp.exp(m_prev - m_new) * acc_sc[...] + jnp.dot(p.astype(BF16), v_ref[0],
                                                                         preferred_element_type=F32)
            m_sc[...] = m_new

        if causal:
            pl.when(j < i)(functools.partial(step, False))
            pl.when(j == i)(functools.partial(step, True))
        else:
            step(False)

        @pl.when(j == (i if causal else nk - 1))
        def _():
            acc = acc_sc[...]
            lane = lax.broadcasted_iota(jnp.int32, acc.shape, 1)
            l = jnp.sum(jnp.where(lane == SUM_LANE, acc, 0.0), axis=1, keepdims=True)
            o_ref[0] = acc_sc[:, :HEAD_DIM] / l
            lse_ref[0] = m_sc[...] + jnp.log(l)

        if carry:
            @pl.when(step_no == last_no)
            def _():
                for cp in _direct_copies(carry[0], c_in, c_out, *sems):
                    cp.wait()

    q_spec = pl.BlockSpec((1, tq, d_), lambda h, a, b: (h, blocks(a, b)[0], 0))
    q1_spec = pl.BlockSpec((1, tq, 1), lambda h, a, b: (h, blocks(a, b)[0], 0))
    k_spec = pl.BlockSpec((1, tk, d_), lambda h, a, b: (h, blocks(a, b)[1], 0))
    in_specs = [q_spec, k_spec, pl.BlockSpec((1, tk, LANES), lambda h, a, b: (h, blocks(a, b)[1], 0))]
    operands = [q, k, v_wide]
    if bias:
        in_specs.append(pl.BlockSpec((1, 1, tk), lambda h, a, b: (h, 0, blocks(a, b)[1])))
        operands.append(ck)
    out_specs = [q_spec, q1_spec]
    out_shape = [jax.ShapeDtypeStruct((h_, sq, d_), F32), jax.ShapeDtypeStruct((h_, sq, 1), F32)]
    scratch = [pltpu.VMEM((tq, 1), F32), pltpu.VMEM((tq, LANES), F32)]
    if carry:
        c_in_specs, c_out_specs, c_shapes, c_sems = _exchange_operands(*carry)
        in_specs, out_specs, out_shape, scratch = in_specs + c_in_specs, out_specs + c_out_specs, out_shape + c_shapes, scratch + c_sems
        operands = operands + list(carry[1])
    return pl.pallas_call(
        body, name=name, grid=grid,
        in_specs=in_specs, out_specs=tuple(out_specs), out_shape=tuple(out_shape), scratch_shapes=scratch,
        compiler_params=_params("arbitrary" if carry else "parallel", "arbitrary", "arbitrary"),
    )(*operands)


def attn_bwd(q, k, v, ck, o, lse, do, causal, name, carry=None):
    h_, sq, d_ = q.shape
    sk = k.shape[1]
    tq = _pick(sq, (512, 256, 128))
    tk = tq if causal else _pick(sk, (512, 256, 128))
    nq, nk = sq // tq, sk // tk
    assert not causal or nq == 1 or nq % 2 == 0
    bias = ck is not None
    grid, blocks = _attn_grid(h_, nk, nq, causal, False)
    nc = len(carry[1]) if carry else 0

    def body(*refs):
        ins, c_in, outs, c_out, scratch, sems = _split_refs(
            refs, (7 if bias else 6, nc, 5 if bias else 3, nc, 3 if bias else 2, 3 if carry else 0))
        if bias:
            q_ref, k_ref, v_ref, ck_ref, o_ref, lse_ref, do_ref = ins
            dq_ref, dk_ref, dv_ref, dc_ref, dcq_ref = outs
            dk_sc, dv_sc, dc_sc = scratch
        else:
            q_ref, k_ref, v_ref, o_ref, lse_ref, do_ref = ins
            dq_ref, dk_ref, dv_ref = outs
            dk_sc, dv_sc = scratch
        j, i = blocks(pl.program_id(1), pl.program_id(2))
        if carry:
            step_no, last_no = _grid_step(grid)

            @pl.when(step_no == 0)
            def _():
                for cp in _direct_copies(carry[0], c_in, c_out, *sems):
                    cp.start()

        @pl.when((pl.program_id(1) == 0) & (pl.program_id(2) == 0))
        def _():
            dq_ref[...] = jnp.zeros(dq_ref.shape, F32)
            if bias:
                dcq_ref[...] = jnp.zeros(dcq_ref.shape, F32)

        @pl.when(i == (j if causal else 0))
        def _():
            dk_sc[...] = jnp.zeros(dk_sc.shape, F32)
            dv_sc[...] = jnp.zeros(dv_sc.shape, F32)
            if bias:
                dc_sc[...] = jnp.zeros(dc_sc.shape, F32)

        def step(masked):
            qb, kb = q_ref[0], k_ref[0]
            dof = do_ref[0]
            dob = dof.astype(BF16)
            s = _scores(qb, kb, ck_ref[0] if bias else None, masked)
            p = jnp.exp(s - lse_ref[0])
            dp = lax.dot_general(dob, v_ref[0].astype(BF16), CONTRACT_1, preferred_element_type=F32)
            delta = jnp.sum(dof * o_ref[0], axis=1, keepdims=True)
            ds = p * (dp - delta)
            dsb = ds.astype(BF16)
            dv_sc[...] += lax.dot_general(p.astype(BF16), dob, CONTRACT_0, preferred_element_type=F32)
            dk_sc[...] += lax.dot_general(dsb, qb, CONTRACT_0, preferred_element_type=F32)
            rows = pl.ds(pl.multiple_of(i * tq, tq), tq)
            dq_ref[0, rows, :] += jnp.dot(dsb, kb, preferred_element_type=F32)
            if bias:
                dc_sc[...] -= jnp.sum(ds, axis=0, keepdims=True)
                dcq_ref[0, rows, :] += jnp.sum(ds, axis=1, keepdims=True)

        if causal:
            pl.when(i > j)(functools.partial(step, False))
            pl.when(i == j)(functools.partial(step, True))
        else:
            step(False)

        @pl.when(i == nq - 1)
        def _():
            dk_ref[0] = dk_sc[...]
            dv_ref[0] = dv_sc[...]
            if bias:
                dc_ref[0] = dc_sc[...]

        if carry:
            @pl.when(step_no == last_no)
            def _():
                for cp in _direct_copies(carry[0], c_in, c_out, *sems):
                    cp.wait()

    q_spec = pl.BlockSpec((1, tq, d_), lambda h, a, b: (h, blocks(a, b)[1], 0))
    q1_spec = pl.BlockSpec((1, tq, 1), lambda h, a, b: (h, blocks(a, b)[1], 0))
    k_spec = pl.BlockSpec((1, tk, d_), lambda h, a, b: (h, blocks(a, b)[0], 0))
    c_spec = pl.BlockSpec((1, 1, tk), lambda h, a, b: (h, 0, blocks(a, b)[0]))
    in_specs = [q_spec, k_spec, k_spec] + ([c_spec] if bias else []) + [q_spec, q1_spec, q_spec]
    operands = [q, k, v] + ([ck] if bias else []) + [o, lse, do]
    out_specs = [pl.BlockSpec((1, sq, d_), lambda h, a, b: (h, 0, 0)), k_spec, k_spec]
    out_shape = [jax.ShapeDtypeStruct((h_, sq, d_), F32), jax.ShapeDtypeStruct((h_, sk, d_), F32),
                 jax.ShapeDtypeStruct((h_, sk, d_), F32)]
    scratch = [pltpu.VMEM((tk, d_), F32), pltpu.VMEM((tk, d_), F32)]
    if bias:
        out_specs += [c_spec, pl.BlockSpec((1, sq, 1), lambda h, a, b: (h, 0, 0))]
        out_shape += [jax.ShapeDtypeStruct((h_, 1, sk), F32), jax.ShapeDtypeStruct((h_, sq, 1), F32)]
        scratch.append(pltpu.VMEM((1, tk), F32))
    if carry:
        c_in_specs, c_out_specs, c_shapes, c_sems = _exchange_operands(*carry)
        in_specs, out_specs, out_shape, scratch = in_specs + c_in_specs, out_specs + c_out_specs, out_shape + c_shapes, scratch + c_sems
        operands = operands + list(carry[1])
    return pl.pallas_call(
        body, name=name, grid=grid,
        in_specs=in_specs, out_specs=tuple(out_specs), out_shape=tuple(out_shape),
        scratch_shapes=scratch,
        compiler_params=_params("arbitrary" if carry else "parallel", "arbitrary", "arbitrary"),
    )(*operands)


def _tri(lower):
    row = lax.broadcasted_iota(jnp.int32, (CHUNK, CHUNK), 0)
    col = lax.broadcasted_iota(jnp.int32, (CHUNK, CHUNK), 1)
    return jnp.where((col <= row) if lower else (col >= row), 1.0, 0.0).astype(F32)


def fox_gate_fwd(f, b, name):
    s_ = f.shape[0]

    def body(f_ref, b_ref, c_ref, carry):
        @pl.when(pl.program_id(0) == 0)
        def _():
            carry[...] = jnp.zeros(carry.shape, F32)

        xv = f_ref[...] + b_ref[...]
        log_f = jnp.minimum(xv, 0.0) - jnp.log(1.0 + jnp.exp(-jnp.abs(xv)))
        c = jnp.dot(_tri(True), log_f, precision=lax.Precision.HIGHEST, preferred_element_type=F32) + carry[...]
        c_ref[...] = c
        carry[...] = c[CHUNK - 1:CHUNK, :]

    blk = pl.BlockSpec((CHUNK, LANES), lambda i: (i, 0))
    return pl.pallas_call(
        body, name=name, grid=(s_ // CHUNK,),
        in_specs=[blk, pl.BlockSpec((1, LANES), lambda i: (0, 0))], out_specs=blk,
        out_shape=jax.ShapeDtypeStruct((s_, LANES), F32),
        scratch_shapes=[pltpu.VMEM((1, LANES), F32)],
        compiler_params=_params("arbitrary"),
    )(f, b)


def fox_gate_bwd(dc, f, b, name):
    s_ = f.shape[0]
    n = s_ // CHUNK

    def body(dc_ref, f_ref, b_ref, df_ref, db_ref, carry):
        @pl.when(pl.program_id(0) == 0)
        def _():
            carry[...] = jnp.zeros(carry.shape, F32)
            db_ref[...] = jnp.zeros(db_ref.shape, F32)

        dlog = jnp.dot(_tri(False), dc_ref[...], precision=lax.Precision.HIGHEST,
                       preferred_element_type=F32) + carry[...]
        df = dlog * _sigmoid(-(f_ref[...] + b_ref[...]))
        df_ref[...] = df
        db_ref[...] += jnp.sum(df, axis=0, keepdims=True)
        carry[...] = dlog[0:1, :]

    blk = pl.BlockSpec((CHUNK, LANES), lambda i: (n - 1 - i, 0))
    vec = pl.BlockSpec((1, LANES), lambda i: (0, 0))
    return pl.pallas_call(
        body, name=name, grid=(n,),
        in_specs=[blk, blk, vec], out_specs=(blk, vec),
        out_shape=(jax.ShapeDtypeStruct((s_, LANES), F32), jax.ShapeDtypeStruct((1, LANES), F32)),
        scratch_shapes=[pltpu.VMEM((1, LANES), F32)],
        compiler_params=_params("arbitrary"),
    )(dc, f, b)


GELU_K = math.sqrt(2.0 / math.pi)
GELU_C = 0.044715


def _gelu(x):
    return 0.5 * x * (1.0 + jnp.tanh(GELU_K * (x + GELU_C * (x * x * x))))


def _gelu_grad(x):
    t = jnp.tanh(GELU_K * (x + GELU_C * (x * x * x)))
    return 0.5 * (1.0 + t) + 0.5 * x * (1.0 - t * t) * (GELU_K * (1.0 + 3.0 * GELU_C * (x * x)))


def _tril_mask():
    row = lax.broadcasted_iota(jnp.int32, (CHUNK, CHUNK), 0)
    col = lax.broadcasted_iota(jnp.int32, (CHUNK, CHUNK), 1)
    return col <= row


def _gmlp_specs(s_, ts):
    row = pl.BlockSpec((1, ts, HEAD_DIM), lambda g, i: (g, i, 0))
    gain = pl.BlockSpec((1, 1, HEAD_DIM), lambda g, i: (g, 0, 0))
    w = pl.BlockSpec((1, CHUNK, CHUNK), lambda g, i: (g, 0, 0))
    b = pl.BlockSpec((1, CHUNK, 1), lambda g, i: (g, 0, 0))
    return row, gain, w, b


def gmlp_fwd(pu, pv, gain, w, b, name):
    g_, s_, d_ = pu.shape
    ts = _pick(s_, (1024, 512, 256, 128))

    def body(pu_ref, pv_ref, g_ref, w_ref, b_ref, o_ref):
        v = _gelu(pv_ref[0])
        r = lax.rsqrt(jnp.mean(v * v, axis=-1, keepdims=True) + EPS)
        vn = (v * r * g_ref[0]).astype(BF16)
        wt = jnp.where(_tril_mask(), w_ref[0], 0.0).astype(BF16)
        for c in range(ts // CHUNK):
            rows = pl.ds(c * CHUNK, CHUNK)
            gate = jnp.dot(wt, vn[c * CHUNK:(c + 1) * CHUNK], preferred_element_type=F32) + b_ref[0]
            o_ref[0, rows, :] = _gelu(pu_ref[0, rows, :]) * gate

    row, gspec, wspec, bspec = _gmlp_specs(s_, ts)
    return pl.pallas_call(
        body, name=name, grid=(g_, s_ // ts),
        in_specs=[row, row, gspec, wspec, bspec], out_specs=row,
        out_shape=jax.ShapeDtypeStruct((g_, s_, d_), F32),
        compiler_params=_params("parallel", "parallel"),
    )(pu, pv, gain, w, b)


def gmlp_bwd(pu, pv, gain, w, b, dout, name):
    g_, s_, d_ = pu.shape
    ts = _pick(s_, (1024, 512, 256, 128))

    def body(pu_ref, pv_ref, g_ref, w_ref, b_ref, do_ref, dpu_ref, dpv_ref, dw_ref, db_ref, dg_ref):
        @pl.when(pl.program_id(1) == 0)
        def _():
            dw_ref[...] = jnp.zeros(dw_ref.shape, F32)
            db_ref[...] = jnp.zeros(db_ref.shape, F32)
            dg_ref[...] = jnp.zeros(dg_ref.shape, F32)

        gain_v = g_ref[0]
        mask = _tril_mask()
        wt = jnp.where(mask, w_ref[0], 0.0).astype(BF16)
        dw = jnp.zeros((CHUNK, CHUNK), F32)
        db = jnp.zeros((CHUNK, 1), F32)
        dg = jnp.zeros((1, d_), F32)
        for c in range(ts // CHUNK):
            rows = pl.ds(c * CHUNK, CHUNK)
            pu_c = pu_ref[0, rows, :]
            pv_c = pv_ref[0, rows, :]
            do_c = do_ref[0, rows, :]
            u = _gelu(pu_c)
            v = _gelu(pv_c)
            r = lax.rsqrt(jnp.mean(v * v, axis=-1, keepdims=True) + EPS)
            n = v * r
            vn = (n * gain_v).astype(BF16)
            gate = jnp.dot(wt, vn, preferred_element_type=F32) + b_ref[0]
            dgate = do_c * u
            dgate_b = dgate.astype(BF16)
            dpu_ref[0, rows, :] = do_c * gate * _gelu_grad(pu_c)
            db = db + jnp.sum(dgate, axis=1, keepdims=True)
            dw = dw + lax.dot_general(dgate_b, vn, CONTRACT_1, preferred_element_type=F32)
            dvn = lax.dot_general(wt, dgate_b, CONTRACT_0, preferred_element_type=F32)
            dg = dg + jnp.sum(dvn * n, axis=0, keepdims=True)
            dn = dvn * gain_v
            dv = r * (dn - n * jnp.mean(dn * n, axis=-1, keepdims=True))
            dpv_ref[0, rows, :] = dv * _gelu_grad(pv_c)
        dw_ref[0] += jnp.where(mask, dw, 0.0)
        db_ref[0] += db
        dg_ref[0] += dg

    row, gspec, wspec, bspec = _gmlp_specs(s_, ts)
    return pl.pallas_call(
        body, name=name, grid=(g_, s_ // ts),
        in_specs=[row, row, gspec, wspec, bspec, row],
        out_specs=(row, row, wspec, bspec, gspec),
        out_shape=(jax.ShapeDtypeStruct((g_, s_, d_), F32), jax.ShapeDtypeStruct((g_, s_, d_), F32),
                   jax.ShapeDtypeStruct((g_, CHUNK, CHUNK), F32), jax.ShapeDtypeStruct((g_, CHUNK, 1), F32),
                   jax.ShapeDtypeStruct((g_, 1, d_), F32)),
        compiler_params=_params("parallel", "arbitrary"),
    )(pu, pv, gain, w, b, dout)


def loss_head(y, t, name):
    s_, d_ = y.shape
    tr = _pick(s_, (1024, 512, 256, 128))

    def body(y_ref, t_ref, dy_ref, l_ref):
        err = y_ref[...] - t_ref[...]
        dy_ref[...] = err * (1.0 / d_)
        part = jnp.full(l_ref.shape, jnp.sum(err * err) * (0.5 / d_), F32)

        @pl.when(pl.program_id(0) == 0)
        def _():
            l_ref[...] = part

        @pl.when(pl.program_id(0) != 0)
        def _():
            l_ref[...] += part

    blk = pl.BlockSpec((tr, d_), lambda i: (i, 0))
    dy, l = pl.pallas_call(
        body, name=name, grid=(s_ // tr,),
        in_specs=[blk, blk], out_specs=(blk, pl.BlockSpec((8, LANES), lambda i: (0, 0))),
        out_shape=(jax.ShapeDtypeStruct((s_, d_), F32), jax.ShapeDtypeStruct((8, LANES), F32)),
        compiler_params=_params("arbitrary"),
    )(y, t)
    return dy, l[0, 0]


def adamw(w, g_slots, m, v, name):
    r_, c_ = w.shape
    tr = _pick(r_, (1024, 512, 352, 256, 224, 128, 64, 32, 16, 8))

    def body(w_ref, gs_ref, m_ref, v_ref, g_ref, d_ref, nm_ref, nv_ref):
        g = gs_ref[0].astype(F32)
        for k in range(1, N_DEV):
            g = g + gs_ref[k].astype(F32)
        m_new = ADAM_B1 * m_ref[...] + (1.0 - ADAM_B1) * g
        v_new = ADAM_B2 * v_ref[...] + (1.0 - ADAM_B2) * (g * g)
        m_hat = m_new / (1.0 - ADAM_B1 ** ADAM_STEP)
        v_hat = v_new / (1.0 - ADAM_B2 ** ADAM_STEP)
        g_ref[...] = g
        d_ref[...] = -ADAM_LR * (m_hat / (jnp.sqrt(v_hat) + ADAM_EPS) + ADAM_WD * w_ref[...])
        nm_ref[...] = m_new
        nv_ref[...] = v_new

    blk = pl.BlockSpec((tr, c_), lambda i: (i, 0))
    out = jax.ShapeDtypeStruct((r_, c_), F32)
    return pl.pallas_call(
        body, name=name, grid=(r_ // tr,),
        in_specs=[blk, pl.BlockSpec((N_DEV, tr, c_), lambda i: (0, i, 0)), blk, blk],
        out_specs=(blk, blk, blk, blk), out_shape=(out, out, out, out),
        compiler_params=_params("parallel"),
    )(w, g_slots, m, v)


def _position():
    x, y, c = lax.axis_index("x"), lax.axis_index("y"), lax.axis_index("c")
    return x, y, c


def _slot(px, py, pc):
    return 4 * px + 2 * py + pc


def all_gather_multi(blocks, name):
    n = len(blocks)

    def body(*refs):
        x_refs, out_refs = refs[:n], refs[n:2 * n]
        send_sems, recv_sems, local_sems = refs[2 * n:]
        x, y, c = _position()
        me, sibling = (x, y, c), (x, y, 1 - c)
        chips = [(1 - x, y), (x, 1 - y), (1 - x, 1 - y)]

        def copy(b, k, owner, to, src=None):
            dst = out_refs[b].at[_slot(*owner)]
            return pltpu.make_async_remote_copy(
                src_ref=dst if src is None else src, dst_ref=dst,
                send_sem=send_sems.at[b, k], recv_sem=recv_sems.at[b, k], device_id=to, device_id_type=MESH)

        mine = [pltpu.make_async_copy(x_refs[b], out_refs[b].at[_slot(*me)], local_sems.at[b]) for b in range(n)]
        for cp in mine:
            cp.start()
        first = [copy(b, 1 + j, me, (*chip, c), src=x_refs[b]) for j, chip in enumerate(chips) for b in range(n)]
        first += [copy(b, 0, me, sibling, src=x_refs[b]) for b in range(n)]
        for cp in first:
            cp.start()
        passed = []
        for j, chip in enumerate(chips):
            for b in range(n):
                copy(b, 1 + j, (*chip, c), me).wait_recv()
                cp = copy(b, 4 + j, (*chip, c), sibling)
                cp.start()
                passed.append(cp)
        for b in range(n):
            copy(b, 0, sibling, me).wait_recv()
        for j, chip in enumerate(chips):
            for b in range(n):
                copy(b, 4 + j, (*chip, 1 - c), me).wait_recv()
        for cp in first + passed:
            cp.wait_send()
        for cp in mine:
            cp.wait()

    any_spec = pl.BlockSpec(memory_space=pl.ANY)
    return pl.pallas_call(
        body, name=name,
        in_specs=[any_spec] * n, out_specs=tuple([any_spec] * n),
        out_shape=tuple(jax.ShapeDtypeStruct((N_DEV,) + blk.shape, blk.dtype) for blk in blocks),
        scratch_shapes=[pltpu.SemaphoreType.DMA((n, 7)), pltpu.SemaphoreType.DMA((n, 7)), pltpu.SemaphoreType.DMA((n,))],
    )(*blocks)


def _direct_copies(scatter, in_refs, out_refs, send_sems, recv_sems, local_sems):
    x, y, c = _position()
    me = _slot(x, y, c)
    src = (lambda ref, slot: ref.at[slot]) if scatter else (lambda ref, slot: ref)
    copies = [pltpu.make_async_copy(src(ref, me), out.at[me], local_sems.at[b])
              for b, (ref, out) in enumerate(zip(in_refs, out_refs))]
    for k in range(1, N_DEV):
        px = 1 - x if k & 4 else x
        py = 1 - y if k & 2 else y
        pc = 1 - c if k & 1 else c
        copies += [pltpu.make_async_remote_copy(
            src_ref=src(ref, _slot(px, py, pc)), dst_ref=out.at[me],
            send_sem=send_sems.at[b, k - 1], recv_sem=recv_sems.at[b, k - 1],
            device_id=(px, py, pc), device_id_type=MESH) for b, (ref, out) in enumerate(zip(in_refs, out_refs))]
    return copies


def _exchange_operands(scatter, bufs):
    n = len(bufs)
    any_spec = pl.BlockSpec(memory_space=pl.ANY)
    shapes = [jax.ShapeDtypeStruct(b.shape if scatter else (N_DEV,) + b.shape, b.dtype) for b in bufs]
    sems = [pltpu.SemaphoreType.DMA((n, 7)), pltpu.SemaphoreType.DMA((n, 7)), pltpu.SemaphoreType.DMA((n,))]
    return [any_spec] * n, [any_spec] * n, shapes, sems


def all_to_all_multi(slot_bufs, name):
    n = len(slot_bufs)

    def body(*refs):
        copies = _direct_copies(True, refs[:n], refs[n:2 * n], *refs[2 * n:])
        for cp in copies:
            cp.start()
        for cp in copies:
            cp.wait()

    in_specs, out_specs, out_shape, sems = _exchange_operands(True, slot_bufs)
    return pl.pallas_call(
        body, name=name, in_specs=in_specs, out_specs=tuple(out_specs), out_shape=tuple(out_shape),
        scratch_shapes=sems,
    )(*slot_bufs)


def _seg_len(n):
    return -(-n // SEG_ALIGN) * SEG_ALIGN


def _pack(arrays, lead=()):
    parts, total = [], 0
    for a in arrays:
        flat = a.reshape(lead + (-1,))
        pad = _seg_len(flat.shape[-1]) - flat.shape[-1]
        parts.append(jnp.pad(flat, [(0, 0)] * len(lead) + [(0, pad)]) if pad else flat)
        total += flat.shape[-1] + pad
    tail = -(-total // PACK_ALIGN) * PACK_ALIGN - total
    if tail:
        parts.append(jnp.zeros(lead + (tail,), parts[0].dtype))
    return jnp.concatenate(parts, axis=-1).reshape(lead + (-1, LANES))


def _unpack(packed, shapes, lead=()):
    flat = packed.reshape(lead + (-1,))
    out, off = [], 0
    for shp in shapes:
        n = math.prod(shp)
        out.append(flat[..., off:off + n].reshape(lead + tuple(shp)))
        off += _seg_len(n)
    return out


def _heads(a, n_heads):
    return a.reshape(a.shape[0], n_heads, HEAD_DIM).transpose(1, 0, 2)


def _unheads(a):
    return a.transpose(1, 0, 2).reshape(a.shape[1], a.shape[0] * HEAD_DIM)


def _head_gain(g, n_heads):
    return jnp.broadcast_to(g.reshape(1, 1, HEAD_DIM), (n_heads, 1, HEAD_DIM))


def _ffn_forward(x, gain, w_in_t, w_out, tag):
    h = rmsnorm_fwd(x[None], gain.reshape(1, 1, -1), BF16, f"{tag}_norm")[0]
    a, b, u = ffn_in(h, w_in_t, f"{tag}_in")
    y = matmul(u, w_out, f"{tag}_out", residual=x, scale=0.5)
    return y, (x, h, a, b, u)


def _ffn_backward(dy, saved, gain, w_in_t, w_out, tag):
    x, h, a, b, u = saved
    f_ = w_out.shape[0]
    dw_out = matmul_tn(u, dy, f"{tag}_dwout", scale=0.5, out_dtype=BF16)
    da, db = ffn_bwd_act(dy, w_out, a, b, f"{tag}_dact")
    dh = matmul(da, w_in_t[:f_], f"{tag}_dh_a")
    dh = matmul(db, w_in_t[f_:], f"{tag}_dh_b", residual=dh)
    dw_in_t = jnp.concatenate([matmul_tn(da, h, f"{tag}_dwin_a", out_dtype=BF16),
                               matmul_tn(db, h, f"{tag}_dwin_b", out_dtype=BF16)], axis=0)
    dx, dgain = rmsnorm_bwd(x[None], gain.reshape(1, 1, -1), dh[None], f"{tag}_dnorm", residual=dy[None])
    return dx[0], dgain.reshape(-1), dw_in_t, dw_out


def _mem_forward(mq, mem_n, w_kv, g_q, g_k, tag):
    gq = _head_gain(g_q, MEM_HEADS)
    gk = _head_gain(g_k, MEM_HEADS)
    qn = rmsnorm_fwd(mq, gq * QK_SCALE, BF16, f"{tag}_qnorm")
    kv = matmul(mem_n, w_kv, f"{tag}_kv")
    k = _heads(kv[:, :MEM_WIDTH], MEM_HEADS)
    v = _heads(kv[:, MEM_WIDTH:], MEM_HEADS)
    kn = rmsnorm_fwd(k, gk, BF16, f"{tag}_knorm")
    o, lse = attn_fwd(qn, kn, _widen_v(v), None, False, f"{tag}_attn")
    return o, (mq, gq, gk, qn, k, kn, v, o, lse)


def _mem_backward(do, saved, mem_n, w_kv, tag):
    mq, gq, gk, qn, k, kn, v, o, lse = saved
    dqn, dkn, dv = attn_bwd(qn, kn, v, None, o, lse, do, False, f"{tag}_dattn")
    dmq, dgq = rmsnorm_bwd(mq, gq, dqn * QK_SCALE, f"{tag}_dqnorm")
    dk, dgk = rmsnorm_bwd(k, gk, dkn, f"{tag}_dknorm")
    dkv = jnp.concatenate([_unheads(dk), _unheads(dv)], axis=1)
    dw_kv = matmul_tn(mem_n, dkv, f"{tag}_dwkv", out_dtype=BF16)
    dmem_n = matmul(dkv, w_kv, f"{tag}_dmem", transpose_b=True)
    return dmq, dgq.sum(axis=0).reshape(-1), dgk.sum(axis=0).reshape(-1), dw_kv, dmem_n


def _fox_split(w_in):
    t3 = 3 * TOK_WIDTH
    pad = jnp.zeros(w_in.shape[:-1] + (LANES - FOX_HEADS,), w_in.dtype)
    return jnp.concatenate([w_in[..., :t3], w_in[..., t3 + FOX_HEADS:], w_in[..., t3:t3 + FOX_HEADS], pad], axis=-1)


def _fox_unsplit(w):
    t3 = 3 * TOK_WIDTH
    return jnp.concatenate([w[..., :t3], w[..., t3 + MEM_WIDTH:t3 + MEM_WIDTH + FOX_HEADS], w[..., t3:t3 + MEM_WIDTH]],
                           axis=-1)


def _fox_forward(h, w_split, b_f, g_q, g_k, tag, carry=None):
    t3 = 3 * TOK_WIDTH
    proj = matmul(h, w_split, f"{tag}_proj")
    qkv = _heads(proj[:, :t3], 3 * FOX_HEADS)
    mq = _heads(proj[:, t3:t3 + MEM_WIDTH], MEM_HEADS)
    f_pad = proj[:, t3 + MEM_WIDTH:]
    b_pad = jnp.pad(b_f.reshape(1, -1), ((0, 0), (0, LANES - FOX_HEADS)))
    gains = jnp.concatenate([_head_gain(g_q, FOX_HEADS), _head_gain(g_k, FOX_HEADS)], axis=0)
    qk = qkv[:2 * FOX_HEADS]
    scaled = jnp.concatenate([gains[:FOX_HEADS] * QK_SCALE, gains[FOX_HEADS:]], axis=0)
    qkn = rmsnorm_fwd(qk, scaled, BF16, f"{tag}_qknorm")
    v = qkv[2 * FOX_HEADS:]
    c = fox_gate_fwd(f_pad, b_pad, f"{tag}_gate")
    ck = c[:, :FOX_HEADS].T[:, None, :]
    qn, kn = qkn[:FOX_HEADS], qkn[FOX_HEADS:]
    o, lse, *carried = attn_fwd(qn, kn, _widen_v(v), ck, True, f"{tag}_attn", carry)
    return o, mq, (qk, gains, qn, kn, v, ck, o, lse, f_pad, b_pad), carried


def _fox_backward(do, dmq, saved, tag, carry=None):
    qk, gains, qn, kn, v, ck, o, lse, f_pad, b_pad = saved
    dqn, dkn, dv, dck, dcq, *carried = attn_bwd(qn, kn, v, ck, o, lse, do, True, f"{tag}_dattn", carry)
    dqk, dgains = rmsnorm_bwd(qk, gains, jnp.concatenate([dqn * QK_SCALE, dkn], axis=0), f"{tag}_dqknorm")
    dc = jnp.pad((dck[:, 0, :] + dcq[:, :, 0]).T, ((0, 0), (0, LANES - FOX_HEADS)))
    df, db = fox_gate_bwd(dc, f_pad, b_pad, f"{tag}_dgate")
    dproj = jnp.concatenate([_unheads(dqk), _unheads(dv), _unheads(dmq), df], axis=1)
    dgains = dgains.reshape(2, FOX_HEADS, HEAD_DIM).sum(axis=1)
    return dproj, db[0, :FOX_HEADS], dgains[0], dgains[1], carried


def _gmlp_forward(h, w_in_t, v_gain, w_s, b_s, tag):
    proj = matmul(h, w_in_t, f"{tag}_proj", transpose_b=True)
    pu = _heads(proj[:, :TOK_WIDTH], FOX_HEADS)
    pv = _heads(proj[:, TOK_WIDTH:2 * TOK_WIDTH], FOX_HEADS)
    mq = _heads(proj[:, 2 * TOK_WIDTH:], MEM_HEADS)
    gain = v_gain.reshape(FOX_HEADS, 1, HEAD_DIM)
    b = b_s[:, :, None]
    o = gmlp_fwd(pu, pv, gain, w_s, b, f"{tag}_sgu")
    return o, mq, (pu, pv, gain, w_s, b)


def _gmlp_backward(do, dmq, saved, tag):
    pu, pv, gain, w_s, b = saved
    dpu, dpv, dw, db, dg = gmlp_bwd(pu, pv, gain, w_s, b, do, f"{tag}_dsgu")
    dproj = jnp.concatenate([_unheads(dpu), _unheads(dpv), _unheads(dmq)], axis=1)
    return dproj, dg.reshape(-1), dw, db[:, :, 0]


BIG = ("ffn1_w_in", "ffn1_w_out", "ffn2_w_in", "ffn2_w_out", "w_out", "mem_w_kv", "fox_w_in", "gmlp_w_in")
COLUMN_SHARDED = ("ffn1_w_in", "ffn2_w_in", "gmlp_w_in")
REPLICATED =("norm_ffn1", "norm_mix", "norm_ffn2", "mem_norm", "mem_q_norm", "mem_k_norm", "fox_b_f",
              "fox_q_norm", "fox_k_norm", "gmlp_w_s", "gmlp_b_s")
WEIGHTS = ("norm_ffn1", "ffn1_w_in", "ffn1_w_out", "norm_mix", "norm_ffn2", "ffn2_w_in", "ffn2_w_out", "w_out",
           "mem_norm", "mem_w_kv", "mem_q_norm", "mem_k_norm", "fox_w_in", "fox_b_f", "fox_q_norm", "fox_k_norm",
           "gmlp_w_in", "gmlp_v_norm", "gmlp_w_s", "gmlp_b_s")


def _to_transport(name, a):
    if name in COLUMN_SHARDED:
        return jnp.swapaxes(a, -1, -2)
    return _fox_split(a) if name == "fox_w_in" else a


def _from_transport(name, a):
    if name in COLUMN_SHARDED:
        return jnp.swapaxes(a, -1, -2)
    return _fox_unsplit(a) if name == "fox_w_in" else a


def kernel(x, mem, norm_ffn1, ffn1_w_in, ffn1_w_out, norm_mix, norm_ffn2, ffn2_w_in, ffn2_w_out, w_out, mem_norm, mem_w_kv, mem_q_norm, mem_k_norm, fox_w_in, fox_b_f, fox_q_norm, fox_k_norm, gmlp_w_in, gmlp_v_norm, gmlp_w_s, gmlp_b_s, loss_target, m_norm_ffn1, m_ffn1_w_in, m_ffn1_w_out, m_norm_mix, m_norm_ffn2, m_ffn2_w_in, m_ffn2_w_out, m_w_out, m_mem_norm, m_mem_w_kv, m_mem_q_norm, m_mem_k_norm, m_fox_w_in, m_fox_b_f, m_fox_q_norm, m_fox_k_norm, m_gmlp_w_in, m_gmlp_v_norm, m_gmlp_w_s, m_gmlp_b_s, v_norm_ffn1, v_ffn1_w_in, v_ffn1_w_out, v_norm_mix, v_norm_ffn2, v_ffn2_w_in, v_ffn2_w_out, v_w_out, v_mem_norm, v_mem_w_kv, v_mem_q_norm, v_mem_k_norm, v_fox_w_in, v_fox_b_f, v_fox_q_norm, v_fox_k_norm, v_gmlp_w_in, v_gmlp_v_norm, v_gmlp_w_s, v_gmlp_b_s):
    w = dict(norm_ffn1=norm_ffn1, ffn1_w_in=ffn1_w_in, ffn1_w_out=ffn1_w_out, norm_mix=norm_mix, norm_ffn2=norm_ffn2, ffn2_w_in=ffn2_w_in, ffn2_w_out=ffn2_w_out, w_out=w_out, mem_norm=mem_norm, mem_w_kv=mem_w_kv, mem_q_norm=mem_q_norm, mem_k_norm=mem_k_norm, fox_w_in=fox_w_in, fox_b_f=fox_b_f, fox_q_norm=fox_q_norm, fox_k_norm=fox_k_norm, gmlp_w_in=gmlp_w_in, gmlp_v_norm=gmlp_v_norm, gmlp_w_s=gmlp_w_s, gmlp_b_s=gmlp_b_s)
    m = dict(norm_ffn1=m_norm_ffn1, ffn1_w_in=m_ffn1_w_in, ffn1_w_out=m_ffn1_w_out, norm_mix=m_norm_mix, norm_ffn2=m_norm_ffn2, ffn2_w_in=m_ffn2_w_in, ffn2_w_out=m_ffn2_w_out, w_out=m_w_out, mem_norm=m_mem_norm, mem_w_kv=m_mem_w_kv, mem_q_norm=m_mem_q_norm, mem_k_norm=m_mem_k_norm, fox_w_in=m_fox_w_in, fox_b_f=m_fox_b_f, fox_q_norm=m_fox_q_norm, fox_k_norm=m_fox_k_norm, gmlp_w_in=m_gmlp_w_in, gmlp_v_norm=m_gmlp_v_norm, gmlp_w_s=m_gmlp_w_s, gmlp_b_s=m_gmlp_b_s)
    v = dict(norm_ffn1=v_norm_ffn1, ffn1_w_in=v_ffn1_w_in, ffn1_w_out=v_ffn1_w_out, norm_mix=v_norm_mix, norm_ffn2=v_norm_ffn2, ffn2_w_in=v_ffn2_w_in, ffn2_w_out=v_ffn2_w_out, w_out=v_w_out, mem_norm=v_mem_norm, mem_w_kv=v_mem_w_kv, mem_q_norm=v_mem_q_norm, mem_k_norm=v_mem_k_norm, fox_w_in=v_fox_w_in, fox_b_f=v_fox_b_f, fox_q_norm=v_fox_q_norm, fox_k_norm=v_fox_k_norm, gmlp_w_in=v_gmlp_w_in, gmlp_v_norm=v_gmlp_v_norm, gmlp_w_s=v_gmlp_w_s, gmlp_b_s=v_gmlp_b_s)

    depth = norm_ffn1.shape[0]
    x0 = x[0]
    mem0 = mem[0]
    target = loss_target[0]
    me = _slot(*_position())

    keys = [(n, i) for n in BIG for i in range(w[n].shape[0])]
    local = {k: _to_transport(k[0], w[k[0]][k[1]]) for k in keys}
    n_gain, gain_len = gmlp_v_norm.shape
    pad_gain = lambda a: jnp.pad(a, ((0, 8 - n_gain), (0, LANES - gain_len)))
    edge = [("ffn1_w_in", 0), ("ffn1_w_out", 0), ("fox_w_in", 0)]
    inner = [k for k in keys if k not in edge]
    stack = lambda g: g.reshape(-1, g.shape[-1])
    full = {k: stack(g) for k, g in zip(edge, all_gather_multi([local[k].astype(BF16) for k in edge], "gather_weights"))}
    gather_inner = (False, [local[k].astype(BF16) for k in inner] + [pad_gain(gmlp_v_norm)])

    mem_n = rmsnorm_fwd(mem0[None], mem_norm.reshape(1, 1, -1), BF16, "mem_norm")[0]
    saved = []
    xi = x0
    for i in range(depth):
        kind, j = i % 2, i // 2
        x1, ffn1_saved = _ffn_forward(xi, norm_ffn1[i], full["ffn1_w_in", i], full["ffn1_w_out", i], f"l{i}_ffn1")
        h = rmsnorm_fwd(x1[None], norm_mix[i].reshape(1, 1, -1), BF16, f"l{i}_mixnorm")[0]
        if kind == 0:
            tok, mq, mix_saved, arrived = _fox_forward(h, full["fox_w_in", j], fox_b_f[j], fox_q_norm[j], fox_k_norm[j],
                                                       f"l{i}_fox", gather_inner if i == 0 else None)
            if i == 0:
                full.update({k: stack(g) for k, g in zip(inner, arrived)})
                v_gain_full = arrived[-1][:, :n_gain, :gain_len]
        else:
            tok, mq, mix_saved = _gmlp_forward(h, full["gmlp_w_in", j], v_gain_full[:, j, :].reshape(-1), gmlp_w_s[j],
                                               gmlp_b_s[j], f"l{i}_gmlp")
        mo, mem_saved = _mem_forward(mq, mem_n, full["mem_w_kv", i], mem_q_norm[i], mem_k_norm[i], f"l{i}_mem")
        cat = _unheads(jnp.concatenate([tok, mo], axis=0)).astype(BF16)
        x2 = matmul(cat, full["w_out", i], f"l{i}_wout", residual=x1)
        x3, ffn2_saved = _ffn_forward(x2, norm_ffn2[i], full["ffn2_w_in", i], full["ffn2_w_out", i], f"l{i}_ffn2")
        saved.append((ffn1_saved, x1, h, mix_saved, mem_saved, cat, ffn2_saved))
        xi = x3

    dy, loss_part = loss_head(xi, target, "loss_head")
    loss = lax.psum(loss_part, ("x", "y", "c"))

    small = {n: [None] * w[n].shape[0] for n in REPLICATED + ("gmlp_v_norm",) if n != "mem_norm"}
    big = {}
    slots_of = lambda g: g.reshape((N_DEV, -1, g.shape[-1]))
    dmem_n = None
    for i in reversed(range(depth)):
        kind, j = i % 2, i // 2
        ffn1_saved, x1, h, mix_saved, mem_saved, cat, ffn2_saved = saved[i]
        dy, small["norm_ffn2"][i], big["ffn2_w_in", i], big["ffn2_w_out", i] = _ffn_backward(
            dy, ffn2_saved, norm_ffn2[i], full["ffn2_w_in", i], full["ffn2_w_out", i], f"l{i}_ffn2")
        big["w_out", i] = matmul_tn(cat, dy, f"l{i}_dwout", out_dtype=BF16)
        dcat = _heads(matmul(dy, full["w_out", i], f"l{i}_dcat", transpose_b=True), FOX_HEADS + MEM_HEADS)
        dmq, small["mem_q_norm"][i], small["mem_k_norm"][i], big["mem_w_kv", i], dmem_i = _mem_backward(
            dcat[FOX_HEADS:], mem_saved, mem_n, full["mem_w_kv", i], f"l{i}_mem")
        dmem_n = dmem_i if dmem_n is None else dmem_n + dmem_i
        if kind == 0:
            scatter_inner = (True, [slots_of(big[k]) for k in inner]) if i == 0 else None
            dproj, small["fox_b_f"][j], small["fox_q_norm"][j], small["fox_k_norm"][j], arrived = _fox_backward(
                dcat[:FOX_HEADS], dmq, mix_saved, f"l{i}_fox", scatter_inner)
            if i == 0:
                got = dict(zip(inner, arrived))
            big["fox_w_in", j] = matmul_tn(h, dproj, f"l{i}_fox_dwin", out_dtype=BF16)
            dh = matmul(dproj, full["fox_w_in", j], f"l{i}_fox_dh", transpose_b=True)
        else:
            dproj, small["gmlp_v_norm"][j], small["gmlp_w_s"][j], small["gmlp_b_s"][j] = _gmlp_backward(
                dcat[:FOX_HEADS], dmq, mix_saved, f"l{i}_gmlp")
            big["gmlp_w_in", j] = matmul_tn(dproj, h, f"l{i}_gmlp_dwin", out_dtype=BF16)
            dh = matmul(dproj, full["gmlp_w_in", j], f"l{i}_gmlp_dh")
        dy, dg_mix = rmsnorm_bwd(x1[None], norm_mix[i].reshape(1, 1, -1), dh[None], f"l{i}_dmixnorm", residual=dy[None])
        dy, small["norm_mix"][i] = dy[0], dg_mix.reshape(-1)
        dy, small["norm_ffn1"][i], big["ffn1_w_in", i], big["ffn1_w_out", i] = _ffn_backward(
            dy, ffn1_saved, norm_ffn1[i], full["ffn1_w_in", i], full["ffn1_w_out", i], f"l{i}_ffn1")
    grad_x = dy[None]
    _, dg_mem = rmsnorm_bwd(mem0[None], mem_norm.reshape(1, 1, -1), dmem_n[None], "dmem_norm")
    small = {n: jnp.stack(g) for n, g in small.items()}
    small["mem_norm"] = dg_mem.reshape(-1)

    got.update(zip(edge, all_to_all_multi([slots_of(big[k]) for k in edge], "exchange_grads")))
    results = {n: [[None] * w[n].shape[0] for _ in range(4)] for n in BIG}
    for k in keys:
        n, i = k
        outs = adamw(local[k], got[k], _to_transport(n, m[n][i]), _to_transport(n, v[n][i]), f"adamw_{n}_{i}")
        for q in range(4):
            results[n][q][i] = _from_transport(n, outs[q])
    sharded = {n: [jnp.stack(r) for r in results[n]] for n in BIG}

    small_names = REPLICATED + ("gmlp_v_norm",)
    (small_got,) = all_gather_multi([_pack([small[n] for n in small_names])], "gather_small_grads")
    rep_shapes = [w[n].shape for n in REPLICATED]
    gain_seg = jnp.zeros((n_gain, N_DEV * gain_len), F32)
    pack_rep = lambda d: _pack([d[n] for n in REPLICATED] + [gain_seg])
    outs = adamw(pack_rep(w), small_got, pack_rep(m), pack_rep(v), "adamw_replicated")
    replicated = [dict(zip(REPLICATED, _unpack(o, rep_shapes))) for o in outs]
    gain_parts = _unpack(small_got, rep_shapes + [(n_gain, N_DEV * gain_len)], lead=(N_DEV,))[-1]
    gain_slots = lax.dynamic_slice_in_dim(gain_parts, me * gain_len, gain_len, axis=2)
    gain_slots = jnp.pad(gain_slots, ((0, 0), (0, 8 - n_gain), (0, LANES - gain_len)))
    outs = adamw(pad_gain(gmlp_v_norm), gain_slots, pad_gain(m["gmlp_v_norm"]), pad_gain(v["gmlp_v_norm"]),
                 "adamw_gmlp_v_norm")
    sharded["gmlp_v_norm"] = [o[:n_gain, :gain_len] for o in outs]

    out = [loss, grad_x]
    for q in range(4):
        out += [(replicated[q][n] if n in REPLICATED else sharded[n][q]) for n in WEIGHTS]
    return tuple(out)
```

```python
import functools
import math

import jax
import jax.numpy as jnp
from jax import lax
from jax.experimental import pallas as pl
from jax.experimental.pallas import tpu as pltpu

F32 = jnp.float32
BF16 = jnp.bfloat16

EPS = 1e-6
HEAD_DIM = 64
FOX_HEADS = 12
MEM_HEADS = 4
TOK_WIDTH = FOX_HEADS * HEAD_DIM
MEM_WIDTH = MEM_HEADS * HEAD_DIM
CHUNK = 128
LANES = 128
N_DEV = 8
SEG_ALIGN = 16 * LANES
PACK_ROWS = 1024
PACK_ALIGN = PACK_ROWS * LANES

ADAM_LR = 0.001
ADAM_B1 = 0.9
ADAM_B2 = 0.999
ADAM_EPS = 1e-08
ADAM_WD = 0.01
ADAM_STEP = 10

VMEM_LIMIT_BYTES = 48 * 1024 * 1024
MESH = pl.DeviceIdType.MESH
CONTRACT_0 = (((0,), (0,)), ((), ()))
CONTRACT_1 = (((1,), (1,)), ((), ()))


def _params(*semantics):
    return pltpu.CompilerParams(dimension_semantics=semantics, vmem_limit_bytes=VMEM_LIMIT_BYTES)


def _pick(n, candidates):
    for c in candidates:
        if c <= n and n % c == 0:
            return c
    return n


def _sigmoid(x):
    return 1.0 / (1.0 + jnp.exp(-x))


def _row_tile(r, w):
    return _pick(r, (1024,) if w >= 512 else (2048, 1024, 512, 256))


def rmsnorm_fwd(x, gain, out_dtype, name):
    g_, r_, w_ = x.shape
    tr = _row_tile(r_, w_)

    def body(x_ref, g_ref, y_ref):
        xv = x_ref[0].astype(F32)
        r = lax.rsqrt(jnp.mean(xv * xv, axis=-1, keepdims=True) + EPS)
        y_ref[0] = (xv * r * g_ref[0]).astype(y_ref.dtype)

    return pl.pallas_call(
        body, name=name, grid=(g_, r_ // tr),
        in_specs=[pl.BlockSpec((1, tr, w_), lambda g, i: (g, i, 0)),
                  pl.BlockSpec((1, 1, w_), lambda g, i: (g, 0, 0))],
        out_specs=pl.BlockSpec((1, tr, w_), lambda g, i: (g, i, 0)),
        out_shape=jax.ShapeDtypeStruct((g_, r_, w_), out_dtype),
        compiler_params=_params("parallel", "parallel"),
    )(x, gain)


def rmsnorm_bwd(x, gain, dy, name, residual=None):
    g_, r_, w_ = x.shape
    tr = _row_tile(r_, w_)
    has_res = residual is not None

    def body(*refs):
        if has_res:
            x_ref, g_ref, dy_ref, res_ref, dx_ref, dg_ref = refs
        else:
            x_ref, g_ref, dy_ref, dx_ref, dg_ref = refs
        xv = x_ref[0].astype(F32)
        dyv = dy_ref[0].astype(F32)
        r = lax.rsqrt(jnp.mean(xv * xv, axis=-1, keepdims=True) + EPS)
        n = xv * r
        dn = dyv * g_ref[0]
        dx = r * (dn - n * jnp.mean(dn * n, axis=-1, keepdims=True))
        if has_res:
            dx = dx + res_ref[0]
        dx_ref[0] = dx
        part = jnp.sum(dyv * n, axis=0, keepdims=True)

        @pl.when(pl.program_id(1) == 0)
        def _():
            dg_ref[0] = part

        @pl.when(pl.program_id(1) != 0)
        def _():
            dg_ref[0] += part

    row = pl.BlockSpec((1, tr, w_), lambda g, i: (g, i, 0))
    vec = pl.BlockSpec((1, 1, w_), lambda g, i: (g, 0, 0))
    operands = (x, gain, dy) + ((residual,) if has_res else ())
    return pl.pallas_call(
        body, name=name, grid=(g_, r_ // tr),
        in_specs=[row, vec, row] + ([row] if has_res else []),
        out_specs=(row, vec),
        out_shape=(jax.ShapeDtypeStruct((g_, r_, w_), F32), jax.ShapeDtypeStruct((g_, 1, w_), F32)),
        compiler_params=_params("parallel", "arbitrary"),
    )(*operands)


def matmul(a, b, name, out_dtype=F32, residual=None, scale=None, transpose_b=False):
    m_, k_ = a.shape
    n_ = b.shape[0] if transpose_b else b.shape[1]
    tm = _pick(m_, (512, 256, 128))
    tn = _pick(n_, (1408, 1024, 896, 512, 256, 128))
    has_res = residual is not None

    def body(*refs):
        if has_res:
            a_ref, b_ref, res_ref, o_ref = refs
        else:
            a_ref, b_ref, o_ref = refs
        av, bv = a_ref[...].astype(BF16), b_ref[...].astype(BF16)
        if transpose_b:
            acc = lax.dot_general(av, bv, CONTRACT_1, preferred_element_type=F32)
        else:
            acc = jnp.dot(av, bv, preferred_element_type=F32)
        if scale is not None:
            acc = acc * scale
        if has_res:
            acc = acc + res_ref[...]
        o_ref[...] = acc.astype(o_ref.dtype)

    out_spec = pl.BlockSpec((tm, tn), lambda j, i: (i, j))
    b_spec = pl.BlockSpec((tn, k_), lambda j, i: (j, 0)) if transpose_b else pl.BlockSpec((k_, tn), lambda j, i: (0, j))
    operands = (a, b) + ((residual,) if has_res else ())
    return pl.pallas_call(
        body, name=name, grid=(n_ // tn, m_ // tm),
        in_specs=[pl.BlockSpec((tm, k_), lambda j, i: (i, 0)), b_spec] + ([out_spec] if has_res else []),
        out_specs=out_spec,
        out_shape=jax.ShapeDtypeStruct((m_, n_), out_dtype),
        compiler_params=_params("parallel", "parallel"),
    )(*operands)


def matmul_tn(a, b, name, scale=None, out_dtype=F32):
    s_, k_ = a.shape
    n_ = b.shape[1]
    tk = _pick(k_, (1024, 1408, 896, 512, 256, 128))
    tn = _pick(n_, (1408, 1024, 896, 512, 256, 128))
    ts = _pick(s_, (512, 256, 128))
    ns = s_ // ts

    def body(a_ref, b_ref, o_ref, acc_ref):
        part = lax.dot_general(a_ref[...].astype(BF16), b_ref[...].astype(BF16), CONTRACT_0,
                               preferred_element_type=F32)
        step = pl.program_id(2)

        @pl.when(step == 0)
        def _():
            acc_ref[...] = part

        @pl.when(step != 0)
        def _():
            acc_ref[...] += part

        @pl.when(step == ns - 1)
        def _():
            acc = acc_ref[...]
            o_ref[...] = (acc if scale is None else acc * scale).astype(o_ref.dtype)

    return pl.pallas_call(
        body, name=name, grid=(k_ // tk, n_ // tn, ns),
        in_specs=[pl.BlockSpec((ts, tk), lambda i, j, s: (s, i)),
                  pl.BlockSpec((ts, tn), lambda i, j, s: (s, j))],
        out_specs=pl.BlockSpec((tk, tn), lambda i, j, s: (i, j)),
        out_shape=jax.ShapeDtypeStruct((k_, n_), out_dtype),
        scratch_shapes=[pltpu.VMEM((tk, tn), F32)],
        compiler_params=_params("parallel", "parallel", "arbitrary"),
    )(a, b)


def ffn_in(h, w_in_t, name, carry=None):
    s_, d_ = h.shape
    f_ = w_in_t.shape[0] // 2
    tm = _pick(s_, (512, 256, 128))
    tn = _pick(f_, (1408, 1024, 512, 256, 128))
    nb = f_ // tn

    def body(h_ref, wa_ref, wb_ref, a_ref, b_ref, u_ref):
        hv = h_ref[...]
        a = lax.dot_general(hv, wa_ref[...], CONTRACT_1, preferred_element_type=F32)
        b = lax.dot_general(hv, wb_ref[...], CONTRACT_1, preferred_element_type=F32)
        a_ref[...] = a.astype(BF16)
        b_ref[...] = b.astype(BF16)
        u_ref[...] = (a * _sigmoid(a) * b).astype(BF16)

    o_spec = pl.BlockSpec((tm, tn), lambda j, i: (i, j))
    out = jax.ShapeDtypeStruct((s_, f_), BF16)
    return _carried_call(
        body, name, (nb, s_ // tm),
        [pl.BlockSpec((tm, d_), lambda j, i: (i, 0)),
         pl.BlockSpec((tn, d_), lambda j, i: (j, 0)),
         pl.BlockSpec((tn, d_), lambda j, i: (j + nb, 0))],
        [o_spec, o_spec, o_spec], [out, out, out], [], [h, w_in_t, w_in_t], carry)


def _carried_call(body, name, grid, in_specs, out_specs, out_shape, scratch, operands, carry):
    if not carry:
        return pl.pallas_call(
            body, name=name, grid=grid, in_specs=in_specs, out_specs=tuple(out_specs), out_shape=tuple(out_shape),
            scratch_shapes=scratch, compiler_params=_params(*["parallel"] * len(grid)))(*operands)
    counts = (len(in_specs), len(carry[1]), len(out_specs), len(carry[1]), len(scratch), 3)

    def carrying(*refs):
        ins, c_in, outs, c_out, scr, sems = _split_refs(refs, counts)
        step_no, last_no = _grid_step(grid)

        @pl.when(step_no == 0)
        def _():
            for cp in _direct_copies(carry[0], c_in, c_out, *sems):
                cp.start()

        body(*ins, *outs, *scr)

        @pl.when(step_no == last_no)
        def _():
            for cp in _direct_copies(carry[0], c_in, c_out, *sems):
                cp.wait()

    c_in_specs, c_out_specs, c_shapes, c_sems = _exchange_operands(*carry)
    return pl.pallas_call(
        carrying, name=name, grid=grid, in_specs=in_specs + c_in_specs, out_specs=tuple(out_specs + c_out_specs),
        out_shape=tuple(out_shape + c_shapes), scratch_shapes=scratch + c_sems,
        compiler_params=_params(*["arbitrary"] * len(grid)))(*operands, *carry[1])


def ffn_bwd_act(dy, w_out, a, b, name, carry=None):
    s_, d_ = dy.shape
    f_ = w_out.shape[0]
    tm = _pick(s_, (512, 256, 128))
    tn = _pick(f_, (1408, 1024, 512, 256, 128))

    def body(dy_ref, w_ref, a_ref, b_ref, da_ref, db_ref):
        du = 0.5 * lax.dot_general(dy_ref[...].astype(BF16), w_ref[...], CONTRACT_1, preferred_element_type=F32)
        av = a_ref[...].astype(F32)
        bv = b_ref[...].astype(F32)
        sig = _sigmoid(av)
        da_ref[...] = (du * bv * (sig * (1.0 + av * (1.0 - sig)))).astype(BF16)
        db_ref[...] = (du * (av * sig)).astype(BF16)

    t_spec = pl.BlockSpec((tm, tn), lambda j, i: (i, j))
    out = jax.ShapeDtypeStruct((s_, f_), BF16)
    return _carried_call(
        body, name, (f_ // tn, s_ // tm),
        [pl.BlockSpec((tm, d_), lambda j, i: (i, 0)), pl.BlockSpec((tn, d_), lambda j, i: (j, 0)), t_spec, t_spec],
        [t_spec, t_spec], [out, out], [], [dy, w_out, a, b], carry)


def _fold(a, b, n, forward):
    if forward:
        low = b <= a
        return jnp.where(low, a, n - 1 - a), jnp.where(low, b, b - a - 1)
    low = b < n - a
    return jnp.where(low, a, n - 1 - a), jnp.where(low, a + b, b - 1)


def _attn_grid(h_, n_outer, n_inner, causal, forward):
    if causal and n_outer % 2 == 0:
        return (h_, n_outer // 2, n_outer + 1), lambda a, b: _fold(a, b, n_outer, forward)
    return (h_, n_outer, n_inner), lambda a, b: (a, b)


QK_SCALE = 1.0 / math.sqrt(HEAD_DIM)
SUM_LANE = HEAD_DIM


def _scores(q, k, ck, masked):
    s = lax.dot_general(q, k, CONTRACT_1, preferred_element_type=F32)
    if ck is not None:
        s = s - ck
    if masked:
        row = lax.broadcasted_iota(jnp.int32, s.shape, 0)
        col = lax.broadcasted_iota(jnp.int32, s.shape, 1)
        s = jnp.where(col <= row, s, -jnp.inf)
    return s


def _widen_v(v):
    ones = jnp.ones(v.shape[:2] + (1,), BF16)
    zeros = jnp.zeros(v.shape[:2] + (LANES - HEAD_DIM - 1,), BF16)
    return jnp.concatenate([v.astype(BF16), ones, zeros], axis=-1)


def _grid_step(grid):
    step = 0
    for axis, n in enumerate(grid):
        step = step * n + pl.program_id(axis)
    return step, math.prod(grid) - 1


def _split_refs(refs, counts):
    out, at = [], 0
    for n in counts:
        out.append(refs[at:at + n])
        at += n
    return out


def attn_fwd(q, k, v_wide, ck, causal, name, carry=None):
    h_, sq, d_ = q.shape
    sk = k.shape[1]
    tq = _pick(sq, (512, 256, 128))
    tk = tq if causal else _pick(sk, (512, 256, 128))
    nq, nk = sq // tq, sk // tk
    assert not causal or nq == 1 or nq % 2 == 0
    bias = ck is not None
    grid, blocks = _attn_grid(h_, nq, nk, causal, True)
    nc = len(carry[1]) if carry else 0

    def body(*refs):
        ins, c_in, outs, c_out, scratch, sems = _split_refs(refs, (4 if bias else 3, nc, 2, nc, 2, 3 if carry else 0))
        q_ref, k_ref, v_ref = ins[:3]
        ck_ref = ins[3] if bias else None
        (o_ref, lse_ref), (m_sc, acc_sc) = outs, scratch
        i, j = blocks(pl.program_id(1), pl.program_id(2))
        if carry:
            step_no, last_no = _grid_step(grid)

            @pl.when(step_no == 0)
            def _():
                for cp in _direct_copies(carry[0], c_in, c_out, *sems):
                    cp.start()

        @pl.when(j == 0)
        def _():
            m_sc[...] = jnp.full(m_sc.shape, -jnp.inf, F32)
            acc_sc[...] = jnp.zeros(acc_sc.shape, F32)

        def step(masked):
            s = _scores(q_ref[0], k_ref[0], ck_ref[0] if bias else None, masked)
            m_prev = m_sc[...]
            m_new = jnp.maximum(m_prev, jnp.max(s, axis=1, keepdims=True))
            p = jnp.exp(s - m_new)
            acc_sc[...] = jnp.exp(m_prev - m_new) * acc_sc[...] + jnp.dot(p.astype(BF16), v_ref[0],
                                                                         preferred_element_type=F32)
            m_sc[...] = m_new

        if causal:
            pl.when(j < i)(functools.partial(step, False))
            pl.when(j == i)(functools.partial(step, True))
        else:
            step(False)

        @pl.when(j == (i if causal else nk - 1))
        def _():
            acc = acc_sc[...]
            lane = lax.broadcasted_iota(jnp.int32, acc.shape, 1)
            l = jnp.sum(jnp.where(lane == SUM_LANE, acc, 0.0), axis=1, keepdims=True)
            o_ref[0] = acc_sc[:, :HEAD_DIM] / l
            lse_ref[0] = m_sc[...] + jnp.log(l)

        if carry:
            @pl.when(step_no == last_no)
            def _():
                for cp in _direct_copies(carry[0], c_in, c_out, *sems):
                    cp.wait()

    q_spec = pl.BlockSpec((1, tq, d_), lambda h, a, b: (h, blocks(a, b)[0], 0))
    q1_spec = pl.BlockSpec((1, tq, 1), lambda h, a, b: (h, blocks(a, b)[0], 0))
    k_spec = pl.BlockSpec((1, tk, d_), lambda h, a, b: (h, blocks(a, b)[1], 0))
    in_specs = [q_spec, k_spec, pl.BlockSpec((1, tk, LANES), lambda h, a, b: (h, blocks(a, b)[1], 0))]
    operands = [q, k, v_wide]
    if bias:
        in_specs.append(pl.BlockSpec((1, 1, tk), lambda h, a, b: (h, 0, blocks(a, b)[1])))
        operands.append(ck)
    out_specs = [q_spec, q1_spec]
    out_shape = [jax.ShapeDtypeStruct((h_, sq, d_), F32), jax.ShapeDtypeStruct((h_, sq, 1), F32)]
    scratch = [pltpu.VMEM((tq, 1), F32), pltpu.VMEM((tq, LANES), F32)]
    if carry:
        c_in_specs, c_out_specs, c_shapes, c_sems = _exchange_operands(*carry)
        in_specs, out_specs, out_shape, scratch = in_specs + c_in_specs, out_specs + c_out_specs, out_shape + c_shapes, scratch + c_sems
        operands = operands + list(carry[1])
    return pl.pallas_call(
        body, name=name, grid=grid,
        in_specs=in_specs, out_specs=tuple(out_specs), out_shape=tuple(out_shape), scratch_shapes=scratch,
        compiler_params=_params("arbitrary" if carry else "parallel", "arbitrary", "arbitrary"),
    )(*operands)


def attn_bwd(q, k, v, ck, o, lse, do, causal, name, carry=None):
    h_, sq, d_ = q.shape
    sk = k.shape[1]
    tq = _pick(sq, (512, 256, 128))
    tk = tq if causal else _pick(sk, (512, 256, 128))
    nq, nk = sq // tq, sk // tk
    assert not causal or nq == 1 or nq % 2 == 0
    bias = ck is not None
    grid, blocks = _attn_grid(h_, nk, nq, causal, False)
    nc = len(carry[1]) if carry else 0

    def body(*refs):
        ins, c_in, outs, c_out, scratch, sems = _split_refs(
            refs, (7 if bias else 6, nc, 5 if bias else 3, nc, 3 if bias else 2, 3 if carry else 0))
        if bias:
            q_ref, k_ref, v_ref, ck_ref, o_ref, lse_ref, do_ref = ins
            dq_ref, dk_ref, dv_ref, dc_ref, dcq_ref = outs
            dk_sc, dv_sc, dc_sc = scratch
        else:
            q_ref, k_ref, v_ref, o_ref, lse_ref, do_ref = ins
            dq_ref, dk_ref, dv_ref = outs
            dk_sc, dv_sc = scratch
        j, i = blocks(pl.program_id(1), pl.program_id(2))
        if carry:
            step_no, last_no = _grid_step(grid)

            @pl.when(step_no == 0)
            def _():
                for cp in _direct_copies(carry[0], c_in, c_out, *sems):
                    cp.start()

        @pl.when((pl.program_id(1) == 0) & (pl.program_id(2) == 0))
        def _():
            dq_ref[...] = jnp.zeros(dq_ref.shape, F32)
            if bias:
                dcq_ref[...] = jnp.zeros(dcq_ref.shape, F32)

        @pl.when(i == (j if causal else 0))
        def _():
            dk_sc[...] = jnp.zeros(dk_sc.shape, F32)
            dv_sc[...] = jnp.zeros(dv_sc.shape, F32)
            if bias:
                dc_sc[...] = jnp.zeros(dc_sc.shape, F32)

        def step(masked):
            qb, kb = q_ref[0], k_ref[0]
            dof = do_ref[0]
            dob = dof.astype(BF16)
            s = _scores(qb, kb, ck_ref[0] if bias else None, masked)
            p = jnp.exp(s - lse_ref[0])
            dp = lax.dot_general(dob, v_ref[0].astype(BF16), CONTRACT_1, preferred_element_type=F32)
            delta = jnp.sum(dof * o_ref[0], axis=1, keepdims=True)
            ds = p * (dp - delta)
            dsb = ds.astype(BF16)
            dv_sc[...] += lax.dot_general(p.astype(BF16), dob, CONTRACT_0, preferred_element_type=F32)
            dk_sc[...] += lax.dot_general(dsb, qb, CONTRACT_0, preferred_element_type=F32)
            rows = pl.ds(pl.multiple_of(i * tq, tq), tq)
            dq_ref[0, rows, :] += jnp.dot(dsb, kb, preferred_element_type=F32)
            if bias:
                dc_sc[...] -= jnp.sum(ds, axis=0, keepdims=True)
                dcq_ref[0, rows, :] += jnp.sum(ds, axis=1, keepdims=True)

        if causal:
            pl.when(i > j)(functools.partial(step, False))
            pl.when(i == j)(functools.partial(step, True))
        else:
            step(False)

        @pl.when(i == nq - 1)
        def _():
            dk_ref[0] = dk_sc[...]
            dv_ref[0] = dv_sc[...]
            if bias:
                dc_ref[0] = dc_sc[...]

        if carry:
            @pl.when(step_no == last_no)
            def _():
                for cp in _direct_copies(carry[0], c_in, c_out, *sems):
                    cp.wait()

    q_spec = pl.BlockSpec((1, tq, d_), lambda h, a, b: (h, blocks(a, b)[1], 0))
    q1_spec = pl.BlockSpec((1, tq, 1), lambda h, a, b: (h, blocks(a, b)[1], 0))
    k_spec = pl.BlockSpec((1, tk, d_), lambda h, a, b: (h, blocks(a, b)[0], 0))
    c_spec = pl.BlockSpec((1, 1, tk), lambda h, a, b: (h, 0, blocks(a, b)[0]))
    in_specs = [q_spec, k_spec, k_spec] + ([c_spec] if bias else []) + [q_spec, q1_spec, q_spec]
    operands = [q, k, v] + ([ck] if bias else []) + [o, lse, do]
    out_specs = [pl.BlockSpec((1, sq, d_), lambda h, a, b: (h, 0, 0)), k_spec, k_spec]
    out_shape = [jax.ShapeDtypeStruct((h_, sq, d_), F32), jax.ShapeDtypeStruct((h_, sk, d_), F32),
                 jax.ShapeDtypeStruct((h_, sk, d_), F32)]
    scratch = [pltpu.VMEM((tk, d_), F32), pltpu.VMEM((tk, d_), F32)]
    if bias:
        out_specs += [c_spec, pl.BlockSpec((1, sq, 1), lambda h, a, b: (h, 0, 0))]
        out_shape += [jax.ShapeDtypeStruct((h_, 1, sk), F32), jax.ShapeDtypeStruct((h_, sq, 1), F32)]
        scratch.append(pltpu.VMEM((1, tk), F32))
    if carry:
        c_in_specs, c_out_specs, c_shapes, c_sems = _exchange_operands(*carry)
        in_specs, out_specs, out_shape, scratch = in_specs + c_in_specs, out_specs + c_out_specs, out_shape + c_shapes, scratch + c_sems
        operands = operands + list(carry[1])
    return pl.pallas_call(
        body, name=name, grid=grid,
        in_specs=in_specs, out_specs=tuple(out_specs), out_shape=tuple(out_shape),
        scratch_shapes=scratch,
        compiler_params=_params("arbitrary" if carry else "parallel", "arbitrary", "arbitrary"),
    )(*operands)


def _tri(lower):
    row = lax.broadcasted_iota(jnp.int32, (CHUNK, CHUNK), 0)
    col = lax.broadcasted_iota(jnp.int32, (CHUNK, CHUNK), 1)
    return jnp.where((col <= row) if lower else (col >= row), 1.0, 0.0).astype(F32)


def fox_gate_fwd(f, b, name):
    s_ = f.shape[0]

    def body(f_ref, b_ref, c_ref, carry):
        @pl.when(pl.program_id(0) == 0)
        def _():
            carry[...] = jnp.zeros(carry.shape, F32)

        xv = f_ref[...] + b_ref[...]
        log_f = jnp.minimum(xv, 0.0) - jnp.log(1.0 + jnp.exp(-jnp.abs(xv)))
        c = jnp.dot(_tri(True), log_f, precision=lax.Precision.HIGHEST, preferred_element_type=F32) + carry[...]
        c_ref[...] = c
        carry[...] = c[CHUNK - 1:CHUNK, :]

    blk = pl.BlockSpec((CHUNK, LANES), lambda i: (i, 0))
    return pl.pallas_call(
        body, name=name, grid=(s_ // CHUNK,),
        in_specs=[blk, pl.BlockSpec((1, LANES), lambda i: (0, 0))], out_specs=blk,
        out_shape=jax.ShapeDtypeStruct((s_, LANES), F32),
        scratch_shapes=[pltpu.VMEM((1, LANES), F32)],
        compiler_params=_params("arbitrary"),
    )(f, b)


def fox_gate_bwd(dc, f, b, name):
    s_ = f.shape[0]
    n = s_ // CHUNK

    def body(dc_ref, f_ref, b_ref, df_ref, db_ref, carry):
        @pl.when(pl.program_id(0) == 0)
        def _():
            carry[...] = jnp.zeros(carry.shape, F32)
            db_ref[...] = jnp.zeros(db_ref.shape, F32)

        dlog = jnp.dot(_tri(False), dc_ref[...], precision=lax.Precision.HIGHEST,
                       preferred_element_type=F32) + carry[...]
        df = dlog * _sigmoid(-(f_ref[...] + b_ref[...]))
        df_ref[...] = df
        db_ref[...] += jnp.sum(df, axis=0, keepdims=True)
        carry[...] = dlog[0:1, :]

    blk = pl.BlockSpec((CHUNK, LANES), lambda i: (n - 1 - i, 0))
    vec = pl.BlockSpec((1, LANES), lambda i: (0, 0))
    return pl.pallas_call(
        body, name=name, grid=(n,),
        in_specs=[blk, blk, vec], out_specs=(blk, vec),
        out_shape=(jax.ShapeDtypeStruct((s_, LANES), F32), jax.ShapeDtypeStruct((1, LANES), F32)),
        scratch_shapes=[pltpu.VMEM((1, LANES), F32)],
        compiler_params=_params("arbitrary"),
    )(dc, f, b)


GELU_K = math.sqrt(2.0 / math.pi)
GELU_C = 0.044715


def _gelu(x):
    return 0.5 * x * (1.0 + jnp.tanh(GELU_K * (x + GELU_C * (x * x * x))))


def _gelu_grad(x):
    t = jnp.tanh(GELU_K * (x + GELU_C * (x * x * x)))
    return 0.5 * (1.0 + t) + 0.5 * x * (1.0 - t * t) * (GELU_K * (1.0 + 3.0 * GELU_C * (x * x)))


def _tril_mask():
    row = lax.broadcasted_iota(jnp.int32, (CHUNK, CHUNK), 0)
    col = lax.broadcasted_iota(jnp.int32, (CHUNK, CHUNK), 1)
    return col <= row


def _gmlp_specs(s_, ts):
    row = pl.BlockSpec((1, ts, HEAD_DIM), lambda g, i: (g, i, 0))
    gain = pl.BlockSpec((1, 1, HEAD_DIM), lambda g, i: (g, 0, 0))
    w = pl.BlockSpec((1, CHUNK, CHUNK), lambda g, i: (g, 0, 0))
    b = pl.BlockSpec((1, CHUNK, 1), lambda g, i: (g, 0, 0))
    return row, gain, w, b


def gmlp_fwd(pu, pv, gain, w, b, name):
    g_, s_, d_ = pu.shape
    ts = _pick(s_, (1024, 512, 256, 128))

    def body(pu_ref, pv_ref, g_ref, w_ref, b_ref, o_ref):
        v = _gelu(pv_ref[0])
        r = lax.rsqrt(jnp.mean(v * v, axis=-1, keepdims=True) + EPS)
        vn = (v * r * g_ref[0]).astype(BF16)
        wt = jnp.where(_tril_mask(), w_ref[0], 0.0).astype(BF16)
        for c in range(ts // CHUNK):
            rows = pl.ds(c * CHUNK, CHUNK)
            gate = jnp.dot(wt, vn[c * CHUNK:(c + 1) * CHUNK], preferred_element_type=F32) + b_ref[0]
            o_ref[0, rows, :] = _gelu(pu_ref[0, rows, :]) * gate

    row, gspec, wspec, bspec = _gmlp_specs(s_, ts)
    return pl.pallas_call(
        body, name=name, grid=(g_, s_ // ts),
        in_specs=[row, row, gspec, wspec, bspec], out_specs=row,
        out_shape=jax.ShapeDtypeStruct((g_, s_, d_), F32),
        compiler_params=_params("parallel", "parallel"),
    )(pu, pv, gain, w, b)


def gmlp_bwd(pu, pv, gain, w, b, dout, name):
    g_, s_, d_ = pu.shape
    ts = _pick(s_, (1024, 512, 256, 128))

    def body(pu_ref, pv_ref, g_ref, w_ref, b_ref, do_ref, dpu_ref, dpv_ref, dw_ref, db_ref, dg_ref):
        @pl.when(pl.program_id(1) == 0)
        def _():
            dw_ref[...] = jnp.zeros(dw_ref.shape, F32)
            db_ref[...] = jnp.zeros(db_ref.shape, F32)
            dg_ref[...] = jnp.zeros(dg_ref.shape, F32)

        gain_v = g_ref[0]
        mask = _tril_mask()
        wt = jnp.where(mask, w_ref[0], 0.0).astype(BF16)
        dw = jnp.zeros((CHUNK, CHUNK), F32)
        db = jnp.zeros((CHUNK, 1), F32)
        dg = jnp.zeros((1, d_), F32)
        for c in range(ts // CHUNK):
            rows = pl.ds(c * CHUNK, CHUNK)
            pu_c = pu_ref[0, rows, :]
            pv_c = pv_ref[0, rows, :]
            do_c = do_ref[0, rows, :]
            u = _gelu(pu_c)
            v = _gelu(pv_c)
            r = lax.rsqrt(jnp.mean(v * v, axis=-1, keepdims=True) + EPS)
            n = v * r
            vn = (n * gain_v).astype(BF16)
            gate = jnp.dot(wt, vn, preferred_element_type=F32) + b_ref[0]
            dgate = do_c * u
            dgate_b = dgate.astype(BF16)
            dpu_ref[0, rows, :] = do_c * gate * _gelu_grad(pu_c)
            db = db + jnp.sum(dgate, axis=1, keepdims=True)
            dw = dw + lax.dot_general(dgate_b, vn, CONTRACT_1, preferred_element_type=F32)
            dvn = lax.dot_general(wt, dgate_b, CONTRACT_0, preferred_element_type=F32)
            dg = dg + jnp.sum(dvn * n, axis=0, keepdims=True)
            dn = dvn * gain_v
            dv = r * (dn - n * jnp.mean(dn * n, axis=-1, keepdims=True))
            dpv_ref[0, rows, :] = dv * _gelu_grad(pv_c)
        dw_ref[0] += jnp.where(mask, dw, 0.0)
        db_ref[0] += db
        dg_ref[0] += dg

    row, gspec, wspec, bspec = _gmlp_specs(s_, ts)
    return pl.pallas_call(
        body, name=name, grid=(g_, s_ // ts),
        in_specs=[row, row, gspec, wspec, bspec, row],
        out_specs=(row, row, wspec, bspec, gspec),
        out_shape=(jax.ShapeDtypeStruct((g_, s_, d_), F32), jax.ShapeDtypeStruct((g_, s_, d_), F32),
                   jax.ShapeDtypeStruct((g_, CHUNK, CHUNK), F32), jax.ShapeDtypeStruct((g_, CHUNK, 1), F32),
                   jax.ShapeDtypeStruct((g_, 1, d_), F32)),
        compiler_params=_params("parallel", "arbitrary"),
    )(pu, pv, gain, w, b, dout)


def loss_head(y, t, name):
    s_, d_ = y.shape
    tr = _pick(s_, (1024, 512, 256, 128))

    def body(y_ref, t_ref, dy_ref, l_ref):
        err = y_ref[...] - t_ref[...]
        dy_ref[...] = err * (1.0 / d_)
        part = jnp.full(l_ref.shape, jnp.sum(err * err) * (0.5 / d_), F32)

        @pl.when(pl.program_id(0) == 0)
        def _():
            l_ref[...] = part

        @pl.when(pl.program_id(0) != 0)
        def _():
            l_ref[...] += part

    blk = pl.BlockSpec((tr, d_), lambda i: (i, 0))
    dy, l = pl.pallas_call(
        body, name=name, grid=(s_ // tr,),
        in_specs=[blk, blk], out_specs=(blk, pl.BlockSpec((8, LANES), lambda i: (0, 0))),
        out_shape=(jax.ShapeDtypeStruct((s_, d_), F32), jax.ShapeDtypeStruct((8, LANES), F32)),
        compiler_params=_params("arbitrary"),
    )(y, t)
    return dy, l[0, 0]


def adamw(w, g_slots, m, v, name):
    r_, c_ = w.shape
    tr = _pick(r_, (1024, 512, 352, 256, 224, 128, 64, 32, 16, 8))

    def body(w_ref, gs_ref, m_ref, v_ref, g_ref, d_ref, nm_ref, nv_ref):
        g = gs_ref[0].astype(F32)
        for k in range(1, N_DEV):
            g = g + gs_ref[k].astype(F32)
        m_new = ADAM_B1 * m_ref[...] + (1.0 - ADAM_B1) * g
        v_new = ADAM_B2 * v_ref[...] + (1.0 - ADAM_B2) * (g * g)
        m_hat = m_new / (1.0 - ADAM_B1 ** ADAM_STEP)
        v_hat = v_new / (1.0 - ADAM_B2 ** ADAM_STEP)
        g_ref[...] = g
        d_ref[...] = -ADAM_LR * (m_hat / (jnp.sqrt(v_hat) + ADAM_EPS) + ADAM_WD * w_ref[...])
        nm_ref[...] = m_new
        nv_ref[...] = v_new

    blk = pl.BlockSpec((tr, c_), lambda i: (i, 0))
    out = jax.ShapeDtypeStruct((r_, c_), F32)
    return pl.pallas_call(
        body, name=name, grid=(r_ // tr,),
        in_specs=[blk, pl.BlockSpec((N_DEV, tr, c_), lambda i: (0, i, 0)), blk, blk],
        out_specs=(blk, blk, blk, blk), out_shape=(out, out, out, out),
        compiler_params=_params("parallel"),
    )(w, g_slots, m, v)


def _position():
    x, y, c = lax.axis_index("x"), lax.axis_index("y"), lax.axis_index("c")
    return x, y, c


def _slot(px, py, pc):
    return 4 * px + 2 * py + pc


def all_gather_multi(blocks, name):
    n = len(blocks)

    def body(*refs):
        x_refs, out_refs = refs[:n], refs[n:2 * n]
        send_sems, recv_sems, local_sems = refs[2 * n:]
        x, y, c = _position()
        me, sibling = (x, y, c), (x, y, 1 - c)
        chips = [(1 - x, y), (x, 1 - y), (1 - x, 1 - y)]

        def copy(b, k, owner, to, src=None):
            dst = out_refs[b].at[_slot(*owner)]
            return pltpu.make_async_remote_copy(
                src_ref=dst if src is None else src, dst_ref=dst,
                send_sem=send_sems.at[b, k], recv_sem=recv_sems.at[b, k], device_id=to, device_id_type=MESH)

        mine = [pltpu.make_async_copy(x_refs[b], out_refs[b].at[_slot(*me)], local_sems.at[b]) for b in range(n)]
        for cp in mine:
            cp.start()
        first = [copy(b, 1 + j, me, (*chip, c), src=x_refs[b]) for j, chip in enumerate(chips) for b in range(n)]
        first += [copy(b, 0, me, sibling, src=x_refs[b]) for b in range(n)]
        for cp in first:
            cp.start()
        passed = []
        for j, chip in enumerate(chips):
            for b in range(n):
                copy(b, 1 + j, (*chip, c), me).wait_recv()
                cp = copy(b, 4 + j, (*chip, c), sibling)
                cp.start()
                passed.append(cp)
        for b in range(n):
            copy(b, 0, sibling, me).wait_recv()
        for j, chip in enumerate(chips):
            for b in range(n):
                copy(b, 4 + j, (*chip, 1 - c), me).wait_recv()
        for cp in first + passed:
            cp.wait_send()
        for cp in mine:
            cp.wait()

    any_spec = pl.BlockSpec(memory_space=pl.ANY)
    return pl.pallas_call(
        body, name=name,
        in_specs=[any_spec] * n, out_specs=tuple([any_spec] * n),
        out_shape=tuple(jax.ShapeDtypeStruct((N_DEV,) + blk.shape, blk.dtype) for blk in blocks),
        scratch_shapes=[pltpu.SemaphoreType.DMA((n, 7)), pltpu.SemaphoreType.DMA((n, 7)), pltpu.SemaphoreType.DMA((n,))],
    )(*blocks)


def _direct_copies(scatter, in_refs, out_refs, send_sems, recv_sems, local_sems):
    x, y, c = _position()
    me = _slot(x, y, c)
    src = (lambda ref, slot: ref.at[slot]) if scatter else (lambda ref, slot: ref)
    copies = [pltpu.make_async_copy(src(ref, me), out.at[me], local_sems.at[b])
              for b, (ref, out) in enumerate(zip(in_refs, out_refs))]
    for k in range(1, N_DEV):
        px = 1 - x if k & 4 else x
        py = 1 - y if k & 2 else y
        pc = 1 - c if k & 1 else c
        copies += [pltpu.make_async_remote_copy(
            src_ref=src(ref, _slot(px, py, pc)), dst_ref=out.at[me],
            send_sem=send_sems.at[b, k - 1], recv_sem=recv_sems.at[b, k - 1],
            device_id=(px, py, pc), device_id_type=MESH) for b, (ref, out) in enumerate(zip(in_refs, out_refs))]
    return copies


def _exchange_operands(scatter, bufs):
    n = len(bufs)
    any_spec = pl.BlockSpec(memory_space=pl.ANY)
    shapes = [jax.ShapeDtypeStruct(b.shape if scatter else (N_DEV,) + b.shape, b.dtype) for b in bufs]
    sems = [pltpu.SemaphoreType.DMA((n, 7)), pltpu.SemaphoreType.DMA((n, 7)), pltpu.SemaphoreType.DMA((n,))]
    return [any_spec] * n, [any_spec] * n, shapes, sems


def all_to_all_multi(slot_bufs, name):
    n = len(slot_bufs)

    def body(*refs):
        copies = _direct_copies(True, refs[:n], refs[n:2 * n], *refs[2 * n:])
        for cp in copies:
            cp.start()
        for cp in copies:
            cp.wait()

    in_specs, out_specs, out_shape, sems = _exchange_operands(True, slot_bufs)
    return pl.pallas_call(
        body, name=name, in_specs=in_specs, out_specs=tuple(out_specs), out_shape=tuple(out_shape),
        scratch_shapes=sems,
    )(*slot_bufs)


def _seg_len(n):
    return -(-n // SEG_ALIGN) * SEG_ALIGN


def _pack(arrays, lead=()):
    parts, total = [], 0
    for a in arrays:
        flat = a.reshape(lead + (-1,))
        pad = _seg_len(flat.shape[-1]) - flat.shape[-1]
        parts.append(jnp.pad(flat, [(0, 0)] * len(lead) + [(0, pad)]) if pad else flat)
        total += flat.shape[-1] + pad
    tail = -(-total // PACK_ALIGN) * PACK_ALIGN - total
    if tail:
        parts.append(jnp.zeros(lead + (tail,), parts[0].dtype))
    return jnp.concatenate(parts, axis=-1).reshape(lead + (-1, LANES))


def _unpack(packed, shapes, lead=()):
    flat = packed.reshape(lead + (-1,))
    out, off = [], 0
    for shp in shapes:
        n = math.prod(shp)
        out.append(flat[..., off:off + n].reshape(lead + tuple(shp)))
        off += _seg_len(n)
    return out


def _heads(a, n_heads):
    return a.reshape(a.shape[0], n_heads, HEAD_DIM).transpose(1, 0, 2)


def _unheads(a):
    return a.transpose(1, 0, 2).reshape(a.shape[1], a.shape[0] * HEAD_DIM)


def _head_gain(g, n_heads):
    return jnp.broadcast_to(g.reshape(1, 1, HEAD_DIM), (n_heads, 1, HEAD_DIM))


def _ffn_forward(x, gain, w_in_t, w_out, tag, carry=None):
    h = rmsnorm_fwd(x[None], gain.reshape(1, 1, -1), BF16, f"{tag}_norm")[0]
    a, b, u, *carried = ffn_in(h, w_in_t, f"{tag}_in", carry)
    if carry:
        w_out = w_out(carried)
    y = matmul(u, w_out, f"{tag}_out", residual=x, scale=0.5)
    return y, (x, h, a, b, u), carried


def _ffn_backward(dy, saved, gain, w_in_t, w_out, tag, carry_of=None):
    x, h, a, b, u = saved
    f_ = w_out.shape[0]
    dw_out = matmul_tn(u, dy, f"{tag}_dwout", scale=0.5, out_dtype=BF16)
    da, db, *carried = ffn_bwd_act(dy, w_out, a, b, f"{tag}_dact", carry_of(dw_out) if carry_of else None)
    dh = matmul(da, w_in_t[:f_], f"{tag}_dh_a")
    dh = matmul(db, w_in_t[f_:], f"{tag}_dh_b", residual=dh)
    dw_in_t = jnp.concatenate([matmul_tn(da, h, f"{tag}_dwin_a", out_dtype=BF16),
                               matmul_tn(db, h, f"{tag}_dwin_b", out_dtype=BF16)], axis=0)
    dx, dgain = rmsnorm_bwd(x[None], gain.reshape(1, 1, -1), dh[None], f"{tag}_dnorm", residual=dy[None])
    return dx[0], dgain.reshape(-1), dw_in_t, dw_out, carried


def _mem_forward(mq, mem_n, w_kv, g_q, g_k, tag):
    gq = _head_gain(g_q, MEM_HEADS)
    gk = _head_gain(g_k, MEM_HEADS)
    qn = rmsnorm_fwd(mq, gq * QK_SCALE, BF16, f"{tag}_qnorm")
    kv = matmul(mem_n, w_kv, f"{tag}_kv")
    k = _heads(kv[:, :MEM_WIDTH], MEM_HEADS)
    v = _heads(kv[:, MEM_WIDTH:], MEM_HEADS)
    kn = rmsnorm_fwd(k, gk, BF16, f"{tag}_knorm")
    o, lse = attn_fwd(qn, kn, _widen_v(v), None, False, f"{tag}_attn")
    return o, (mq, gq, gk, qn, k, kn, v, o, lse)


def _mem_backward(do, saved, mem_n, w_kv, tag):
    mq, gq, gk, qn, k, kn, v, o, lse = saved
    dqn, dkn, dv = attn_bwd(qn, kn, v, None, o, lse, do, False, f"{tag}_dattn")
    dmq, dgq = rmsnorm_bwd(mq, gq, dqn * QK_SCALE, f"{tag}_dqnorm")
    dk, dgk = rmsnorm_bwd(k, gk, dkn, f"{tag}_dknorm")
    dkv = jnp.concatenate([_unheads(dk), _unheads(dv)], axis=1)
    dw_kv = matmul_tn(mem_n, dkv, f"{tag}_dwkv", out_dtype=BF16)
    dmem_n = matmul(dkv, w_kv, f"{tag}_dmem", transpose_b=True)
    return dmq, dgq.sum(axis=0).reshape(-1), dgk.sum(axis=0).reshape(-1), dw_kv, dmem_n


def _fox_split(w_in):
    t3 = 3 * TOK_WIDTH
    pad = jnp.zeros(w_in.shape[:-1] + (LANES - FOX_HEADS,), w_in.dtype)
    return jnp.concatenate([w_in[..., :t3], w_in[..., t3 + FOX_HEADS:], w_in[..., t3:t3 + FOX_HEADS], pad], axis=-1)


def _fox_unsplit(w):
    t3 = 3 * TOK_WIDTH
    return jnp.concatenate([w[..., :t3], w[..., t3 + MEM_WIDTH:t3 + MEM_WIDTH + FOX_HEADS], w[..., t3:t3 + MEM_WIDTH]],
                           axis=-1)


def _fox_forward(h, w_split, b_f, g_q, g_k, tag, carry=None):
    t3 = 3 * TOK_WIDTH
    proj = matmul(h, w_split, f"{tag}_proj")
    qkv = _heads(proj[:, :t3], 3 * FOX_HEADS)
    mq = _heads(proj[:, t3:t3 + MEM_WIDTH], MEM_HEADS)
    f_pad = proj[:, t3 + MEM_WIDTH:]
    b_pad = jnp.pad(b_f.reshape(1, -1), ((0, 0), (0, LANES - FOX_HEADS)))
    gains = jnp.concatenate([_head_gain(g_q, FOX_HEADS), _head_gain(g_k, FOX_HEADS)], axis=0)
    qk = qkv[:2 * FOX_HEADS]
    scaled = jnp.concatenate([gains[:FOX_HEADS] * QK_SCALE, gains[FOX_HEADS:]], axis=0)
    qkn = rmsnorm_fwd(qk, scaled, BF16, f"{tag}_qknorm")
    v = qkv[2 * FOX_HEADS:]
    c = fox_gate_fwd(f_pad, b_pad, f"{tag}_gate")
    ck = c[:, :FOX_HEADS].T[:, None, :]
    qn, kn = qkn[:FOX_HEADS], qkn[FOX_HEADS:]
    o, lse, *carried = attn_fwd(qn, kn, _widen_v(v), ck, True, f"{tag}_attn", carry)
    return o, mq, (qk, gains, qn, kn, v, ck, o, lse, f_pad, b_pad), carried


def _fox_backward(do, dmq, saved, tag, carry=None):
    qk, gains, qn, kn, v, ck, o, lse, f_pad, b_pad = saved
    dqn, dkn, dv, dck, dcq, *carried = attn_bwd(qn, kn, v, ck, o, lse, do, True, f"{tag}_dattn", carry)
    dqk, dgains = rmsnorm_bwd(qk, gains, jnp.concatenate([dqn * QK_SCALE, dkn], axis=0), f"{tag}_dqknorm")
    dc = jnp.pad((dck[:, 0, :] + dcq[:, :, 0]).T, ((0, 0), (0, LANES - FOX_HEADS)))
    df, db = fox_gate_bwd(dc, f_pad, b_pad, f"{tag}_dgate")
    dproj = jnp.concatenate([_unheads(dqk), _unheads(dv), _unheads(dmq), df], axis=1)
    dgains = dgains.reshape(2, FOX_HEADS, HEAD_DIM).sum(axis=1)
    return dproj, db[0, :FOX_HEADS], dgains[0], dgains[1], carried


def _gmlp_forward(h, w_in_t, v_gain, w_s, b_s, tag):
    proj = matmul(h, w_in_t, f"{tag}_proj", transpose_b=True)
    pu = _heads(proj[:, :TOK_WIDTH], FOX_HEADS)
    pv = _heads(proj[:, TOK_WIDTH:2 * TOK_WIDTH], FOX_HEADS)
    mq = _heads(proj[:, 2 * TOK_WIDTH:], MEM_HEADS)
    gain = v_gain.reshape(FOX_HEADS, 1, HEAD_DIM)
    b = b_s[:, :, None]
    o = gmlp_fwd(pu, pv, gain, w_s, b, f"{tag}_sgu")
    return o, mq, (pu, pv, gain, w_s, b)


def _gmlp_backward(do, dmq, saved, tag):
    pu, pv, gain, w_s, b = saved
    dpu, dpv, dw, db, dg = gmlp_bwd(pu, pv, gain, w_s, b, do, f"{tag}_dsgu")
    dproj = jnp.concatenate([_unheads(dpu), _unheads(dpv), _unheads(dmq)], axis=1)
    return dproj, dg.reshape(-1), dw, db[:, :, 0]


BIG = ("ffn1_w_in", "ffn1_w_out", "ffn2_w_in", "ffn2_w_out", "w_out", "mem_w_kv", "fox_w_in", "gmlp_w_in")
COLUMN_SHARDED = ("ffn1_w_in", "ffn2_w_in", "gmlp_w_in")
REPLICATED =("norm_ffn1", "norm_mix", "norm_ffn2", "mem_norm", "mem_q_norm", "mem_k_norm", "fox_b_f",
              "fox_q_norm", "fox_k_norm", "gmlp_w_s", "gmlp_b_s")
WEIGHTS = ("norm_ffn1", "ffn1_w_in", "ffn1_w_out", "norm_mix", "norm_ffn2", "ffn2_w_in", "ffn2_w_out", "w_out",
           "mem_norm", "mem_w_kv", "mem_q_norm", "mem_k_norm", "fox_w_in", "fox_b_f", "fox_q_norm", "fox_k_norm",
           "gmlp_w_in", "gmlp_v_norm", "gmlp_w_s", "gmlp_b_s")


def _to_transport(name, a):
    if name in COLUMN_SHARDED:
        return jnp.swapaxes(a, -1, -2)
    return _fox_split(a) if name == "fox_w_in" else a


def _from_transport(name, a):
    if name in COLUMN_SHARDED:
        return jnp.swapaxes(a, -1, -2)
    return _fox_unsplit(a) if name == "fox_w_in" else a


def kernel(x, mem, norm_ffn1, ffn1_w_in, ffn1_w_out, norm_mix, norm_ffn2, ffn2_w_in, ffn2_w_out, w_out, mem_norm, mem_w_kv, mem_q_norm, mem_k_norm, fox_w_in, fox_b_f, fox_q_norm, fox_k_norm, gmlp_w_in, gmlp_v_norm, gmlp_w_s, gmlp_b_s, loss_target, m_norm_ffn1, m_ffn1_w_in, m_ffn1_w_out, m_norm_mix, m_norm_ffn2, m_ffn2_w_in, m_ffn2_w_out, m_w_out, m_mem_norm, m_mem_w_kv, m_mem_q_norm, m_mem_k_norm, m_fox_w_in, m_fox_b_f, m_fox_q_norm, m_fox_k_norm, m_gmlp_w_in, m_gmlp_v_norm, m_gmlp_w_s, m_gmlp_b_s, v_norm_ffn1, v_ffn1_w_in, v_ffn1_w_out, v_norm_mix, v_norm_ffn2, v_ffn2_w_in, v_ffn2_w_out, v_w_out, v_mem_norm, v_mem_w_kv, v_mem_q_norm, v_mem_k_norm, v_fox_w_in, v_fox_b_f, v_fox_q_norm, v_fox_k_norm, v_gmlp_w_in, v_gmlp_v_norm, v_gmlp_w_s, v_gmlp_b_s):
    w = dict(norm_ffn1=norm_ffn1, ffn1_w_in=ffn1_w_in, ffn1_w_out=ffn1_w_out, norm_mix=norm_mix, norm_ffn2=norm_ffn2, ffn2_w_in=ffn2_w_in, ffn2_w_out=ffn2_w_out, w_out=w_out, mem_norm=mem_norm, mem_w_kv=mem_w_kv, mem_q_norm=mem_q_norm, mem_k_norm=mem_k_norm, fox_w_in=fox_w_in, fox_b_f=fox_b_f, fox_q_norm=fox_q_norm, fox_k_norm=fox_k_norm, gmlp_w_in=gmlp_w_in, gmlp_v_norm=gmlp_v_norm, gmlp_w_s=gmlp_w_s, gmlp_b_s=gmlp_b_s)
    m = dict(norm_ffn1=m_norm_ffn1, ffn1_w_in=m_ffn1_w_in, ffn1_w_out=m_ffn1_w_out, norm_mix=m_norm_mix, norm_ffn2=m_norm_ffn2, ffn2_w_in=m_ffn2_w_in, ffn2_w_out=m_ffn2_w_out, w_out=m_w_out, mem_norm=m_mem_norm, mem_w_kv=m_mem_w_kv, mem_q_norm=m_mem_q_norm, mem_k_norm=m_mem_k_norm, fox_w_in=m_fox_w_in, fox_b_f=m_fox_b_f, fox_q_norm=m_fox_q_norm, fox_k_norm=m_fox_k_norm, gmlp_w_in=m_gmlp_w_in, gmlp_v_norm=m_gmlp_v_norm, gmlp_w_s=m_gmlp_w_s, gmlp_b_s=m_gmlp_b_s)
    v = dict(norm_ffn1=v_norm_ffn1, ffn1_w_in=v_ffn1_w_in, ffn1_w_out=v_ffn1_w_out, norm_mix=v_norm_mix, norm_ffn2=v_norm_ffn2, ffn2_w_in=v_ffn2_w_in, ffn2_w_out=v_ffn2_w_out, w_out=v_w_out, mem_norm=v_mem_norm, mem_w_kv=v_mem_w_kv, mem_q_norm=v_mem_q_norm, mem_k_norm=v_mem_k_norm, fox_w_in=v_fox_w_in, fox_b_f=v_fox_b_f, fox_q_norm=v_fox_q_norm, fox_k_norm=v_fox_k_norm, gmlp_w_in=v_gmlp_w_in, gmlp_v_norm=v_gmlp_v_norm, gmlp_w_s=v_gmlp_w_s, gmlp_b_s=v_gmlp_b_s)

    depth = norm_ffn1.shape[0]
    x0 = x[0]
    mem0 = mem[0]
    target = loss_target[0]
    me = _slot(*_position())

    keys = [(n, i) for n in BIG for i in range(w[n].shape[0])]
    local = {k: _to_transport(k[0], w[k[0]][k[1]]) for k in keys}
    n_gain, gain_len = gmlp_v_norm.shape
    pad_gain = lambda a: jnp.pad(a, ((0, 8 - n_gain), (0, LANES - gain_len)))
    first = [("ffn1_w_in", 0)]
    early = [("ffn1_w_out", 0), ("fox_w_in", 0)]
    inner = [k for k in keys if k not in first + early]
    stack = lambda g: g.reshape(-1, g.shape[-1])
    full = {k: stack(g) for k, g in zip(first, all_gather_multi([local[k].astype(BF16) for k in first], "gather_weights"))}
    gather_early = (False, [local[k].astype(BF16) for k in early])
    gather_inner = (False, [local[k].astype(BF16) for k in inner] + [pad_gain(gmlp_v_norm)])

    mem_n = rmsnorm_fwd(mem0[None], mem_norm.reshape(1, 1, -1), BF16, "mem_norm")[0]
    saved = []
    xi = x0
    for i in range(depth):
        kind, j = i % 2, i // 2
        if i == 0:
            x1, ffn1_saved, arrived = _ffn_forward(xi, norm_ffn1[i], full["ffn1_w_in", i], lambda got: stack(got[0]),
                                                   f"l{i}_ffn1", gather_early)
            full.update({k: stack(g) for k, g in zip(early, arrived)})
        else:
            x1, ffn1_saved, _ = _ffn_forward(xi, norm_ffn1[i], full["ffn1_w_in", i], full["ffn1_w_out", i], f"l{i}_ffn1")
        h = rmsnorm_fwd(x1[None], norm_mix[i].reshape(1, 1, -1), BF16, f"l{i}_mixnorm")[0]
        if kind == 0:
            tok, mq, mix_saved, arrived = _fox_forward(h, full["fox_w_in", j], fox_b_f[j], fox_q_norm[j], fox_k_norm[j],
                                                       f"l{i}_fox", gather_inner if i == 0 else None)
            if i == 0:
                full.update({k: stack(g) for k, g in zip(inner, arrived)})
                v_gain_full = arrived[-1][:, :n_gain, :gain_len]
        else:
            tok, mq, mix_saved = _gmlp_forward(h, full["gmlp_w_in", j], v_gain_full[:, j, :].reshape(-1), gmlp_w_s[j],
                                               gmlp_b_s[j], f"l{i}_gmlp")
        mo, mem_saved = _mem_forward(mq, mem_n, full["mem_w_kv", i], mem_q_norm[i], mem_k_norm[i], f"l{i}_mem")
        cat = _unheads(jnp.concatenate([tok, mo], axis=0)).astype(BF16)
        x2 = matmul(cat, full["w_out", i], f"l{i}_wout", residual=x1)
        x3, ffn2_saved, _ = _ffn_forward(x2, norm_ffn2[i], full["ffn2_w_in", i], full["ffn2_w_out", i], f"l{i}_ffn2")
        saved.append((ffn1_saved, x1, h, mix_saved, mem_saved, cat, ffn2_saved))
        xi = x3

    dy, loss_part = loss_head(xi, target, "loss_head")
    loss = lax.psum(loss_part, ("x", "y", "c"))

    small = {n: [None] * w[n].shape[0] for n in REPLICATED + ("gmlp_v_norm",) if n != "mem_norm"}
    big = {}
    slots_of = lambda g: g.reshape((N_DEV, -1, g.shape[-1]))
    dmem_n = None
    for i in reversed(range(depth)):
        kind, j = i % 2, i // 2
        ffn1_saved, x1, h, mix_saved, mem_saved, cat, ffn2_saved = saved[i]
        dy, small["norm_ffn2"][i], big["ffn2_w_in", i], big["ffn2_w_out", i], _ = _ffn_backward(
            dy, ffn2_saved, norm_ffn2[i], full["ffn2_w_in", i], full["ffn2_w_out", i], f"l{i}_ffn2")
        big["w_out", i] = matmul_tn(cat, dy, f"l{i}_dwout", out_dtype=BF16)
        dcat = _heads(matmul(dy, full["w_out", i], f"l{i}_dcat", transpose_b=True), FOX_HEADS + MEM_HEADS)
        dmq, small["mem_q_norm"][i], small["mem_k_norm"][i], big["mem_w_kv", i], dmem_i = _mem_backward(
            dcat[FOX_HEADS:], mem_saved, mem_n, full["mem_w_kv", i], f"l{i}_mem")
        dmem_n = dmem_i if dmem_n is None else dmem_n + dmem_i
        if kind == 0:
            scatter_inner = (True, [slots_of(big[k]) for k in inner]) if i == 0 else None
            dproj, small["fox_b_f"][j], small["fox_q_norm"][j], small["fox_k_norm"][j], arrived = _fox_backward(
                dcat[:FOX_HEADS], dmq, mix_saved, f"l{i}_fox", scatter_inner)
            if i == 0:
                got = dict(zip(inner, arrived))
            big["fox_w_in", j] = matmul_tn(h, dproj, f"l{i}_fox_dwin", out_dtype=BF16)
            dh = matmul(dproj, full["fox_w_in", j], f"l{i}_fox_dh", transpose_b=True)
        else:
            dproj, small["gmlp_v_norm"][j], small["gmlp_w_s"][j], small["gmlp_b_s"][j] = _gmlp_backward(
                dcat[:FOX_HEADS], dmq, mix_saved, f"l{i}_gmlp")
            big["gmlp_w_in", j] = matmul_tn(dproj, h, f"l{i}_gmlp_dwin", out_dtype=BF16)
            dh = matmul(dproj, full["gmlp_w_in", j], f"l{i}_gmlp_dh")
        dy, dg_mix = rmsnorm_bwd(x1[None], norm_mix[i].reshape(1, 1, -1), dh[None], f"l{i}_dmixnorm", residual=dy[None])
        dy, small["norm_mix"][i] = dy[0], dg_mix.reshape(-1)
        scatter_early = (lambda dw_out: (True, [slots_of(dw_out), slots_of(big["fox_w_in", 0])])) if i == 0 else None
        dy, small["norm_ffn1"][i], big["ffn1_w_in", i], big["ffn1_w_out", i], arrived = _ffn_backward(
            dy, ffn1_saved, norm_ffn1[i], full["ffn1_w_in", i], full["ffn1_w_out", i], f"l{i}_ffn1", scatter_early)
        if i == 0:
            got.update(zip(early, arrived))
    grad_x = dy[None]
    _, dg_mem = rmsnorm_bwd(mem0[None], mem_norm.reshape(1, 1, -1), dmem_n[None], "dmem_norm")
    small = {n: jnp.stack(g) for n, g in small.items()}
    small["mem_norm"] = dg_mem.reshape(-1)

    got.update(zip(first, all_to_all_multi([slots_of(big[k]) for k in first], "exchange_grads")))
    results = {n: [[None] * w[n].shape[0] for _ in range(4)] for n in BIG}
    for k in keys:
        n, i = k
        outs = adamw(local[k], got[k], _to_transport(n, m[n][i]), _to_transport(n, v[n][i]), f"adamw_{n}_{i}")
        for q in range(4):
            results[n][q][i] = _from_transport(n, outs[q])
    sharded = {n: [jnp.stack(r) for r in results[n]] for n in BIG}

    small_names = REPLICATED + ("gmlp_v_norm",)
    (small_got,) = all_gather_multi([_pack([small[n] for n in small_names])], "gather_small_grads")
    rep_shapes = [w[n].shape for n in REPLICATED]
    gain_seg = jnp.zeros((n_gain, N_DEV * gain_len), F32)
    pack_rep = lambda d: _pack([d[n] for n in REPLICATED] + [gain_seg])
    outs = adamw(pack_rep(w), small_got, pack_rep(m), pack_rep(v), "adamw_replicated")
    replicated = [dict(zip(REPLICATED, _unpack(o, rep_shapes))) for o in outs]
    gain_parts = _unpack(small_got, rep_shapes + [(n_gain, N_DEV * gain_len)], lead=(N_DEV,))[-1]
    gain_slots = lax.dynamic_slice_in_dim(gain_parts, me * gain_len, gain_len, axis=2)
    gain_slots = jnp.pad(gain_slots, ((0, 0), (0, 8 - n_gain), (0, LANES - gain_len)))
    outs = adamw(pad_gain(gmlp_v_norm), gain_slots, pad_gain(m["gmlp_v_norm"]), pad_gain(v["gmlp_v_norm"]),
                 "adamw_gmlp_v_norm")
    sharded["gmlp_v_norm"] = [o[:n_gain, :gain_len] for o in outs]

    out = [loss, grad_x]
    for q in range(4):
        out += [(replicated[q][n] if n in REPLICATED else sharded[n][q]) for n in WEIGHTS]
    return tuple(out)
```

```python
import functools
import math

import jax
import jax.numpy as jnp
from jax import lax
from jax.experimental import pallas as pl
from jax.experimental.pallas import tpu as pltpu

F32 = jnp.float32
BF16 = jnp.bfloat16

EPS = 1e-6
HEAD_DIM = 64
FOX_HEADS = 12
MEM_HEADS = 4
TOK_WIDTH = FOX_HEADS * HEAD_DIM
MEM_WIDTH = MEM_HEADS * HEAD_DIM
CHUNK = 128
LANES = 128
N_DEV = 8
SEG_ALIGN = 16 * LANES
PACK_ROWS = 1024
PACK_ALIGN = PACK_ROWS * LANES

ADAM_LR = 0.001
ADAM_B1 = 0.9
ADAM_B2 = 0.999
ADAM_EPS = 1e-08
ADAM_WD = 0.01
ADAM_STEP = 10

VMEM_LIMIT_BYTES = 48 * 1024 * 1024
MESH = pl.DeviceIdType.MESH
CONTRACT_0 = (((0,), (0,)), ((), ()))
CONTRACT_1 = (((1,), (1,)), ((), ()))


def _params(*semantics):
    return pltpu.CompilerParams(dimension_semantics=semantics, vmem_limit_bytes=VMEM_LIMIT_BYTES)


def _pick(n, candidates):
    for c in candidates:
        if c <= n and n % c == 0:
            return c
    return n


def _sigmoid(x):
    return 0.5 * jnp.tanh(0.5 * x) + 0.5


def _row_tile(r, w):
    return _pick(r, (1024,) if w >= 512 else (2048, 1024, 512, 256))


def rmsnorm_fwd(x, gain, out_dtype, name):
    g_, r_, w_ = x.shape
    tr = _row_tile(r_, w_)

    def body(x_ref, g_ref, y_ref):
        xv = x_ref[0].astype(F32)
        r = lax.rsqrt(jnp.mean(xv * xv, axis=-1, keepdims=True) + EPS)
        y_ref[0] = (xv * r * g_ref[0]).astype(y_ref.dtype)

    return pl.pallas_call(
        body, name=name, grid=(g_, r_ // tr),
        in_specs=[pl.BlockSpec((1, tr, w_), lambda g, i: (g, i, 0)),
                  pl.BlockSpec((1, 1, w_), lambda g, i: (g, 0, 0))],
        out_specs=pl.BlockSpec((1, tr, w_), lambda g, i: (g, i, 0)),
        out_shape=jax.ShapeDtypeStruct((g_, r_, w_), out_dtype),
        compiler_params=_params("parallel", "parallel"),
    )(x, gain)


def rmsnorm_bwd(x, gain, dy, name, residual=None):
    g_, r_, w_ = x.shape
    tr = _row_tile(r_, w_)
    has_res = residual is not None

    def body(*refs):
        if has_res:
            x_ref, g_ref, dy_ref, res_ref, dx_ref, dg_ref = refs
        else:
            x_ref, g_ref, dy_ref, dx_ref, dg_ref = refs
        xv = x_ref[0].astype(F32)
        dyv = dy_ref[0].astype(F32)
        r = lax.rsqrt(jnp.mean(xv * xv, axis=-1, keepdims=True) + EPS)
        n = xv * r
        dn = dyv * g_ref[0]
        dx = r * (dn - n * jnp.mean(dn * n, axis=-1, keepdims=True))
        if has_res:
            dx = dx + res_ref[0]
        dx_ref[0] = dx
        part = jnp.sum(dyv * n, axis=0, keepdims=True)

        @pl.when(pl.program_id(1) == 0)
        def _():
            dg_ref[0] = part

        @pl.when(pl.program_id(1) != 0)
        def _():
            dg_ref[0] += part

    row = pl.BlockSpec((1, tr, w_), lambda g, i: (g, i, 0))
    vec = pl.BlockSpec((1, 1, w_), lambda g, i: (g, 0, 0))
    operands = (x, gain, dy) + ((residual,) if has_res else ())
    return pl.pallas_call(
        body, name=name, grid=(g_, r_ // tr),
        in_specs=[row, vec, row] + ([row] if has_res else []),
        out_specs=(row, vec),
        out_shape=(jax.ShapeDtypeStruct((g_, r_, w_), F32), jax.ShapeDtypeStruct((g_, 1, w_), F32)),
        compiler_params=_params("parallel", "arbitrary"),
    )(*operands)


def matmul(a, b, name, out_dtype=F32, residual=None, scale=None, transpose_b=False):
    m_, k_ = a.shape
    n_ = b.shape[0] if transpose_b else b.shape[1]
    tm = _pick(m_, (512, 256, 128))
    tn = _pick(n_, (1408, 1024, 896, 512, 256, 128))
    has_res = residual is not None

    def body(*refs):
        if has_res:
            a_ref, b_ref, res_ref, o_ref = refs
        else:
            a_ref, b_ref, o_ref = refs
        av, bv = a_ref[...].astype(BF16), b_ref[...].astype(BF16)
        if transpose_b:
            acc = lax.dot_general(av, bv, CONTRACT_1, preferred_element_type=F32)
        else:
            acc = jnp.dot(av, bv, preferred_element_type=F32)
        if scale is not None:
            acc = acc * scale
        if has_res:
            acc = acc + res_ref[...]
        o_ref[...] = acc.astype(o_ref.dtype)

    out_spec = pl.BlockSpec((tm, tn), lambda j, i: (i, j))
    b_spec = pl.BlockSpec((tn, k_), lambda j, i: (j, 0)) if transpose_b else pl.BlockSpec((k_, tn), lambda j, i: (0, j))
    operands = (a, b) + ((residual,) if has_res else ())
    return pl.pallas_call(
        body, name=name, grid=(n_ // tn, m_ // tm),
        in_specs=[pl.BlockSpec((tm, k_), lambda j, i: (i, 0)), b_spec] + ([out_spec] if has_res else []),
        out_specs=out_spec,
        out_shape=jax.ShapeDtypeStruct((m_, n_), out_dtype),
        compiler_params=_params("parallel", "parallel"),
    )(*operands)


def matmul_tn(a, b, name, scale=None, out_dtype=F32):
    s_, k_ = a.shape
    n_ = b.shape[1]
    tk = _pick(k_, (1024, 1408, 896, 512, 256, 128))
    tn = _pick(n_, (1408, 1024, 896, 512, 256, 128))
    ts = _pick(s_, (2048, 1024, 512, 256, 128))
    ns = s_ // ts

    def body(a_ref, b_ref, o_ref, acc_ref):
        part = lax.dot_general(a_ref[...].astype(BF16), b_ref[...].astype(BF16), CONTRACT_0,
                               preferred_element_type=F32)
        step = pl.program_id(2)

        @pl.when(step == 0)
        def _():
            acc_ref[...] = part

        @pl.when(step != 0)
        def _():
            acc_ref[...] += part

        @pl.when(step == ns - 1)
        def _():
            acc = acc_ref[...]
            o_ref[...] = (acc if scale is None else acc * scale).astype(o_ref.dtype)

    return pl.pallas_call(
        body, name=name, grid=(k_ // tk, n_ // tn, ns),
        in_specs=[pl.BlockSpec((ts, tk), lambda i, j, s: (s, i)),
                  pl.BlockSpec((ts, tn), lambda i, j, s: (s, j))],
        out_specs=pl.BlockSpec((tk, tn), lambda i, j, s: (i, j)),
        out_shape=jax.ShapeDtypeStruct((k_, n_), out_dtype),
        scratch_shapes=[pltpu.VMEM((tk, tn), F32)],
        compiler_params=_params("parallel", "parallel", "arbitrary"),
    )(a, b)


def ffn_in(h, w_in_t, name, carry=None):
    s_, d_ = h.shape
    f_ = w_in_t.shape[0] // 2
    tm = _pick(s_, (512, 256, 128))
    tn = _pick(f_, (1408, 1024, 512, 256, 128))
    nb = f_ // tn

    def body(h_ref, wa_ref, wb_ref, a_ref, b_ref, u_ref):
        hv = h_ref[...]
        a = lax.dot_general(hv, wa_ref[...], CONTRACT_1, preferred_element_type=F32)
        b = lax.dot_general(hv, wb_ref[...], CONTRACT_1, preferred_element_type=F32)
        a_ref[...] = a.astype(BF16)
        b_ref[...] = b.astype(BF16)
        u_ref[...] = (a * _sigmoid(a) * b).astype(BF16)

    o_spec = pl.BlockSpec((tm, tn), lambda j, i: (i, j))
    out = jax.ShapeDtypeStruct((s_, f_), BF16)
    return _carried_call(
        body, name, (nb, s_ // tm),
        [pl.BlockSpec((tm, d_), lambda j, i: (i, 0)),
         pl.BlockSpec((tn, d_), lambda j, i: (j, 0)),
         pl.BlockSpec((tn, d_), lambda j, i: (j + nb, 0))],
        [o_spec, o_spec, o_spec], [out, out, out], [], [h, w_in_t, w_in_t], carry)


def _carried_call(body, name, grid, in_specs, out_specs, out_shape, scratch, operands, carry):
    if not carry:
        return pl.pallas_call(
            body, name=name, grid=grid, in_specs=in_specs, out_specs=tuple(out_specs), out_shape=tuple(out_shape),
            scratch_shapes=scratch, compiler_params=_params(*["parallel"] * len(grid)))(*operands)
    counts = (len(in_specs), len(carry[1]), len(out_specs), len(carry[1]), len(scratch), 3)

    def carrying(*refs):
        ins, c_in, outs, c_out, scr, sems = _split_refs(refs, counts)
        step_no, last_no = _grid_step(grid)

        @pl.when(step_no == 0)
        def _():
            for cp in _direct_copies(carry[0], c_in, c_out, *sems):
                cp.start()

        body(*ins, *outs, *scr)

        @pl.when(step_no == last_no)
        def _():
            for cp in _direct_copies(carry[0], c_in, c_out, *sems):
                cp.wait()

    c_in_specs, c_out_specs, c_shapes, c_sems = _exchange_operands(*carry)
    return pl.pallas_call(
        carrying, name=name, grid=grid, in_specs=in_specs + c_in_specs, out_specs=tuple(out_specs + c_out_specs),
        out_shape=tuple(out_shape + c_shapes), scratch_shapes=scratch + c_sems,
        compiler_params=_params(*["arbitrary"] * len(grid)))(*operands, *carry[1])


def ffn_bwd_act(dy, w_out, a, b, name, carry=None):
    s_, d_ = dy.shape
    f_ = w_out.shape[0]
    tm = _pick(s_, (512, 256, 128))
    tn = _pick(f_, (1408, 1024, 512, 256, 128))

    def body(dy_ref, w_ref, a_ref, b_ref, da_ref, db_ref):
        du = 0.5 * lax.dot_general(dy_ref[...].astype(BF16), w_ref[...], CONTRACT_1, preferred_element_type=F32)
        av = a_ref[...].astype(F32)
        bv = b_ref[...].astype(F32)
        sig = _sigmoid(av)
        da_ref[...] = (du * bv * (sig * (1.0 + av * (1.0 - sig)))).astype(BF16)
        db_ref[...] = (du * (av * sig)).astype(BF16)

    t_spec = pl.BlockSpec((tm, tn), lambda j, i: (i, j))
    out = jax.ShapeDtypeStruct((s_, f_), BF16)
    return _carried_call(
        body, name, (f_ // tn, s_ // tm),
        [pl.BlockSpec((tm, d_), lambda j, i: (i, 0)), pl.BlockSpec((tn, d_), lambda j, i: (j, 0)), t_spec, t_spec],
        [t_spec, t_spec], [out, out], [], [dy, w_out, a, b], carry)


def _fold(a, b, n, forward):
    if forward:
        low = b <= a
        return jnp.where(low, a, n - 1 - a), jnp.where(low, b, b - a - 1)
    low = b < n - a
    return jnp.where(low, a, n - 1 - a), jnp.where(low, a + b, b - 1)


def _attn_grid(h_, n_outer, n_inner, causal, forward):
    if causal and n_outer % 2 == 0:
        return (h_, n_outer // 2, n_outer + 1), lambda a, b: _fold(a, b, n_outer, forward)
    return (h_, n_outer, n_inner), lambda a, b: (a, b)


QK_SCALE = 1.0 / math.sqrt(HEAD_DIM)
SUM_LANE = HEAD_DIM


def _scores(q, k, ck, masked):
    s = lax.dot_general(q, k, CONTRACT_1, preferred_element_type=F32)
    if ck is not None:
        s = s - ck
    if masked:
        row = lax.broadcasted_iota(jnp.int32, s.shape, 0)
        col = lax.broadcasted_iota(jnp.int32, s.shape, 1)
        s = jnp.where(col <= row, s, -jnp.inf)
    return s


def _widen_v(v):
    ones = jnp.ones(v.shape[:2] + (1,), BF16)
    zeros = jnp.zeros(v.shape[:2] + (LANES - HEAD_DIM - 1,), BF16)
    return jnp.concatenate([v.astype(BF16), ones, zeros], axis=-1)


def _grid_step(grid):
    step = 0
    for axis, n in enumerate(grid):
        step = step * n + pl.program_id(axis)
    return step, math.prod(grid) - 1


def _split_refs(refs, counts):
    out, at = [], 0
    for n in counts:
        out.append(refs[at:at + n])
        at += n
    return out


def attn_fwd(q, k, v_wide, ck, causal, name, carry=None):
    h_, sq, d_ = q.shape
    sk = k.shape[1]
    tq = _pick(sq, (512, 256, 128))
    tk = tq if causal else _pick(sk, (512, 256, 128))
    nq, nk = sq // tq, sk // tk
    assert not causal or nq == 1 or nq % 2 == 0
    bias = ck is not None
    grid, blocks = _attn_grid(h_, nq, nk, causal, True)
    nc = len(carry[1]) if carry else 0

    def body(*refs):
        ins, c_in, outs, c_out, scratch, sems = _split_refs(refs, (4 if bias else 3, nc, 2, nc, 2, 3 if carry else 0))
        q_ref, k_ref, v_ref = ins[:3]
        ck_ref = ins[3] if bias else None
        (o_ref, lse_ref), (m_sc, acc_sc) = outs, scratch
        i, j = blocks(pl.program_id(1), pl.program_id(2))
        if carry:
            step_no, last_no = _grid_step(grid)

            @pl.when(step_no == 0)
            def _():
                for cp in _direct_copies(carry[0], c_in, c_out, *sems):
                    cp.start()

        @pl.when(j == 0)
        def _():
            m_sc[...] = jnp.full(m_sc.shape, -jnp.inf, F32)
            acc_sc[...] = jnp.zeros(acc_sc.shape, F32)

        def step(masked):
            s = _scores(q_ref[0], k_ref[0], ck_ref[0] if bias else None, masked)
            m_prev = m_sc[...]
            m_new = jnp.maximum(m_prev, jnp.max(s, axis=1, keepdims=True))
            p = jnp.exp(s - m_new)
            acc_sc[...] = jnp.exp(m_prev - m_new) * acc_sc[...] + jnp.dot(p.astype(BF16), v_ref[0],
                                                                         preferred_element_type=F32)
            m_sc[...] = m_new

        if causal:
            pl.when(j < i)(functools.partial(step, False))
            pl.when(j == i)(functools.partial(step, True))
        else:
            step(False)

        @pl.when(j == (i if causal else nk - 1))
        def _():
            acc = acc_sc[...]
            lane = lax.broadcasted_iota(jnp.int32, acc.shape, 1)
            l = jnp.sum(jnp.where(lane == SUM_LANE, acc, 0.0), axis=1, keepdims=True)
            o_ref[0] = acc_sc[:, :HEAD_DIM] / l
            lse_ref[0] = m_sc[...] + jnp.log(l)

        if carry:
            @pl.when(step_no == last_no)
            def _():
                for cp in _direct_copies(carry[0], c_in, c_out, *sems):
                    cp.wait()

    q_spec = pl.BlockSpec((1, tq, d_), lambda h, a, b: (h, blocks(a, b)[0], 0))
    q1_spec = pl.BlockSpec((1, tq, 1), lambda h, a, b: (h, blocks(a, b)[0], 0))
    k_spec = pl.BlockSpec((1, tk, d_), lambda h, a, b: (h, blocks(a, b)[1], 0))
    in_specs = [q_spec, k_spec, pl.BlockSpec((1, tk, LANES), lambda h, a, b: (h, blocks(a, b)[1], 0))]
    operands = [q, k, v_wide]
    if bias:
        in_specs.append(pl.BlockSpec((1, 1, tk), lambda h, a, b: (h, 0, blocks(a, b)[1])))
        operands.append(ck)
    out_specs = [q_spec, q1_spec]
    out_shape = [jax.ShapeDtypeStruct((h_, sq, d_), F32), jax.ShapeDtypeStruct((h_, sq, 1), F32)]
    scratch = [pltpu.VMEM((tq, 1), F32), pltpu.VMEM((tq, LANES), F32)]
    if carry:
        c_in_specs, c_out_specs, c_shapes, c_sems = _exchange_operands(*carry)
        in_specs, out_specs, out_shape, scratch = in_specs + c_in_specs, out_specs + c_out_specs, out_shape + c_shapes, scratch + c_sems
        operands = operands + list(carry[1])
    return pl.pallas_call(
        body, name=name, grid=grid,
        in_specs=in_specs, out_specs=tuple(out_specs), out_shape=tuple(out_shape), scratch_shapes=scratch,
        compiler_params=_params("arbitrary" if carry else "parallel", "arbitrary", "arbitrary"),
    )(*operands)


def attn_bwd(q, k, v, ck, o, lse, do, causal, name, carry=None):
    h_, sq, d_ = q.shape
    sk = k.shape[1]
    tq = _pick(sq, (512, 256, 128))
    tk = tq if causal else _pick(sk, (512, 256, 128))
    nq, nk = sq // tq, sk // tk
    assert not causal or nq == 1 or nq % 2 == 0
    bias = ck is not None
    grid, blocks = _attn_grid(h_, nk, nq, causal, False)
    nc = len(carry[1]) if carry else 0

    def body(*refs):
        ins, c_in, outs, c_out, scratch, sems = _split_refs(
            refs, (7 if bias else 6, nc, 5 if bias else 3, nc, 3 if bias else 2, 3 if carry else 0))
        if bias:
            q_ref, k_ref, v_ref, ck_ref, o_ref, lse_ref, do_ref = ins
            dq_ref, dk_ref, dv_ref, dc_ref, dcq_ref = outs
            dk_sc, dv_sc, dc_sc = scratch
        else:
            q_ref, k_ref, v_ref, o_ref, lse_ref, do_ref = ins
            dq_ref, dk_ref, dv_ref = outs
            dk_sc, dv_sc = scratch
        j, i = blocks(pl.program_id(1), pl.program_id(2))
        if carry:
            step_no, last_no = _grid_step(grid)

            @pl.when(step_no == 0)
            def _():
                for cp in _direct_copies(carry[0], c_in, c_out, *sems):
                    cp.start()

        @pl.when((pl.program_id(1) == 0) & (pl.program_id(2) == 0))
        def _():
            dq_ref[...] = jnp.zeros(dq_ref.shape, F32)
            if bias:
                dcq_ref[...] = jnp.zeros(dcq_ref.shape, F32)

        @pl.when(i == (j if causal else 0))
        def _():
            dk_sc[...] = jnp.zeros(dk_sc.shape, F32)
            dv_sc[...] = jnp.zeros(dv_sc.shape, F32)
            if bias:
                dc_sc[...] = jnp.zeros(dc_sc.shape, F32)

        def step(masked):
            qb, kb = q_ref[0], k_ref[0]
            dof = do_ref[0]
            dob = dof.astype(BF16)
            s = _scores(qb, kb, ck_ref[0] if bias else None, masked)
            p = jnp.exp(s - lse_ref[0])
            dp = lax.dot_general(dob, v_ref[0].astype(BF16), CONTRACT_1, preferred_element_type=F32)
            delta = jnp.sum(dof * o_ref[0], axis=1, keepdims=True)
            ds = p * (dp - delta)
            dsb = ds.astype(BF16)
            dv_sc[...] += lax.dot_general(p.astype(BF16), dob, CONTRACT_0, preferred_element_type=F32)
            dk_sc[...] += lax.dot_general(dsb, qb, CONTRACT_0, preferred_element_type=F32)
            rows = pl.ds(pl.multiple_of(i * tq, tq), tq)
            dq_ref[0, rows, :] += jnp.dot(dsb, kb, preferred_element_type=F32)
            if bias:
                dc_sc[...] -= jnp.sum(ds, axis=0, keepdims=True)
                dcq_ref[0, rows, :] += jnp.sum(ds, axis=1, keepdims=True)

        if causal:
            pl.when(i > j)(functools.partial(step, False))
            pl.when(i == j)(functools.partial(step, True))
        else:
            step(False)

        @pl.when(i == nq - 1)
        def _():
            dk_ref[0] = dk_sc[...]
            dv_ref[0] = dv_sc[...]
            if bias:
                dc_ref[0] = dc_sc[...]

        if carry:
            @pl.when(step_no == last_no)
            def _():
                for cp in _direct_copies(carry[0], c_in, c_out, *sems):
                    cp.wait()

    q_spec = pl.BlockSpec((1, tq, d_), lambda h, a, b: (h, blocks(a, b)[1], 0))
    q1_spec = pl.BlockSpec((1, tq, 1), lambda h, a, b: (h, blocks(a, b)[1], 0))
    k_spec = pl.BlockSpec((1, tk, d_), lambda h, a, b: (h, blocks(a, b)[0], 0))
    c_spec = pl.BlockSpec((1, 1, tk), lambda h, a, b: (h, 0, blocks(a, b)[0]))
    in_specs = [q_spec, k_spec, k_spec] + ([c_spec] if bias else []) + [q_spec, q1_spec, q_spec]
    operands = [q, k, v] + ([ck] if bias else []) + [o, lse, do]
    out_specs = [pl.BlockSpec((1, sq, d_), lambda h, a, b: (h, 0, 0)), k_spec, k_spec]
    out_shape = [jax.ShapeDtypeStruct((h_, sq, d_), F32), jax.ShapeDtypeStruct((h_, sk, d_), F32),
                 jax.ShapeDtypeStruct((h_, sk, d_), F32)]
    scratch = [pltpu.VMEM((tk, d_), F32), pltpu.VMEM((tk, d_), F32)]
    if bias:
        out_specs += [c_spec, pl.BlockSpec((1, sq, 1), lambda h, a, b: (h, 0, 0))]
        out_shape += [jax.ShapeDtypeStruct((h_, 1, sk), F32), jax.ShapeDtypeStruct((h_, sq, 1), F32)]
        scratch.append(pltpu.VMEM((1, tk), F32))
    if carry:
        c_in_specs, c_out_specs, c_shapes, c_sems = _exchange_operands(*carry)
        in_specs, out_specs, out_shape, scratch = in_specs + c_in_specs, out_specs + c_out_specs, out_shape + c_shapes, scratch + c_sems
        operands = operands + list(carry[1])
    return pl.pallas_call(
        body, name=name, grid=grid,
        in_specs=in_specs, out_specs=tuple(out_specs), out_shape=tuple(out_shape),
        scratch_shapes=scratch,
        compiler_params=_params("arbitrary" if carry else "parallel", "arbitrary", "arbitrary"),
    )(*operands)


def _tri(lower):
    row = lax.broadcasted_iota(jnp.int32, (CHUNK, CHUNK), 0)
    col = lax.broadcasted_iota(jnp.int32, (CHUNK, CHUNK), 1)
    return jnp.where((col <= row) if lower else (col >= row), 1.0, 0.0).astype(F32)


def fox_gate_fwd(f, b, name):
    s_ = f.shape[0]

    def body(f_ref, b_ref, c_ref, carry):
        @pl.when(pl.program_id(0) == 0)
        def _():
            carry[...] = jnp.zeros(carry.shape, F32)

        xv = f_ref[...] + b_ref[...]
        log_f = jnp.minimum(xv, 0.0) - jnp.log(1.0 + jnp.exp(-jnp.abs(xv)))
        c = jnp.dot(_tri(True), log_f, precision=lax.Precision.HIGHEST, preferred_element_type=F32) + carry[...]
        c_ref[...] = c
        carry[...] = c[CHUNK - 1:CHUNK, :]

    blk = pl.BlockSpec((CHUNK, LANES), lambda i: (i, 0))
    return pl.pallas_call(
        body, name=name, grid=(s_ // CHUNK,),
        in_specs=[blk, pl.BlockSpec((1, LANES), lambda i: (0, 0))], out_specs=blk,
        out_shape=jax.ShapeDtypeStruct((s_, LANES), F32),
        scratch_shapes=[pltpu.VMEM((1, LANES), F32)],
        compiler_params=_params("arbitrary"),
    )(f, b)


def fox_gate_bwd(dc, f, b, name):
    s_ = f.shape[0]
    n = s_ // CHUNK

    def body(dc_ref, f_ref, b_ref, df_ref, db_ref, carry):
        @pl.when(pl.program_id(0) == 0)
        def _():
            carry[...] = jnp.zeros(carry.shape, F32)
            db_ref[...] = jnp.zeros(db_ref.shape, F32)

        dlog = jnp.dot(_tri(False), dc_ref[...], precision=lax.Precision.HIGHEST,
                       preferred_element_type=F32) + carry[...]
        df = dlog * _sigmoid(-(f_ref[...] + b_ref[...]))
        df_ref[...] = df
        db_ref[...] += jnp.sum(df, axis=0, keepdims=True)
        carry[...] = dlog[0:1, :]

    blk = pl.BlockSpec((CHUNK, LANES), lambda i: (n - 1 - i, 0))
    vec = pl.BlockSpec((1, LANES), lambda i: (0, 0))
    return pl.pallas_call(
        body, name=name, grid=(n,),
        in_specs=[blk, blk, vec], out_specs=(blk, vec),
        out_shape=(jax.ShapeDtypeStruct((s_, LANES), F32), jax.ShapeDtypeStruct((1, LANES), F32)),
        scratch_shapes=[pltpu.VMEM((1, LANES), F32)],
        compiler_params=_params("arbitrary"),
    )(dc, f, b)


GELU_K = math.sqrt(2.0 / math.pi)
GELU_C = 0.044715


def _gelu(x):
    return 0.5 * x * (1.0 + jnp.tanh(GELU_K * (x + GELU_C * (x * x * x))))


def _gelu_grad(x):
    t = jnp.tanh(GELU_K * (x + GELU_C * (x * x * x)))
    return 0.5 * (1.0 + t) + 0.5 * x * (1.0 - t * t) * (GELU_K * (1.0 + 3.0 * GELU_C * (x * x)))


def _tril_mask():
    row = lax.broadcasted_iota(jnp.int32, (CHUNK, CHUNK), 0)
    col = lax.broadcasted_iota(jnp.int32, (CHUNK, CHUNK), 1)
    return col <= row


def _gmlp_specs(s_, ts):
    row = pl.BlockSpec((1, ts, HEAD_DIM), lambda g, i: (g, i, 0))
    gain = pl.BlockSpec((1, 1, HEAD_DIM), lambda g, i: (g, 0, 0))
    w = pl.BlockSpec((1, CHUNK, CHUNK), lambda g, i: (g, 0, 0))
    b = pl.BlockSpec((1, CHUNK, 1), lambda g, i: (g, 0, 0))
    return row, gain, w, b


def gmlp_fwd(pu, pv, gain, w, b, name):
    g_, s_, d_ = pu.shape
    ts = _pick(s_, (1024, 512, 256, 128))

    def body(pu_ref, pv_ref, g_ref, w_ref, b_ref, o_ref):
        v = _gelu(pv_ref[0])
        r = lax.rsqrt(jnp.mean(v * v, axis=-1, keepdims=True) + EPS)
        vn = (v * r * g_ref[0]).astype(BF16)
        wt = jnp.where(_tril_mask(), w_ref[0], 0.0).astype(BF16)
        for c in range(ts // CHUNK):
            rows = pl.ds(c * CHUNK, CHUNK)
            gate = jnp.dot(wt, vn[c * CHUNK:(c + 1) * CHUNK], preferred_element_type=F32) + b_ref[0]
            o_ref[0, rows, :] = _gelu(pu_ref[0, rows, :]) * gate

    row, gspec, wspec, bspec = _gmlp_specs(s_, ts)
    return pl.pallas_call(
        body, name=name, grid=(g_, s_ // ts),
        in_specs=[row, row, gspec, wspec, bspec], out_specs=row,
        out_shape=jax.ShapeDtypeStruct((g_, s_, d_), F32),
        compiler_params=_params("parallel", "parallel"),
    )(pu, pv, gain, w, b)


def gmlp_bwd(pu, pv, gain, w, b, dout, name):
    g_, s_, d_ = pu.shape
    ts = _pick(s_, (1024, 512, 256, 128))

    def body(pu_ref, pv_ref, g_ref, w_ref, b_ref, do_ref, dpu_ref, dpv_ref, dw_ref, db_ref, dg_ref):
        @pl.when(pl.program_id(1) == 0)
        def _():
            dw_ref[...] = jnp.zeros(dw_ref.shape, F32)
            db_ref[...] = jnp.zeros(db_ref.shape, F32)
            dg_ref[...] = jnp.zeros(dg_ref.shape, F32)

        gain_v = g_ref[0]
        mask = _tril_mask()
        wt = jnp.where(mask, w_ref[0], 0.0).astype(BF16)
        dw = jnp.zeros((CHUNK, CHUNK), F32)
        db = jnp.zeros((CHUNK, 1), F32)
        dg = jnp.zeros((1, d_), F32)
        for c in range(ts // CHUNK):
            rows = pl.ds(c * CHUNK, CHUNK)
            pu_c = pu_ref[0, rows, :]
            pv_c = pv_ref[0, rows, :]
            do_c = do_ref[0, rows, :]
            u = _gelu(pu_c)
            v = _gelu(pv_c)
            r = lax.rsqrt(jnp.mean(v * v, axis=-1, keepdims=True) + EPS)
            n = v * r
            vn = (n * gain_v).astype(BF16)
            gate = jnp.dot(wt, vn, preferred_element_type=F32) + b_ref[0]
            dgate = do_c * u
            dgate_b = dgate.astype(BF16)
            dpu_ref[0, rows, :] = do_c * gate * _gelu_grad(pu_c)
            db = db + jnp.sum(dgate, axis=1, keepdims=True)
            dw = dw + lax.dot_general(dgate_b, vn, CONTRACT_1, preferred_element_type=F32)
            dvn = lax.dot_general(wt, dgate_b, CONTRACT_0, preferred_element_type=F32)
            dg = dg + jnp.sum(dvn * n, axis=0, keepdims=True)
            dn = dvn * gain_v
            dv = r * (dn - n * jnp.mean(dn * n, axis=-1, keepdims=True))
            dpv_ref[0, rows, :] = dv * _gelu_grad(pv_c)
        dw_ref[0] += jnp.where(mask, dw, 0.0)
        db_ref[0] += db
        dg_ref[0] += dg

    row, gspec, wspec, bspec = _gmlp_specs(s_, ts)
    return pl.pallas_call(
        body, name=name, grid=(g_, s_ // ts),
        in_specs=[row, row, gspec, wspec, bspec, row],
        out_specs=(row, row, wspec, bspec, gspec),
        out_shape=(jax.ShapeDtypeStruct((g_, s_, d_), F32), jax.ShapeDtypeStruct((g_, s_, d_), F32),
                   jax.ShapeDtypeStruct((g_, CHUNK, CHUNK), F32), jax.ShapeDtypeStruct((g_, CHUNK, 1), F32),
                   jax.ShapeDtypeStruct((g_, 1, d_), F32)),
        compiler_params=_params("parallel", "arbitrary"),
    )(pu, pv, gain, w, b, dout)


def loss_head(y, t, name):
    s_, d_ = y.shape
    tr = _pick(s_, (1024, 512, 256, 128))

    def body(y_ref, t_ref, dy_ref, l_ref):
        err = y_ref[...] - t_ref[...]
        dy_ref[...] = err * (1.0 / d_)
        part = jnp.full(l_ref.shape, jnp.sum(err * err) * (0.5 / d_), F32)

        @pl.when(pl.program_id(0) == 0)
        def _():
            l_ref[...] = part

        @pl.when(pl.program_id(0) != 0)
        def _():
            l_ref[...] += part

    blk = pl.BlockSpec((tr, d_), lambda i: (i, 0))
    dy, l = pl.pallas_call(
        body, name=name, grid=(s_ // tr,),
        in_specs=[blk, blk], out_specs=(blk, pl.BlockSpec((8, LANES), lambda i: (0, 0))),
        out_shape=(jax.ShapeDtypeStruct((s_, d_), F32), jax.ShapeDtypeStruct((8, LANES), F32)),
        compiler_params=_params("arbitrary"),
    )(y, t)
    return dy, l[0, 0]


def adamw(w, g_slots, m, v, name):
    r_, c_ = w.shape
    tr = _pick(r_, (1024, 512, 352, 256, 224, 128, 64, 32, 16, 8))

    def body(w_ref, gs_ref, m_ref, v_ref, g_ref, d_ref, nm_ref, nv_ref):
        g = gs_ref[0].astype(F32)
        for k in range(1, N_DEV):
            g = g + gs_ref[k].astype(F32)
        m_new = ADAM_B1 * m_ref[...] + (1.0 - ADAM_B1) * g
        v_new = ADAM_B2 * v_ref[...] + (1.0 - ADAM_B2) * (g * g)
        m_hat = m_new / (1.0 - ADAM_B1 ** ADAM_STEP)
        v_hat = v_new / (1.0 - ADAM_B2 ** ADAM_STEP)
        g_ref[...] = g
        d_ref[...] = -ADAM_LR * (m_hat / (jnp.sqrt(v_hat) + ADAM_EPS) + ADAM_WD * w_ref[...])
        nm_ref[...] = m_new
        nv_ref[...] = v_new

    blk = pl.BlockSpec((tr, c_), lambda i: (i, 0))
    out = jax.ShapeDtypeStruct((r_, c_), F32)
    return pl.pallas_call(
        body, name=name, grid=(r_ // tr,),
        in_specs=[blk, pl.BlockSpec((N_DEV, tr, c_), lambda i: (0, i, 0)), blk, blk],
        out_specs=(blk, blk, blk, blk), out_shape=(out, out, out, out),
        compiler_params=_params("parallel"),
    )(w, g_slots, m, v)


def _position():
    x, y, c = lax.axis_index("x"), lax.axis_index("y"), lax.axis_index("c")
    return x, y, c


def _slot(px, py, pc):
    return 4 * px + 2 * py + pc


def all_gather_multi(blocks, name):
    n = len(blocks)

    def body(*refs):
        x_refs, out_refs = refs[:n], refs[n:2 * n]
        send_sems, recv_sems, local_sems = refs[2 * n:]
        x, y, c = _position()
        me, sibling = (x, y, c), (x, y, 1 - c)
        chips = [(1 - x, y), (x, 1 - y), (1 - x, 1 - y)]

        def copy(b, k, owner, to, src=None):
            dst = out_refs[b].at[_slot(*owner)]
            return pltpu.make_async_remote_copy(
                src_ref=dst if src is None else src, dst_ref=dst,
                send_sem=send_sems.at[b, k], recv_sem=recv_sems.at[b, k], device_id=to, device_id_type=MESH)

        mine = [pltpu.make_async_copy(x_refs[b], out_refs[b].at[_slot(*me)], local_sems.at[b]) for b in range(n)]
        for cp in mine:
            cp.start()
        first = [copy(b, 1 + j, me, (*chip, c), src=x_refs[b]) for j, chip in enumerate(chips) for b in range(n)]
        first += [copy(b, 0, me, sibling, src=x_refs[b]) for b in range(n)]
        for cp in first:
            cp.start()
        passed = []
        for j, chip in enumerate(chips):
            for b in range(n):
                copy(b, 1 + j, (*chip, c), me).wait_recv()
                cp = copy(b, 4 + j, (*chip, c), sibling)
                cp.start()
                passed.append(cp)
        for b in range(n):
            copy(b, 0, sibling, me).wait_recv()
        for j, chip in enumerate(chips):
            for b in range(n):
                copy(b, 4 + j, (*chip, 1 - c), me).wait_recv()
        for cp in first + passed:
            cp.wait_send()
        for cp in mine:
            cp.wait()

    any_spec = pl.BlockSpec(memory_space=pl.ANY)
    return pl.pallas_call(
        body, name=name,
        in_specs=[any_spec] * n, out_specs=tuple([any_spec] * n),
        out_shape=tuple(jax.ShapeDtypeStruct((N_DEV,) + blk.shape, blk.dtype) for blk in blocks),
        scratch_shapes=[pltpu.SemaphoreType.DMA((n, 7)), pltpu.SemaphoreType.DMA((n, 7)), pltpu.SemaphoreType.DMA((n,))],
    )(*blocks)


def _direct_copies(scatter, in_refs, out_refs, send_sems, recv_sems, local_sems):
    x, y, c = _position()
    me = _slot(x, y, c)
    src = (lambda ref, slot: ref.at[slot]) if scatter else (lambda ref, slot: ref)
    copies = [pltpu.make_async_copy(src(ref, me), out.at[me], local_sems.at[b])
              for b, (ref, out) in enumerate(zip(in_refs, out_refs))]
    for k in range(1, N_DEV):
        px = 1 - x if k & 4 else x
        py = 1 - y if k & 2 else y
        pc = 1 - c if k & 1 else c
        copies += [pltpu.make_async_remote_copy(
            src_ref=src(ref, _slot(px, py, pc)), dst_ref=out.at[me],
            send_sem=send_sems.at[b, k - 1], recv_sem=recv_sems.at[b, k - 1],
            device_id=(px, py, pc), device_id_type=MESH) for b, (ref, out) in enumerate(zip(in_refs, out_refs))]
    return copies


def _exchange_operands(scatter, bufs):
    n = len(bufs)
    any_spec = pl.BlockSpec(memory_space=pl.ANY)
    shapes = [jax.ShapeDtypeStruct(b.shape if scatter else (N_DEV,) + b.shape, b.dtype) for b in bufs]
    sems = [pltpu.SemaphoreType.DMA((n, 7)), pltpu.SemaphoreType.DMA((n, 7)), pltpu.SemaphoreType.DMA((n,))]
    return [any_spec] * n, [any_spec] * n, shapes, sems


def all_to_all_multi(slot_bufs, name):
    n = len(slot_bufs)

    def body(*refs):
        copies = _direct_copies(True, refs[:n], refs[n:2 * n], *refs[2 * n:])
        for cp in copies:
            cp.start()
        for cp in copies:
            cp.wait()

    in_specs, out_specs, out_shape, sems = _exchange_operands(True, slot_bufs)
    return pl.pallas_call(
        body, name=name, in_specs=in_specs, out_specs=tuple(out_specs), out_shape=tuple(out_shape),
        scratch_shapes=sems,
    )(*slot_bufs)


def _seg_len(n):
    return -(-n // SEG_ALIGN) * SEG_ALIGN


def _pack(arrays, lead=()):
    parts, total = [], 0
    for a in arrays:
        flat = a.reshape(lead + (-1,))
        pad = _seg_len(flat.shape[-1]) - flat.shape[-1]
        parts.append(jnp.pad(flat, [(0, 0)] * len(lead) + [(0, pad)]) if pad else flat)
        total += flat.shape[-1] + pad
    tail = -(-total // PACK_ALIGN) * PACK_ALIGN - total
    if tail:
        parts.append(jnp.zeros(lead + (tail,), parts[0].dtype))
    return jnp.concatenate(parts, axis=-1).reshape(lead + (-1, LANES))


def _unpack(packed, shapes, lead=()):
    flat = packed.reshape(lead + (-1,))
    out, off = [], 0
    for shp in shapes:
        n = math.prod(shp)
        out.append(flat[..., off:off + n].reshape(lead + tuple(shp)))
        off += _seg_len(n)
    return out


def _heads(a, n_heads):
    return a.reshape(a.shape[0], n_heads, HEAD_DIM).transpose(1, 0, 2)


def _unheads(a):
    return a.transpose(1, 0, 2).reshape(a.shape[1], a.shape[0] * HEAD_DIM)


def _head_gain(g, n_heads):
    return jnp.broadcast_to(g.reshape(1, 1, HEAD_DIM), (n_heads, 1, HEAD_DIM))


def _ffn_forward(x, gain, w_in_t, w_out, tag, carry=None):
    h = rmsnorm_fwd(x[None], gain.reshape(1, 1, -1), BF16, f"{tag}_norm")[0]
    a, b, u, *carried = ffn_in(h, w_in_t, f"{tag}_in", carry)
    if carry:
        w_out = w_out(carried)
    y = matmul(u, w_out, f"{tag}_out", residual=x, scale=0.5)
    return y, (x, h, a, b, u), carried


def _ffn_backward(dy, saved, gain, w_in_t, w_out, tag, carry_of=None):
    x, h, a, b, u = saved
    f_ = w_out.shape[0]
    dw_out = matmul_tn(u, dy, f"{tag}_dwout", scale=0.5, out_dtype=BF16)
    da, db, *carried = ffn_bwd_act(dy, w_out, a, b, f"{tag}_dact", carry_of(dw_out) if carry_of else None)
    dh = matmul(da, w_in_t[:f_], f"{tag}_dh_a")
    dh = matmul(db, w_in_t[f_:], f"{tag}_dh_b", residual=dh)
    dw_in_t = jnp.concatenate([matmul_tn(da, h, f"{tag}_dwin_a", out_dtype=BF16),
                               matmul_tn(db, h, f"{tag}_dwin_b", out_dtype=BF16)], axis=0)
    dx, dgain = rmsnorm_bwd(x[None], gain.reshape(1, 1, -1), dh[None], f"{tag}_dnorm", residual=dy[None])
    return dx[0], dgain.reshape(-1), dw_in_t, dw_out, carried


def _mem_forward(mq, mem_n, w_kv, g_q, g_k, tag):
    gq = _head_gain(g_q, MEM_HEADS)
    gk = _head_gain(g_k, MEM_HEADS)
    qn = rmsnorm_fwd(mq, gq * QK_SCALE, BF16, f"{tag}_qnorm")
    kv = matmul(mem_n, w_kv, f"{tag}_kv")
    k = _heads(kv[:, :MEM_WIDTH], MEM_HEADS)
    v = _heads(kv[:, MEM_WIDTH:], MEM_HEADS)
    kn = rmsnorm_fwd(k, gk, BF16, f"{tag}_knorm")
    o, lse = attn_fwd(qn, kn, _widen_v(v), None, False, f"{tag}_attn")
    return o, (mq, gq, gk, qn, k, kn, v, o, lse)


def _mem_backward(do, saved, mem_n, w_kv, tag):
    mq, gq, gk, qn, k, kn, v, o, lse = saved
    dqn, dkn, dv = attn_bwd(qn, kn, v, None, o, lse, do, False, f"{tag}_dattn")
    dmq, dgq = rmsnorm_bwd(mq, gq, dqn * QK_SCALE, f"{tag}_dqnorm")
    dk, dgk = rmsnorm_bwd(k, gk, dkn, f"{tag}_dknorm")
    dkv = jnp.concatenate([_unheads(dk), _unheads(dv)], axis=1)
    dw_kv = matmul_tn(mem_n, dkv, f"{tag}_dwkv", out_dtype=BF16)
    dmem_n = matmul(dkv, w_kv, f"{tag}_dmem", transpose_b=True)
    return dmq, dgq.sum(axis=0).reshape(-1), dgk.sum(axis=0).reshape(-1), dw_kv, dmem_n


def _fox_split(w_in):
    t3 = 3 * TOK_WIDTH
    pad = jnp.zeros(w_in.shape[:-1] + (LANES - FOX_HEADS,), w_in.dtype)
    return jnp.concatenate([w_in[..., :t3], w_in[..., t3 + FOX_HEADS:], w_in[..., t3:t3 + FOX_HEADS], pad], axis=-1)


def _fox_unsplit(w):
    t3 = 3 * TOK_WIDTH
    return jnp.concatenate([w[..., :t3], w[..., t3 + MEM_WIDTH:t3 + MEM_WIDTH + FOX_HEADS], w[..., t3:t3 + MEM_WIDTH]],
                           axis=-1)


def _fox_forward(h, w_split, b_f, g_q, g_k, tag, carry=None):
    t3 = 3 * TOK_WIDTH
    proj = matmul(h, w_split, f"{tag}_proj")
    qkv = _heads(proj[:, :t3], 3 * FOX_HEADS)
    mq = _heads(proj[:, t3:t3 + MEM_WIDTH], MEM_HEADS)
    f_pad = proj[:, t3 + MEM_WIDTH:]
    b_pad = jnp.pad(b_f.reshape(1, -1), ((0, 0), (0, LANES - FOX_HEADS)))
    gains = jnp.concatenate([_head_gain(g_q, FOX_HEADS), _head_gain(g_k, FOX_HEADS)], axis=0)
    qk = qkv[:2 * FOX_HEADS]
    scaled = jnp.concatenate([gains[:FOX_HEADS] * QK_SCALE, gains[FOX_HEADS:]], axis=0)
    qkn = rmsnorm_fwd(qk, scaled, BF16, f"{tag}_qknorm")
    v = qkv[2 * FOX_HEADS:]
    c = fox_gate_fwd(f_pad, b_pad, f"{tag}_gate")
    ck = c[:, :FOX_HEADS].T[:, None, :]
    qn, kn = qkn[:FOX_HEADS], qkn[FOX_HEADS:]
    o, lse, *carried = attn_fwd(qn, kn, _widen_v(v), ck, True, f"{tag}_attn", carry)
    return o, mq, (qk, gains, qn, kn, v, ck, o, lse, f_pad, b_pad), carried


def _fox_backward(do, dmq, saved, tag, carry=None):
    qk, gains, qn, kn, v, ck, o, lse, f_pad, b_pad = saved
    dqn, dkn, dv, dck, dcq, *carried = attn_bwd(qn, kn, v, ck, o, lse, do, True, f"{tag}_dattn", carry)
    dqk, dgains = rmsnorm_bwd(qk, gains, jnp.concatenate([dqn * QK_SCALE, dkn], axis=0), f"{tag}_dqknorm")
    dc = jnp.pad((dck[:, 0, :] + dcq[:, :, 0]).T, ((0, 0), (0, LANES - FOX_HEADS)))
    df, db = fox_gate_bwd(dc, f_pad, b_pad, f"{tag}_dgate")
    dproj = jnp.concatenate([_unheads(dqk), _unheads(dv), _unheads(dmq), df], axis=1)
    dgains = dgains.reshape(2, FOX_HEADS, HEAD_DIM).sum(axis=1)
    return dproj, db[0, :FOX_HEADS], dgains[0], dgains[1], carried


def _gmlp_forward(h, w_in_t, v_gain, w_s, b_s, tag):
    proj = matmul(h, w_in_t, f"{tag}_proj", transpose_b=True)
    pu = _heads(proj[:, :TOK_WIDTH], FOX_HEADS)
    pv = _heads(proj[:, TOK_WIDTH:2 * TOK_WIDTH], FOX_HEADS)
    mq = _heads(proj[:, 2 * TOK_WIDTH:], MEM_HEADS)
    gain = v_gain.reshape(FOX_HEADS, 1, HEAD_DIM)
    b = b_s[:, :, None]
    o = gmlp_fwd(pu, pv, gain, w_s, b, f"{tag}_sgu")
    return o, mq, (pu, pv, gain, w_s, b)


def _gmlp_backward(do, dmq, saved, tag):
    pu, pv, gain, w_s, b = saved
    dpu, dpv, dw, db, dg = gmlp_bwd(pu, pv, gain, w_s, b, do, f"{tag}_dsgu")
    dproj = jnp.concatenate([_unheads(dpu), _unheads(dpv), _unheads(dmq)], axis=1)
    return dproj, dg.reshape(-1), dw, db[:, :, 0]


BIG = ("ffn1_w_in", "ffn1_w_out", "ffn2_w_in", "ffn2_w_out", "w_out", "mem_w_kv", "fox_w_in", "gmlp_w_in")
COLUMN_SHARDED = ("ffn1_w_in", "ffn2_w_in", "gmlp_w_in")
REPLICATED =("norm_ffn1", "norm_mix", "norm_ffn2", "mem_norm", "mem_q_norm", "mem_k_norm", "fox_b_f",
              "fox_q_norm", "fox_k_norm", "gmlp_w_s", "gmlp_b_s")
WEIGHTS = ("norm_ffn1", "ffn1_w_in", "ffn1_w_out", "norm_mix", "norm_ffn2", "ffn2_w_in", "ffn2_w_out", "w_out",
           "mem_norm", "mem_w_kv", "mem_q_norm", "mem_k_norm", "fox_w_in", "fox_b_f", "fox_q_norm", "fox_k_norm",
           "gmlp_w_in", "gmlp_v_norm", "gmlp_w_s", "gmlp_b_s")


def _to_transport(name, a):
    if name in COLUMN_SHARDED:
        return jnp.swapaxes(a, -1, -2)
    return _fox_split(a) if name == "fox_w_in" else a


def _from_transport(name, a):
    if name in COLUMN_SHARDED:
        return jnp.swapaxes(a, -1, -2)
    return _fox_unsplit(a) if name == "fox_w_in" else a


def kernel(x, mem, norm_ffn1, ffn1_w_in, ffn1_w_out, norm_mix, norm_ffn2, ffn2_w_in, ffn2_w_out, w_out, mem_norm, mem_w_kv, mem_q_norm, mem_k_norm, fox_w_in, fox_b_f, fox_q_norm, fox_k_norm, gmlp_w_in, gmlp_v_norm, gmlp_w_s, gmlp_b_s, loss_target, m_norm_ffn1, m_ffn1_w_in, m_ffn1_w_out, m_norm_mix, m_norm_ffn2, m_ffn2_w_in, m_ffn2_w_out, m_w_out, m_mem_norm, m_mem_w_kv, m_mem_q_norm, m_mem_k_norm, m_fox_w_in, m_fox_b_f, m_fox_q_norm, m_fox_k_norm, m_gmlp_w_in, m_gmlp_v_norm, m_gmlp_w_s, m_gmlp_b_s, v_norm_ffn1, v_ffn1_w_in, v_ffn1_w_out, v_norm_mix, v_norm_ffn2, v_ffn2_w_in, v_ffn2_w_out, v_w_out, v_mem_norm, v_mem_w_kv, v_mem_q_norm, v_mem_k_norm, v_fox_w_in, v_fox_b_f, v_fox_q_norm, v_fox_k_norm, v_gmlp_w_in, v_gmlp_v_norm, v_gmlp_w_s, v_gmlp_b_s):
    w = dict(norm_ffn1=norm_ffn1, ffn1_w_in=ffn1_w_in, ffn1_w_out=ffn1_w_out, norm_mix=norm_mix, norm_ffn2=norm_ffn2, ffn2_w_in=ffn2_w_in, ffn2_w_out=ffn2_w_out, w_out=w_out, mem_norm=mem_norm, mem_w_kv=mem_w_kv, mem_q_norm=mem_q_norm, mem_k_norm=mem_k_norm, fox_w_in=fox_w_in, fox_b_f=fox_b_f, fox_q_norm=fox_q_norm, fox_k_norm=fox_k_norm, gmlp_w_in=gmlp_w_in, gmlp_v_norm=gmlp_v_norm, gmlp_w_s=gmlp_w_s, gmlp_b_s=gmlp_b_s)
    m = dict(norm_ffn1=m_norm_ffn1, ffn1_w_in=m_ffn1_w_in, ffn1_w_out=m_ffn1_w_out, norm_mix=m_norm_mix, norm_ffn2=m_norm_ffn2, ffn2_w_in=m_ffn2_w_in, ffn2_w_out=m_ffn2_w_out, w_out=m_w_out, mem_norm=m_mem_norm, mem_w_kv=m_mem_w_kv, mem_q_norm=m_mem_q_norm, mem_k_norm=m_mem_k_norm, fox_w_in=m_fox_w_in, fox_b_f=m_fox_b_f, fox_q_norm=m_fox_q_norm, fox_k_norm=m_fox_k_norm, gmlp_w_in=m_gmlp_w_in, gmlp_v_norm=m_gmlp_v_norm, gmlp_w_s=m_gmlp_w_s, gmlp_b_s=m_gmlp_b_s)
    v = dict(norm_ffn1=v_norm_ffn1, ffn1_w_in=v_ffn1_w_in, ffn1_w_out=v_ffn1_w_out, norm_mix=v_norm_mix, norm_ffn2=v_norm_ffn2, ffn2_w_in=v_ffn2_w_in, ffn2_w_out=v_ffn2_w_out, w_out=v_w_out, mem_norm=v_mem_norm, mem_w_kv=v_mem_w_kv, mem_q_norm=v_mem_q_norm, mem_k_norm=v_mem_k_norm, fox_w_in=v_fox_w_in, fox_b_f=v_fox_b_f, fox_q_norm=v_fox_q_norm, fox_k_norm=v_fox_k_norm, gmlp_w_in=v_gmlp_w_in, gmlp_v_norm=v_gmlp_v_norm, gmlp_w_s=v_gmlp_w_s, gmlp_b_s=v_gmlp_b_s)

    depth = norm_ffn1.shape[0]
    x0 = x[0]
    mem0 = mem[0]
    target = loss_target[0]
    me = _slot(*_position())

    keys = [(n, i) for n in BIG for i in range(w[n].shape[0])]
    local = {k: _to_transport(k[0], w[k[0]][k[1]]) for k in keys}
    n_gain, gain_len = gmlp_v_norm.shape
    pad_gain = lambda a: jnp.pad(a, ((0, 8 - n_gain), (0, LANES - gain_len)))
    first = [("ffn1_w_in", 0)]
    early = [("ffn1_w_out", 0), ("fox_w_in", 0)]
    inner = [k for k in keys if k not in first + early]
    stack = lambda g: g.reshape(-1, g.shape[-1])
    full = {k: stack(g) for k, g in zip(first, all_gather_multi([local[k].astype(BF16) for k in first], "gather_weights"))}
    gather_early = (False, [local[k].astype(BF16) for k in early])
    gather_inner = (False, [local[k].astype(BF16) for k in inner] + [pad_gain(gmlp_v_norm)])

    mem_n = rmsnorm_fwd(mem0[None], mem_norm.reshape(1, 1, -1), BF16, "mem_norm")[0]
    saved = []
    xi = x0
    for i in range(depth):
        kind, j = i % 2, i // 2
        if i == 0:
            x1, ffn1_saved, arrived = _ffn_forward(xi, norm_ffn1[i], full["ffn1_w_in", i], lambda got: stack(got[0]),
                                                   f"l{i}_ffn1", gather_early)
            full.update({k: stack(g) for k, g in zip(early, arrived)})
        else:
            x1, ffn1_saved, _ = _ffn_forward(xi, norm_ffn1[i], full["ffn1_w_in", i], full["ffn1_w_out", i], f"l{i}_ffn1")
        h = rmsnorm_fwd(x1[None], norm_mix[i].reshape(1, 1, -1), BF16, f"l{i}_mixnorm")[0]
        if kind == 0:
            tok, mq, mix_saved, arrived = _fox_forward(h, full["fox_w_in", j], fox_b_f[j], fox_q_norm[j], fox_k_norm[j],
                                                       f"l{i}_fox", gather_inner if i == 0 else None)
            if i == 0:
                full.update({k: stack(g) for k, g in zip(inner, arrived)})
                v_gain_full = arrived[-1][:, :n_gain, :gain_len]
        else:
            tok, mq, mix_saved = _gmlp_forward(h, full["gmlp_w_in", j], v_gain_full[:, j, :].reshape(-1), gmlp_w_s[j],
                                               gmlp_b_s[j], f"l{i}_gmlp")
        mo, mem_saved = _mem_forward(mq, mem_n, full["mem_w_kv", i], mem_q_norm[i], mem_k_norm[i], f"l{i}_mem")
        cat = _unheads(jnp.concatenate([tok, mo], axis=0)).astype(BF16)
        x2 = matmul(cat, full["w_out", i], f"l{i}_wout", residual=x1)
        x3, ffn2_saved, _ = _ffn_forward(x2, norm_ffn2[i], full["ffn2_w_in", i], full["ffn2_w_out", i], f"l{i}_ffn2")
        saved.append((ffn1_saved, x1, h, mix_saved, mem_saved, cat, ffn2_saved))
        xi = x3

    dy, loss_part = loss_head(xi, target, "loss_head")
    loss = lax.psum(loss_part, ("x", "y", "c"))

    small = {n: [None] * w[n].shape[0] for n in REPLICATED + ("gmlp_v_norm",) if n != "mem_norm"}
    big = {}
    slots_of = lambda g: g.reshape((N_DEV, -1, g.shape[-1]))
    dmem_n = None
    for i in reversed(range(depth)):
        kind, j = i % 2, i // 2
        ffn1_saved, x1, h, mix_saved, mem_saved, cat, ffn2_saved = saved[i]
        dy, small["norm_ffn2"][i], big["ffn2_w_in", i], big["ffn2_w_out", i], _ = _ffn_backward(
            dy, ffn2_saved, norm_ffn2[i], full["ffn2_w_in", i], full["ffn2_w_out", i], f"l{i}_ffn2")
        big["w_out", i] = matmul_tn(cat, dy, f"l{i}_dwout", out_dtype=BF16)
        dcat = _heads(matmul(dy, full["w_out", i], f"l{i}_dcat", transpose_b=True), FOX_HEADS + MEM_HEADS)
        dmq, small["mem_q_norm"][i], small["mem_k_norm"][i], big["mem_w_kv", i], dmem_i = _mem_backward(
            dcat[FOX_HEADS:], mem_saved, mem_n, full["mem_w_kv", i], f"l{i}_mem")
        dmem_n = dmem_i if dmem_n is None else dmem_n + dmem_i
        if kind == 0:
            scatter_inner = (True, [slots_of(big[k]) for k in inner]) if i == 0 else None
            dproj, small["fox_b_f"][j], small["fox_q_norm"][j], small["fox_k_norm"][j], arrived = _fox_backward(
                dcat[:FOX_HEADS], dmq, mix_saved, f"l{i}_fox", scatter_inner)
            if i == 0:
                got = dict(zip(inner, arrived))
            big["fox_w_in", j] = matmul_tn(h, dproj, f"l{i}_fox_dwin", out_dtype=BF16)
            dh = matmul(dproj, full["fox_w_in", j], f"l{i}_fox_dh", transpose_b=True)
        else:
            dproj, small["gmlp_v_norm"][j], small["gmlp_w_s"][j], small["gmlp_b_s"][j] = _gmlp_backward(
                dcat[:FOX_HEADS], dmq, mix_saved, f"l{i}_gmlp")
            big["gmlp_w_in", j] = matmul_tn(dproj, h, f"l{i}_gmlp_dwin", out_dtype=BF16)
            dh = matmul(dproj, full["gmlp_w_in", j], f"l{i}_gmlp_dh")
        dy, dg_mix = rmsnorm_bwd(x1[None], norm_mix[i].reshape(1, 1, -1), dh[None], f"l{i}_dmixnorm", residual=dy[None])
        dy, small["norm_mix"][i] = dy[0], dg_mix.reshape(-1)
        scatter_early = (lambda dw_out: (True, [slots_of(dw_out), slots_of(big["fox_w_in", 0])])) if i == 0 else None
        dy, small["norm_ffn1"][i], big["ffn1_w_in", i], big["ffn1_w_out", i], arrived = _ffn_backward(
            dy, ffn1_saved, norm_ffn1[i], full["ffn1_w_in", i], full["ffn1_w_out", i], f"l{i}_ffn1", scatter_early)
        if i == 0:
            got.update(zip(early, arrived))
    grad_x = dy[None]
    _, dg_mem = rmsnorm_bwd(mem0[None], mem_norm.reshape(1, 1, -1), dmem_n[None], "dmem_norm")
    small = {n: jnp.stack(g) for n, g in small.items()}
    small["mem_norm"] = dg_mem.reshape(-1)

    got.update(zip(first, all_to_all_multi([slots_of(big[k]) for k in first], "exchange_grads")))
    results = {n: [[None] * w[n].shape[0] for _ in range(4)] for n in BIG}
    for k in keys:
        n, i = k
        outs = adamw(local[k], got[k], _to_transport(n, m[n][i]), _to_transport(n, v[n][i]), f"adamw_{n}_{i}")
        for q in range(4):
            results[n][q][i] = _from_transport(n, outs[q])
    sharded = {n: [jnp.stack(r) for r in results[n]] for n in BIG}

    small_names = REPLICATED + ("gmlp_v_norm",)
    (small_got,) = all_gather_multi([_pack([small[n] for n in small_names])], "gather_small_grads")
    rep_shapes = [w[n].shape for n in REPLICATED]
    gain_seg = jnp.zeros((n_gain, N_DEV * gain_len), F32)
    pack_rep = lambda d: _pack([d[n] for n in REPLICATED] + [gain_seg])
    outs = adamw(pack_rep(w), small_got, pack_rep(m), pack_rep(v), "adamw_replicated")
    replicated = [dict(zip(REPLICATED, _unpack(o, rep_shapes))) for o in outs]
    gain_parts = _unpack(small_got, rep_shapes + [(n_gain, N_DEV * gain_len)], lead=(N_DEV,))[-1]
    gain_slots = lax.dynamic_slice_in_dim(gain_parts, me * gain_len, gain_len, axis=2)
    gain_slots = jnp.pad(gain_slots, ((0, 0), (0, 8 - n_gain), (0, LANES - gain_len)))
    outs = adamw(pad_gain(gmlp_v_norm), gain_slots, pad_gain(m["gmlp_v_norm"]), pad_gain(v["gmlp_v_norm"]),
                 "adamw_gmlp_v_norm")
    sharded["gmlp_v_norm"] = [o[:n_gain, :gain_len] for o in outs]

    out = [loss, grad_x]
    for q in range(4):
        out += [(replicated[q][n] if n in REPLICATED else sharded[n][q]) for n in WEIGHTS]
    return tuple(out)
```

```python
import functools
import math

import jax
import jax.numpy as jnp
from jax import lax
from jax.experimental import pallas as pl
from jax.experimental.pallas import tpu as pltpu

F32 = jnp.float32
BF16 = jnp.bfloat16

EPS = 1e-6
HEAD_DIM = 64
FOX_HEADS = 12
MEM_HEADS = 4
TOK_WIDTH = FOX_HEADS * HEAD_DIM
MEM_WIDTH = MEM_HEADS * HEAD_DIM
CHUNK = 128
LANES = 128
N_DEV = 8
SEG_ALIGN = 16 * LANES
PACK_ROWS = 1024
PACK_ALIGN = PACK_ROWS * LANES

ADAM_LR = 0.001
ADAM_B1 = 0.9
ADAM_B2 = 0.999
ADAM_EPS = 1e-08
ADAM_WD = 0.01
ADAM_STEP = 10

VMEM_LIMIT_BYTES = 48 * 1024 * 1024
MESH = pl.DeviceIdType.MESH
CONTRACT_0 = (((0,), (0,)), ((), ()))
CONTRACT_1 = (((1,), (1,)), ((), ()))


def _params(*semantics):
    return pltpu.CompilerParams(dimension_semantics=semantics, vmem_limit_bytes=VMEM_LIMIT_BYTES)


def _pick(n, candidates):
    for c in candidates:
        if c <= n and n % c == 0:
            return c
    return n


def _sigmoid(x):
    return 0.5 * jnp.tanh(0.5 * x) + 0.5


def _row_tile(r, w):
    return _pick(r, (1024,) if w >= 512 else (2048, 1024, 512, 256))


def rmsnorm_fwd(x, gain, out_dtype, name):
    g_, r_, w_ = x.shape
    tr = _row_tile(r_, w_)

    def body(x_ref, g_ref, y_ref):
        xv = x_ref[0].astype(F32)
        r = lax.rsqrt(jnp.mean(xv * xv, axis=-1, keepdims=True) + EPS)
        y_ref[0] = (xv * r * g_ref[0]).astype(y_ref.dtype)

    return pl.pallas_call(
        body, name=name, grid=(g_, r_ // tr),
        in_specs=[pl.BlockSpec((1, tr, w_), lambda g, i: (g, i, 0)),
                  pl.BlockSpec((1, 1, w_), lambda g, i: (g, 0, 0))],
        out_specs=pl.BlockSpec((1, tr, w_), lambda g, i: (g, i, 0)),
        out_shape=jax.ShapeDtypeStruct((g_, r_, w_), out_dtype),
        compiler_params=_params("parallel", "parallel"),
    )(x, gain)


def rmsnorm_bwd(x, gain, dy, name, residual=None):
    g_, r_, w_ = x.shape
    tr = _row_tile(r_, w_)
    has_res = residual is not None

    def body(*refs):
        if has_res:
            x_ref, g_ref, dy_ref, res_ref, dx_ref, dg_ref = refs
        else:
            x_ref, g_ref, dy_ref, dx_ref, dg_ref = refs
        xv = x_ref[0].astype(F32)
        dyv = dy_ref[0].astype(F32)
        r = lax.rsqrt(jnp.mean(xv * xv, axis=-1, keepdims=True) + EPS)
        n = xv * r
        dn = dyv * g_ref[0]
        dx = r * (dn - n * jnp.mean(dn * n, axis=-1, keepdims=True))
        if has_res:
            dx = dx + res_ref[0]
        dx_ref[0] = dx
        part = jnp.sum(dyv * n, axis=0, keepdims=True)

        @pl.when(pl.program_id(1) == 0)
        def _():
            dg_ref[0] = part

        @pl.when(pl.program_id(1) != 0)
        def _():
            dg_ref[0] += part

    row = pl.BlockSpec((1, tr, w_), lambda g, i: (g, i, 0))
    vec = pl.BlockSpec((1, 1, w_), lambda g, i: (g, 0, 0))
    operands = (x, gain, dy) + ((residual,) if has_res else ())
    return pl.pallas_call(
        body, name=name, grid=(g_, r_ // tr),
        in_specs=[row, vec, row] + ([row] if has_res else []),
        out_specs=(row, vec),
        out_shape=(jax.ShapeDtypeStruct((g_, r_, w_), F32), jax.ShapeDtypeStruct((g_, 1, w_), F32)),
        compiler_params=_params("parallel", "arbitrary"),
    )(*operands)


def matmul(a, b, name, out_dtype=F32, residual=None, scale=None, transpose_b=False):
    m_, k_ = a.shape
    n_ = b.shape[0] if transpose_b else b.shape[1]
    tm = _pick(m_, (512, 256, 128))
    tn = _pick(n_, (1408, 1024, 896, 512, 256, 128))
    has_res = residual is not None

    def body(*refs):
        if has_res:
            a_ref, b_ref, res_ref, o_ref = refs
        else:
            a_ref, b_ref, o_ref = refs
        av, bv = a_ref[...].astype(BF16), b_ref[...].astype(BF16)
        if transpose_b:
            acc = lax.dot_general(av, bv, CONTRACT_1, preferred_element_type=F32)
        else:
            acc = jnp.dot(av, bv, preferred_element_type=F32)
        if scale is not None:
            acc = acc * scale
        if has_res:
            acc = acc + res_ref[...]
        o_ref[...] = acc.astype(o_ref.dtype)

    out_spec = pl.BlockSpec((tm, tn), lambda j, i: (i, j))
    b_spec = pl.BlockSpec((tn, k_), lambda j, i: (j, 0)) if transpose_b else pl.BlockSpec((k_, tn), lambda j, i: (0, j))
    operands = (a, b) + ((residual,) if has_res else ())
    return pl.pallas_call(
        body, name=name, grid=(n_ // tn, m_ // tm),
        in_specs=[pl.BlockSpec((tm, k_), lambda j, i: (i, 0)), b_spec] + ([out_spec] if has_res else []),
        out_specs=out_spec,
        out_shape=jax.ShapeDtypeStruct((m_, n_), out_dtype),
        compiler_params=_params("parallel", "parallel"),
    )(*operands)


def matmul_tn(a, b, name, scale=None, out_dtype=F32):
    s_, k_ = a.shape
    n_ = b.shape[1]
    tk = _pick(k_, (1024, 1408, 896, 512, 256, 128))
    tn = _pick(n_, (1408, 1024, 896, 512, 256, 128))
    ts = _pick(s_, (2048, 1024, 512, 256, 128))
    ns = s_ // ts

    def body(a_ref, b_ref, o_ref, acc_ref):
        part = lax.dot_general(a_ref[...].astype(BF16), b_ref[...].astype(BF16), CONTRACT_0,
                               preferred_element_type=F32)
        step = pl.program_id(2)

        @pl.when(step == 0)
        def _():
            acc_ref[...] = part

        @pl.when(step != 0)
        def _():
            acc_ref[...] += part

        @pl.when(step == ns - 1)
        def _():
            acc = acc_ref[...]
            o_ref[...] = (acc if scale is None else acc * scale).astype(o_ref.dtype)

    return pl.pallas_call(
        body, name=name, grid=(k_ // tk, n_ // tn, ns),
        in_specs=[pl.BlockSpec((ts, tk), lambda i, j, s: (s, i)),
                  pl.BlockSpec((ts, tn), lambda i, j, s: (s, j))],
        out_specs=pl.BlockSpec((tk, tn), lambda i, j, s: (i, j)),
        out_shape=jax.ShapeDtypeStruct((k_, n_), out_dtype),
        scratch_shapes=[pltpu.VMEM((tk, tn), F32)],
        compiler_params=_params("parallel", "parallel", "arbitrary"),
    )(a, b)


def ffn_in(h, w_in_t, name, carry=None):
    s_, d_ = h.shape
    f_ = w_in_t.shape[0] // 2
    tm = _pick(s_, (512, 256, 128))
    tn = _pick(f_, (1408, 1024, 512, 256, 128))
    nb = f_ // tn

    def body(h_ref, wa_ref, wb_ref, a_ref, b_ref, u_ref):
        hv = h_ref[...]
        a = lax.dot_general(hv, wa_ref[...], CONTRACT_1, preferred_element_type=F32)
        b = lax.dot_general(hv, wb_ref[...], CONTRACT_1, preferred_element_type=F32)
        a_ref[...] = a.astype(BF16)
        b_ref[...] = b.astype(BF16)
        u_ref[...] = (a * _sigmoid(a) * b).astype(BF16)

    o_spec = pl.BlockSpec((tm, tn), lambda j, i: (i, j))
    out = jax.ShapeDtypeStruct((s_, f_), BF16)
    return _carried_call(
        body, name, (nb, s_ // tm),
        [pl.BlockSpec((tm, d_), lambda j, i: (i, 0)),
         pl.BlockSpec((tn, d_), lambda j, i: (j, 0)),
         pl.BlockSpec((tn, d_), lambda j, i: (j + nb, 0))],
        [o_spec, o_spec, o_spec], [out, out, out], [], [h, w_in_t, w_in_t], carry)


def _carried_call(body, name, grid, in_specs, out_specs, out_shape, scratch, operands, carry):
    if not carry:
        return pl.pallas_call(
            body, name=name, grid=grid, in_specs=in_specs, out_specs=tuple(out_specs), out_shape=tuple(out_shape),
            scratch_shapes=scratch, compiler_params=_params(*["parallel"] * len(grid)))(*operands)
    counts = (len(in_specs), len(carry[1]), len(out_specs), len(carry[1]), len(scratch), 3)

    def carrying(*refs):
        ins, c_in, outs, c_out, scr, sems = _split_refs(refs, counts)
        step_no, last_no = _grid_step(grid)

        @pl.when(step_no == 0)
        def _():
            for cp in _direct_copies(carry[0], c_in, c_out, *sems):
                cp.start()

        body(*ins, *outs, *scr)

        @pl.when(step_no == last_no)
        def _():
            for cp in _direct_copies(carry[0], c_in, c_out, *sems):
                cp.wait()

    c_in_specs, c_out_specs, c_shapes, c_sems = _exchange_operands(*carry)
    return pl.pallas_call(
        carrying, name=name, grid=grid, in_specs=in_specs + c_in_specs, out_specs=tuple(out_specs + c_out_specs),
        out_shape=tuple(out_shape + c_shapes), scratch_shapes=scratch + c_sems,
        compiler_params=_params(*["arbitrary"] * len(grid)))(*operands, *carry[1])


def matmul_pair(a1, a2, b, name, carry=None):
    m_, k_ = a1.shape
    n_ = b.shape[1]
    tm = _pick(m_, (512, 256, 128))
    tn = _pick(n_, (1024, 512, 256, 128))

    def body(a1_ref, a2_ref, b1_ref, b2_ref, o_ref):
        o_ref[...] = (jnp.dot(a1_ref[...], b1_ref[...], preferred_element_type=F32)
                      + jnp.dot(a2_ref[...], b2_ref[...], preferred_element_type=F32))

    a_spec = pl.BlockSpec((tm, k_), lambda j, i: (i, 0))
    return _carried_call(
        body, name, (n_ // tn, m_ // tm),
        [a_spec, a_spec, pl.BlockSpec((k_, tn), lambda j, i: (0, j)), pl.BlockSpec((k_, tn), lambda j, i: (1, j))],
        [pl.BlockSpec((tm, tn), lambda j, i: (i, j))], [jax.ShapeDtypeStruct((m_, n_), F32)], [], [a1, a2, b, b], carry)


def ffn_bwd_act(dy, w_out, a, b, name, carry=None):
    s_, d_ = dy.shape
    f_ = w_out.shape[0]
    tm = _pick(s_, (512, 256, 128))
    tn = _pick(f_, (1408, 1024, 512, 256, 128))

    def body(dy_ref, w_ref, a_ref, b_ref, da_ref, db_ref):
        du = 0.5 * lax.dot_general(dy_ref[...].astype(BF16), w_ref[...], CONTRACT_1, preferred_element_type=F32)
        av = a_ref[...].astype(F32)
        bv = b_ref[...].astype(F32)
        sig = _sigmoid(av)
        da_ref[...] = (du * bv * (sig * (1.0 + av * (1.0 - sig)))).astype(BF16)
        db_ref[...] = (du * (av * sig)).astype(BF16)

    t_spec = pl.BlockSpec((tm, tn), lambda j, i: (i, j))
    out = jax.ShapeDtypeStruct((s_, f_), BF16)
    return _carried_call(
        body, name, (f_ // tn, s_ // tm),
        [pl.BlockSpec((tm, d_), lambda j, i: (i, 0)), pl.BlockSpec((tn, d_), lambda j, i: (j, 0)), t_spec, t_spec],
        [t_spec, t_spec], [out, out], [], [dy, w_out, a, b], carry)


def _fold(a, b, n, forward):
    if forward:
        low = b <= a
        return jnp.where(low, a, n - 1 - a), jnp.where(low, b, b - a - 1)
    low = b < n - a
    return jnp.where(low, a, n - 1 - a), jnp.where(low, a + b, b - 1)


def _attn_grid(h_, n_outer, n_inner, causal, forward):
    if causal and n_outer % 2 == 0:
        return (h_, n_outer // 2, n_outer + 1), lambda a, b: _fold(a, b, n_outer, forward)
    return (h_, n_outer, n_inner), lambda a, b: (a, b)


QK_SCALE = 1.0 / math.sqrt(HEAD_DIM)
SUM_LANE = HEAD_DIM


def _scores(q, k, ck, masked):
    s = lax.dot_general(q, k, CONTRACT_1, preferred_element_type=F32)
    if ck is not None:
        s = s - ck
    if masked:
        row = lax.broadcasted_iota(jnp.int32, s.shape, 0)
        col = lax.broadcasted_iota(jnp.int32, s.shape, 1)
        s = jnp.where(col <= row, s, -jnp.inf)
    return s


def _widen_v(v):
    ones = jnp.ones(v.shape[:2] + (1,), BF16)
    zeros = jnp.zeros(v.shape[:2] + (LANES - HEAD_DIM - 1,), BF16)
    return jnp.concatenate([v.astype(BF16), ones, zeros], axis=-1)


def _grid_step(grid):
    step = 0
    for axis, n in enumerate(grid):
        step = step * n + pl.program_id(axis)
    return step, math.prod(grid) - 1


def _split_refs(refs, counts):
    out, at = [], 0
    for n in counts:
        out.append(refs[at:at + n])
        at += n
    return out


def attn_fwd(q, k, v_wide, ck, causal, name, carry=None):
    h_, sq, d_ = q.shape
    sk = k.shape[1]
    tq = _pick(sq, (512, 256, 128))
    tk = tq if causal else _pick(sk, (512, 256, 128))
    nq, nk = sq // tq, sk // tk
    assert not causal or nq == 1 or nq % 2 == 0
    bias = ck is not None
    grid, blocks = _attn_grid(h_, nq, nk, causal, True)
    nc = len(carry[1]) if carry else 0

    def body(*refs):
        ins, c_in, outs, c_out, scratch, sems = _split_refs(refs, (4 if bias else 3, nc, 2, nc, 2, 3 if carry else 0))
        q_ref, k_ref, v_ref = ins[:3]
        ck_ref = ins[3] if bias else None
        (o_ref, lse_ref), (m_sc, acc_sc) = outs, scratch
        i, j = blocks(pl.program_id(1), pl.program_id(2))
        if carry:
            step_no, last_no = _grid_step(grid)

            @pl.when(step_no == 0)
            def _():
                for cp in _direct_copies(carry[0], c_in, c_out, *sems):
                    cp.start()

        @pl.when(j == 0)
        def _():
            m_sc[...] = jnp.full(m_sc.shape, -jnp.inf, F32)
            acc_sc[...] = jnp.zeros(acc_sc.shape, F32)

        def step(masked):
            s = _scores(q_ref[0], k_ref[0], ck_ref[0] if bias else None, masked)
            m_prev = m_sc[...]
            m_new = jnp.maximum(m_prev, jnp.max(s, axis=1, keepdims=True))
            p = jnp.exp(s - m_new)
            acc_sc[...] = jnp.exp(m_prev - m_new) * acc_sc[...] + jnp.dot(p.astype(BF16), v_ref[0],
                                                                         preferred_element_type=F32)
            m_sc[...] = m_new

        if causal:
            pl.when(j < i)(functools.partial(step, False))
            pl.when(j == i)(functools.partial(step, True))
        else:
            step(False)

        @pl.when(j == (i if causal else nk - 1))
        def _():
            acc = acc_sc[...]
            lane = lax.broadcasted_iota(jnp.int32, acc.shape, 1)
            l = jnp.sum(jnp.where(lane == SUM_LANE, acc, 0.0), axis=1, keepdims=True)
            o_ref[0] = acc_sc[:, :HEAD_DIM] / l
            lse_ref[0] = m_sc[...] + jnp.log(l)

        if carry:
            @pl.when(step_no == last_no)
            def _():
                for cp in _direct_copies(carry[0], c_in, c_out, *sems):
                    cp.wait()

    q_spec = pl.BlockSpec((1, tq, d_), lambda h, a, b: (h, blocks(a, b)[0], 0))
    q1_spec = pl.BlockSpec((1, tq, 1), lambda h, a, b: (h, blocks(a, b)[0], 0))
    k_spec = pl.BlockSpec((1, tk, d_), lambda h, a, b: (h, blocks(a, b)[1], 0))
    in_specs = [q_spec, k_spec, pl.BlockSpec((1, tk, LANES), lambda h, a, b: (h, blocks(a, b)[1], 0))]
    operands = [q, k, v_wide]
    if bias:
        in_specs.append(pl.BlockSpec((1, 1, tk), lambda h, a, b: (h, 0, blocks(a, b)[1])))
        operands.append(ck)
    out_specs = [q_spec, q1_spec]
    out_shape = [jax.ShapeDtypeStruct((h_, sq, d_), F32), jax.ShapeDtypeStruct((h_, sq, 1), F32)]
    scratch = [pltpu.VMEM((tq, 1), F32), pltpu.VMEM((tq, LANES), F32)]
    if carry:
        c_in_specs, c_out_specs, c_shapes, c_sems = _exchange_operands(*carry)
        in_specs, out_specs, out_shape, scratch = in_specs + c_in_specs, out_specs + c_out_specs, out_shape + c_shapes, scratch + c_sems
        operands = operands + list(carry[1])
    return pl.pallas_call(
        body, name=name, grid=grid,
        in_specs=in_specs, out_specs=tuple(out_specs), out_shape=tuple(out_shape), scratch_shapes=scratch,
        compiler_params=_params("arbitrary" if carry else "parallel", "arbitrary", "arbitrary"),
    )(*operands)


def attn_bwd(q, k, v, ck, o, lse, do, causal, name, carry=None):
    h_, sq, d_ = q.shape
    sk = k.shape[1]
    tq = _pick(sq, (512, 256, 128))
    tk = tq if causal else _pick(sk, (512, 256, 128))
    nq, nk = sq // tq, sk // tk
    assert not causal or nq == 1 or nq % 2 == 0
    bias = ck is not None
    grid, blocks = _attn_grid(h_, nk, nq, causal, False)
    nc = len(carry[1]) if carry else 0

    def body(*refs):
        ins, c_in, outs, c_out, scratch, sems = _split_refs(
            refs, (7 if bias else 6, nc, 5 if bias else 3, nc, 3 if bias else 2, 3 if carry else 0))
        if bias:
            q_ref, k_ref, v_ref, ck_ref, o_ref, lse_ref, do_ref = ins
            dq_ref, dk_ref, dv_ref, dc_ref, dcq_ref = outs
            dk_sc, dv_sc, dc_sc = scratch
        else:
            q_ref, k_ref, v_ref, o_ref, lse_ref, do_ref = ins
            dq_ref, dk_ref, dv_ref = outs
            dk_sc, dv_sc = scratch
        j, i = blocks(pl.program_id(1), pl.program_id(2))
        if carry:
            step_no, last_no = _grid_step(grid)

            @pl.when(step_no == 0)
            def _():
                for cp in _direct_copies(carry[0], c_in, c_out, *sems):
                    cp.start()

        @pl.when((pl.program_id(1) == 0) & (pl.program_id(2) == 0))
        def _():
            dq_ref[...] = jnp.zeros(dq_ref.shape, F32)
            if bias:
                dcq_ref[...] = jnp.zeros(dcq_ref.shape, F32)

        @pl.when(i == (j if causal else 0))
        def _():
            dk_sc[...] = jnp.zeros(dk_sc.shape, F32)
            dv_sc[...] = jnp.zeros(dv_sc.shape, F32)
            if bias:
                dc_sc[...] = jnp.zeros(dc_sc.shape, F32)

        def step(masked):
            qb, kb = q_ref[0], k_ref[0]
            dof = do_ref[0]
            dob = dof.astype(BF16)
            s = _scores(qb, kb, ck_ref[0] if bias else None, masked)
            p = jnp.exp(s - lse_ref[0])
            dp = lax.dot_general(dob, v_ref[0].astype(BF16), CONTRACT_1, preferred_element_type=F32)
            delta = jnp.sum(dof * o_ref[0], axis=1, keepdims=True)
            ds = p * (dp - delta)
            dsb = ds.astype(BF16)
            dv_sc[...] += lax.dot_general(p.astype(BF16), dob, CONTRACT_0, preferred_element_type=F32)
            dk_sc[...] += lax.dot_general(dsb, qb, CONTRACT_0, preferred_element_type=F32)
            rows = pl.ds(pl.multiple_of(i * tq, tq), tq)
            dq_ref[0, rows, :] += jnp.dot(dsb, kb, preferred_element_type=F32)
            if bias:
                dc_sc[...] -= jnp.sum(ds, axis=0, keepdims=True)
                dcq_ref[0, rows, :] += jnp.sum(ds, axis=1, keepdims=True)

        if causal:
            pl.when(i > j)(functools.partial(step, False))
            pl.when(i == j)(functools.partial(step, True))
        else:
            step(False)

        @pl.when(i == nq - 1)
        def _():
            dk_ref[0] = dk_sc[...]
            dv_ref[0] = dv_sc[...]
            if bias:
                dc_ref[0] = dc_sc[...]

        if carry:
            @pl.when(step_no == last_no)
            def _():
                for cp in _direct_copies(carry[0], c_in, c_out, *sems):
                    cp.wait()

    q_spec = pl.BlockSpec((1, tq, d_), lambda h, a, b: (h, blocks(a, b)[1], 0))
    q1_spec = pl.BlockSpec((1, tq, 1), lambda h, a, b: (h, blocks(a, b)[1], 0))
    k_spec = pl.BlockSpec((1, tk, d_), lambda h, a, b: (h, blocks(a, b)[0], 0))
    c_spec = pl.BlockSpec((1, 1, tk), lambda h, a, b: (h, 0, blocks(a, b)[0]))
    in_specs = [q_spec, k_spec, k_spec] + ([c_spec] if bias else []) + [q_spec, q1_spec, q_spec]
    operands = [q, k, v] + ([ck] if bias else []) + [o, lse, do]
    out_specs = [pl.BlockSpec((1, sq, d_), lambda h, a, b: (h, 0, 0)), k_spec, k_spec]
    out_shape = [jax.ShapeDtypeStruct((h_, sq, d_), F32), jax.ShapeDtypeStruct((h_, sk, d_), F32),
                 jax.ShapeDtypeStruct((h_, sk, d_), F32)]
    scratch = [pltpu.VMEM((tk, d_), F32), pltpu.VMEM((tk, d_), F32)]
    if bias:
        out_specs += [c_spec, pl.BlockSpec((1, sq, 1), lambda h, a, b: (h, 0, 0))]
        out_shape += [jax.ShapeDtypeStruct((h_, 1, sk), F32), jax.ShapeDtypeStruct((h_, sq, 1), F32)]
        scratch.append(pltpu.VMEM((1, tk), F32))
    if carry:
        c_in_specs, c_out_specs, c_shapes, c_sems = _exchange_operands(*carry)
        in_specs, out_specs, out_shape, scratch = in_specs + c_in_specs, out_specs + c_out_specs, out_shape + c_shapes, scratch + c_sems
        operands = operands + list(carry[1])
    return pl.pallas_call(
        body, name=name, grid=grid,
        in_specs=in_specs, out_specs=tuple(out_specs), out_shape=tuple(out_shape),
        scratch_shapes=scratch,
        compiler_params=_params("arbitrary" if carry else "parallel", "arbitrary", "arbitrary"),
    )(*operands)


def _tri(lower):
    row = lax.broadcasted_iota(jnp.int32, (CHUNK, CHUNK), 0)
    col = lax.broadcasted_iota(jnp.int32, (CHUNK, CHUNK), 1)
    return jnp.where((col <= row) if lower else (col >= row), 1.0, 0.0).astype(F32)


def fox_gate_fwd(f, b, name):
    s_ = f.shape[0]

    def body(f_ref, b_ref, c_ref, carry):
        @pl.when(pl.program_id(0) == 0)
        def _():
            carry[...] = jnp.zeros(carry.shape, F32)

        xv = f_ref[...] + b_ref[...]
        log_f = jnp.minimum(xv, 0.0) - jnp.log(1.0 + jnp.exp(-jnp.abs(xv)))
        c = jnp.dot(_tri(True), log_f, precision=lax.Precision.HIGHEST, preferred_element_type=F32) + carry[...]
        c_ref[...] = c
        carry[...] = c[CHUNK - 1:CHUNK, :]

    blk = pl.BlockSpec((CHUNK, LANES), lambda i: (i, 0))
    return pl.pallas_call(
        body, name=name, grid=(s_ // CHUNK,),
        in_specs=[blk, pl.BlockSpec((1, LANES), lambda i: (0, 0))], out_specs=blk,
        out_shape=jax.ShapeDtypeStruct((s_, LANES), F32),
        scratch_shapes=[pltpu.VMEM((1, LANES), F32)],
        compiler_params=_params("arbitrary"),
    )(f, b)


def fox_gate_bwd(dc, f, b, name):
    s_ = f.shape[0]
    n = s_ // CHUNK

    def body(dc_ref, f_ref, b_ref, df_ref, db_ref, carry):
        @pl.when(pl.program_id(0) == 0)
        def _():
            carry[...] = jnp.zeros(carry.shape, F32)
            db_ref[...] = jnp.zeros(db_ref.shape, F32)

        dlog = jnp.dot(_tri(False), dc_ref[...], precision=lax.Precision.HIGHEST,
                       preferred_element_type=F32) + carry[...]
        df = dlog * _sigmoid(-(f_ref[...] + b_ref[...]))
        df_ref[...] = df
        db_ref[...] += jnp.sum(df, axis=0, keepdims=True)
        carry[...] = dlog[0:1, :]

    blk = pl.BlockSpec((CHUNK, LANES), lambda i: (n - 1 - i, 0))
    vec = pl.BlockSpec((1, LANES), lambda i: (0, 0))
    return pl.pallas_call(
        body, name=name, grid=(n,),
        in_specs=[blk, blk, vec], out_specs=(blk, vec),
        out_shape=(jax.ShapeDtypeStruct((s_, LANES), F32), jax.ShapeDtypeStruct((1, LANES), F32)),
        scratch_shapes=[pltpu.VMEM((1, LANES), F32)],
        compiler_params=_params("arbitrary"),
    )(dc, f, b)


GELU_K = math.sqrt(2.0 / math.pi)
GELU_C = 0.044715


def _gelu(x):
    return 0.5 * x * (1.0 + jnp.tanh(GELU_K * (x + GELU_C * (x * x * x))))


def _gelu_grad(x):
    t = jnp.tanh(GELU_K * (x + GELU_C * (x * x * x)))
    return 0.5 * (1.0 + t) + 0.5 * x * (1.0 - t * t) * (GELU_K * (1.0 + 3.0 * GELU_C * (x * x)))


def _tril_mask():
    row = lax.broadcasted_iota(jnp.int32, (CHUNK, CHUNK), 0)
    col = lax.broadcasted_iota(jnp.int32, (CHUNK, CHUNK), 1)
    return col <= row


def _gmlp_specs(s_, ts):
    row = pl.BlockSpec((1, ts, HEAD_DIM), lambda g, i: (g, i, 0))
    gain = pl.BlockSpec((1, 1, HEAD_DIM), lambda g, i: (g, 0, 0))
    w = pl.BlockSpec((1, CHUNK, CHUNK), lambda g, i: (g, 0, 0))
    b = pl.BlockSpec((1, CHUNK, 1), lambda g, i: (g, 0, 0))
    return row, gain, w, b


def gmlp_fwd(pu, pv, gain, w, b, name):
    g_, s_, d_ = pu.shape
    ts = _pick(s_, (1024, 512, 256, 128))

    def body(pu_ref, pv_ref, g_ref, w_ref, b_ref, o_ref):
        v = _gelu(pv_ref[0])
        r = lax.rsqrt(jnp.mean(v * v, axis=-1, keepdims=True) + EPS)
        vn = (v * r * g_ref[0]).astype(BF16)
        wt = jnp.where(_tril_mask(), w_ref[0], 0.0).astype(BF16)
        for c in range(ts // CHUNK):
            rows = pl.ds(c * CHUNK, CHUNK)
            gate = jnp.dot(wt, vn[c * CHUNK:(c + 1) * CHUNK], preferred_element_type=F32) + b_ref[0]
            o_ref[0, rows, :] = _gelu(pu_ref[0, rows, :]) * gate

    row, gspec, wspec, bspec = _gmlp_specs(s_, ts)
    return pl.pallas_call(
        body, name=name, grid=(g_, s_ // ts),
        in_specs=[row, row, gspec, wspec, bspec], out_specs=row,
        out_shape=jax.ShapeDtypeStruct((g_, s_, d_), F32),
        compiler_params=_params("parallel", "parallel"),
    )(pu, pv, gain, w, b)


def gmlp_bwd(pu, pv, gain, w, b, dout, name):
    g_, s_, d_ = pu.shape
    ts = _pick(s_, (1024, 512, 256, 128))

    def body(pu_ref, pv_ref, g_ref, w_ref, b_ref, do_ref, dpu_ref, dpv_ref, dw_ref, db_ref, dg_ref):
        @pl.when(pl.program_id(1) == 0)
        def _():
            dw_ref[...] = jnp.zeros(dw_ref.shape, F32)
            db_ref[...] = jnp.zeros(db_ref.shape, F32)
            dg_ref[...] = jnp.zeros(dg_ref.shape, F32)

        gain_v = g_ref[0]
        mask = _tril_mask()
        wt = jnp.where(mask, w_ref[0], 0.0).astype(BF16)
        dw = jnp.zeros((CHUNK, CHUNK), F32)
        db = jnp.zeros((CHUNK, 1), F32)
        dg = jnp.zeros((1, d_), F32)
        for c in range(ts // CHUNK):
            rows = pl.ds(c * CHUNK, CHUNK)
            pu_c = pu_ref[0, rows, :]
            pv_c = pv_ref[0, rows, :]
            do_c = do_ref[0, rows, :]
            u = _gelu(pu_c)
            v = _gelu(pv_c)
            r = lax.rsqrt(jnp.mean(v * v, axis=-1, keepdims=True) + EPS)
            n = v * r
            vn = (n * gain_v).astype(BF16)
            gate = jnp.dot(wt, vn, preferred_element_type=F32) + b_ref[0]
            dgate = do_c * u
            dgate_b = dgate.astype(BF16)
            dpu_ref[0, rows, :] = do_c * gate * _gelu_grad(pu_c)
            db = db + jnp.sum(dgate, axis=1, keepdims=True)
            dw = dw + lax.dot_general(dgate_b, vn, CONTRACT_1, preferred_element_type=F32)
            dvn = lax.dot_general(wt, dgate_b, CONTRACT_0, preferred_element_type=F32)
            dg = dg + jnp.sum(dvn * n, axis=0, keepdims=True)
            dn = dvn * gain_v
            dv = r * (dn - n * jnp.mean(dn * n, axis=-1, keepdims=True))
            dpv_ref[0, rows, :] = dv * _gelu_grad(pv_c)
        dw_ref[0] += jnp.where(mask, dw, 0.0)
        db_ref[0] += db
        dg_ref[0] += dg

    row, gspec, wspec, bspec = _gmlp_specs(s_, ts)
    return pl.pallas_call(
        body, name=name, grid=(g_, s_ // ts),
        in_specs=[row, row, gspec, wspec, bspec, row],
        out_specs=(row, row, wspec, bspec, gspec),
        out_shape=(jax.ShapeDtypeStruct((g_, s_, d_), F32), jax.ShapeDtypeStruct((g_, s_, d_), F32),
                   jax.ShapeDtypeStruct((g_, CHUNK, CHUNK), F32), jax.ShapeDtypeStruct((g_, CHUNK, 1), F32),
                   jax.ShapeDtypeStruct((g_, 1, d_), F32)),
        compiler_params=_params("parallel", "arbitrary"),
    )(pu, pv, gain, w, b, dout)


def loss_head(y, t, name):
    s_, d_ = y.shape
    tr = _pick(s_, (1024, 512, 256, 128))

    def body(y_ref, t_ref, dy_ref, l_ref):
        err = y_ref[...] - t_ref[...]
        dy_ref[...] = err * (1.0 / d_)
        part = jnp.full(l_ref.shape, jnp.sum(err * err) * (0.5 / d_), F32)

        @pl.when(pl.program_id(0) == 0)
        def _():
            l_ref[...] = part

        @pl.when(pl.program_id(0) != 0)
        def _():
            l_ref[...] += part

    blk = pl.BlockSpec((tr, d_), lambda i: (i, 0))
    dy, l = pl.pallas_call(
        body, name=name, grid=(s_ // tr,),
        in_specs=[blk, blk], out_specs=(blk, pl.BlockSpec((8, LANES), lambda i: (0, 0))),
        out_shape=(jax.ShapeDtypeStruct((s_, d_), F32), jax.ShapeDtypeStruct((8, LANES), F32)),
        compiler_params=_params("arbitrary"),
    )(y, t)
    return dy, l[0, 0]


def adamw(w, g_slots, m, v, name):
    r_, c_ = w.shape
    tr = _pick(r_, (1024, 512, 352, 256, 224, 128, 64, 32, 16, 8))

    def body(w_ref, gs_ref, m_ref, v_ref, g_ref, d_ref, nm_ref, nv_ref):
        g = gs_ref[0].astype(F32)
        for k in range(1, N_DEV):
            g = g + gs_ref[k].astype(F32)
        m_new = ADAM_B1 * m_ref[...] + (1.0 - ADAM_B1) * g
        v_new = ADAM_B2 * v_ref[...] + (1.0 - ADAM_B2) * (g * g)
        m_hat = m_new / (1.0 - ADAM_B1 ** ADAM_STEP)
        v_hat = v_new / (1.0 - ADAM_B2 ** ADAM_STEP)
        g_ref[...] = g
        d_ref[...] = -ADAM_LR * (m_hat / (jnp.sqrt(v_hat) + ADAM_EPS) + ADAM_WD * w_ref[...])
        nm_ref[...] = m_new
        nv_ref[...] = v_new

    blk = pl.BlockSpec((tr, c_), lambda i: (i, 0))
    out = jax.ShapeDtypeStruct((r_, c_), F32)
    return pl.pallas_call(
        body, name=name, grid=(r_ // tr,),
        in_specs=[blk, pl.BlockSpec((N_DEV, tr, c_), lambda i: (0, i, 0)), blk, blk],
        out_specs=(blk, blk, blk, blk), out_shape=(out, out, out, out),
        compiler_params=_params("parallel"),
    )(w, g_slots, m, v)


def _position():
    x, y, c = lax.axis_index("x"), lax.axis_index("y"), lax.axis_index("c")
    return x, y, c


def _slot(px, py, pc):
    return 4 * px + 2 * py + pc


def all_gather_multi(blocks, name):
    n = len(blocks)

    def body(*refs):
        x_refs, out_refs = refs[:n], refs[n:2 * n]
        send_sems, recv_sems, local_sems = refs[2 * n:]
        x, y, c = _position()
        me, sibling = (x, y, c), (x, y, 1 - c)
        chips = [(1 - x, y), (x, 1 - y), (1 - x, 1 - y)]

        def copy(b, k, owner, to, src=None):
            dst = out_refs[b].at[_slot(*owner)]
            return pltpu.make_async_remote_copy(
                src_ref=dst if src is None else src, dst_ref=dst,
                send_sem=send_sems.at[b, k], recv_sem=recv_sems.at[b, k], device_id=to, device_id_type=MESH)

        mine = [pltpu.make_async_copy(x_refs[b], out_refs[b].at[_slot(*me)], local_sems.at[b]) for b in range(n)]
        for cp in mine:
            cp.start()
        first = [copy(b, 1 + j, me, (*chip, c), src=x_refs[b]) for j, chip in enumerate(chips) for b in range(n)]
        first += [copy(b, 0, me, sibling, src=x_refs[b]) for b in range(n)]
        for cp in first:
            cp.start()
        passed = []
        for j, chip in enumerate(chips):
            for b in range(n):
                copy(b, 1 + j, (*chip, c), me).wait_recv()
                cp = copy(b, 4 + j, (*chip, c), sibling)
                cp.start()
                passed.append(cp)
        for b in range(n):
            copy(b, 0, sibling, me).wait_recv()
        for j, chip in enumerate(chips):
            for b in range(n):
                copy(b, 4 + j, (*chip, 1 - c), me).wait_recv()
        for cp in first + passed:
            cp.wait_send()
        for cp in mine:
            cp.wait()

    any_spec = pl.BlockSpec(memory_space=pl.ANY)
    return pl.pallas_call(
        body, name=name,
        in_specs=[any_spec] * n, out_specs=tuple([any_spec] * n),
        out_shape=tuple(jax.ShapeDtypeStruct((N_DEV,) + blk.shape, blk.dtype) for blk in blocks),
        scratch_shapes=[pltpu.SemaphoreType.DMA((n, 7)), pltpu.SemaphoreType.DMA((n, 7)), pltpu.SemaphoreType.DMA((n,))],
    )(*blocks)


def _direct_copies(scatter, in_refs, out_refs, send_sems, recv_sems, local_sems):
    x, y, c = _position()
    me = _slot(x, y, c)
    src = (lambda ref, slot: ref.at[slot]) if scatter else (lambda ref, slot: ref)
    copies = [pltpu.make_async_copy(src(ref, me), out.at[me], local_sems.at[b])
              for b, (ref, out) in enumerate(zip(in_refs, out_refs))]
    for k in range(1, N_DEV):
        px = 1 - x if k & 4 else x
        py = 1 - y if k & 2 else y
        pc = 1 - c if k & 1 else c
        copies += [pltpu.make_async_remote_copy(
            src_ref=src(ref, _slot(px, py, pc)), dst_ref=out.at[me],
            send_sem=send_sems.at[b, k - 1], recv_sem=recv_sems.at[b, k - 1],
            device_id=(px, py, pc), device_id_type=MESH) for b, (ref, out) in enumerate(zip(in_refs, out_refs))]
    return copies


def _exchange_operands(scatter, bufs):
    n = len(bufs)
    any_spec = pl.BlockSpec(memory_space=pl.ANY)
    shapes = [jax.ShapeDtypeStruct(b.shape if scatter else (N_DEV,) + b.shape, b.dtype) for b in bufs]
    sems = [pltpu.SemaphoreType.DMA((n, 7)), pltpu.SemaphoreType.DMA((n, 7)), pltpu.SemaphoreType.DMA((n,))]
    return [any_spec] * n, [any_spec] * n, shapes, sems


def _seg_len(n):
    return -(-n // SEG_ALIGN) * SEG_ALIGN


def _pack(arrays, lead=()):
    parts, total = [], 0
    for a in arrays:
        flat = a.reshape(lead + (-1,))
        pad = _seg_len(flat.shape[-1]) - flat.shape[-1]
        parts.append(jnp.pad(flat, [(0, 0)] * len(lead) + [(0, pad)]) if pad else flat)
        total += flat.shape[-1] + pad
    tail = -(-total // PACK_ALIGN) * PACK_ALIGN - total
    if tail:
        parts.append(jnp.zeros(lead + (tail,), parts[0].dtype))
    return jnp.concatenate(parts, axis=-1).reshape(lead + (-1, LANES))


def _unpack(packed, shapes, lead=()):
    flat = packed.reshape(lead + (-1,))
    out, off = [], 0
    for shp in shapes:
        n = math.prod(shp)
        out.append(flat[..., off:off + n].reshape(lead + tuple(shp)))
        off += _seg_len(n)
    return out


def _heads(a, n_heads):
    return a.reshape(a.shape[0], n_heads, HEAD_DIM).transpose(1, 0, 2)


def _unheads(a):
    return a.transpose(1, 0, 2).reshape(a.shape[1], a.shape[0] * HEAD_DIM)


def _head_gain(g, n_heads):
    return jnp.broadcast_to(g.reshape(1, 1, HEAD_DIM), (n_heads, 1, HEAD_DIM))


def _ffn_forward(x, gain, w_in_t, w_out, tag, carry=None):
    h = rmsnorm_fwd(x[None], gain.reshape(1, 1, -1), BF16, f"{tag}_norm")[0]
    a, b, u, *carried = ffn_in(h, w_in_t, f"{tag}_in", carry)
    if carry:
        w_out = w_out(carried)
    y = matmul(u, w_out, f"{tag}_out", residual=x, scale=0.5)
    return y, (x, h, a, b, u), carried


def _ffn_backward(dy, saved, gain, w_in_t, w_out, tag, carry_of=None):
    x, h, a, b, u = saved
    dw_out = matmul_tn(u, dy, f"{tag}_dwout", scale=0.5, out_dtype=BF16)
    da, db, *carried_1 = ffn_bwd_act(dy, w_out, a, b, f"{tag}_dact", carry_of(dw_out, None) if carry_of else None)
    dw_in_t = jnp.concatenate([matmul_tn(da, h, f"{tag}_dwin_a", out_dtype=BF16),
                               matmul_tn(db, h, f"{tag}_dwin_b", out_dtype=BF16)], axis=0)
    dh, *carried_2 = matmul_pair(da, db, w_in_t, f"{tag}_dh", carry_of(None, dw_in_t) if carry_of else None)
    dx, dgain = rmsnorm_bwd(x[None], gain.reshape(1, 1, -1), dh[None], f"{tag}_dnorm", residual=dy[None])
    return dx[0], dgain.reshape(-1), dw_in_t, dw_out, carried_1 + carried_2


def _mem_forward(mq, mem_n, w_kv, g_q, g_k, tag):
    gq = _head_gain(g_q, MEM_HEADS)
    gk = _head_gain(g_k, MEM_HEADS)
    qn = rmsnorm_fwd(mq, gq * QK_SCALE, BF16, f"{tag}_qnorm")
    kv = matmul(mem_n, w_kv, f"{tag}_kv")
    k = _heads(kv[:, :MEM_WIDTH], MEM_HEADS)
    v = _heads(kv[:, MEM_WIDTH:], MEM_HEADS)
    kn = rmsnorm_fwd(k, gk, BF16, f"{tag}_knorm")
    o, lse = attn_fwd(qn, kn, _widen_v(v), None, False, f"{tag}_attn")
    return o, (mq, gq, gk, qn, k, kn, v, o, lse)


def _mem_backward(do, saved, mem_n, w_kv, tag):
    mq, gq, gk, qn, k, kn, v, o, lse = saved
    dqn, dkn, dv = attn_bwd(qn, kn, v, None, o, lse, do, False, f"{tag}_dattn")
    dmq, dgq = rmsnorm_bwd(mq, gq, dqn * QK_SCALE, f"{tag}_dqnorm")
    dk, dgk = rmsnorm_bwd(k, gk, dkn, f"{tag}_dknorm")
    dkv = jnp.concatenate([_unheads(dk), _unheads(dv)], axis=1)
    dw_kv = matmul_tn(mem_n, dkv, f"{tag}_dwkv", out_dtype=BF16)
    dmem_n = matmul(dkv, w_kv, f"{tag}_dmem", transpose_b=True)
    return dmq, dgq.sum(axis=0).reshape(-1), dgk.sum(axis=0).reshape(-1), dw_kv, dmem_n


def _fox_split(w_in):
    t3 = 3 * TOK_WIDTH
    pad = jnp.zeros(w_in.shape[:-1] + (LANES - FOX_HEADS,), w_in.dtype)
    return jnp.concatenate([w_in[..., :t3], w_in[..., t3 + FOX_HEADS:], w_in[..., t3:t3 + FOX_HEADS], pad], axis=-1)


def _fox_unsplit(w):
    t3 = 3 * TOK_WIDTH
    return jnp.concatenate([w[..., :t3], w[..., t3 + MEM_WIDTH:t3 + MEM_WIDTH + FOX_HEADS], w[..., t3:t3 + MEM_WIDTH]],
                           axis=-1)


def _fox_forward(h, w_split, b_f, g_q, g_k, tag, carry=None):
    t3 = 3 * TOK_WIDTH
    proj = matmul(h, w_split, f"{tag}_proj")
    qkv = _heads(proj[:, :t3], 3 * FOX_HEADS)
    mq = _heads(proj[:, t3:t3 + MEM_WIDTH], MEM_HEADS)
    f_pad = proj[:, t3 + MEM_WIDTH:]
    b_pad = jnp.pad(b_f.reshape(1, -1), ((0, 0), (0, LANES - FOX_HEADS)))
    gains = jnp.concatenate([_head_gain(g_q, FOX_HEADS), _head_gain(g_k, FOX_HEADS)], axis=0)
    qk = qkv[:2 * FOX_HEADS]
    scaled = jnp.concatenate([gains[:FOX_HEADS] * QK_SCALE, gains[FOX_HEADS:]], axis=0)
    qkn = rmsnorm_fwd(qk, scaled, BF16, f"{tag}_qknorm")
    v = qkv[2 * FOX_HEADS:]
    c = fox_gate_fwd(f_pad, b_pad, f"{tag}_gate")
    ck = c[:, :FOX_HEADS].T[:, None, :]
    qn, kn = qkn[:FOX_HEADS], qkn[FOX_HEADS:]
    o, lse, *carried = attn_fwd(qn, kn, _widen_v(v), ck, True, f"{tag}_attn", carry)
    return o, mq, (qk, gains, qn, kn, v, ck, o, lse, f_pad, b_pad), carried


def _fox_backward(do, dmq, saved, tag, carry=None):
    qk, gains, qn, kn, v, ck, o, lse, f_pad, b_pad = saved
    dqn, dkn, dv, dck, dcq, *carried = attn_bwd(qn, kn, v, ck, o, lse, do, True, f"{tag}_dattn", carry)
    dqk, dgains = rmsnorm_bwd(qk, gains, jnp.concatenate([dqn * QK_SCALE, dkn], axis=0), f"{tag}_dqknorm")
    dc = jnp.pad((dck[:, 0, :] + dcq[:, :, 0]).T, ((0, 0), (0, LANES - FOX_HEADS)))
    df, db = fox_gate_bwd(dc, f_pad, b_pad, f"{tag}_dgate")
    dproj = jnp.concatenate([_unheads(dqk), _unheads(dv), _unheads(dmq), df], axis=1)
    dgains = dgains.reshape(2, FOX_HEADS, HEAD_DIM).sum(axis=1)
    return dproj, db[0, :FOX_HEADS], dgains[0], dgains[1], carried


def _gmlp_forward(h, w_in_t, v_gain, w_s, b_s, tag):
    proj = matmul(h, w_in_t, f"{tag}_proj", transpose_b=True)
    pu = _heads(proj[:, :TOK_WIDTH], FOX_HEADS)
    pv = _heads(proj[:, TOK_WIDTH:2 * TOK_WIDTH], FOX_HEADS)
    mq = _heads(proj[:, 2 * TOK_WIDTH:], MEM_HEADS)
    gain = v_gain.reshape(FOX_HEADS, 1, HEAD_DIM)
    b = b_s[:, :, None]
    o = gmlp_fwd(pu, pv, gain, w_s, b, f"{tag}_sgu")
    return o, mq, (pu, pv, gain, w_s, b)


def _gmlp_backward(do, dmq, saved, tag):
    pu, pv, gain, w_s, b = saved
    dpu, dpv, dw, db, dg = gmlp_bwd(pu, pv, gain, w_s, b, do, f"{tag}_dsgu")
    dproj = jnp.concatenate([_unheads(dpu), _unheads(dpv), _unheads(dmq)], axis=1)
    return dproj, dg.reshape(-1), dw, db[:, :, 0]


BIG = ("ffn1_w_in", "ffn1_w_out", "ffn2_w_in", "ffn2_w_out", "w_out", "mem_w_kv", "fox_w_in", "gmlp_w_in")
COLUMN_SHARDED = ("ffn1_w_in", "ffn2_w_in", "gmlp_w_in")
REPLICATED =("norm_ffn1", "norm_mix", "norm_ffn2", "mem_norm", "mem_q_norm", "mem_k_norm", "fox_b_f",
              "fox_q_norm", "fox_k_norm", "gmlp_w_s", "gmlp_b_s")
WEIGHTS = ("norm_ffn1", "ffn1_w_in", "ffn1_w_out", "norm_mix", "norm_ffn2", "ffn2_w_in", "ffn2_w_out", "w_out",
           "mem_norm", "mem_w_kv", "mem_q_norm", "mem_k_norm", "fox_w_in", "fox_b_f", "fox_q_norm", "fox_k_norm",
           "gmlp_w_in", "gmlp_v_norm", "gmlp_w_s", "gmlp_b_s")


def _to_transport(name, a):
    if name in COLUMN_SHARDED:
        return jnp.swapaxes(a, -1, -2)
    return _fox_split(a) if name == "fox_w_in" else a


def _from_transport(name, a):
    if name in COLUMN_SHARDED:
        return jnp.swapaxes(a, -1, -2)
    return _fox_unsplit(a) if name == "fox_w_in" else a


def kernel(x, mem, norm_ffn1, ffn1_w_in, ffn1_w_out, norm_mix, norm_ffn2, ffn2_w_in, ffn2_w_out, w_out, mem_norm, mem_w_kv, mem_q_norm, mem_k_norm, fox_w_in, fox_b_f, fox_q_norm, fox_k_norm, gmlp_w_in, gmlp_v_norm, gmlp_w_s, gmlp_b_s, loss_target, m_norm_ffn1, m_ffn1_w_in, m_ffn1_w_out, m_norm_mix, m_norm_ffn2, m_ffn2_w_in, m_ffn2_w_out, m_w_out, m_mem_norm, m_mem_w_kv, m_mem_q_norm, m_mem_k_norm, m_fox_w_in, m_fox_b_f, m_fox_q_norm, m_fox_k_norm, m_gmlp_w_in, m_gmlp_v_norm, m_gmlp_w_s, m_gmlp_b_s, v_norm_ffn1, v_ffn1_w_in, v_ffn1_w_out, v_norm_mix, v_norm_ffn2, v_ffn2_w_in, v_ffn2_w_out, v_w_out, v_mem_norm, v_mem_w_kv, v_mem_q_norm, v_mem_k_norm, v_fox_w_in, v_fox_b_f, v_fox_q_norm, v_fox_k_norm, v_gmlp_w_in, v_gmlp_v_norm, v_gmlp_w_s, v_gmlp_b_s):
    w = dict(norm_ffn1=norm_ffn1, ffn1_w_in=ffn1_w_in, ffn1_w_out=ffn1_w_out, norm_mix=norm_mix, norm_ffn2=norm_ffn2, ffn2_w_in=ffn2_w_in, ffn2_w_out=ffn2_w_out, w_out=w_out, mem_norm=mem_norm, mem_w_kv=mem_w_kv, mem_q_norm=mem_q_norm, mem_k_norm=mem_k_norm, fox_w_in=fox_w_in, fox_b_f=fox_b_f, fox_q_norm=fox_q_norm, fox_k_norm=fox_k_norm, gmlp_w_in=gmlp_w_in, gmlp_v_norm=gmlp_v_norm, gmlp_w_s=gmlp_w_s, gmlp_b_s=gmlp_b_s)
    m = dict(norm_ffn1=m_norm_ffn1, ffn1_w_in=m_ffn1_w_in, ffn1_w_out=m_ffn1_w_out, norm_mix=m_norm_mix, norm_ffn2=m_norm_ffn2, ffn2_w_in=m_ffn2_w_in, ffn2_w_out=m_ffn2_w_out, w_out=m_w_out, mem_norm=m_mem_norm, mem_w_kv=m_mem_w_kv, mem_q_norm=m_mem_q_norm, mem_k_norm=m_mem_k_norm, fox_w_in=m_fox_w_in, fox_b_f=m_fox_b_f, fox_q_norm=m_fox_q_norm, fox_k_norm=m_fox_k_norm, gmlp_w_in=m_gmlp_w_in, gmlp_v_norm=m_gmlp_v_norm, gmlp_w_s=m_gmlp_w_s, gmlp_b_s=m_gmlp_b_s)
    v = dict(norm_ffn1=v_norm_ffn1, ffn1_w_in=v_ffn1_w_in, ffn1_w_out=v_ffn1_w_out, norm_mix=v_norm_mix, norm_ffn2=v_norm_ffn2, ffn2_w_in=v_ffn2_w_in, ffn2_w_out=v_ffn2_w_out, w_out=v_w_out, mem_norm=v_mem_norm, mem_w_kv=v_mem_w_kv, mem_q_norm=v_mem_q_norm, mem_k_norm=v_mem_k_norm, fox_w_in=v_fox_w_in, fox_b_f=v_fox_b_f, fox_q_norm=v_fox_q_norm, fox_k_norm=v_fox_k_norm, gmlp_w_in=v_gmlp_w_in, gmlp_v_norm=v_gmlp_v_norm, gmlp_w_s=v_gmlp_w_s, gmlp_b_s=v_gmlp_b_s)

    depth = norm_ffn1.shape[0]
    x0 = x[0]
    mem0 = mem[0]
    target = loss_target[0]
    me = _slot(*_position())

    keys = [(n, i) for n in BIG for i in range(w[n].shape[0])]
    local = {k: _to_transport(k[0], w[k[0]][k[1]]) for k in keys}
    n_gain, gain_len = gmlp_v_norm.shape
    pad_gain = lambda a: jnp.pad(a, ((0, 8 - n_gain), (0, LANES - gain_len)))
    first = [("ffn1_w_in", 0)]
    early = [("ffn1_w_out", 0), ("fox_w_in", 0)]
    inner = [k for k in keys if k not in first + early]
    stack = lambda g: g.reshape(-1, g.shape[-1])
    full = {k: stack(g) for k, g in zip(first, all_gather_multi([local[k].astype(BF16) for k in first], "gather_weights"))}
    gather_early = (False, [local[k].astype(BF16) for k in early])
    gather_inner = (False, [local[k].astype(BF16) for k in inner] + [pad_gain(gmlp_v_norm)])

    mem_n = rmsnorm_fwd(mem0[None], mem_norm.reshape(1, 1, -1), BF16, "mem_norm")[0]
    saved = []
    xi = x0
    for i in range(depth):
        kind, j = i % 2, i // 2
        if i == 0:
            x1, ffn1_saved, arrived = _ffn_forward(xi, norm_ffn1[i], full["ffn1_w_in", i], lambda got: stack(got[0]),
                                                   f"l{i}_ffn1", gather_early)
            full.update({k: stack(g) for k, g in zip(early, arrived)})
        else:
            x1, ffn1_saved, _ = _ffn_forward(xi, norm_ffn1[i], full["ffn1_w_in", i], full["ffn1_w_out", i], f"l{i}_ffn1")
        h = rmsnorm_fwd(x1[None], norm_mix[i].reshape(1, 1, -1), BF16, f"l{i}_mixnorm")[0]
        if kind == 0:
            tok, mq, mix_saved, arrived = _fox_forward(h, full["fox_w_in", j], fox_b_f[j], fox_q_norm[j], fox_k_norm[j],
                                                       f"l{i}_fox", gather_inner if i == 0 else None)
            if i == 0:
                full.update({k: stack(g) for k, g in zip(inner, arrived)})
                v_gain_full = arrived[-1][:, :n_gain, :gain_len]
        else:
            tok, mq, mix_saved = _gmlp_forward(h, full["gmlp_w_in", j], v_gain_full[:, j, :].reshape(-1), gmlp_w_s[j],
                                               gmlp_b_s[j], f"l{i}_gmlp")
        mo, mem_saved = _mem_forward(mq, mem_n, full["mem_w_kv", i], mem_q_norm[i], mem_k_norm[i], f"l{i}_mem")
        cat = _unheads(jnp.concatenate([tok, mo], axis=0)).astype(BF16)
        x2 = matmul(cat, full["w_out", i], f"l{i}_wout", residual=x1)
        x3, ffn2_saved, _ = _ffn_forward(x2, norm_ffn2[i], full["ffn2_w_in", i], full["ffn2_w_out", i], f"l{i}_ffn2")
        saved.append((ffn1_saved, x1, h, mix_saved, mem_saved, cat, ffn2_saved))
        xi = x3

    dy, loss_part = loss_head(xi, target, "loss_head")
    loss = lax.psum(loss_part, ("x", "y", "c"))

    small = {n: [None] * w[n].shape[0] for n in REPLICATED + ("gmlp_v_norm",) if n != "mem_norm"}
    big = {}
    slots_of = lambda g: g.reshape((N_DEV, -1, g.shape[-1]))
    dmem_n = None
    for i in reversed(range(depth)):
        kind, j = i % 2, i // 2
        ffn1_saved, x1, h, mix_saved, mem_saved, cat, ffn2_saved = saved[i]
        dy, small["norm_ffn2"][i], big["ffn2_w_in", i], big["ffn2_w_out", i], _ = _ffn_backward(
            dy, ffn2_saved, norm_ffn2[i], full["ffn2_w_in", i], full["ffn2_w_out", i], f"l{i}_ffn2")
        big["w_out", i] = matmul_tn(cat, dy, f"l{i}_dwout", out_dtype=BF16)
        dcat = _heads(matmul(dy, full["w_out", i], f"l{i}_dcat", transpose_b=True), FOX_HEADS + MEM_HEADS)
        dmq, small["mem_q_norm"][i], small["mem_k_norm"][i], big["mem_w_kv", i], dmem_i = _mem_backward(
            dcat[FOX_HEADS:], mem_saved, mem_n, full["mem_w_kv", i], f"l{i}_mem")
        dmem_n = dmem_i if dmem_n is None else dmem_n + dmem_i
        if kind == 0:
            scatter_inner = (True, [slots_of(big[k]) for k in inner]) if i == 0 else None
            dproj, small["fox_b_f"][j], small["fox_q_norm"][j], small["fox_k_norm"][j], arrived = _fox_backward(
                dcat[:FOX_HEADS], dmq, mix_saved, f"l{i}_fox", scatter_inner)
            if i == 0:
                got = dict(zip(inner, arrived))
            big["fox_w_in", j] = matmul_tn(h, dproj, f"l{i}_fox_dwin", out_dtype=BF16)
            dh = matmul(dproj, full["fox_w_in", j], f"l{i}_fox_dh", transpose_b=True)
        else:
            dproj, small["gmlp_v_norm"][j], small["gmlp_w_s"][j], small["gmlp_b_s"][j] = _gmlp_backward(
                dcat[:FOX_HEADS], dmq, mix_saved, f"l{i}_gmlp")
            big["gmlp_w_in", j] = matmul_tn(dproj, h, f"l{i}_gmlp_dwin", out_dtype=BF16)
            dh = matmul(dproj, full["gmlp_w_in", j], f"l{i}_gmlp_dh")
        dy, dg_mix = rmsnorm_bwd(x1[None], norm_mix[i].reshape(1, 1, -1), dh[None], f"l{i}_dmixnorm", residual=dy[None])
        dy, small["norm_mix"][i] = dy[0], dg_mix.reshape(-1)
        def scatter_last(dw_out, dw_in_t):
            return (True, [slots_of(dw_in_t)] if dw_out is None else [slots_of(dw_out), slots_of(big["fox_w_in", 0])])

        dy, small["norm_ffn1"][i], big["ffn1_w_in", i], big["ffn1_w_out", i], arrived = _ffn_backward(
            dy, ffn1_saved, norm_ffn1[i], full["ffn1_w_in", i], full["ffn1_w_out", i], f"l{i}_ffn1",
            scatter_last if i == 0 else None)
        if i == 0:
            got.update(zip(early + first, arrived))
    grad_x = dy[None]
    _, dg_mem = rmsnorm_bwd(mem0[None], mem_norm.reshape(1, 1, -1), dmem_n[None], "dmem_norm")
    small = {n: jnp.stack(g) for n, g in small.items()}
    small["mem_norm"] = dg_mem.reshape(-1)

    results ={n: [[None] * w[n].shape[0] for _ in range(4)] for n in BIG}
    for k in keys:
        n, i = k
        outs = adamw(local[k], got[k], _to_transport(n, m[n][i]), _to_transport(n, v[n][i]), f"adamw_{n}_{i}")
        for q in range(4):
            results[n][q][i] = _from_transport(n, outs[q])
    sharded = {n: [jnp.stack(r) for r in results[n]] for n in BIG}

    small_names = REPLICATED + ("gmlp_v_norm",)
    (small_got,) = all_gather_multi([_pack([small[n] for n in small_names])], "gather_small_grads")
    rep_shapes = [w[n].shape for n in REPLICATED]
    gain_seg = jnp.zeros((n_gain, N_DEV * gain_len), F32)
    pack_rep = lambda d: _pack([d[n] for n in REPLICATED] + [gain_seg])
    outs = adamw(pack_rep(w), small_got, pack_rep(m), pack_rep(v), "adamw_replicated")
    replicated = [dict(zip(REPLICATED, _unpack(o, rep_shapes))) for o in outs]
    gain_parts = _unpack(small_got, rep_shapes + [(n_gain, N_DEV * gain_len)], lead=(N_DEV,))[-1]
    gain_slots = lax.dynamic_slice_in_dim(gain_parts, me * gain_len, gain_len, axis=2)
    gain_slots = jnp.pad(gain_slots, ((0, 0), (0, 8 - n_gain), (0, LANES - gain_len)))
    outs = adamw(pad_gain(gmlp_v_norm), gain_slots, pad_gain(m["gmlp_v_norm"]), pad_gain(v["gmlp_v_norm"]),
                 "adamw_gmlp_v_norm")
    sharded["gmlp_v_norm"] = [o[:n_gain, :gain_len] for o in outs]

    out = [loss, grad_x]
    for q in range(4):
        out += [(replicated[q][n] if n in REPLICATED else sharded[n][q]) for n in WEIGHTS]
    return tuple(out)
```

```python
import functools
import math

import jax
import jax.numpy as jnp
from jax import lax
from jax.experimental import pallas as pl
from jax.experimental.pallas import tpu as pltpu

F32 = jnp.float32
BF16 = jnp.bfloat16

EPS = 1e-6
HEAD_DIM = 64
FOX_HEADS = 12
MEM_HEADS = 4
TOK_WIDTH = FOX_HEADS * HEAD_DIM
MEM_WIDTH = MEM_HEADS * HEAD_DIM
CHUNK = 128
LANES = 128
N_DEV = 8
SEG_ALIGN = 16 * LANES
PACK_ROWS = 1024
PACK_ALIGN = PACK_ROWS * LANES

ADAM_LR = 0.001
ADAM_B1 = 0.9
ADAM_B2 = 0.999
ADAM_EPS = 1e-08
ADAM_WD = 0.01
ADAM_STEP = 10

VMEM_LIMIT_BYTES = 48 * 1024 * 1024
MESH = pl.DeviceIdType.MESH
CONTRACT_0 = (((0,), (0,)), ((), ()))
CONTRACT_1 = (((1,), (1,)), ((), ()))


def _params(*semantics):
    return pltpu.CompilerParams(dimension_semantics=semantics, vmem_limit_bytes=VMEM_LIMIT_BYTES)


def _pick(n, candidates):
    for c in candidates:
        if c <= n and n % c == 0:
            return c
    return n


def _sigmoid(x):
    return 0.5 * jnp.tanh(0.5 * x) + 0.5


def _row_tile(r, w):
    return _pick(r, (1024,) if w >= 512 else (2048, 1024, 512, 256))


def rmsnorm_fwd(x, gain, out_dtype, name):
    g_, r_, w_ = x.shape
    tr = _row_tile(r_, w_)

    def body(x_ref, g_ref, y_ref):
        xv = x_ref[0].astype(F32)
        r = lax.rsqrt(jnp.mean(xv * xv, axis=-1, keepdims=True) + EPS)
        y_ref[0] = (xv * r * g_ref[0]).astype(y_ref.dtype)

    return pl.pallas_call(
        body, name=name, grid=(g_, r_ // tr),
        in_specs=[pl.BlockSpec((1, tr, w_), lambda g, i: (g, i, 0)),
                  pl.BlockSpec((1, 1, w_), lambda g, i: (g, 0, 0))],
        out_specs=pl.BlockSpec((1, tr, w_), lambda g, i: (g, i, 0)),
        out_shape=jax.ShapeDtypeStruct((g_, r_, w_), out_dtype),
        compiler_params=_params("parallel", "parallel"),
    )(x, gain)


def rmsnorm_bwd(x, gain, dy, name, residual=None):
    g_, r_, w_ = x.shape
    tr = _row_tile(r_, w_)
    has_res = residual is not None

    def body(*refs):
        if has_res:
            x_ref, g_ref, dy_ref, res_ref, dx_ref, dg_ref = refs
        else:
            x_ref, g_ref, dy_ref, dx_ref, dg_ref = refs
        xv = x_ref[0].astype(F32)
        dyv = dy_ref[0].astype(F32)
        r = lax.rsqrt(jnp.mean(xv * xv, axis=-1, keepdims=True) + EPS)
        n = xv * r
        dn = dyv * g_ref[0]
        dx = r * (dn - n * jnp.mean(dn * n, axis=-1, keepdims=True))
        if has_res:
            dx = dx + res_ref[0]
        dx_ref[0] = dx
        part = jnp.sum(dyv * n, axis=0, keepdims=True)

        @pl.when(pl.program_id(1) == 0)
        def _():
            dg_ref[0] = part

        @pl.when(pl.program_id(1) != 0)
        def _():
            dg_ref[0] += part

    row = pl.BlockSpec((1, tr, w_), lambda g, i: (g, i, 0))
    vec = pl.BlockSpec((1, 1, w_), lambda g, i: (g, 0, 0))
    operands = (x, gain, dy) + ((residual,) if has_res else ())
    return pl.pallas_call(
        body, name=name, grid=(g_, r_ // tr),
        in_specs=[row, vec, row] + ([row] if has_res else []),
        out_specs=(row, vec),
        out_shape=(jax.ShapeDtypeStruct((g_, r_, w_), F32), jax.ShapeDtypeStruct((g_, 1, w_), F32)),
        compiler_params=_params("parallel", "arbitrary"),
    )(*operands)


def matmul(a, b, name, out_dtype=F32, residual=None, scale=None, transpose_b=False):
    m_, k_ = a.shape
    n_ = b.shape[0] if transpose_b else b.shape[1]
    tm = _pick(m_, (512, 256, 128))
    tn = _pick(n_, (1408, 1024, 896, 512, 256, 128))
    has_res = residual is not None

    def body(*refs):
        if has_res:
            a_ref, b_ref, res_ref, o_ref = refs
        else:
            a_ref, b_ref, o_ref = refs
        av, bv = a_ref[...].astype(BF16), b_ref[...].astype(BF16)
        if transpose_b:
            acc = lax.dot_general(av, bv, CONTRACT_1, preferred_element_type=F32)
        else:
            acc = jnp.dot(av, bv, preferred_element_type=F32)
        if scale is not None:
            acc = acc * scale
        if has_res:
            acc = acc + res_ref[...]
        o_ref[...] = acc.astype(o_ref.dtype)

    out_spec = pl.BlockSpec((tm, tn), lambda j, i: (i, j))
    b_spec = pl.BlockSpec((tn, k_), lambda j, i: (j, 0)) if transpose_b else pl.BlockSpec((k_, tn), lambda j, i: (0, j))
    operands = (a, b) + ((residual,) if has_res else ())
    return pl.pallas_call(
        body, name=name, grid=(n_ // tn, m_ // tm),
        in_specs=[pl.BlockSpec((tm, k_), lambda j, i: (i, 0)), b_spec] + ([out_spec] if has_res else []),
        out_specs=out_spec,
        out_shape=jax.ShapeDtypeStruct((m_, n_), out_dtype),
        compiler_params=_params("parallel", "parallel"),
    )(*operands)


def matmul_tn(a, b, name, scale=None, out_dtype=F32):
    s_, k_ = a.shape
    n_ = b.shape[1]
    tk = _pick(k_, (1024, 1408, 896, 512, 256, 128))
    tn = _pick(n_, (1408, 1024, 896, 512, 256, 128))
    ts = _pick(s_, (2048, 1024, 512, 256, 128))
    ns = s_ // ts

    def body(a_ref, b_ref, o_ref, acc_ref):
        part = lax.dot_general(a_ref[...].astype(BF16), b_ref[...].astype(BF16), CONTRACT_0,
                               preferred_element_type=F32)
        step = pl.program_id(2)

        @pl.when(step == 0)
        def _():
            acc_ref[...] = part

        @pl.when(step != 0)
        def _():
            acc_ref[...] += part

        @pl.when(step == ns - 1)
        def _():
            acc = acc_ref[...]
            o_ref[...] = (acc if scale is None else acc * scale).astype(o_ref.dtype)

    return pl.pallas_call(
        body, name=name, grid=(k_ // tk, n_ // tn, ns),
        in_specs=[pl.BlockSpec((ts, tk), lambda i, j, s: (s, i)),
                  pl.BlockSpec((ts, tn), lambda i, j, s: (s, j))],
        out_specs=pl.BlockSpec((tk, tn), lambda i, j, s: (i, j)),
        out_shape=jax.ShapeDtypeStruct((k_, n_), out_dtype),
        scratch_shapes=[pltpu.VMEM((tk, tn), F32)],
        compiler_params=_params("parallel", "parallel", "arbitrary"),
    )(a, b)


def ffn_in(h, w_in_t, name, carry=None):
    s_, d_ = h.shape
    f_ = w_in_t.shape[0] // 2
    tm = _pick(s_, (512, 256, 128))
    tn = _pick(f_, (1408, 1024, 512, 256, 128))
    nb = f_ // tn

    def body(h_ref, wa_ref, wb_ref, a_ref, b_ref, u_ref):
        hv = h_ref[...]
        a = lax.dot_general(hv, wa_ref[...], CONTRACT_1, preferred_element_type=F32)
        b = lax.dot_general(hv, wb_ref[...], CONTRACT_1, preferred_element_type=F32)
        a_ref[...] = a.astype(BF16)
        b_ref[...] = b.astype(BF16)
        u_ref[...] = (a * _sigmoid(a) * b).astype(BF16)

    o_spec = pl.BlockSpec((tm, tn), lambda j, i: (i, j))
    out = jax.ShapeDtypeStruct((s_, f_), BF16)
    return _carried_call(
        body, name, (nb, s_ // tm),
        [pl.BlockSpec((tm, d_), lambda j, i: (i, 0)),
         pl.BlockSpec((tn, d_), lambda j, i: (j, 0)),
         pl.BlockSpec((tn, d_), lambda j, i: (j + nb, 0))],
        [o_spec, o_spec, o_spec], [out, out, out], [], [h, w_in_t, w_in_t], carry)


def _carried_call(body, name, grid, in_specs, out_specs, out_shape, scratch, operands, carry):
    if not carry:
        return pl.pallas_call(
            body, name=name, grid=grid, in_specs=in_specs, out_specs=tuple(out_specs), out_shape=tuple(out_shape),
            scratch_shapes=scratch, compiler_params=_params(*["parallel"] * len(grid)))(*operands)
    counts = (len(in_specs), len(carry[1]), len(out_specs), len(carry[1]), len(scratch), 3)

    def carrying(*refs):
        ins, c_in, outs, c_out, scr, sems = _split_refs(refs, counts)
        step_no, last_no = _grid_step(grid)

        @pl.when(step_no == 0)
        def _():
            for cp in _direct_copies(carry[0], c_in, c_out, *sems):
                cp.start()

        body(*ins, *outs, *scr)

        @pl.when(step_no == last_no)
        def _():
            for cp in _direct_copies(carry[0], c_in, c_out, *sems):
                cp.wait()

    c_in_specs, c_out_specs, c_shapes, c_sems = _exchange_operands(*carry)
    return pl.pallas_call(
        carrying, name=name, grid=grid, in_specs=in_specs + c_in_specs, out_specs=tuple(out_specs + c_out_specs),
        out_shape=tuple(out_shape + c_shapes), scratch_shapes=scratch + c_sems,
        compiler_params=_params(*["arbitrary"] * len(grid)))(*operands, *carry[1])


def matmul_pair(a1, a2, b, name, carry=None):
    m_, k_ = a1.shape
    n_ = b.shape[1]
    tm = _pick(m_, (512, 256, 128))
    tn = _pick(n_, (1024, 512, 256, 128))

    def body(a1_ref, a2_ref, b1_ref, b2_ref, o_ref):
        o_ref[...] = (jnp.dot(a1_ref[...], b1_ref[...], preferred_element_type=F32)
                      + jnp.dot(a2_ref[...], b2_ref[...], preferred_element_type=F32))

    a_spec = pl.BlockSpec((tm, k_), lambda j, i: (i, 0))
    return _carried_call(
        body, name, (n_ // tn, m_ // tm),
        [a_spec, a_spec, pl.BlockSpec((k_, tn), lambda j, i: (0, j)), pl.BlockSpec((k_, tn), lambda j, i: (1, j))],
        [pl.BlockSpec((tm, tn), lambda j, i: (i, j))], [jax.ShapeDtypeStruct((m_, n_), F32)], [], [a1, a2, b, b], carry)


def ffn_bwd_act(dy, w_out, a, b, name, carry=None):
    s_, d_ = dy.shape
    f_ = w_out.shape[0]
    tm = _pick(s_, (512, 256, 128))
    tn = _pick(f_, (1408, 1024, 512, 256, 128))

    def body(dy_ref, w_ref, a_ref, b_ref, da_ref, db_ref):
        du = 0.5 * lax.dot_general(dy_ref[...].astype(BF16), w_ref[...], CONTRACT_1, preferred_element_type=F32)
        av = a_ref[...].astype(F32)
        bv = b_ref[...].astype(F32)
        sig = _sigmoid(av)
        da_ref[...] = (du * bv * (sig * (1.0 + av * (1.0 - sig)))).astype(BF16)
        db_ref[...] = (du * (av * sig)).astype(BF16)

    t_spec = pl.BlockSpec((tm, tn), lambda j, i: (i, j))
    out = jax.ShapeDtypeStruct((s_, f_), BF16)
    return _carried_call(
        body, name, (f_ // tn, s_ // tm),
        [pl.BlockSpec((tm, d_), lambda j, i: (i, 0)), pl.BlockSpec((tn, d_), lambda j, i: (j, 0)), t_spec, t_spec],
        [t_spec, t_spec], [out, out], [], [dy, w_out, a, b], carry)


def _fold(a, b, n, forward):
    if forward:
        low = b <= a
        return jnp.where(low, a, n - 1 - a), jnp.where(low, b, b - a - 1)
    low = b < n - a
    return jnp.where(low, a, n - 1 - a), jnp.where(low, a + b, b - 1)


def _attn_grid(h_, n_outer, n_inner, causal, forward):
    if causal and n_outer % 2 == 0:
        return (h_, n_outer // 2, n_outer + 1), lambda a, b: _fold(a, b, n_outer, forward)
    return (h_, n_outer, n_inner), lambda a, b: (a, b)


QK_SCALE = 1.0 / math.sqrt(HEAD_DIM)
SUM_LANE = HEAD_DIM


def _scores(q, k, ck, masked):
    s = lax.dot_general(q, k, CONTRACT_1, preferred_element_type=F32)
    if ck is not None:
        s = s - ck
    if masked:
        row = lax.broadcasted_iota(jnp.int32, s.shape, 0)
        col = lax.broadcasted_iota(jnp.int32, s.shape, 1)
        s = jnp.where(col <= row, s, -jnp.inf)
    return s


def _widen_v(v):
    ones = jnp.ones(v.shape[:2] + (1,), BF16)
    zeros = jnp.zeros(v.shape[:2] + (LANES - HEAD_DIM - 1,), BF16)
    return jnp.concatenate([v.astype(BF16), ones, zeros], axis=-1)


def _grid_step(grid):
    step = 0
    for axis, n in enumerate(grid):
        step = step * n + pl.program_id(axis)
    return step, math.prod(grid) - 1


def _split_refs(refs, counts):
    out, at = [], 0
    for n in counts:
        out.append(refs[at:at + n])
        at += n
    return out


def attn_fwd(q, k, v_wide, ck, causal, name, carry=None):
    h_, sq, d_ = q.shape
    sk = k.shape[1]
    tq = _pick(sq, (1024, 512, 256, 128) if causal else (512, 256, 128))
    tk = tq if causal else _pick(sk, (512, 256, 128))
    nq, nk = sq // tq, sk // tk
    assert not causal or nq == 1 or nq % 2 == 0
    bias = ck is not None
    grid, blocks = _attn_grid(h_, nq, nk, causal, True)
    nc = len(carry[1]) if carry else 0

    def body(*refs):
        ins, c_in, outs, c_out, scratch, sems = _split_refs(refs, (4 if bias else 3, nc, 2, nc, 2, 3 if carry else 0))
        q_ref, k_ref, v_ref = ins[:3]
        ck_ref = ins[3] if bias else None
        (o_ref, lse_ref), (m_sc, acc_sc) = outs, scratch
        i, j = blocks(pl.program_id(1), pl.program_id(2))
        if carry:
            step_no, last_no = _grid_step(grid)

            @pl.when(step_no == 0)
            def _():
                for cp in _direct_copies(carry[0], c_in, c_out, *sems):
                    cp.start()

        @pl.when(j == 0)
        def _():
            m_sc[...] = jnp.full(m_sc.shape, -jnp.inf, F32)
            acc_sc[...] = jnp.zeros(acc_sc.shape, F32)

        def step(masked):
            s = _scores(q_ref[0], k_ref[0], ck_ref[0] if bias else None, masked)
            m_prev = m_sc[...]
            m_new = jnp.maximum(m_prev, jnp.max(s, axis=1, keepdims=True))
            p = jnp.exp(s - m_new)
            acc_sc[...] = jnp.exp(m_prev - m_new) * acc_sc[...] + jnp.dot(p.astype(BF16), v_ref[0],
                                                                         preferred_element_type=F32)
            m_sc[...] = m_new

        if causal:
            pl.when(j < i)(functools.partial(step, False))
            pl.when(j == i)(functools.partial(step, True))
        else:
            step(False)

        @pl.when(j == (i if causal else nk - 1))
        def _():
            acc = acc_sc[...]
            lane = lax.broadcasted_iota(jnp.int32, acc.shape, 1)
            l = jnp.sum(jnp.where(lane == SUM_LANE, acc, 0.0), axis=1, keepdims=True)
            o_ref[0] = acc_sc[:, :HEAD_DIM] / l
            lse_ref[0] = m_sc[...] + jnp.log(l)

        if carry:
            @pl.when(step_no == last_no)
            def _():
                for cp in _direct_copies(carry[0], c_in, c_out, *sems):
                    cp.wait()

    q_spec = pl.BlockSpec((1, tq, d_), lambda h, a, b: (h, blocks(a, b)[0], 0))
    q1_spec = pl.BlockSpec((1, tq, 1), lambda h, a, b: (h, blocks(a, b)[0], 0))
    k_spec = pl.BlockSpec((1, tk, d_), lambda h, a, b: (h, blocks(a, b)[1], 0))
    in_specs = [q_spec, k_spec, pl.BlockSpec((1, tk, LANES), lambda h, a, b: (h, blocks(a, b)[1], 0))]
    operands = [q, k, v_wide]
    if bias:
        in_specs.append(pl.BlockSpec((1, 1, tk), lambda h, a, b: (h, 0, blocks(a, b)[1])))
        operands.append(ck)
    out_specs = [q_spec, q1_spec]
    out_shape = [jax.ShapeDtypeStruct((h_, sq, d_), F32), jax.ShapeDtypeStruct((h_, sq, 1), F32)]
    scratch = [pltpu.VMEM((tq, 1), F32), pltpu.VMEM((tq, LANES), F32)]
    if carry:
        c_in_specs, c_out_specs, c_shapes, c_sems = _exchange_operands(*carry)
        in_specs, out_specs, out_shape, scratch = in_specs + c_in_specs, out_specs + c_out_specs, out_shape + c_shapes, scratch + c_sems
        operands = operands + list(carry[1])
    return pl.pallas_call(
        body, name=name, grid=grid,
        in_specs=in_specs, out_specs=tuple(out_specs), out_shape=tuple(out_shape), scratch_shapes=scratch,
        compiler_params=_params("arbitrary" if carry else "parallel", "arbitrary", "arbitrary"),
    )(*operands)


def attn_bwd(q, k, v, ck, o, lse, do, causal, name, carry=None):
    h_, sq, d_ = q.shape
    sk = k.shape[1]
    tq = _pick(sq, (1024, 512, 256, 128) if causal else (512, 256, 128))
    tk = tq if causal else _pick(sk, (512, 256, 128))
    nq, nk = sq // tq, sk // tk
    assert not causal or nq == 1 or nq % 2 == 0
    bias = ck is not None
    grid, blocks = _attn_grid(h_, nk, nq, causal, False)
    nc = len(carry[1]) if carry else 0

    def body(*refs):
        ins, c_in, outs, c_out, scratch, sems = _split_refs(
            refs, (7 if bias else 6, nc, 5 if bias else 3, nc, 3 if bias else 2, 3 if carry else 0))
        if bias:
            q_ref, k_ref, v_ref, ck_ref, o_ref, lse_ref, do_ref = ins
            dq_ref, dk_ref, dv_ref, dc_ref, dcq_ref = outs
            dk_sc, dv_sc, dc_sc = scratch
        else:
            q_ref, k_ref, v_ref, o_ref, lse_ref, do_ref = ins
            dq_ref, dk_ref, dv_ref = outs
            dk_sc, dv_sc = scratch
        j, i = blocks(pl.program_id(1), pl.program_id(2))
        if carry:
            step_no, last_no = _grid_step(grid)

            @pl.when(step_no == 0)
            def _():
                for cp in _direct_copies(carry[0], c_in, c_out, *sems):
                    cp.start()

        @pl.when((pl.program_id(1) == 0) & (pl.program_id(2) == 0))
        def _():
            dq_ref[...] = jnp.zeros(dq_ref.shape, F32)
            if bias:
                dcq_ref[...] = jnp.zeros(dcq_ref.shape, F32)

        @pl.when(i == (j if causal else 0))
        def _():
            dk_sc[...] = jnp.zeros(dk_sc.shape, F32)
            dv_sc[...] = jnp.zeros(dv_sc.shape, F32)
            if bias:
                dc_sc[...] = jnp.zeros(dc_sc.shape, F32)

        def step(masked):
            qb, kb = q_ref[0], k_ref[0]
            dof = do_ref[0]
            dob = dof.astype(BF16)
            s = _scores(qb, kb, ck_ref[0] if bias else None, masked)
            p = jnp.exp(s - lse_ref[0])
            dp = lax.dot_general(dob, v_ref[0].astype(BF16), CONTRACT_1, preferred_element_type=F32)
            delta = jnp.sum(dof * o_ref[0], axis=1, keepdims=True)
            ds = p * (dp - delta)
            dsb = ds.astype(BF16)
            dv_sc[...] += lax.dot_general(p.astype(BF16), dob, CONTRACT_0, preferred_element_type=F32)
            dk_sc[...] += lax.dot_general(dsb, qb, CONTRACT_0, preferred_element_type=F32)
            rows = pl.ds(pl.multiple_of(i * tq, tq), tq)
            dq_ref[0, rows, :] += jnp.dot(dsb, kb, preferred_element_type=F32)
            if bias:
                dc_sc[...] -= jnp.sum(ds, axis=0, keepdims=True)
                dcq_ref[0, rows, :] += jnp.sum(ds, axis=1, keepdims=True)

        if causal:
            pl.when(i > j)(functools.partial(step, False))
            pl.when(i == j)(functools.partial(step, True))
        else:
            step(False)

        @pl.when(i == nq - 1)
        def _():
            dk_ref[0] = dk_sc[...]
            dv_ref[0] = dv_sc[...]
            if bias:
                dc_ref[0] = dc_sc[...]

        if carry:
            @pl.when(step_no == last_no)
            def _():
                for cp in _direct_copies(carry[0], c_in, c_out, *sems):
                    cp.wait()

    q_spec = pl.BlockSpec((1, tq, d_), lambda h, a, b: (h, blocks(a, b)[1], 0))
    q1_spec = pl.BlockSpec((1, tq, 1), lambda h, a, b: (h, blocks(a, b)[1], 0))
    k_spec = pl.BlockSpec((1, tk, d_), lambda h, a, b: (h, blocks(a, b)[0], 0))
    c_spec = pl.BlockSpec((1, 1, tk), lambda h, a, b: (h, 0, blocks(a, b)[0]))
    in_specs = [q_spec, k_spec, k_spec] + ([c_spec] if bias else []) + [q_spec, q1_spec, q_spec]
    operands = [q, k, v] + ([ck] if bias else []) + [o, lse, do]
    out_specs = [pl.BlockSpec((1, sq, d_), lambda h, a, b: (h, 0, 0)), k_spec, k_spec]
    out_shape = [jax.ShapeDtypeStruct((h_, sq, d_), F32), jax.ShapeDtypeStruct((h_, sk, d_), F32),
                 jax.ShapeDtypeStruct((h_, sk, d_), F32)]
    scratch = [pltpu.VMEM((tk, d_), F32), pltpu.VMEM((tk, d_), F32)]
    if bias:
        out_specs += [c_spec, pl.BlockSpec((1, sq, 1), lambda h, a, b: (h, 0, 0))]
        out_shape += [jax.ShapeDtypeStruct((h_, 1, sk), F32), jax.ShapeDtypeStruct((h_, sq, 1), F32)]
        scratch.append(pltpu.VMEM((1, tk), F32))
    if carry:
        c_in_specs, c_out_specs, c_shapes, c_sems = _exchange_operands(*carry)
        in_specs, out_specs, out_shape, scratch = in_specs + c_in_specs, out_specs + c_out_specs, out_shape + c_shapes, scratch + c_sems
        operands = operands + list(carry[1])
    return pl.pallas_call(
        body, name=name, grid=grid,
        in_specs=in_specs, out_specs=tuple(out_specs), out_shape=tuple(out_shape),
        scratch_shapes=scratch,
        compiler_params=_params("arbitrary" if carry else "parallel", "arbitrary", "arbitrary"),
    )(*operands)


def _tri(lower):
    row = lax.broadcasted_iota(jnp.int32, (CHUNK, CHUNK), 0)
    col = lax.broadcasted_iota(jnp.int32, (CHUNK, CHUNK), 1)
    return jnp.where((col <= row) if lower else (col >= row), 1.0, 0.0).astype(F32)


def fox_gate_fwd(f, b, name):
    s_ = f.shape[0]

    def body(f_ref, b_ref, c_ref, carry):
        @pl.when(pl.program_id(0) == 0)
        def _():
            carry[...] = jnp.zeros(carry.shape, F32)

        xv = f_ref[...] + b_ref[...]
        log_f = jnp.minimum(xv, 0.0) - jnp.log(1.0 + jnp.exp(-jnp.abs(xv)))
        c = jnp.dot(_tri(True), log_f, precision=lax.Precision.HIGHEST, preferred_element_type=F32) + carry[...]
        c_ref[...] = c
        carry[...] = c[CHUNK - 1:CHUNK, :]

    blk = pl.BlockSpec((CHUNK, LANES), lambda i: (i, 0))
    return pl.pallas_call(
        body, name=name, grid=(s_ // CHUNK,),
        in_specs=[blk, pl.BlockSpec((1, LANES), lambda i: (0, 0))], out_specs=blk,
        out_shape=jax.ShapeDtypeStruct((s_, LANES), F32),
        scratch_shapes=[pltpu.VMEM((1, LANES), F32)],
        compiler_params=_params("arbitrary"),
    )(f, b)


def fox_gate_bwd(dc, f, b, name):
    s_ = f.shape[0]
    n = s_ // CHUNK

    def body(dc_ref, f_ref, b_ref, df_ref, db_ref, carry):
        @pl.when(pl.program_id(0) == 0)
        def _():
            carry[...] = jnp.zeros(carry.shape, F32)
            db_ref[...] = jnp.zeros(db_ref.shape, F32)

        dlog = jnp.dot(_tri(False), dc_ref[...], precision=lax.Precision.HIGHEST,
                       preferred_element_type=F32) + carry[...]
        df = dlog * _sigmoid(-(f_ref[...] + b_ref[...]))
        df_ref[...] = df
        db_ref[...] += jnp.sum(df, axis=0, keepdims=True)
        carry[...] = dlog[0:1, :]

    blk = pl.BlockSpec((CHUNK, LANES), lambda i: (n - 1 - i, 0))
    vec = pl.BlockSpec((1, LANES), lambda i: (0, 0))
    return pl.pallas_call(
        body, name=name, grid=(n,),
        in_specs=[blk, blk, vec], out_specs=(blk, vec),
        out_shape=(jax.ShapeDtypeStruct((s_, LANES), F32), jax.ShapeDtypeStruct((1, LANES), F32)),
        scratch_shapes=[pltpu.VMEM((1, LANES), F32)],
        compiler_params=_params("arbitrary"),
    )(dc, f, b)


GELU_K = math.sqrt(2.0 / math.pi)
GELU_C = 0.044715


def _gelu(x):
    return 0.5 * x * (1.0 + jnp.tanh(GELU_K * (x + GELU_C * (x * x * x))))


def _gelu_grad(x):
    t = jnp.tanh(GELU_K * (x + GELU_C * (x * x * x)))
    return 0.5 * (1.0 + t) + 0.5 * x * (1.0 - t * t) * (GELU_K * (1.0 + 3.0 * GELU_C * (x * x)))


def _tril_mask():
    row = lax.broadcasted_iota(jnp.int32, (CHUNK, CHUNK), 0)
    col = lax.broadcasted_iota(jnp.int32, (CHUNK, CHUNK), 1)
    return col <= row


def _gmlp_specs(s_, ts):
    row = pl.BlockSpec((1, ts, HEAD_DIM), lambda g, i: (g, i, 0))
    gain = pl.BlockSpec((1, 1, HEAD_DIM), lambda g, i: (g, 0, 0))
    w = pl.BlockSpec((1, CHUNK, CHUNK), lambda g, i: (g, 0, 0))
    b = pl.BlockSpec((1, CHUNK, 1), lambda g, i: (g, 0, 0))
    return row, gain, w, b


def gmlp_fwd(pu, pv, gain, w, b, name):
    g_, s_, d_ = pu.shape
    ts = _pick(s_, (1024, 512, 256, 128))

    def body(pu_ref, pv_ref, g_ref, w_ref, b_ref, o_ref):
        v = _gelu(pv_ref[0])
        r = lax.rsqrt(jnp.mean(v * v, axis=-1, keepdims=True) + EPS)
        vn = (v * r * g_ref[0]).astype(BF16)
        wt = jnp.where(_tril_mask(), w_ref[0], 0.0).astype(BF16)
        for c in range(ts // CHUNK):
            rows = pl.ds(c * CHUNK, CHUNK)
            gate = jnp.dot(wt, vn[c * CHUNK:(c + 1) * CHUNK], preferred_element_type=F32) + b_ref[0]
            o_ref[0, rows, :] = _gelu(pu_ref[0, rows, :]) * gate

    row, gspec, wspec, bspec = _gmlp_specs(s_, ts)
    return pl.pallas_call(
        body, name=name, grid=(g_, s_ // ts),
        in_specs=[row, row, gspec, wspec, bspec], out_specs=row,
        out_shape=jax.ShapeDtypeStruct((g_, s_, d_), F32),
        compiler_params=_params("parallel", "parallel"),
    )(pu, pv, gain, w, b)


def gmlp_bwd(pu, pv, gain, w, b, dout, name):
    g_, s_, d_ = pu.shape
    ts = _pick(s_, (1024, 512, 256, 128))

    def body(pu_ref, pv_ref, g_ref, w_ref, b_ref, do_ref, dpu_ref, dpv_ref, dw_ref, db_ref, dg_ref):
        @pl.when(pl.program_id(1) == 0)
        def _():
            dw_ref[...] = jnp.zeros(dw_ref.shape, F32)
            db_ref[...] = jnp.zeros(db_ref.shape, F32)
            dg_ref[...] = jnp.zeros(dg_ref.shape, F32)

        gain_v = g_ref[0]
        mask = _tril_mask()
        wt = jnp.where(mask, w_ref[0], 0.0).astype(BF16)
        dw = jnp.zeros((CHUNK, CHUNK), F32)
        db = jnp.zeros((CHUNK, 1), F32)
        dg = jnp.zeros((1, d_), F32)
        for c in range(ts // CHUNK):
            rows = pl.ds(c * CHUNK, CHUNK)
            pu_c = pu_ref[0, rows, :]
            pv_c = pv_ref[0, rows, :]
            do_c = do_ref[0, rows, :]
            u = _gelu(pu_c)
            v = _gelu(pv_c)
            r = lax.rsqrt(jnp.mean(v * v, axis=-1, keepdims=True) + EPS)
            n = v * r
            vn = (n * gain_v).astype(BF16)
            gate = jnp.dot(wt, vn, preferred_element_type=F32) + b_ref[0]
            dgate = do_c * u
            dgate_b = dgate.astype(BF16)
            dpu_ref[0, rows, :] = do_c * gate * _gelu_grad(pu_c)
            db = db + jnp.sum(dgate, axis=1, keepdims=True)
            dw = dw + lax.dot_general(dgate_b, vn, CONTRACT_1, preferred_element_type=F32)
            dvn = lax.dot_general(wt, dgate_b, CONTRACT_0, preferred_element_type=F32)
            dg = dg + jnp.sum(dvn * n, axis=0, keepdims=True)
            dn = dvn * gain_v
            dv = r * (dn - n * jnp.mean(dn * n, axis=-1, keepdims=True))
            dpv_ref[0, rows, :] = dv * _gelu_grad(pv_c)
        dw_ref[0] += jnp.where(mask, dw, 0.0)
        db_ref[0] += db
        dg_ref[0] += dg

    row, gspec, wspec, bspec = _gmlp_specs(s_, ts)
    return pl.pallas_call(
        body, name=name, grid=(g_, s_ // ts),
        in_specs=[row, row, gspec, wspec, bspec, row],
        out_specs=(row, row, wspec, bspec, gspec),
        out_shape=(jax.ShapeDtypeStruct((g_, s_, d_), F32), jax.ShapeDtypeStruct((g_, s_, d_), F32),
                   jax.ShapeDtypeStruct((g_, CHUNK, CHUNK), F32), jax.ShapeDtypeStruct((g_, CHUNK, 1), F32),
                   jax.ShapeDtypeStruct((g_, 1, d_), F32)),
        compiler_params=_params("parallel", "arbitrary"),
    )(pu, pv, gain, w, b, dout)


def loss_head(y, t, name):
    s_, d_ = y.shape
    tr = _pick(s_, (1024, 512, 256, 128))

    def body(y_ref, t_ref, dy_ref, l_ref):
        err = y_ref[...] - t_ref[...]
        dy_ref[...] = err * (1.0 / d_)
        part = jnp.full(l_ref.shape, jnp.sum(err * err) * (0.5 / d_), F32)

        @pl.when(pl.program_id(0) == 0)
        def _():
            l_ref[...] = part

        @pl.when(pl.program_id(0) != 0)
        def _():
            l_ref[...] += part

    blk = pl.BlockSpec((tr, d_), lambda i: (i, 0))
    dy, l = pl.pallas_call(
        body, name=name, grid=(s_ // tr,),
        in_specs=[blk, blk], out_specs=(blk, pl.BlockSpec((8, LANES), lambda i: (0, 0))),
        out_shape=(jax.ShapeDtypeStruct((s_, d_), F32), jax.ShapeDtypeStruct((8, LANES), F32)),
        compiler_params=_params("arbitrary"),
    )(y, t)
    return dy, l[0, 0]


def adamw(w, g_slots, m, v, name):
    r_, c_ = w.shape
    tr = _pick(r_, (1024, 512, 352, 256, 224, 128, 64, 32, 16, 8))

    def body(w_ref, gs_ref, m_ref, v_ref, g_ref, d_ref, nm_ref, nv_ref):
        g = gs_ref[0].astype(F32)
        for k in range(1, N_DEV):
            g = g + gs_ref[k].astype(F32)
        m_new = ADAM_B1 * m_ref[...] + (1.0 - ADAM_B1) * g
        v_new = ADAM_B2 * v_ref[...] + (1.0 - ADAM_B2) * (g * g)
        m_hat = m_new / (1.0 - ADAM_B1 ** ADAM_STEP)
        v_hat = v_new / (1.0 - ADAM_B2 ** ADAM_STEP)
        g_ref[...] = g
        d_ref[...] = -ADAM_LR * (m_hat / (jnp.sqrt(v_hat) + ADAM_EPS) + ADAM_WD * w_ref[...])
        nm_ref[...] = m_new
        nv_ref[...] = v_new

    blk = pl.BlockSpec((tr, c_), lambda i: (i, 0))
    out = jax.ShapeDtypeStruct((r_, c_), F32)
    return pl.pallas_call(
        body, name=name, grid=(r_ // tr,),
        in_specs=[blk, pl.BlockSpec((N_DEV, tr, c_), lambda i: (0, i, 0)), blk, blk],
        out_specs=(blk, blk, blk, blk), out_shape=(out, out, out, out),
        compiler_params=_params("parallel"),
    )(w, g_slots, m, v)


def _position():
    x, y, c = lax.axis_index("x"), lax.axis_index("y"), lax.axis_index("c")
    return x, y, c


def _slot(px, py, pc):
    return 4 * px + 2 * py + pc


def all_gather_multi(blocks, name):
    n = len(blocks)

    def body(*refs):
        x_refs, out_refs = refs[:n], refs[n:2 * n]
        send_sems, recv_sems, local_sems = refs[2 * n:]
        x, y, c = _position()
        me, sibling = (x, y, c), (x, y, 1 - c)
        chips = [(1 - x, y), (x, 1 - y), (1 - x, 1 - y)]

        def copy(b, k, owner, to, src=None):
            dst = out_refs[b].at[_slot(*owner)]
            return pltpu.make_async_remote_copy(
                src_ref=dst if src is None else src, dst_ref=dst,
                send_sem=send_sems.at[b, k], recv_sem=recv_sems.at[b, k], device_id=to, device_id_type=MESH)

        mine = [pltpu.make_async_copy(x_refs[b], out_refs[b].at[_slot(*me)], local_sems.at[b]) for b in range(n)]
        for cp in mine:
            cp.start()
        first = [copy(b, 1 + j, me, (*chip, c), src=x_refs[b]) for j, chip in enumerate(chips) for b in range(n)]
        first += [copy(b, 0, me, sibling, src=x_refs[b]) for b in range(n)]
        for cp in first:
            cp.start()
        passed = []
        for j, chip in enumerate(chips):
            for b in range(n):
                copy(b, 1 + j, (*chip, c), me).wait_recv()
                cp = copy(b, 4 + j, (*chip, c), sibling)
                cp.start()
                passed.append(cp)
        for b in range(n):
            copy(b, 0, sibling, me).wait_recv()
        for j, chip in enumerate(chips):
            for b in range(n):
                copy(b, 4 + j, (*chip, 1 - c), me).wait_recv()
        for cp in first + passed:
            cp.wait_send()
        for cp in mine:
            cp.wait()

    any_spec = pl.BlockSpec(memory_space=pl.ANY)
    return pl.pallas_call(
        body, name=name,
        in_specs=[any_spec] * n, out_specs=tuple([any_spec] * n),
        out_shape=tuple(jax.ShapeDtypeStruct((N_DEV,) + blk.shape, blk.dtype) for blk in blocks),
        scratch_shapes=[pltpu.SemaphoreType.DMA((n, 7)), pltpu.SemaphoreType.DMA((n, 7)), pltpu.SemaphoreType.DMA((n,))],
    )(*blocks)


def _direct_copies(scatter, in_refs, out_refs, send_sems, recv_sems, local_sems):
    x, y, c = _position()
    me = _slot(x, y, c)
    src = (lambda ref, slot: ref.at[slot]) if scatter else (lambda ref, slot: ref)
    copies = [pltpu.make_async_copy(src(ref, me), out.at[me], local_sems.at[b])
              for b, (ref, out) in enumerate(zip(in_refs, out_refs))]
    for k in range(1, N_DEV):
        px = 1 - x if k & 4 else x
        py = 1 - y if k & 2 else y
        pc = 1 - c if k & 1 else c
        copies += [pltpu.make_async_remote_copy(
            src_ref=src(ref, _slot(px, py, pc)), dst_ref=out.at[me],
            send_sem=send_sems.at[b, k - 1], recv_sem=recv_sems.at[b, k - 1],
            device_id=(px, py, pc), device_id_type=MESH) for b, (ref, out) in enumerate(zip(in_refs, out_refs))]
    return copies


def _exchange_operands(scatter, bufs):
    n = len(bufs)
    any_spec = pl.BlockSpec(memory_space=pl.ANY)
    shapes = [jax.ShapeDtypeStruct(b.shape if scatter else (N_DEV,) + b.shape, b.dtype) for b in bufs]
    sems = [pltpu.SemaphoreType.DMA((n, 7)), pltpu.SemaphoreType.DMA((n, 7)), pltpu.SemaphoreType.DMA((n,))]
    return [any_spec] * n, [any_spec] * n, shapes, sems


def _seg_len(n):
    return -(-n // SEG_ALIGN) * SEG_ALIGN


def _pack(arrays, lead=()):
    parts, total = [], 0
    for a in arrays:
        flat = a.reshape(lead + (-1,))
        pad = _seg_len(flat.shape[-1]) - flat.shape[-1]
        parts.append(jnp.pad(flat, [(0, 0)] * len(lead) + [(0, pad)]) if pad else flat)
        total += flat.shape[-1] + pad
    tail = -(-total // PACK_ALIGN) * PACK_ALIGN - total
    if tail:
        parts.append(jnp.zeros(lead + (tail,), parts[0].dtype))
    return jnp.concatenate(parts, axis=-1).reshape(lead + (-1, LANES))


def _unpack(packed, shapes, lead=()):
    flat = packed.reshape(lead + (-1,))
    out, off = [], 0
    for shp in shapes:
        n = math.prod(shp)
        out.append(flat[..., off:off + n].reshape(lead + tuple(shp)))
        off += _seg_len(n)
    return out


def _heads(a, n_heads):
    return a.reshape(a.shape[0], n_heads, HEAD_DIM).transpose(1, 0, 2)


def _unheads(a):
    return a.transpose(1, 0, 2).reshape(a.shape[1], a.shape[0] * HEAD_DIM)


def _head_gain(g, n_heads):
    return jnp.broadcast_to(g.reshape(1, 1, HEAD_DIM), (n_heads, 1, HEAD_DIM))


def _ffn_forward(x, gain, w_in_t, w_out, tag, carry=None):
    h = rmsnorm_fwd(x[None], gain.reshape(1, 1, -1), BF16, f"{tag}_norm")[0]
    a, b, u, *carried = ffn_in(h, w_in_t, f"{tag}_in", carry)
    if carry:
        w_out = w_out(carried)
    y = matmul(u, w_out, f"{tag}_out", residual=x, scale=0.5)
    return y, (x, h, a, b, u), carried


def _ffn_backward(dy, saved, gain, w_in_t, w_out, tag, carry_of=None):
    x, h, a, b, u = saved
    dw_out = matmul_tn(u, dy, f"{tag}_dwout", scale=0.5, out_dtype=BF16)
    da, db, *carried_1 = ffn_bwd_act(dy, w_out, a, b, f"{tag}_dact", carry_of(dw_out, None) if carry_of else None)
    dw_in_t = jnp.concatenate([matmul_tn(da, h, f"{tag}_dwin_a", out_dtype=BF16),
                               matmul_tn(db, h, f"{tag}_dwin_b", out_dtype=BF16)], axis=0)
    dh, *carried_2 = matmul_pair(da, db, w_in_t, f"{tag}_dh", carry_of(None, dw_in_t) if carry_of else None)
    dx, dgain = rmsnorm_bwd(x[None], gain.reshape(1, 1, -1), dh[None], f"{tag}_dnorm", residual=dy[None])
    return dx[0], dgain.reshape(-1), dw_in_t, dw_out, carried_1 + carried_2


def _mem_forward(mq, mem_n, w_kv, g_q, g_k, tag):
    gq = _head_gain(g_q, MEM_HEADS)
    gk = _head_gain(g_k, MEM_HEADS)
    qn = rmsnorm_fwd(mq, gq * QK_SCALE, BF16, f"{tag}_qnorm")
    kv = matmul(mem_n, w_kv, f"{tag}_kv")
    k = _heads(kv[:, :MEM_WIDTH], MEM_HEADS)
    v = _heads(kv[:, MEM_WIDTH:], MEM_HEADS)
    kn = rmsnorm_fwd(k, gk, BF16, f"{tag}_knorm")
    o, lse = attn_fwd(qn, kn, _widen_v(v), None, False, f"{tag}_attn")
    return o, (mq, gq, gk, qn, k, kn, v, o, lse)


def _mem_backward(do, saved, mem_n, w_kv, tag):
    mq, gq, gk, qn, k, kn, v, o, lse = saved
    dqn, dkn, dv = attn_bwd(qn, kn, v, None, o, lse, do, False, f"{tag}_dattn")
    dmq, dgq = rmsnorm_bwd(mq, gq, dqn * QK_SCALE, f"{tag}_dqnorm")
    dk, dgk = rmsnorm_bwd(k, gk, dkn, f"{tag}_dknorm")
    dkv = jnp.concatenate([_unheads(dk), _unheads(dv)], axis=1)
    dw_kv = matmul_tn(mem_n, dkv, f"{tag}_dwkv", out_dtype=BF16)
    dmem_n = matmul(dkv, w_kv, f"{tag}_dmem", transpose_b=True)
    return dmq, dgq.sum(axis=0).reshape(-1), dgk.sum(axis=0).reshape(-1), dw_kv, dmem_n


def _fox_split(w_in):
    t3 = 3 * TOK_WIDTH
    pad = jnp.zeros(w_in.shape[:-1] + (LANES - FOX_HEADS,), w_in.dtype)
    return jnp.concatenate([w_in[..., :t3], w_in[..., t3 + FOX_HEADS:], w_in[..., t3:t3 + FOX_HEADS], pad], axis=-1)


def _fox_unsplit(w):
    t3 = 3 * TOK_WIDTH
    return jnp.concatenate([w[..., :t3], w[..., t3 + MEM_WIDTH:t3 + MEM_WIDTH + FOX_HEADS], w[..., t3:t3 + MEM_WIDTH]],
                           axis=-1)


def _fox_forward(h, w_split, b_f, g_q, g_k, tag, carry=None):
    t3 = 3 * TOK_WIDTH
    proj = matmul(h, w_split, f"{tag}_proj")
    qkv = _heads(proj[:, :t3], 3 * FOX_HEADS)
    mq = _heads(proj[:, t3:t3 + MEM_WIDTH], MEM_HEADS)
    f_pad = proj[:, t3 + MEM_WIDTH:]
    b_pad = jnp.pad(b_f.reshape(1, -1), ((0, 0), (0, LANES - FOX_HEADS)))
    gains = jnp.concatenate([_head_gain(g_q, FOX_HEADS), _head_gain(g_k, FOX_HEADS)], axis=0)
    qk = qkv[:2 * FOX_HEADS]
    scaled = jnp.concatenate([gains[:FOX_HEADS] * QK_SCALE, gains[FOX_HEADS:]], axis=0)
    qkn = rmsnorm_fwd(qk, scaled, BF16, f"{tag}_qknorm")
    v = qkv[2 * FOX_HEADS:]
    c = fox_gate_fwd(f_pad, b_pad, f"{tag}_gate")
    ck = c[:, :FOX_HEADS].T[:, None, :]
    qn, kn = qkn[:FOX_HEADS], qkn[FOX_HEADS:]
    o, lse, *carried = attn_fwd(qn, kn, _widen_v(v), ck, True, f"{tag}_attn", carry)
    return o, mq, (qk, gains, qn, kn, v, ck, o, lse, f_pad, b_pad), carried


def _fox_backward(do, dmq, saved, tag, carry=None):
    qk, gains, qn, kn, v, ck, o, lse, f_pad, b_pad = saved
    dqn, dkn, dv, dck, dcq, *carried = attn_bwd(qn, kn, v, ck, o, lse, do, True, f"{tag}_dattn", carry)
    dqk, dgains = rmsnorm_bwd(qk, gains, jnp.concatenate([dqn * QK_SCALE, dkn], axis=0), f"{tag}_dqknorm")
    dc = jnp.pad((dck[:, 0, :] + dcq[:, :, 0]).T, ((0, 0), (0, LANES - FOX_HEADS)))
    df, db = fox_gate_bwd(dc, f_pad, b_pad, f"{tag}_dgate")
    dproj = jnp.concatenate([_unheads(dqk), _unheads(dv), _unheads(dmq), df], axis=1)
    dgains = dgains.reshape(2, FOX_HEADS, HEAD_DIM).sum(axis=1)
    return dproj, db[0, :FOX_HEADS], dgains[0], dgains[1], carried


def _gmlp_forward(h, w_in_t, v_gain, w_s, b_s, tag):
    proj = matmul(h, w_in_t, f"{tag}_proj", transpose_b=True)
    pu = _heads(proj[:, :TOK_WIDTH], FOX_HEADS)
    pv = _heads(proj[:, TOK_WIDTH:2 * TOK_WIDTH], FOX_HEADS)
    mq = _heads(proj[:, 2 * TOK_WIDTH:], MEM_HEADS)
    gain = v_gain.reshape(FOX_HEADS, 1, HEAD_DIM)
    b = b_s[:, :, None]
    o = gmlp_fwd(pu, pv, gain, w_s, b, f"{tag}_sgu")
    return o, mq, (pu, pv, gain, w_s, b)


def _gmlp_backward(do, dmq, saved, tag):
    pu, pv, gain, w_s, b = saved
    dpu, dpv, dw, db, dg = gmlp_bwd(pu, pv, gain, w_s, b, do, f"{tag}_dsgu")
    dproj = jnp.concatenate([_unheads(dpu), _unheads(dpv), _unheads(dmq)], axis=1)
    return dproj, dg.reshape(-1), dw, db[:, :, 0]


BIG = ("ffn1_w_in", "ffn1_w_out", "ffn2_w_in", "ffn2_w_out", "w_out", "mem_w_kv", "fox_w_in", "gmlp_w_in")
COLUMN_SHARDED = ("ffn1_w_in", "ffn2_w_in", "gmlp_w_in")
REPLICATED =("norm_ffn1", "norm_mix", "norm_ffn2", "mem_norm", "mem_q_norm", "mem_k_norm", "fox_b_f",
              "fox_q_norm", "fox_k_norm", "gmlp_w_s", "gmlp_b_s")
WEIGHTS = ("norm_ffn1", "ffn1_w_in", "ffn1_w_out", "norm_mix", "norm_ffn2", "ffn2_w_in", "ffn2_w_out", "w_out",
           "mem_norm", "mem_w_kv", "mem_q_norm", "mem_k_norm", "fox_w_in", "fox_b_f", "fox_q_norm", "fox_k_norm",
           "gmlp_w_in", "gmlp_v_norm", "gmlp_w_s", "gmlp_b_s")


def _to_transport(name, a):
    if name in COLUMN_SHARDED:
        return jnp.swapaxes(a, -1, -2)
    return _fox_split(a) if name == "fox_w_in" else a


def _from_transport(name, a):
    if name in COLUMN_SHARDED:
        return jnp.swapaxes(a, -1, -2)
    return _fox_unsplit(a) if name == "fox_w_in" else a


def kernel(x, mem, norm_ffn1, ffn1_w_in, ffn1_w_out, norm_mix, norm_ffn2, ffn2_w_in, ffn2_w_out, w_out, mem_norm, mem_w_kv, mem_q_norm, mem_k_norm, fox_w_in, fox_b_f, fox_q_norm, fox_k_norm, gmlp_w_in, gmlp_v_norm, gmlp_w_s, gmlp_b_s, loss_target, m_norm_ffn1, m_ffn1_w_in, m_ffn1_w_out, m_norm_mix, m_norm_ffn2, m_ffn2_w_in, m_ffn2_w_out, m_w_out, m_mem_norm, m_mem_w_kv, m_mem_q_norm, m_mem_k_norm, m_fox_w_in, m_fox_b_f, m_fox_q_norm, m_fox_k_norm, m_gmlp_w_in, m_gmlp_v_norm, m_gmlp_w_s, m_gmlp_b_s, v_norm_ffn1, v_ffn1_w_in, v_ffn1_w_out, v_norm_mix, v_norm_ffn2, v_ffn2_w_in, v_ffn2_w_out, v_w_out, v_mem_norm, v_mem_w_kv, v_mem_q_norm, v_mem_k_norm, v_fox_w_in, v_fox_b_f, v_fox_q_norm, v_fox_k_norm, v_gmlp_w_in, v_gmlp_v_norm, v_gmlp_w_s, v_gmlp_b_s):
    w = dict(norm_ffn1=norm_ffn1, ffn1_w_in=ffn1_w_in, ffn1_w_out=ffn1_w_out, norm_mix=norm_mix, norm_ffn2=norm_ffn2, ffn2_w_in=ffn2_w_in, ffn2_w_out=ffn2_w_out, w_out=w_out, mem_norm=mem_norm, mem_w_kv=mem_w_kv, mem_q_norm=mem_q_norm, mem_k_norm=mem_k_norm, fox_w_in=fox_w_in, fox_b_f=fox_b_f, fox_q_norm=fox_q_norm, fox_k_norm=fox_k_norm, gmlp_w_in=gmlp_w_in, gmlp_v_norm=gmlp_v_norm, gmlp_w_s=gmlp_w_s, gmlp_b_s=gmlp_b_s)
    m = dict(norm_ffn1=m_norm_ffn1, ffn1_w_in=m_ffn1_w_in, ffn1_w_out=m_ffn1_w_out, norm_mix=m_norm_mix, norm_ffn2=m_norm_ffn2, ffn2_w_in=m_ffn2_w_in, ffn2_w_out=m_ffn2_w_out, w_out=m_w_out, mem_norm=m_mem_norm, mem_w_kv=m_mem_w_kv, mem_q_norm=m_mem_q_norm, mem_k_norm=m_mem_k_norm, fox_w_in=m_fox_w_in, fox_b_f=m_fox_b_f, fox_q_norm=m_fox_q_norm, fox_k_norm=m_fox_k_norm, gmlp_w_in=m_gmlp_w_in, gmlp_v_norm=m_gmlp_v_norm, gmlp_w_s=m_gmlp_w_s, gmlp_b_s=m_gmlp_b_s)
    v = dict(norm_ffn1=v_norm_ffn1, ffn1_w_in=v_ffn1_w_in, ffn1_w_out=v_ffn1_w_out, norm_mix=v_norm_mix, norm_ffn2=v_norm_ffn2, ffn2_w_in=v_ffn2_w_in, ffn2_w_out=v_ffn2_w_out, w_out=v_w_out, mem_norm=v_mem_norm, mem_w_kv=v_mem_w_kv, mem_q_norm=v_mem_q_norm, mem_k_norm=v_mem_k_norm, fox_w_in=v_fox_w_in, fox_b_f=v_fox_b_f, fox_q_norm=v_fox_q_norm, fox_k_norm=v_fox_k_norm, gmlp_w_in=v_gmlp_w_in, gmlp_v_norm=v_gmlp_v_norm, gmlp_w_s=v_gmlp_w_s, gmlp_b_s=v_gmlp_b_s)

    depth = norm_ffn1.shape[0]
    x0 = x[0]
    mem0 = mem[0]
    target = loss_target[0]
    me = _slot(*_position())

    keys = [(n, i) for n in BIG for i in range(w[n].shape[0])]
    local = {k: _to_transport(k[0], w[k[0]][k[1]]) for k in keys}
    n_gain, gain_len = gmlp_v_norm.shape
    pad_gain = lambda a: jnp.pad(a, ((0, 8 - n_gain), (0, LANES - gain_len)))
    first = [("ffn1_w_in", 0)]
    early = [("ffn1_w_out", 0), ("fox_w_in", 0)]
    inner = [k for k in keys if k not in first + early]
    stack = lambda g: g.reshape(-1, g.shape[-1])
    full = {k: stack(g) for k, g in zip(first, all_gather_multi([local[k].astype(BF16) for k in first], "gather_weights"))}
    gather_early = (False, [local[k].astype(BF16) for k in early])
    gather_inner = (False, [local[k].astype(BF16) for k in inner] + [pad_gain(gmlp_v_norm)])

    mem_n = rmsnorm_fwd(mem0[None], mem_norm.reshape(1, 1, -1), BF16, "mem_norm")[0]
    saved = []
    xi = x0
    for i in range(depth):
        kind, j = i % 2, i // 2
        if i == 0:
            x1, ffn1_saved, arrived = _ffn_forward(xi, norm_ffn1[i], full["ffn1_w_in", i], lambda got: stack(got[0]),
                                                   f"l{i}_ffn1", gather_early)
            full.update({k: stack(g) for k, g in zip(early, arrived)})
        else:
            x1, ffn1_saved, _ = _ffn_forward(xi, norm_ffn1[i], full["ffn1_w_in", i], full["ffn1_w_out", i], f"l{i}_ffn1")
        h = rmsnorm_fwd(x1[None], norm_mix[i].reshape(1, 1, -1), BF16, f"l{i}_mixnorm")[0]
        if kind == 0:
            tok, mq, mix_saved, arrived = _fox_forward(h, full["fox_w_in", j], fox_b_f[j], fox_q_norm[j], fox_k_norm[j],
                                                       f"l{i}_fox", gather_inner if i == 0 else None)
            if i == 0:
                full.update({k: stack(g) for k, g in zip(inner, arrived)})
                v_gain_full = arrived[-1][:, :n_gain, :gain_len]
        else:
            tok, mq, mix_saved = _gmlp_forward(h, full["gmlp_w_in", j], v_gain_full[:, j, :].reshape(-1), gmlp_w_s[j],
                                               gmlp_b_s[j], f"l{i}_gmlp")
        mo, mem_saved = _mem_forward(mq, mem_n, full["mem_w_kv", i], mem_q_norm[i], mem_k_norm[i], f"l{i}_mem")
        cat = _unheads(jnp.concatenate([tok, mo], axis=0)).astype(BF16)
        x2 = matmul(cat, full["w_out", i], f"l{i}_wout", residual=x1)
        x3, ffn2_saved, _ = _ffn_forward(x2, norm_ffn2[i], full["ffn2_w_in", i], full["ffn2_w_out", i], f"l{i}_ffn2")
        saved.append((ffn1_saved, x1, h, mix_saved, mem_saved, cat, ffn2_saved))
        xi = x3

    dy, loss_part = loss_head(xi, target, "loss_head")
    loss = lax.psum(loss_part, ("x", "y", "c"))

    small = {n: [None] * w[n].shape[0] for n in REPLICATED + ("gmlp_v_norm",) if n != "mem_norm"}
    big = {}
    slots_of = lambda g: g.reshape((N_DEV, -1, g.shape[-1]))
    dmem_n = None
    for i in reversed(range(depth)):
        kind, j = i % 2, i // 2
        ffn1_saved, x1, h, mix_saved, mem_saved, cat, ffn2_saved = saved[i]
        dy, small["norm_ffn2"][i], big["ffn2_w_in", i], big["ffn2_w_out", i], _ = _ffn_backward(
            dy, ffn2_saved, norm_ffn2[i], full["ffn2_w_in", i], full["ffn2_w_out", i], f"l{i}_ffn2")
        big["w_out", i] = matmul_tn(cat, dy, f"l{i}_dwout", out_dtype=BF16)
        dcat = _heads(matmul(dy, full["w_out", i], f"l{i}_dcat", transpose_b=True), FOX_HEADS + MEM_HEADS)
        dmq, small["mem_q_norm"][i], small["mem_k_norm"][i], big["mem_w_kv", i], dmem_i = _mem_backward(
            dcat[FOX_HEADS:], mem_saved, mem_n, full["mem_w_kv", i], f"l{i}_mem")
        dmem_n = dmem_i if dmem_n is None else dmem_n + dmem_i
        if kind == 0:
            scatter_inner = (True, [slots_of(big[k]) for k in inner]) if i == 0 else None
            dproj, small["fox_b_f"][j], small["fox_q_norm"][j], small["fox_k_norm"][j], arrived = _fox_backward(
                dcat[:FOX_HEADS], dmq, mix_saved, f"l{i}_fox", scatter_inner)
            if i == 0:
                got = dict(zip(inner, arrived))
            big["fox_w_in", j] = matmul_tn(h, dproj, f"l{i}_fox_dwin", out_dtype=BF16)
            dh = matmul(dproj, full["fox_w_in", j], f"l{i}_fox_dh", transpose_b=True)
        else:
            dproj, small["gmlp_v_norm"][j], small["gmlp_w_s"][j], small["gmlp_b_s"][j] = _gmlp_backward(
                dcat[:FOX_HEADS], dmq, mix_saved, f"l{i}_gmlp")
            big["gmlp_w_in", j] = matmul_tn(dproj, h, f"l{i}_gmlp_dwin", out_dtype=BF16)
            dh = matmul(dproj, full["gmlp_w_in", j], f"l{i}_gmlp_dh")
        dy, dg_mix = rmsnorm_bwd(x1[None], norm_mix[i].reshape(1, 1, -1), dh[None], f"l{i}_dmixnorm", residual=dy[None])
        dy, small["norm_mix"][i] = dy[0], dg_mix.reshape(-1)
        def scatter_last(dw_out, dw_in_t):
            return (True, [slots_of(dw_in_t)] if dw_out is None else [slots_of(dw_out), slots_of(big["fox_w_in", 0])])

        dy, small["norm_ffn1"][i], big["ffn1_w_in", i], big["ffn1_w_out", i], arrived = _ffn_backward(
            dy, ffn1_saved, norm_ffn1[i], full["ffn1_w_in", i], full["ffn1_w_out", i], f"l{i}_ffn1",
            scatter_last if i == 0 else None)
        if i == 0:
            got.update(zip(early + first, arrived))
    grad_x = dy[None]
    _, dg_mem = rmsnorm_bwd(mem0[None], mem_norm.reshape(1, 1, -1), dmem_n[None], "dmem_norm")
    small = {n: jnp.stack(g) for n, g in small.items()}
    small["mem_norm"] = dg_mem.reshape(-1)

    results ={n: [[None] * w[n].shape[0] for _ in range(4)] for n in BIG}
    for k in keys:
        n, i = k
        outs = adamw(local[k], got[k], _to_transport(n, m[n][i]), _to_transport(n, v[n][i]), f"adamw_{n}_{i}")
        for q in range(4):
            results[n][q][i] = _from_transport(n, outs[q])
    sharded = {n: [jnp.stack(r) for r in results[n]] for n in BIG}

    small_names = REPLICATED + ("gmlp_v_norm",)
    (small_got,) = all_gather_multi([_pack([small[n] for n in small_names])], "gather_small_grads")
    rep_shapes = [w[n].shape for n in REPLICATED]
    gain_seg = jnp.zeros((n_gain, N_DEV * gain_len), F32)
    pack_rep = lambda d: _pack([d[n] for n in REPLICATED] + [gain_seg])
    outs = adamw(pack_rep(w), small_got, pack_rep(m), pack_rep(v), "adamw_replicated")
    replicated = [dict(zip(REPLICATED, _unpack(o, rep_shapes))) for o in outs]
    gain_parts = _unpack(small_got, rep_shapes + [(n_gain, N_DEV * gain_len)], lead=(N_DEV,))[-1]
    gain_slots = lax.dynamic_slice_in_dim(gain_parts, me * gain_len, gain_len, axis=2)
    gain_slots = jnp.pad(gain_slots, ((0, 0), (0, 8 - n_gain), (0, LANES - gain_len)))
    outs = adamw(pad_gain(gmlp_v_norm), gain_slots, pad_gain(m["gmlp_v_norm"]), pad_gain(v["gmlp_v_norm"]),
                 "adamw_gmlp_v_norm")
    sharded["gmlp_v_norm"] = [o[:n_gain, :gain_len] for o in outs]

    out = [loss, grad_x]
    for q in range(4):
        out += [(replicated[q][n] if n in REPLICATED else sharded[n][q]) for n in WEIGHTS]
    return tuple(out)
```

```python
import functools
import math

import jax
import jax.numpy as jnp
from jax import lax
from jax.experimental import pallas as pl
from jax.experimental.pallas import tpu as pltpu

F32 = jnp.float32
BF16 = jnp.bfloat16

EPS = 1e-6
HEAD_DIM = 64
FOX_HEADS = 12
MEM_HEADS = 4
TOK_WIDTH = FOX_HEADS * HEAD_DIM
MEM_WIDTH = MEM_HEADS * HEAD_DIM
CHUNK = 128
LANES = 128
N_DEV = 8
SEG_ALIGN = 16 * LANES
PACK_ROWS = 1024
PACK_ALIGN = PACK_ROWS * LANES

ADAM_LR = 0.001
ADAM_B1 = 0.9
ADAM_B2 = 0.999
ADAM_EPS = 1e-08
ADAM_WD = 0.01
ADAM_STEP = 10

VMEM_LIMIT_BYTES = 48 * 1024 * 1024
MESH = pl.DeviceIdType.MESH
CONTRACT_0 = (((0,), (0,)), ((), ()))
CONTRACT_1 = (((1,), (1,)), ((), ()))


def _params(*semantics):
    return pltpu.CompilerParams(dimension_semantics=semantics, vmem_limit_bytes=VMEM_LIMIT_BYTES)


def _pick(n, candidates):
    for c in candidates:
        if c <= n and n % c == 0:
            return c
    return n


def _sigmoid(x):
    return 0.5 * jnp.tanh(0.5 * x) + 0.5


def _row_tile(r, w):
    return _pick(r, (1024,) if w >= 512 else (2048, 1024, 512, 256))


def rmsnorm_fwd(x, gain, out_dtype, name):
    g_, r_, w_ = x.shape
    tr = _row_tile(r_, w_)

    def body(x_ref, g_ref, y_ref):
        xv = x_ref[0].astype(F32)
        r = lax.rsqrt(jnp.mean(xv * xv, axis=-1, keepdims=True) + EPS)
        y_ref[0] = (xv * r * g_ref[0]).astype(y_ref.dtype)

    return pl.pallas_call(
        body, name=name, grid=(g_, r_ // tr),
        in_specs=[pl.BlockSpec((1, tr, w_), lambda g, i: (g, i, 0)),
                  pl.BlockSpec((1, 1, w_), lambda g, i: (g, 0, 0))],
        out_specs=pl.BlockSpec((1, tr, w_), lambda g, i: (g, i, 0)),
        out_shape=jax.ShapeDtypeStruct((g_, r_, w_), out_dtype),
        compiler_params=_params("parallel", "parallel"),
    )(x, gain)


def rmsnorm_bwd(x, gain, dy, name, residual=None):
    g_, r_, w_ = x.shape
    tr = _row_tile(r_, w_)
    has_res = residual is not None

    def body(*refs):
        if has_res:
            x_ref, g_ref, dy_ref, res_ref, dx_ref, dg_ref = refs
        else:
            x_ref, g_ref, dy_ref, dx_ref, dg_ref = refs
        xv = x_ref[0].astype(F32)
        dyv = dy_ref[0].astype(F32)
        r = lax.rsqrt(jnp.mean(xv * xv, axis=-1, keepdims=True) + EPS)
        n = xv * r
        dn = dyv * g_ref[0]
        dx = r * (dn - n * jnp.mean(dn * n, axis=-1, keepdims=True))
        if has_res:
            dx = dx + res_ref[0]
        dx_ref[0] = dx
        part = jnp.sum(dyv * n, axis=0, keepdims=True)

        @pl.when(pl.program_id(1) == 0)
        def _():
            dg_ref[0] = part

        @pl.when(pl.program_id(1) != 0)
        def _():
            dg_ref[0] += part

    row = pl.BlockSpec((1, tr, w_), lambda g, i: (g, i, 0))
    vec = pl.BlockSpec((1, 1, w_), lambda g, i: (g, 0, 0))
    operands = (x, gain, dy) + ((residual,) if has_res else ())
    return pl.pallas_call(
        body, name=name, grid=(g_, r_ // tr),
        in_specs=[row, vec, row] + ([row] if has_res else []),
        out_specs=(row, vec),
        out_shape=(jax.ShapeDtypeStruct((g_, r_, w_), F32), jax.ShapeDtypeStruct((g_, 1, w_), F32)),
        compiler_params=_params("parallel", "arbitrary"),
    )(*operands)


def matmul(a, b, name, out_dtype=F32, residual=None, scale=None, transpose_b=False):
    m_, k_ = a.shape
    n_ = b.shape[0] if transpose_b else b.shape[1]
    tm = _pick(m_, (512, 256, 128))
    tn = _pick(n_, (1408, 1024, 896, 512, 256, 128))
    has_res = residual is not None

    def body(*refs):
        if has_res:
            a_ref, b_ref, res_ref, o_ref = refs
        else:
            a_ref, b_ref, o_ref = refs
        av, bv = a_ref[...].astype(BF16), b_ref[...].astype(BF16)
        if transpose_b:
            acc = lax.dot_general(av, bv, CONTRACT_1, preferred_element_type=F32)
        else:
            acc = jnp.dot(av, bv, preferred_element_type=F32)
        if scale is not None:
            acc = acc * scale
        if has_res:
            acc = acc + res_ref[...]
        o_ref[...] = acc.astype(o_ref.dtype)

    out_spec = pl.BlockSpec((tm, tn), lambda j, i: (i, j))
    b_spec = pl.BlockSpec((tn, k_), lambda j, i: (j, 0)) if transpose_b else pl.BlockSpec((k_, tn), lambda j, i: (0, j))
    operands = (a, b) + ((residual,) if has_res else ())
    return pl.pallas_call(
        body, name=name, grid=(n_ // tn, m_ // tm),
        in_specs=[pl.BlockSpec((tm, k_), lambda j, i: (i, 0)), b_spec] + ([out_spec] if has_res else []),
        out_specs=out_spec,
        out_shape=jax.ShapeDtypeStruct((m_, n_), out_dtype),
        compiler_params=_params("parallel", "parallel"),
    )(*operands)


def matmul_tn(a, b, name, scale=None, out_dtype=F32):
    s_, k_ = a.shape
    n_ = b.shape[1]
    tk = _pick(k_, (1024, 1408, 896, 512, 256, 128))
    tn = _pick(n_, (1408, 1024, 896, 512, 256, 128))
    ts = _pick(s_, (2048, 1024, 512, 256, 128))
    ns = s_ // ts

    def body(a_ref, b_ref, o_ref, acc_ref):
        part = lax.dot_general(a_ref[...].astype(BF16), b_ref[...].astype(BF16), CONTRACT_0,
                               preferred_element_type=F32)
        step = pl.program_id(2)

        @pl.when(step == 0)
        def _():
            acc_ref[...] = part

        @pl.when(step != 0)
        def _():
            acc_ref[...] += part

        @pl.when(step == ns - 1)
        def _():
            acc = acc_ref[...]
            o_ref[...] = (acc if scale is None else acc * scale).astype(o_ref.dtype)

    return pl.pallas_call(
        body, name=name, grid=(k_ // tk, n_ // tn, ns),
        in_specs=[pl.BlockSpec((ts, tk), lambda i, j, s: (s, i)),
                  pl.BlockSpec((ts, tn), lambda i, j, s: (s, j))],
        out_specs=pl.BlockSpec((tk, tn), lambda i, j, s: (i, j)),
        out_shape=jax.ShapeDtypeStruct((k_, n_), out_dtype),
        scratch_shapes=[pltpu.VMEM((tk, tn), F32)],
        compiler_params=_params("parallel", "parallel", "arbitrary"),
    )(a, b)


def ffn_in(h, w_in_t, name, carry=None):
    s_, d_ = h.shape
    f_ = w_in_t.shape[0] // 2
    tm = _pick(s_, (512, 256, 128))
    tn = _pick(f_, (1408, 1024, 512, 256, 128))
    nb = f_ // tn

    def body(h_ref, wa_ref, wb_ref, a_ref, b_ref, u_ref):
        hv = h_ref[...]
        a = lax.dot_general(hv, wa_ref[...], CONTRACT_1, preferred_element_type=F32)
        b = lax.dot_general(hv, wb_ref[...], CONTRACT_1, preferred_element_type=F32)
        a_ref[...] = a.astype(BF16)
        b_ref[...] = b.astype(BF16)
        u_ref[...] = (a * _sigmoid(a) * b).astype(BF16)

    o_spec = pl.BlockSpec((tm, tn), lambda j, i: (i, j))
    out = jax.ShapeDtypeStruct((s_, f_), BF16)
    return _carried_call(
        body, name, (nb, s_ // tm),
        [pl.BlockSpec((tm, d_), lambda j, i: (i, 0)),
         pl.BlockSpec((tn, d_), lambda j, i: (j, 0)),
         pl.BlockSpec((tn, d_), lambda j, i: (j + nb, 0))],
        [o_spec, o_spec, o_spec], [out, out, out], [], [h, w_in_t, w_in_t], carry)


def _carried_call(body, name, grid, in_specs, out_specs, out_shape, scratch, operands, carry):
    if not carry:
        return pl.pallas_call(
            body, name=name, grid=grid, in_specs=in_specs, out_specs=tuple(out_specs), out_shape=tuple(out_shape),
            scratch_shapes=scratch, compiler_params=_params(*["parallel"] * len(grid)))(*operands)
    counts = (len(in_specs), len(carry[1]), len(out_specs), len(carry[1]), len(scratch), 3)

    def carrying(*refs):
        ins, c_in, outs, c_out, scr, sems = _split_refs(refs, counts)
        step_no, last_no = _grid_step(grid)

        @pl.when(step_no == 0)
        def _():
            for cp in _direct_copies(carry[0], c_in, c_out, *sems):
                cp.start()

        body(*ins, *outs, *scr)

        @pl.when(step_no == last_no)
        def _():
            for cp in _direct_copies(carry[0], c_in, c_out, *sems):
                cp.wait()

    c_in_specs, c_out_specs, c_shapes, c_sems = _exchange_operands(*carry)
    return pl.pallas_call(
        carrying, name=name, grid=grid, in_specs=in_specs + c_in_specs, out_specs=tuple(out_specs + c_out_specs),
        out_shape=tuple(out_shape + c_shapes), scratch_shapes=scratch + c_sems,
        compiler_params=_params(*["arbitrary"] * len(grid)))(*operands, *carry[1])


def matmul_pair(a1, a2, b, name, carry=None):
    m_, k_ = a1.shape
    n_ = b.shape[1]
    tm = _pick(m_, (512, 256, 128))
    tn = _pick(n_, (1024, 512, 256, 128))

    def body(a1_ref, a2_ref, b1_ref, b2_ref, o_ref):
        o_ref[...] = (jnp.dot(a1_ref[...], b1_ref[...], preferred_element_type=F32)
                      + jnp.dot(a2_ref[...], b2_ref[...], preferred_element_type=F32))

    a_spec = pl.BlockSpec((tm, k_), lambda j, i: (i, 0))
    return _carried_call(
        body, name, (n_ // tn, m_ // tm),
        [a_spec, a_spec, pl.BlockSpec((k_, tn), lambda j, i: (0, j)), pl.BlockSpec((k_, tn), lambda j, i: (1, j))],
        [pl.BlockSpec((tm, tn), lambda j, i: (i, j))], [jax.ShapeDtypeStruct((m_, n_), F32)], [], [a1, a2, b, b], carry)


def ffn_bwd_act(dy, w_out, a, b, name, carry=None):
    s_, d_ = dy.shape
    f_ = w_out.shape[0]
    tm = _pick(s_, (512, 256, 128))
    tn = _pick(f_, (1408, 1024, 512, 256, 128))

    def body(dy_ref, w_ref, a_ref, b_ref, da_ref, db_ref):
        du = 0.5 * lax.dot_general(dy_ref[...].astype(BF16), w_ref[...], CONTRACT_1, preferred_element_type=F32)
        av = a_ref[...].astype(F32)
        bv = b_ref[...].astype(F32)
        sig = _sigmoid(av)
        da_ref[...] = (du * bv * (sig * (1.0 + av * (1.0 - sig)))).astype(BF16)
        db_ref[...] = (du * (av * sig)).astype(BF16)

    t_spec = pl.BlockSpec((tm, tn), lambda j, i: (i, j))
    out = jax.ShapeDtypeStruct((s_, f_), BF16)
    return _carried_call(
        body, name, (f_ // tn, s_ // tm),
        [pl.BlockSpec((tm, d_), lambda j, i: (i, 0)), pl.BlockSpec((tn, d_), lambda j, i: (j, 0)), t_spec, t_spec],
        [t_spec, t_spec], [out, out], [], [dy, w_out, a, b], carry)


def _fold(a, b, n, forward):
    if forward:
        low = b <= a
        return jnp.where(low, a, n - 1 - a), jnp.where(low, b, b - a - 1)
    low = b < n - a
    return jnp.where(low, a, n - 1 - a), jnp.where(low, a + b, b - 1)


def _attn_grid(h_, n_outer, n_inner, causal, forward):
    if causal and n_outer % 2 == 0:
        return (h_, n_outer // 2, n_outer + 1), lambda a, b: _fold(a, b, n_outer, forward)
    return (h_, n_outer, n_inner), lambda a, b: (a, b)


QK_SCALE = 1.0 / math.sqrt(HEAD_DIM)
SUM_LANE = HEAD_DIM


def _scores(q, k, ck, masked):
    s = lax.dot_general(q, k, CONTRACT_1, preferred_element_type=F32)
    if ck is not None:
        s = s - ck
    if masked:
        row = lax.broadcasted_iota(jnp.int32, s.shape, 0)
        col = lax.broadcasted_iota(jnp.int32, s.shape, 1)
        s = jnp.where(col <= row, s, -jnp.inf)
    return s


def _widen_v(v):
    ones = jnp.ones(v.shape[:2] + (1,), BF16)
    zeros = jnp.zeros(v.shape[:2] + (LANES - HEAD_DIM - 1,), BF16)
    return jnp.concatenate([v.astype(BF16), ones, zeros], axis=-1)


def _grid_step(grid):
    step = 0
    for axis, n in enumerate(grid):
        step = step * n + pl.program_id(axis)
    return step, math.prod(grid) - 1


def _split_refs(refs, counts):
    out, at = [], 0
    for n in counts:
        out.append(refs[at:at + n])
        at += n
    return out


def attn_fwd(q, k, v_wide, ck, causal, name, carry=None):
    h_, sq, d_ = q.shape
    sk = k.shape[1]
    tq = _pick(sq, (1024, 512, 256, 128) if causal else (2048, 1024, 512, 256, 128))
    tk = tq if causal else _pick(sk, (512, 256, 128))
    nq, nk = sq // tq, sk // tk
    assert not causal or nq == 1 or nq % 2 == 0
    bias = ck is not None
    grid, blocks = _attn_grid(h_, nq, nk, causal, True)
    nc = len(carry[1]) if carry else 0

    def body(*refs):
        ins, c_in, outs, c_out, scratch, sems = _split_refs(refs, (4 if bias else 3, nc, 2, nc, 2, 3 if carry else 0))
        q_ref, k_ref, v_ref = ins[:3]
        ck_ref = ins[3] if bias else None
        (o_ref, lse_ref), (m_sc, acc_sc) = outs, scratch
        i, j = blocks(pl.program_id(1), pl.program_id(2))
        if carry:
            step_no, last_no = _grid_step(grid)

            @pl.when(step_no == 0)
            def _():
                for cp in _direct_copies(carry[0], c_in, c_out, *sems):
                    cp.start()

        @pl.when(j == 0)
        def _():
            m_sc[...] = jnp.full(m_sc.shape, -jnp.inf, F32)
            acc_sc[...] = jnp.zeros(acc_sc.shape, F32)

        def step(masked):
            s = _scores(q_ref[0], k_ref[0], ck_ref[0] if bias else None, masked)
            m_prev = m_sc[...]
            m_new = jnp.maximum(m_prev, jnp.max(s, axis=1, keepdims=True))
            p = jnp.exp(s - m_new)
            acc_sc[...] = jnp.exp(m_prev - m_new) * acc_sc[...] + jnp.dot(p.astype(BF16), v_ref[0],
                                                                         preferred_element_type=F32)
            m_sc[...] = m_new

        if causal:
            pl.when(j < i)(functools.partial(step, False))
            pl.when(j == i)(functools.partial(step, True))
        else:
            step(False)

        @pl.when(j == (i if causal else nk - 1))
        def _():
            acc = acc_sc[...]
            lane = lax.broadcasted_iota(jnp.int32, acc.shape, 1)
            l = jnp.sum(jnp.where(lane == SUM_LANE, acc, 0.0), axis=1, keepdims=True)
            o_ref[0] = acc_sc[:, :HEAD_DIM] / l
            lse_ref[0] = m_sc[...] + jnp.log(l)

        if carry:
            @pl.when(step_no == last_no)
            def _():
                for cp in _direct_copies(carry[0], c_in, c_out, *sems):
                    cp.wait()

    q_spec = pl.BlockSpec((1, tq, d_), lambda h, a, b: (h, blocks(a, b)[0], 0))
    q1_spec = pl.BlockSpec((1, tq, 1), lambda h, a, b: (h, blocks(a, b)[0], 0))
    k_spec = pl.BlockSpec((1, tk, d_), lambda h, a, b: (h, blocks(a, b)[1], 0))
    in_specs = [q_spec, k_spec, pl.BlockSpec((1, tk, LANES), lambda h, a, b: (h, blocks(a, b)[1], 0))]
    operands = [q, k, v_wide]
    if bias:
        in_specs.append(pl.BlockSpec((1, 1, tk), lambda h, a, b: (h, 0, blocks(a, b)[1])))
        operands.append(ck)
    out_specs = [q_spec, q1_spec]
    out_shape = [jax.ShapeDtypeStruct((h_, sq, d_), F32), jax.ShapeDtypeStruct((h_, sq, 1), F32)]
    scratch = [pltpu.VMEM((tq, 1), F32), pltpu.VMEM((tq, LANES), F32)]
    if carry:
        c_in_specs, c_out_specs, c_shapes, c_sems = _exchange_operands(*carry)
        in_specs, out_specs, out_shape, scratch = in_specs + c_in_specs, out_specs + c_out_specs, out_shape + c_shapes, scratch + c_sems
        operands = operands + list(carry[1])
    return pl.pallas_call(
        body, name=name, grid=grid,
        in_specs=in_specs, out_specs=tuple(out_specs), out_shape=tuple(out_shape), scratch_shapes=scratch,
        compiler_params=_params("arbitrary" if carry else "parallel", "arbitrary", "arbitrary"),
    )(*operands)


def attn_bwd(q, k, v, ck, o, lse, do, causal, name, carry=None):
    h_, sq, d_ = q.shape
    sk = k.shape[1]
    tq = _pick(sq, (1024, 512, 256, 128) if causal else (2048, 1024, 512, 256, 128))
    tk = tq if causal else _pick(sk, (512, 256, 128))
    nq, nk = sq // tq, sk // tk
    assert not causal or nq == 1 or nq % 2 == 0
    bias = ck is not None
    grid, blocks = _attn_grid(h_, nk, nq, causal, False)
    nc = len(carry[1]) if carry else 0

    def body(*refs):
        ins, c_in, outs, c_out, scratch, sems = _split_refs(
            refs, (7 if bias else 6, nc, 5 if bias else 3, nc, 3 if bias else 2, 3 if carry else 0))
        if bias:
            q_ref, k_ref, v_ref, ck_ref, o_ref, lse_ref, do_ref = ins
            dq_ref, dk_ref, dv_ref, dc_ref, dcq_ref = outs
            dk_sc, dv_sc, dc_sc = scratch
        else:
            q_ref, k_ref, v_ref, o_ref, lse_ref, do_ref = ins
            dq_ref, dk_ref, dv_ref = outs
            dk_sc, dv_sc = scratch
        j, i = blocks(pl.program_id(1), pl.program_id(2))
        if carry:
            step_no, last_no = _grid_step(grid)

            @pl.when(step_no == 0)
            def _():
                for cp in _direct_copies(carry[0], c_in, c_out, *sems):
                    cp.start()

        @pl.when((pl.program_id(1) == 0) & (pl.program_id(2) == 0))
        def _():
            dq_ref[...] = jnp.zeros(dq_ref.shape, F32)
            if bias:
                dcq_ref[...] = jnp.zeros(dcq_ref.shape, F32)

        @pl.when(i == (j if causal else 0))
        def _():
            dk_sc[...] = jnp.zeros(dk_sc.shape, F32)
            dv_sc[...] = jnp.zeros(dv_sc.shape, F32)
            if bias:
                dc_sc[...] = jnp.zeros(dc_sc.shape, F32)

        def step(masked):
            qb, kb = q_ref[0], k_ref[0]
            dof = do_ref[0]
            dob = dof.astype(BF16)
            s = _scores(qb, kb, ck_ref[0] if bias else None, masked)
            p = jnp.exp(s - lse_ref[0])
            dp = lax.dot_general(dob, v_ref[0].astype(BF16), CONTRACT_1, preferred_element_type=F32)
            delta = jnp.sum(dof * o_ref[0], axis=1, keepdims=True)
            ds = p * (dp - delta)
            dsb = ds.astype(BF16)
            dv_sc[...] += lax.dot_general(p.astype(BF16), dob, CONTRACT_0, preferred_element_type=F32)
            dk_sc[...] += lax.dot_general(dsb, qb, CONTRACT_0, preferred_element_type=F32)
            rows = pl.ds(pl.multiple_of(i * tq, tq), tq)
            dq_ref[0, rows, :] += jnp.dot(dsb, kb, preferred_element_type=F32)
            if bias:
                dc_sc[...] -= jnp.sum(ds, axis=0, keepdims=True)
                dcq_ref[0, rows, :] += jnp.sum(ds, axis=1, keepdims=True)

        if causal:
            pl.when(i > j)(functools.partial(step, False))
            pl.when(i == j)(functools.partial(step, True))
        else:
            step(False)

        @pl.when(i == nq - 1)
        def _():
            dk_ref[0] = dk_sc[...]
            dv_ref[0] = dv_sc[...]
            if bias:
                dc_ref[0] = dc_sc[...]

        if carry:
            @pl.when(step_no == last_no)
            def _():
                for cp in _direct_copies(carry[0], c_in, c_out, *sems):
                    cp.wait()

    q_spec = pl.BlockSpec((1, tq, d_), lambda h, a, b: (h, blocks(a, b)[1], 0))
    q1_spec = pl.BlockSpec((1, tq, 1), lambda h, a, b: (h, blocks(a, b)[1], 0))
    k_spec = pl.BlockSpec((1, tk, d_), lambda h, a, b: (h, blocks(a, b)[0], 0))
    c_spec = pl.BlockSpec((1, 1, tk), lambda h, a, b: (h, 0, blocks(a, b)[0]))
    in_specs = [q_spec, k_spec, k_spec] + ([c_spec] if bias else []) + [q_spec, q1_spec, q_spec]
    operands = [q, k, v] + ([ck] if bias else []) + [o, lse, do]
    out_specs = [pl.BlockSpec((1, sq, d_), lambda h, a, b: (h, 0, 0)), k_spec, k_spec]
    out_shape = [jax.ShapeDtypeStruct((h_, sq, d_), F32), jax.ShapeDtypeStruct((h_, sk, d_), F32),
                 jax.ShapeDtypeStruct((h_, sk, d_), F32)]
    scratch = [pltpu.VMEM((tk, d_), F32), pltpu.VMEM((tk, d_), F32)]
    if bias:
        out_specs += [c_spec, pl.BlockSpec((1, sq, 1), lambda h, a, b: (h, 0, 0))]
        out_shape += [jax.ShapeDtypeStruct((h_, 1, sk), F32), jax.ShapeDtypeStruct((h_, sq, 1), F32)]
        scratch.append(pltpu.VMEM((1, tk), F32))
    if carry:
        c_in_specs, c_out_specs, c_shapes, c_sems = _exchange_operands(*carry)
        in_specs, out_specs, out_shape, scratch = in_specs + c_in_specs, out_specs + c_out_specs, out_shape + c_shapes, scratch + c_sems
        operands = operands + list(carry[1])
    return pl.pallas_call(
        body, name=name, grid=grid,
        in_specs=in_specs, out_specs=tuple(out_specs), out_shape=tuple(out_shape),
        scratch_shapes=scratch,
        compiler_params=_params("arbitrary" if carry else "parallel", "arbitrary", "arbitrary"),
    )(*operands)


def _tri(lower):
    row = lax.broadcasted_iota(jnp.int32, (CHUNK, CHUNK), 0)
    col = lax.broadcasted_iota(jnp.int32, (CHUNK, CHUNK), 1)
    return jnp.where((col <= row) if lower else (col >= row), 1.0, 0.0).astype(F32)


def fox_gate_fwd(f, b, name):
    s_ = f.shape[0]

    def body(f_ref, b_ref, c_ref, carry):
        @pl.when(pl.program_id(0) == 0)
        def _():
            carry[...] = jnp.zeros(carry.shape, F32)

        xv = f_ref[...] + b_ref[...]
        log_f = jnp.minimum(xv, 0.0) - jnp.log(1.0 + jnp.exp(-jnp.abs(xv)))
        c = jnp.dot(_tri(True), log_f, precision=lax.Precision.HIGHEST, preferred_element_type=F32) + carry[...]
        c_ref[...] = c
        carry[...] = c[CHUNK - 1:CHUNK, :]

    blk = pl.BlockSpec((CHUNK, LANES), lambda i: (i, 0))
    return pl.pallas_call(
        body, name=name, grid=(s_ // CHUNK,),
        in_specs=[blk, pl.BlockSpec((1, LANES), lambda i: (0, 0))], out_specs=blk,
        out_shape=jax.ShapeDtypeStruct((s_, LANES), F32),
        scratch_shapes=[pltpu.VMEM((1, LANES), F32)],
        compiler_params=_params("arbitrary"),
    )(f, b)


def fox_gate_bwd(dc, f, b, name):
    s_ = f.shape[0]
    n = s_ // CHUNK

    def body(dc_ref, f_ref, b_ref, df_ref, db_ref, carry):
        @pl.when(pl.program_id(0) == 0)
        def _():
            carry[...] = jnp.zeros(carry.shape, F32)
            db_ref[...] = jnp.zeros(db_ref.shape, F32)

        dlog = jnp.dot(_tri(False), dc_ref[...], precision=lax.Precision.HIGHEST,
                       preferred_element_type=F32) + carry[...]
        df = dlog * _sigmoid(-(f_ref[...] + b_ref[...]))
        df_ref[...] = df
        db_ref[...] += jnp.sum(df, axis=0, keepdims=True)
        carry[...] = dlog[0:1, :]

    blk = pl.BlockSpec((CHUNK, LANES), lambda i: (n - 1 - i, 0))
    vec = pl.BlockSpec((1, LANES), lambda i: (0, 0))
    return pl.pallas_call(
        body, name=name, grid=(n,),
        in_specs=[blk, blk, vec], out_specs=(blk, vec),
        out_shape=(jax.ShapeDtypeStruct((s_, LANES), F32), jax.ShapeDtypeStruct((1, LANES), F32)),
        scratch_shapes=[pltpu.VMEM((1, LANES), F32)],
        compiler_params=_params("arbitrary"),
    )(dc, f, b)


GELU_K = math.sqrt(2.0 / math.pi)
GELU_C = 0.044715


def _gelu(x):
    return 0.5 * x * (1.0 + jnp.tanh(GELU_K * (x + GELU_C * (x * x * x))))


def _gelu_grad(x):
    t = jnp.tanh(GELU_K * (x + GELU_C * (x * x * x)))
    return 0.5 * (1.0 + t) + 0.5 * x * (1.0 - t * t) * (GELU_K * (1.0 + 3.0 * GELU_C * (x * x)))


def _tril_mask():
    row = lax.broadcasted_iota(jnp.int32, (CHUNK, CHUNK), 0)
    col = lax.broadcasted_iota(jnp.int32, (CHUNK, CHUNK), 1)
    return col <= row


def _gmlp_specs(s_, ts):
    row = pl.BlockSpec((1, ts, HEAD_DIM), lambda g, i: (g, i, 0))
    gain = pl.BlockSpec((1, 1, HEAD_DIM), lambda g, i: (g, 0, 0))
    w = pl.BlockSpec((1, CHUNK, CHUNK), lambda g, i: (g, 0, 0))
    b = pl.BlockSpec((1, CHUNK, 1), lambda g, i: (g, 0, 0))
    return row, gain, w, b


def gmlp_fwd(pu, pv, gain, w, b, name):
    g_, s_, d_ = pu.shape
    ts = _pick(s_, (1024, 512, 256, 128))

    def body(pu_ref, pv_ref, g_ref, w_ref, b_ref, o_ref):
        v = _gelu(pv_ref[0])
        r = lax.rsqrt(jnp.mean(v * v, axis=-1, keepdims=True) + EPS)
        vn = (v * r * g_ref[0]).astype(BF16)
        wt = jnp.where(_tril_mask(), w_ref[0], 0.0).astype(BF16)
        for c in range(ts // CHUNK):
            rows = pl.ds(c * CHUNK, CHUNK)
            gate = jnp.dot(wt, vn[c * CHUNK:(c + 1) * CHUNK], preferred_element_type=F32) + b_ref[0]
            o_ref[0, rows, :] = _gelu(pu_ref[0, rows, :]) * gate

    row, gspec, wspec, bspec = _gmlp_specs(s_, ts)
    return pl.pallas_call(
        body, name=name, grid=(g_, s_ // ts),
        in_specs=[row, row, gspec, wspec, bspec], out_specs=row,
        out_shape=jax.ShapeDtypeStruct((g_, s_, d_), F32),
        compiler_params=_params("parallel", "parallel"),
    )(pu, pv, gain, w, b)


def gmlp_bwd(pu, pv, gain, w, b, dout, name):
    g_, s_, d_ = pu.shape
    ts = _pick(s_, (1024, 512, 256, 128))

    def body(pu_ref, pv_ref, g_ref, w_ref, b_ref, do_ref, dpu_ref, dpv_ref, dw_ref, db_ref, dg_ref):
        @pl.when(pl.program_id(1) == 0)
        def _():
            dw_ref[...] = jnp.zeros(dw_ref.shape, F32)
            db_ref[...] = jnp.zeros(db_ref.shape, F32)
            dg_ref[...] = jnp.zeros(dg_ref.shape, F32)

        gain_v = g_ref[0]
        mask = _tril_mask()
        wt = jnp.where(mask, w_ref[0], 0.0).astype(BF16)
        dw = jnp.zeros((CHUNK, CHUNK), F32)
        db = jnp.zeros((CHUNK, 1), F32)
        dg = jnp.zeros((1, d_), F32)
        for c in range(ts // CHUNK):
            rows = pl.ds(c * CHUNK, CHUNK)
            pu_c = pu_ref[0, rows, :]
            pv_c = pv_ref[0, rows, :]
            do_c = do_ref[0, rows, :]
            u = _gelu(pu_c)
            v = _gelu(pv_c)
            r = lax.rsqrt(jnp.mean(v * v, axis=-1, keepdims=True) + EPS)
            n = v * r
            vn = (n * gain_v).astype(BF16)
            gate = jnp.dot(wt, vn, preferred_element_type=F32) + b_ref[0]
            dgate = do_c * u
            dgate_b = dgate.astype(BF16)
            dpu_ref[0, rows, :] = do_c * gate * _gelu_grad(pu_c)
            db = db + jnp.sum(dgate, axis=1, keepdims=True)
            dw = dw + lax.dot_general(dgate_b, vn, CONTRACT_1, preferred_element_type=F32)
            dvn = lax.dot_general(wt, dgate_b, CONTRACT_0, preferred_element_type=F32)
            dg = dg + jnp.sum(dvn * n, axis=0, keepdims=True)
            dn = dvn * gain_v
            dv = r * (dn - n * jnp.mean(dn * n, axis=-1, keepdims=True))
            dpv_ref[0, rows, :] = dv * _gelu_grad(pv_c)
        dw_ref[0] += jnp.where(mask, dw, 0.0)
        db_ref[0] += db
        dg_ref[0] += dg

    row, gspec, wspec, bspec = _gmlp_specs(s_, ts)
    return pl.pallas_call(
        body, name=name, grid=(g_, s_ // ts),
        in_specs=[row, row, gspec, wspec, bspec, row],
        out_specs=(row, row, wspec, bspec, gspec),
        out_shape=(jax.ShapeDtypeStruct((g_, s_, d_), F32), jax.ShapeDtypeStruct((g_, s_, d_), F32),
                   jax.ShapeDtypeStruct((g_, CHUNK, CHUNK), F32), jax.ShapeDtypeStruct((g_, CHUNK, 1), F32),
                   jax.ShapeDtypeStruct((g_, 1, d_), F32)),
        compiler_params=_params("parallel", "arbitrary"),
    )(pu, pv, gain, w, b, dout)


def loss_head(y, t, name):
    s_, d_ = y.shape
    tr = _pick(s_, (1024, 512, 256, 128))

    def body(y_ref, t_ref, dy_ref, l_ref):
        err = y_ref[...] - t_ref[...]
        dy_ref[...] = err * (1.0 / d_)
        part = jnp.full(l_ref.shape, jnp.sum(err * err) * (0.5 / d_), F32)

        @pl.when(pl.program_id(0) == 0)
        def _():
            l_ref[...] = part

        @pl.when(pl.program_id(0) != 0)
        def _():
            l_ref[...] += part

    blk = pl.BlockSpec((tr, d_), lambda i: (i, 0))
    dy, l = pl.pallas_call(
        body, name=name, grid=(s_ // tr,),
        in_specs=[blk, blk], out_specs=(blk, pl.BlockSpec((8, LANES), lambda i: (0, 0))),
        out_shape=(jax.ShapeDtypeStruct((s_, d_), F32), jax.ShapeDtypeStruct((8, LANES), F32)),
        compiler_params=_params("arbitrary"),
    )(y, t)
    return dy, l[0, 0]


def adamw(w, g_slots, m, v, name):
    r_, c_ = w.shape
    tr = _pick(r_, (1024, 512, 352, 256, 224, 128, 64, 32, 16, 8))

    def body(w_ref, gs_ref, m_ref, v_ref, g_ref, d_ref, nm_ref, nv_ref):
        g = gs_ref[0].astype(F32)
        for k in range(1, N_DEV):
            g = g + gs_ref[k].astype(F32)
        m_new = ADAM_B1 * m_ref[...] + (1.0 - ADAM_B1) * g
        v_new = ADAM_B2 * v_ref[...] + (1.0 - ADAM_B2) * (g * g)
        m_hat = m_new / (1.0 - ADAM_B1 ** ADAM_STEP)
        v_hat = v_new / (1.0 - ADAM_B2 ** ADAM_STEP)
        g_ref[...] = g
        d_ref[...] = -ADAM_LR * (m_hat / (jnp.sqrt(v_hat) + ADAM_EPS) + ADAM_WD * w_ref[...])
        nm_ref[...] = m_new
        nv_ref[...] = v_new

    blk = pl.BlockSpec((tr, c_), lambda i: (i, 0))
    out = jax.ShapeDtypeStruct((r_, c_), F32)
    return pl.pallas_call(
        body, name=name, grid=(r_ // tr,),
        in_specs=[blk, pl.BlockSpec((N_DEV, tr, c_), lambda i: (0, i, 0)), blk, blk],
        out_specs=(blk, blk, blk, blk), out_shape=(out, out, out, out),
        compiler_params=_params("parallel"),
    )(w, g_slots, m, v)


def _position():
    x, y, c = lax.axis_index("x"), lax.axis_index("y"), lax.axis_index("c")
    return x, y, c


def _slot(px, py, pc):
    return 4 * px + 2 * py + pc


def all_gather_multi(blocks, name):
    n = len(blocks)

    def body(*refs):
        x_refs, out_refs = refs[:n], refs[n:2 * n]
        send_sems, recv_sems, local_sems = refs[2 * n:]
        x, y, c = _position()
        me, sibling = (x, y, c), (x, y, 1 - c)
        chips = [(1 - x, y), (x, 1 - y), (1 - x, 1 - y)]

        def copy(b, k, owner, to, src=None):
            dst = out_refs[b].at[_slot(*owner)]
            return pltpu.make_async_remote_copy(
                src_ref=dst if src is None else src, dst_ref=dst,
                send_sem=send_sems.at[b, k], recv_sem=recv_sems.at[b, k], device_id=to, device_id_type=MESH)

        mine = [pltpu.make_async_copy(x_refs[b], out_refs[b].at[_slot(*me)], local_sems.at[b]) for b in range(n)]
        for cp in mine:
            cp.start()
        first = [copy(b, 1 + j, me, (*chip, c), src=x_refs[b]) for j, chip in enumerate(chips) for b in range(n)]
        first += [copy(b, 0, me, sibling, src=x_refs[b]) for b in range(n)]
        for cp in first:
            cp.start()
        passed = []
        for j, chip in enumerate(chips):
            for b in range(n):
                copy(b, 1 + j, (*chip, c), me).wait_recv()
                cp = copy(b, 4 + j, (*chip, c), sibling)
                cp.start()
                passed.append(cp)
        for b in range(n):
            copy(b, 0, sibling, me).wait_recv()
        for j, chip in enumerate(chips):
            for b in range(n):
                copy(b, 4 + j, (*chip, 1 - c), me).wait_recv()
        for cp in first + passed:
            cp.wait_send()
        for cp in mine:
            cp.wait()

    any_spec = pl.BlockSpec(memory_space=pl.ANY)
    return pl.pallas_call(
        body, name=name,
        in_specs=[any_spec] * n, out_specs=tuple([any_spec] * n),
        out_shape=tuple(jax.ShapeDtypeStruct((N_DEV,) + blk.shape, blk.dtype) for blk in blocks),
        scratch_shapes=[pltpu.SemaphoreType.DMA((n, 7)), pltpu.SemaphoreType.DMA((n, 7)), pltpu.SemaphoreType.DMA((n,))],
    )(*blocks)


def _direct_copies(scatter, in_refs, out_refs, send_sems, recv_sems, local_sems):
    x, y, c = _position()
    me = _slot(x, y, c)
    src = (lambda ref, slot: ref.at[slot]) if scatter else (lambda ref, slot: ref)
    copies = [pltpu.make_async_copy(src(ref, me), out.at[me], local_sems.at[b])
              for b, (ref, out) in enumerate(zip(in_refs, out_refs))]
    for k in range(1, N_DEV):
        px = 1 - x if k & 4 else x
        py = 1 - y if k & 2 else y
        pc = 1 - c if k & 1 else c
        copies += [pltpu.make_async_remote_copy(
            src_ref=src(ref, _slot(px, py, pc)), dst_ref=out.at[me],
            send_sem=send_sems.at[b, k - 1], recv_sem=recv_sems.at[b, k - 1],
            device_id=(px, py, pc), device_id_type=MESH) for b, (ref, out) in enumerate(zip(in_refs, out_refs))]
    return copies


def _exchange_operands(scatter, bufs):
    n = len(bufs)
    any_spec = pl.BlockSpec(memory_space=pl.ANY)
    shapes = [jax.ShapeDtypeStruct(b.shape if scatter else (N_DEV,) + b.shape, b.dtype) for b in bufs]
    sems = [pltpu.SemaphoreType.DMA((n, 7)), pltpu.SemaphoreType.DMA((n, 7)), pltpu.SemaphoreType.DMA((n,))]
    return [any_spec] * n, [any_spec] * n, shapes, sems


def _seg_len(n):
    return -(-n // SEG_ALIGN) * SEG_ALIGN


def _pack(arrays, lead=()):
    parts, total = [], 0
    for a in arrays:
        flat = a.reshape(lead + (-1,))
        pad = _seg_len(flat.shape[-1]) - flat.shape[-1]
        parts.append(jnp.pad(flat, [(0, 0)] * len(lead) + [(0, pad)]) if pad else flat)
        total += flat.shape[-1] + pad
    tail = -(-total // PACK_ALIGN) * PACK_ALIGN - total
    if tail:
        parts.append(jnp.zeros(lead + (tail,), parts[0].dtype))
    return jnp.concatenate(parts, axis=-1).reshape(lead + (-1, LANES))


def _unpack(packed, shapes, lead=()):
    flat = packed.reshape(lead + (-1,))
    out, off = [], 0
    for shp in shapes:
        n = math.prod(shp)
        out.append(flat[..., off:off + n].reshape(lead + tuple(shp)))
        off += _seg_len(n)
    return out


def _heads(a, n_heads):
    return a.reshape(a.shape[0], n_heads, HEAD_DIM).transpose(1, 0, 2)


def _unheads(a):
    return a.transpose(1, 0, 2).reshape(a.shape[1], a.shape[0] * HEAD_DIM)


def _head_gain(g, n_heads):
    return jnp.broadcast_to(g.reshape(1, 1, HEAD_DIM), (n_heads, 1, HEAD_DIM))


def _ffn_forward(x, gain, w_in_t, w_out, tag, carry=None):
    h = rmsnorm_fwd(x[None], gain.reshape(1, 1, -1), BF16, f"{tag}_norm")[0]
    a, b, u, *carried = ffn_in(h, w_in_t, f"{tag}_in", carry)
    if carry:
        w_out = w_out(carried)
    y = matmul(u, w_out, f"{tag}_out", residual=x, scale=0.5)
    return y, (x, h, a, b, u), carried


def _ffn_backward(dy, saved, gain, w_in_t, w_out, tag, carry_of=None):
    x, h, a, b, u = saved
    dw_out = matmul_tn(u, dy, f"{tag}_dwout", scale=0.5, out_dtype=BF16)
    da, db, *carried_1 = ffn_bwd_act(dy, w_out, a, b, f"{tag}_dact", carry_of(dw_out, None) if carry_of else None)
    dw_in_t = jnp.concatenate([matmul_tn(da, h, f"{tag}_dwin_a", out_dtype=BF16),
                               matmul_tn(db, h, f"{tag}_dwin_b", out_dtype=BF16)], axis=0)
    dh, *carried_2 = matmul_pair(da, db, w_in_t, f"{tag}_dh", carry_of(None, dw_in_t) if carry_of else None)
    dx, dgain = rmsnorm_bwd(x[None], gain.reshape(1, 1, -1), dh[None], f"{tag}_dnorm", residual=dy[None])
    return dx[0], dgain.reshape(-1), dw_in_t, dw_out, carried_1 + carried_2


def _mem_forward(mq, mem_n, w_kv, g_q, g_k, tag):
    gq = _head_gain(g_q, MEM_HEADS)
    gk = _head_gain(g_k, MEM_HEADS)
    qn = rmsnorm_fwd(mq, gq * QK_SCALE, BF16, f"{tag}_qnorm")
    kv = matmul(mem_n, w_kv, f"{tag}_kv")
    k = _heads(kv[:, :MEM_WIDTH], MEM_HEADS)
    v = _heads(kv[:, MEM_WIDTH:], MEM_HEADS)
    kn = rmsnorm_fwd(k, gk, BF16, f"{tag}_knorm")
    o, lse = attn_fwd(qn, kn, _widen_v(v), None, False, f"{tag}_attn")
    return o, (mq, gq, gk, qn, k, kn, v, o, lse)


def _mem_backward(do, saved, mem_n, w_kv, tag):
    mq, gq, gk, qn, k, kn, v, o, lse = saved
    dqn, dkn, dv = attn_bwd(qn, kn, v, None, o, lse, do, False, f"{tag}_dattn")
    dmq, dgq = rmsnorm_bwd(mq, gq, dqn * QK_SCALE, f"{tag}_dqnorm")
    dk, dgk = rmsnorm_bwd(k, gk, dkn, f"{tag}_dknorm")
    dkv = jnp.concatenate([_unheads(dk), _unheads(dv)], axis=1)
    dw_kv = matmul_tn(mem_n, dkv, f"{tag}_dwkv", out_dtype=BF16)
    dmem_n = matmul(dkv, w_kv, f"{tag}_dmem", transpose_b=True)
    return dmq, dgq.sum(axis=0).reshape(-1), dgk.sum(axis=0).reshape(-1), dw_kv, dmem_n


def _fox_split(w_in):
    t3 = 3 * TOK_WIDTH
    pad = jnp.zeros(w_in.shape[:-1] + (LANES - FOX_HEADS,), w_in.dtype)
    return jnp.concatenate([w_in[..., :t3], w_in[..., t3 + FOX_HEADS:], w_in[..., t3:t3 + FOX_HEADS], pad], axis=-1)


def _fox_unsplit(w):
    t3 = 3 * TOK_WIDTH
    return jnp.concatenate([w[..., :t3], w[..., t3 + MEM_WIDTH:t3 + MEM_WIDTH + FOX_HEADS], w[..., t3:t3 + MEM_WIDTH]],
                           axis=-1)


def _fox_forward(h, w_split, b_f, g_q, g_k, tag, carry=None):
    t3 = 3 * TOK_WIDTH
    proj = matmul(h, w_split, f"{tag}_proj")
    qkv = _heads(proj[:, :t3], 3 * FOX_HEADS)
    mq = _heads(proj[:, t3:t3 + MEM_WIDTH], MEM_HEADS)
    f_pad = proj[:, t3 + MEM_WIDTH:]
    b_pad = jnp.pad(b_f.reshape(1, -1), ((0, 0), (0, LANES - FOX_HEADS)))
    gains = jnp.concatenate([_head_gain(g_q, FOX_HEADS), _head_gain(g_k, FOX_HEADS)], axis=0)
    qk = qkv[:2 * FOX_HEADS]
    scaled = jnp.concatenate([gains[:FOX_HEADS] * QK_SCALE, gains[FOX_HEADS:]], axis=0)
    qkn = rmsnorm_fwd(qk, scaled, BF16, f"{tag}_qknorm")
    v = qkv[2 * FOX_HEADS:]
    c = fox_gate_fwd(f_pad, b_pad, f"{tag}_gate")
    ck = c[:, :FOX_HEADS].T[:, None, :]
    qn, kn = qkn[:FOX_HEADS], qkn[FOX_HEADS:]
    o, lse, *carried = attn_fwd(qn, kn, _widen_v(v), ck, True, f"{tag}_attn", carry)
    return o, mq, (qk, gains, qn, kn, v, ck, o, lse, f_pad, b_pad), carried


def _fox_backward(do, dmq, saved, tag, carry=None):
    qk, gains, qn, kn, v, ck, o, lse, f_pad, b_pad = saved
    dqn, dkn, dv, dck, dcq, *carried = attn_bwd(qn, kn, v, ck, o, lse, do, True, f"{tag}_dattn", carry)
    dqk, dgains = rmsnorm_bwd(qk, gains, jnp.concatenate([dqn * QK_SCALE, dkn], axis=0), f"{tag}_dqknorm")
    dc = jnp.pad((dck[:, 0, :] + dcq[:, :, 0]).T, ((0, 0), (0, LANES - FOX_HEADS)))
    df, db = fox_gate_bwd(dc, f_pad, b_pad, f"{tag}_dgate")
    dproj = jnp.concatenate([_unheads(dqk), _unheads(dv), _unheads(dmq), df], axis=1)
    dgains = dgains.reshape(2, FOX_HEADS, HEAD_DIM).sum(axis=1)
    return dproj, db[0, :FOX_HEADS], dgains[0], dgains[1], carried


def _gmlp_forward(h, w_in_t, v_gain, w_s, b_s, tag):
    proj = matmul(h, w_in_t, f"{tag}_proj", transpose_b=True)
    pu = _heads(proj[:, :TOK_WIDTH], FOX_HEADS)
    pv = _heads(proj[:, TOK_WIDTH:2 * TOK_WIDTH], FOX_HEADS)
    mq = _heads(proj[:, 2 * TOK_WIDTH:], MEM_HEADS)
    gain = v_gain.reshape(FOX_HEADS, 1, HEAD_DIM)
    b = b_s[:, :, None]
    o = gmlp_fwd(pu, pv, gain, w_s, b, f"{tag}_sgu")
    return o, mq, (pu, pv, gain, w_s, b)


def _gmlp_backward(do, dmq, saved, tag):
    pu, pv, gain, w_s, b = saved
    dpu, dpv, dw, db, dg = gmlp_bwd(pu, pv, gain, w_s, b, do, f"{tag}_dsgu")
    dproj = jnp.concatenate([_unheads(dpu), _unheads(dpv), _unheads(dmq)], axis=1)
    return dproj, dg.reshape(-1), dw, db[:, :, 0]


BIG = ("ffn1_w_in", "ffn1_w_out", "ffn2_w_in", "ffn2_w_out", "w_out", "mem_w_kv", "fox_w_in", "gmlp_w_in")
COLUMN_SHARDED = ("ffn1_w_in", "ffn2_w_in", "gmlp_w_in")
REPLICATED =("norm_ffn1", "norm_mix", "norm_ffn2", "mem_norm", "mem_q_norm", "mem_k_norm", "fox_b_f",
              "fox_q_norm", "fox_k_norm", "gmlp_w_s", "gmlp_b_s")
WEIGHTS = ("norm_ffn1", "ffn1_w_in", "ffn1_w_out", "norm_mix", "norm_ffn2", "ffn2_w_in", "ffn2_w_out", "w_out",
           "mem_norm", "mem_w_kv", "mem_q_norm", "mem_k_norm", "fox_w_in", "fox_b_f", "fox_q_norm", "fox_k_norm",
           "gmlp_w_in", "gmlp_v_norm", "gmlp_w_s", "gmlp_b_s")


def _to_transport(name, a):
    if name in COLUMN_SHARDED:
        return jnp.swapaxes(a, -1, -2)
    return _fox_split(a) if name == "fox_w_in" else a


def _from_transport(name, a):
    if name in COLUMN_SHARDED:
        return jnp.swapaxes(a, -1, -2)
    return _fox_unsplit(a) if name == "fox_w_in" else a


def kernel(x, mem, norm_ffn1, ffn1_w_in, ffn1_w_out, norm_mix, norm_ffn2, ffn2_w_in, ffn2_w_out, w_out, mem_norm, mem_w_kv, mem_q_norm, mem_k_norm, fox_w_in, fox_b_f, fox_q_norm, fox_k_norm, gmlp_w_in, gmlp_v_norm, gmlp_w_s, gmlp_b_s, loss_target, m_norm_ffn1, m_ffn1_w_in, m_ffn1_w_out, m_norm_mix, m_norm_ffn2, m_ffn2_w_in, m_ffn2_w_out, m_w_out, m_mem_norm, m_mem_w_kv, m_mem_q_norm, m_mem_k_norm, m_fox_w_in, m_fox_b_f, m_fox_q_norm, m_fox_k_norm, m_gmlp_w_in, m_gmlp_v_norm, m_gmlp_w_s, m_gmlp_b_s, v_norm_ffn1, v_ffn1_w_in, v_ffn1_w_out, v_norm_mix, v_norm_ffn2, v_ffn2_w_in, v_ffn2_w_out, v_w_out, v_mem_norm, v_mem_w_kv, v_mem_q_norm, v_mem_k_norm, v_fox_w_in, v_fox_b_f, v_fox_q_norm, v_fox_k_norm, v_gmlp_w_in, v_gmlp_v_norm, v_gmlp_w_s, v_gmlp_b_s):
    w = dict(norm_ffn1=norm_ffn1, ffn1_w_in=ffn1_w_in, ffn1_w_out=ffn1_w_out, norm_mix=norm_mix, norm_ffn2=norm_ffn2, ffn2_w_in=ffn2_w_in, ffn2_w_out=ffn2_w_out, w_out=w_out, mem_norm=mem_norm, mem_w_kv=mem_w_kv, mem_q_norm=mem_q_norm, mem_k_norm=mem_k_norm, fox_w_in=fox_w_in, fox_b_f=fox_b_f, fox_q_norm=fox_q_norm, fox_k_norm=fox_k_norm, gmlp_w_in=gmlp_w_in, gmlp_v_norm=gmlp_v_norm, gmlp_w_s=gmlp_w_s, gmlp_b_s=gmlp_b_s)
    m = dict(norm_ffn1=m_norm_ffn1, ffn1_w_in=m_ffn1_w_in, ffn1_w_out=m_ffn1_w_out, norm_mix=m_norm_mix, norm_ffn2=m_norm_ffn2, ffn2_w_in=m_ffn2_w_in, ffn2_w_out=m_ffn2_w_out, w_out=m_w_out, mem_norm=m_mem_norm, mem_w_kv=m_mem_w_kv, mem_q_norm=m_mem_q_norm, mem_k_norm=m_mem_k_norm, fox_w_in=m_fox_w_in, fox_b_f=m_fox_b_f, fox_q_norm=m_fox_q_norm, fox_k_norm=m_fox_k_norm, gmlp_w_in=m_gmlp_w_in, gmlp_v_norm=m_gmlp_v_norm, gmlp_w_s=m_gmlp_w_s, gmlp_b_s=m_gmlp_b_s)
    v = dict(norm_ffn1=v_norm_ffn1, ffn1_w_in=v_ffn1_w_in, ffn1_w_out=v_ffn1_w_out, norm_mix=v_norm_mix, norm_ffn2=v_norm_ffn2, ffn2_w_in=v_ffn2_w_in, ffn2_w_out=v_ffn2_w_out, w_out=v_w_out, mem_norm=v_mem_norm, mem_w_kv=v_mem_w_kv, mem_q_norm=v_mem_q_norm, mem_k_norm=v_mem_k_norm, fox_w_in=v_fox_w_in, fox_b_f=v_fox_b_f, fox_q_norm=v_fox_q_norm, fox_k_norm=v_fox_k_norm, gmlp_w_in=v_gmlp_w_in, gmlp_v_norm=v_gmlp_v_norm, gmlp_w_s=v_gmlp_w_s, gmlp_b_s=v_gmlp_b_s)

    depth = norm_ffn1.shape[0]
    x0 = x[0]
    mem0 = mem[0]
    target = loss_target[0]
    me = _slot(*_position())

    keys = [(n, i) for n in BIG for i in range(w[n].shape[0])]
    local = {k: _to_transport(k[0], w[k[0]][k[1]]) for k in keys}
    n_gain, gain_len = gmlp_v_norm.shape
    pad_gain = lambda a: jnp.pad(a, ((0, 8 - n_gain), (0, LANES - gain_len)))
    first = [("ffn1_w_in", 0)]
    early = [("ffn1_w_out", 0), ("fox_w_in", 0)]
    inner = [k for k in keys if k not in first + early]
    stack = lambda g: g.reshape(-1, g.shape[-1])
    full = {k: stack(g) for k, g in zip(first, all_gather_multi([local[k].astype(BF16) for k in first], "gather_weights"))}
    gather_early = (False, [local[k].astype(BF16) for k in early])
    gather_inner = (False, [local[k].astype(BF16) for k in inner] + [pad_gain(gmlp_v_norm)])

    mem_n = rmsnorm_fwd(mem0[None], mem_norm.reshape(1, 1, -1), BF16, "mem_norm")[0]
    saved = []
    xi = x0
    for i in range(depth):
        kind, j = i % 2, i // 2
        if i == 0:
            x1, ffn1_saved, arrived = _ffn_forward(xi, norm_ffn1[i], full["ffn1_w_in", i], lambda got: stack(got[0]),
                                                   f"l{i}_ffn1", gather_early)
            full.update({k: stack(g) for k, g in zip(early, arrived)})
        else:
            x1, ffn1_saved, _ = _ffn_forward(xi, norm_ffn1[i], full["ffn1_w_in", i], full["ffn1_w_out", i], f"l{i}_ffn1")
        h = rmsnorm_fwd(x1[None], norm_mix[i].reshape(1, 1, -1), BF16, f"l{i}_mixnorm")[0]
        if kind == 0:
            tok, mq, mix_saved, arrived = _fox_forward(h, full["fox_w_in", j], fox_b_f[j], fox_q_norm[j], fox_k_norm[j],
                                                       f"l{i}_fox", gather_inner if i == 0 else None)
            if i == 0:
                full.update({k: stack(g) for k, g in zip(inner, arrived)})
                v_gain_full = arrived[-1][:, :n_gain, :gain_len]
        else:
            tok, mq, mix_saved = _gmlp_forward(h, full["gmlp_w_in", j], v_gain_full[:, j, :].reshape(-1), gmlp_w_s[j],
                                               gmlp_b_s[j], f"l{i}_gmlp")
        mo, mem_saved = _mem_forward(mq, mem_n, full["mem_w_kv", i], mem_q_norm[i], mem_k_norm[i], f"l{i}_mem")
        cat = _unheads(jnp.concatenate([tok, mo], axis=0)).astype(BF16)
        x2 = matmul(cat, full["w_out", i], f"l{i}_wout", residual=x1)
        x3, ffn2_saved, _ = _ffn_forward(x2, norm_ffn2[i], full["ffn2_w_in", i], full["ffn2_w_out", i], f"l{i}_ffn2")
        saved.append((ffn1_saved, x1, h, mix_saved, mem_saved, cat, ffn2_saved))
        xi = x3

    dy, loss_part = loss_head(xi, target, "loss_head")
    loss = lax.psum(loss_part, ("x", "y", "c"))

    small = {n: [None] * w[n].shape[0] for n in REPLICATED + ("gmlp_v_norm",) if n != "mem_norm"}
    big = {}
    slots_of = lambda g: g.reshape((N_DEV, -1, g.shape[-1]))
    dmem_n = None
    for i in reversed(range(depth)):
        kind, j = i % 2, i // 2
        ffn1_saved, x1, h, mix_saved, mem_saved, cat, ffn2_saved = saved[i]
        dy, small["norm_ffn2"][i], big["ffn2_w_in", i], big["ffn2_w_out", i], _ = _ffn_backward(
            dy, ffn2_saved, norm_ffn2[i], full["ffn2_w_in", i], full["ffn2_w_out", i], f"l{i}_ffn2")
        big["w_out", i] = matmul_tn(cat, dy, f"l{i}_dwout", out_dtype=BF16)
        dcat = _heads(matmul(dy, full["w_out", i], f"l{i}_dcat", transpose_b=True), FOX_HEADS + MEM_HEADS)
        dmq, small["mem_q_norm"][i], small["mem_k_norm"][i], big["mem_w_kv", i], dmem_i = _mem_backward(
            dcat[FOX_HEADS:], mem_saved, mem_n, full["mem_w_kv", i], f"l{i}_mem")
        dmem_n = dmem_i if dmem_n is None else dmem_n + dmem_i
        if kind == 0:
            scatter_inner = (True, [slots_of(big[k]) for k in inner]) if i == 0 else None
            dproj, small["fox_b_f"][j], small["fox_q_norm"][j], small["fox_k_norm"][j], arrived = _fox_backward(
                dcat[:FOX_HEADS], dmq, mix_saved, f"l{i}_fox", scatter_inner)
            if i == 0:
                got = dict(zip(inner, arrived))
            big["fox_w_in", j] = matmul_tn(h, dproj, f"l{i}_fox_dwin", out_dtype=BF16)
            dh = matmul(dproj, full["fox_w_in", j], f"l{i}_fox_dh", transpose_b=True)
        else:
            dproj, small["gmlp_v_norm"][j], small["gmlp_w_s"][j], small["gmlp_b_s"][j] = _gmlp_backward(
                dcat[:FOX_HEADS], dmq, mix_saved, f"l{i}_gmlp")
            big["gmlp_w_in", j] = matmul_tn(dproj, h, f"l{i}_gmlp_dwin", out_dtype=BF16)
            dh = matmul(dproj, full["gmlp_w_in", j], f"l{i}_gmlp_dh")
        dy, dg_mix = rmsnorm_bwd(x1[None], norm_mix[i].reshape(1, 1, -1), dh[None], f"l{i}_dmixnorm", residual=dy[None])
        dy, small["norm_mix"][i] = dy[0], dg_mix.reshape(-1)
        def scatter_last(dw_out, dw_in_t):
            return (True, [slots_of(dw_in_t)] if dw_out is None else [slots_of(dw_out), slots_of(big["fox_w_in", 0])])

        dy, small["norm_ffn1"][i], big["ffn1_w_in", i], big["ffn1_w_out", i], arrived = _ffn_backward(
            dy, ffn1_saved, norm_ffn1[i], full["ffn1_w_in", i], full["ffn1_w_out", i], f"l{i}_ffn1",
            scatter_last if i == 0 else None)
        if i == 0:
            got.update(zip(early + first, arrived))
    grad_x = dy[None]
    _, dg_mem = rmsnorm_bwd(mem0[None], mem_norm.reshape(1, 1, -1), dmem_n[None], "dmem_norm")
    small = {n: jnp.stack(g) for n, g in small.items()}
    small["mem_norm"] = dg_mem.reshape(-1)

    results ={n: [[None] * w[n].shape[0] for _ in range(4)] for n in BIG}
    for k in keys:
        n, i = k
        outs = adamw(local[k], got[k], _to_transport(n, m[n][i]), _to_transport(n, v[n][i]), f"adamw_{n}_{i}")
        for q in range(4):
            results[n][q][i] = _from_transport(n, outs[q])
    sharded = {n: [jnp.stack(r) for r in results[n]] for n in BIG}

    small_names = REPLICATED + ("gmlp_v_norm",)
    (small_got,) = all_gather_multi([_pack([small[n] for n in small_names])], "gather_small_grads")
    rep_shapes = [w[n].shape for n in REPLICATED]
    gain_seg = jnp.zeros((n_gain, N_DEV * gain_len), F32)
    pack_rep = lambda d: _pack([d[n] for n in REPLICATED] + [gain_seg])
    outs = adamw(pack_rep(w), small_got, pack_rep(m), pack_rep(v), "adamw_replicated")
    replicated = [dict(zip(REPLICATED, _unpack(o, rep_shapes))) for o in outs]
    gain_parts = _unpack(small_got, rep_shapes + [(n_gain, N_DEV * gain_len)], lead=(N_DEV,))[-1]
    gain_slots = lax.dynamic_slice_in_dim(gain_parts, me * gain_len, gain_len, axis=2)
    gain_slots = jnp.pad(gain_slots, ((0, 0), (0, 8 - n_gain), (0, LANES - gain_len)))
    outs = adamw(pad_gain(gmlp_v_norm), gain_slots, pad_gain(m["gmlp_v_norm"]), pad_gain(v["gmlp_v_norm"]),
                 "adamw_gmlp_v_norm")
    sharded["gmlp_v_norm"] = [o[:n_gain, :gain_len] for o in outs]

    out = [loss, grad_x]
    for q in range(4):
        out += [(replicated[q][n] if n in REPLICATED else sharded[n][q]) for n in WEIGHTS]
    return tuple(out)
```

```python
import functools
import math

import jax
import jax.numpy as jnp
from jax import lax
from jax.experimental import pallas as pl
from jax.experimental.pallas import tpu as pltpu

F32 = jnp.float32
BF16 = jnp.bfloat16

EPS = 1e-6
HEAD_DIM = 64
FOX_HEADS = 12
MEM_HEADS = 4
TOK_WIDTH = FOX_HEADS * HEAD_DIM
MEM_WIDTH = MEM_HEADS * HEAD_DIM
CHUNK = 128
LANES = 128
N_DEV = 8
SEG_ALIGN = 16 * LANES
PACK_ROWS = 1024
PACK_ALIGN = PACK_ROWS * LANES

ADAM_LR = 0.001
ADAM_B1 = 0.9
ADAM_B2 = 0.999
ADAM_EPS = 1e-08
ADAM_WD = 0.01
ADAM_STEP = 10

VMEM_LIMIT_BYTES = 48 * 1024 * 1024
MESH = pl.DeviceIdType.MESH
CONTRACT_0 = (((0,), (0,)), ((), ()))
CONTRACT_1 = (((1,), (1,)), ((), ()))


def _params(*semantics):
    return pltpu.CompilerParams(dimension_semantics=semantics, vmem_limit_bytes=VMEM_LIMIT_BYTES)


def _pick(n, candidates):
    for c in candidates:
        if c <= n and n % c == 0:
            return c
    return n


def _sigmoid(x):
    return 0.5 * jnp.tanh(0.5 * x) + 0.5


def _row_tile(r, w):
    return _pick(r, (1024,) if w >= 512 else (4096, 2048, 1024, 512, 256))


def rmsnorm_fwd(x, gain, out_dtype, name):
    g_, r_, w_ = x.shape
    tr = _row_tile(r_, w_)

    def body(x_ref, g_ref, y_ref):
        xv = x_ref[0].astype(F32)
        r = lax.rsqrt(jnp.mean(xv * xv, axis=-1, keepdims=True) + EPS)
        y_ref[0] = (xv * r * g_ref[0]).astype(y_ref.dtype)

    return pl.pallas_call(
        body, name=name, grid=(g_, r_ // tr),
        in_specs=[pl.BlockSpec((1, tr, w_), lambda g, i: (g, i, 0)),
                  pl.BlockSpec((1, 1, w_), lambda g, i: (g, 0, 0))],
        out_specs=pl.BlockSpec((1, tr, w_), lambda g, i: (g, i, 0)),
        out_shape=jax.ShapeDtypeStruct((g_, r_, w_), out_dtype),
        compiler_params=_params("parallel", "parallel"),
    )(x, gain)


def rmsnorm_bwd(x, gain, dy, name, residual=None):
    g_, r_, w_ = x.shape
    tr = _row_tile(r_, w_)
    has_res = residual is not None

    def body(*refs):
        if has_res:
            x_ref, g_ref, dy_ref, res_ref, dx_ref, dg_ref = refs
        else:
            x_ref, g_ref, dy_ref, dx_ref, dg_ref = refs
        xv = x_ref[0].astype(F32)
        dyv = dy_ref[0].astype(F32)
        r = lax.rsqrt(jnp.mean(xv * xv, axis=-1, keepdims=True) + EPS)
        n = xv * r
        dn = dyv * g_ref[0]
        dx = r * (dn - n * jnp.mean(dn * n, axis=-1, keepdims=True))
        if has_res:
            dx = dx + res_ref[0]
        dx_ref[0] = dx
        part = jnp.sum(dyv * n, axis=0, keepdims=True)

        @pl.when(pl.program_id(1) == 0)
        def _():
            dg_ref[0] = part

        @pl.when(pl.program_id(1) != 0)
        def _():
            dg_ref[0] += part

    row = pl.BlockSpec((1, tr, w_), lambda g, i: (g, i, 0))
    vec = pl.BlockSpec((1, 1, w_), lambda g, i: (g, 0, 0))
    operands = (x, gain, dy) + ((residual,) if has_res else ())
    return pl.pallas_call(
        body, name=name, grid=(g_, r_ // tr),
        in_specs=[row, vec, row] + ([row] if has_res else []),
        out_specs=(row, vec),
        out_shape=(jax.ShapeDtypeStruct((g_, r_, w_), F32), jax.ShapeDtypeStruct((g_, 1, w_), F32)),
        compiler_params=_params("parallel", "arbitrary"),
    )(*operands)


def matmul(a, b, name, out_dtype=F32, residual=None, scale=None, transpose_b=False):
    m_, k_ = a.shape
    n_ = b.shape[0] if transpose_b else b.shape[1]
    tm = _pick(m_, (512, 256, 128))
    tn = _pick(n_, (1408, 1024, 896, 512, 256, 128))
    has_res = residual is not None

    def body(*refs):
        if has_res:
            a_ref, b_ref, res_ref, o_ref = refs
        else:
            a_ref, b_ref, o_ref = refs
        av, bv = a_ref[...].astype(BF16), b_ref[...].astype(BF16)
        if transpose_b:
            acc = lax.dot_general(av, bv, CONTRACT_1, preferred_element_type=F32)
        else:
            acc = jnp.dot(av, bv, preferred_element_type=F32)
        if scale is not None:
            acc = acc * scale
        if has_res:
            acc = acc + res_ref[...]
        o_ref[...] = acc.astype(o_ref.dtype)

    out_spec = pl.BlockSpec((tm, tn), lambda j, i: (i, j))
    b_spec = pl.BlockSpec((tn, k_), lambda j, i: (j, 0)) if transpose_b else pl.BlockSpec((k_, tn), lambda j, i: (0, j))
    operands = (a, b) + ((residual,) if has_res else ())
    return pl.pallas_call(
        body, name=name, grid=(n_ // tn, m_ // tm),
        in_specs=[pl.BlockSpec((tm, k_), lambda j, i: (i, 0)), b_spec] + ([out_spec] if has_res else []),
        out_specs=out_spec,
        out_shape=jax.ShapeDtypeStruct((m_, n_), out_dtype),
        compiler_params=_params("parallel", "parallel"),
    )(*operands)


def matmul_tn(a, b, name, scale=None, out_dtype=F32):
    s_, k_ = a.shape
    n_ = b.shape[1]
    tk = _pick(k_, (1024, 1408, 896, 512, 256, 128))
    tn = _pick(n_, (1408, 1024, 896, 512, 256, 128))
    ts = _pick(s_, (2048, 1024, 512, 256, 128))
    ns = s_ // ts

    def body(a_ref, b_ref, o_ref, acc_ref):
        part = lax.dot_general(a_ref[...].astype(BF16), b_ref[...].astype(BF16), CONTRACT_0,
                               preferred_element_type=F32)
        step = pl.program_id(2)

        @pl.when(step == 0)
        def _():
            acc_ref[...] = part

        @pl.when(step != 0)
        def _():
            acc_ref[...] += part

        @pl.when(step == ns - 1)
        def _():
            acc = acc_ref[...]
            o_ref[...] = (acc if scale is None else acc * scale).astype(o_ref.dtype)

    return pl.pallas_call(
        body, name=name, grid=(k_ // tk, n_ // tn, ns),
        in_specs=[pl.BlockSpec((ts, tk), lambda i, j, s: (s, i)),
                  pl.BlockSpec((ts, tn), lambda i, j, s: (s, j))],
        out_specs=pl.BlockSpec((tk, tn), lambda i, j, s: (i, j)),
        out_shape=jax.ShapeDtypeStruct((k_, n_), out_dtype),
        scratch_shapes=[pltpu.VMEM((tk, tn), F32)],
        compiler_params=_params("parallel", "parallel", "arbitrary"),
    )(a, b)


def ffn_in(h, w_in_t, name, carry=None):
    s_, d_ = h.shape
    f_ = w_in_t.shape[0] // 2
    tm = _pick(s_, (512, 256, 128))
    tn = _pick(f_, (1408, 1024, 512, 256, 128))
    nb = f_ // tn

    def body(h_ref, wa_ref, wb_ref, a_ref, b_ref, u_ref):
        hv = h_ref[...]
        a = lax.dot_general(hv, wa_ref[...], CONTRACT_1, preferred_element_type=F32)
        b = lax.dot_general(hv, wb_ref[...], CONTRACT_1, preferred_element_type=F32)
        a_ref[...] = a.astype(BF16)
        b_ref[...] = b.astype(BF16)
        u_ref[...] = (a * _sigmoid(a) * b).astype(BF16)

    o_spec = pl.BlockSpec((tm, tn), lambda j, i: (i, j))
    out = jax.ShapeDtypeStruct((s_, f_), BF16)
    return _carried_call(
        body, name, (nb, s_ // tm),
        [pl.BlockSpec((tm, d_), lambda j, i: (i, 0)),
         pl.BlockSpec((tn, d_), lambda j, i: (j, 0)),
         pl.BlockSpec((tn, d_), lambda j, i: (j + nb, 0))],
        [o_spec, o_spec, o_spec], [out, out, out], [], [h, w_in_t, w_in_t], carry)


def _carried_call(body, name, grid, in_specs, out_specs, out_shape, scratch, operands, carry):
    if not carry:
        return pl.pallas_call(
            body, name=name, grid=grid, in_specs=in_specs, out_specs=tuple(out_specs), out_shape=tuple(out_shape),
            scratch_shapes=scratch, compiler_params=_params(*["parallel"] * len(grid)))(*operands)
    counts = (len(in_specs), len(carry[1]), len(out_specs), len(carry[1]), len(scratch), 3)

    def carrying(*refs):
        ins, c_in, outs, c_out, scr, sems = _split_refs(refs, counts)
        step_no, last_no = _grid_step(grid)

        @pl.when(step_no == 0)
        def _():
            for cp in _direct_copies(carry[0], c_in, c_out, *sems):
                cp.start()

        body(*ins, *outs, *scr)

        @pl.when(step_no == last_no)
        def _():
            for cp in _direct_copies(carry[0], c_in, c_out, *sems):
                cp.wait()

    c_in_specs, c_out_specs, c_shapes, c_sems = _exchange_operands(*carry)
    return pl.pallas_call(
        carrying, name=name, grid=grid, in_specs=in_specs + c_in_specs, out_specs=tuple(out_specs + c_out_specs),
        out_shape=tuple(out_shape + c_shapes), scratch_shapes=scratch + c_sems,
        compiler_params=_params(*["arbitrary"] * len(grid)))(*operands, *carry[1])


def matmul_pair(a1, a2, b, name, carry=None):
    m_, k_ = a1.shape
    n_ = b.shape[1]
    tm = _pick(m_, (512, 256, 128))
    tn = _pick(n_, (1024, 512, 256, 128))

    def body(a1_ref, a2_ref, b1_ref, b2_ref, o_ref):
        o_ref[...] = (jnp.dot(a1_ref[...], b1_ref[...], preferred_element_type=F32)
                      + jnp.dot(a2_ref[...], b2_ref[...], preferred_element_type=F32))

    a_spec = pl.BlockSpec((tm, k_), lambda j, i: (i, 0))
    return _carried_call(
        body, name, (n_ // tn, m_ // tm),
        [a_spec, a_spec, pl.BlockSpec((k_, tn), lambda j, i: (0, j)), pl.BlockSpec((k_, tn), lambda j, i: (1, j))],
        [pl.BlockSpec((tm, tn), lambda j, i: (i, j))], [jax.ShapeDtypeStruct((m_, n_), F32)], [], [a1, a2, b, b], carry)


def ffn_bwd_act(dy, w_out, a, b, name, carry=None):
    s_, d_ = dy.shape
    f_ = w_out.shape[0]
    tm = _pick(s_, (512, 256, 128))
    tn = _pick(f_, (1408, 1024, 512, 256, 128))

    def body(dy_ref, w_ref, a_ref, b_ref, da_ref, db_ref):
        du = 0.5 * lax.dot_general(dy_ref[...].astype(BF16), w_ref[...], CONTRACT_1, preferred_element_type=F32)
        av = a_ref[...].astype(F32)
        bv = b_ref[...].astype(F32)
        sig = _sigmoid(av)
        da_ref[...] = (du * bv * (sig * (1.0 + av * (1.0 - sig)))).astype(BF16)
        db_ref[...] = (du * (av * sig)).astype(BF16)

    t_spec = pl.BlockSpec((tm, tn), lambda j, i: (i, j))
    out = jax.ShapeDtypeStruct((s_, f_), BF16)
    return _carried_call(
        body, name, (f_ // tn, s_ // tm),
        [pl.BlockSpec((tm, d_), lambda j, i: (i, 0)), pl.BlockSpec((tn, d_), lambda j, i: (j, 0)), t_spec, t_spec],
        [t_spec, t_spec], [out, out], [], [dy, w_out, a, b], carry)


def _fold(a, b, n, forward):
    if forward:
        low = b <= a
        return jnp.where(low, a, n - 1 - a), jnp.where(low, b, b - a - 1)
    low = b < n - a
    return jnp.where(low, a, n - 1 - a), jnp.where(low, a + b, b - 1)


def _attn_grid(h_, n_outer, n_inner, causal, forward):
    if causal and n_outer % 2 == 0:
        return (h_, n_outer // 2, n_outer + 1), lambda a, b: _fold(a, b, n_outer, forward)
    return (h_, n_outer, n_inner), lambda a, b: (a, b)


QK_SCALE = 1.0 / math.sqrt(HEAD_DIM)
SUM_LANE = HEAD_DIM


def _scores(q, k, ck, masked):
    s = lax.dot_general(q, k, CONTRACT_1, preferred_element_type=F32)
    if ck is not None:
        s = s - ck
    if masked:
        row = lax.broadcasted_iota(jnp.int32, s.shape, 0)
        col = lax.broadcasted_iota(jnp.int32, s.shape, 1)
        s = jnp.where(col <= row, s, -jnp.inf)
    return s


def _widen_v(v):
    ones = jnp.ones(v.shape[:2] + (1,), BF16)
    zeros = jnp.zeros(v.shape[:2] + (LANES - HEAD_DIM - 1,), BF16)
    return jnp.concatenate([v.astype(BF16), ones, zeros], axis=-1)


def _grid_step(grid):
    step = 0
    for axis, n in enumerate(grid):
        step = step * n + pl.program_id(axis)
    return step, math.prod(grid) - 1


def _split_refs(refs, counts):
    out, at = [], 0
    for n in counts:
        out.append(refs[at:at + n])
        at += n
    return out


def attn_fwd(q, k, v_wide, ck, causal, name, carry=None):
    h_, sq, d_ = q.shape
    sk = k.shape[1]
    tq = _pick(sq, (1024, 512, 256, 128) if causal else (2048, 1024, 512, 256, 128))
    tk = tq if causal else _pick(sk, (512, 256, 128))
    nq, nk = sq // tq, sk // tk
    assert not causal or nq == 1 or nq % 2 == 0
    bias = ck is not None
    grid, blocks = _attn_grid(h_, nq, nk, causal, True)
    nc = len(carry[1]) if carry else 0

    def body(*refs):
        ins, c_in, outs, c_out, scratch, sems = _split_refs(refs, (4 if bias else 3, nc, 2, nc, 2, 3 if carry else 0))
        q_ref, k_ref, v_ref = ins[:3]
        ck_ref = ins[3] if bias else None
        (o_ref, lse_ref), (m_sc, acc_sc) = outs, scratch
        i, j = blocks(pl.program_id(1), pl.program_id(2))
        if carry:
            step_no, last_no = _grid_step(grid)

            @pl.when(step_no == 0)
            def _():
                for cp in _direct_copies(carry[0], c_in, c_out, *sems):
                    cp.start()

        @pl.when(j == 0)
        def _():
            m_sc[...] = jnp.full(m_sc.shape, -jnp.inf, F32)
            acc_sc[...] = jnp.zeros(acc_sc.shape, F32)

        def step(masked):
            s = _scores(q_ref[0], k_ref[0], ck_ref[0] if bias else None, masked)
            m_prev = m_sc[...]
            m_new = jnp.maximum(m_prev, jnp.max(s, axis=1, keepdims=True))
            p = jnp.exp(s - m_new)
            acc_sc[...] = jnp.exp(m_prev - m_new) * acc_sc[...] + jnp.dot(p.astype(BF16), v_ref[0],
                                                                         preferred_element_type=F32)
            m_sc[...] = m_new

        if causal:
            pl.when(j < i)(functools.partial(step, False))
            pl.when(j == i)(functools.partial(step, True))
        else:
            step(False)

        @pl.when(j == (i if causal else nk - 1))
        def _():
            acc = acc_sc[...]
            lane = lax.broadcasted_iota(jnp.int32, acc.shape, 1)
            l = jnp.sum(jnp.where(lane == SUM_LANE, acc, 0.0), axis=1, keepdims=True)
            o_ref[0] = acc_sc[:, :HEAD_DIM] / l
            lse_ref[0] = m_sc[...] + jnp.log(l)

        if carry:
            @pl.when(step_no == last_no)
            def _():
                for cp in _direct_copies(carry[0], c_in, c_out, *sems):
                    cp.wait()

    q_spec = pl.BlockSpec((1, tq, d_), lambda h, a, b: (h, blocks(a, b)[0], 0))
    q1_spec = pl.BlockSpec((1, tq, 1), lambda h, a, b: (h, blocks(a, b)[0], 0))
    k_spec = pl.BlockSpec((1, tk, d_), lambda h, a, b: (h, blocks(a, b)[1], 0))
    in_specs = [q_spec, k_spec, pl.BlockSpec((1, tk, LANES), lambda h, a, b: (h, blocks(a, b)[1], 0))]
    operands = [q, k, v_wide]
    if bias:
        in_specs.append(pl.BlockSpec((1, 1, tk), lambda h, a, b: (h, 0, blocks(a, b)[1])))
        operands.append(ck)
    out_specs = [q_spec, q1_spec]
    out_shape = [jax.ShapeDtypeStruct((h_, sq, d_), F32), jax.ShapeDtypeStruct((h_, sq, 1), F32)]
    scratch = [pltpu.VMEM((tq, 1), F32), pltpu.VMEM((tq, LANES), F32)]
    if carry:
        c_in_specs, c_out_specs, c_shapes, c_sems = _exchange_operands(*carry)
        in_specs, out_specs, out_shape, scratch = in_specs + c_in_specs, out_specs + c_out_specs, out_shape + c_shapes, scratch + c_sems
        operands = operands + list(carry[1])
    return pl.pallas_call(
        body, name=name, grid=grid,
        in_specs=in_specs, out_specs=tuple(out_specs), out_shape=tuple(out_shape), scratch_shapes=scratch,
        compiler_params=_params("arbitrary" if carry else "parallel", "arbitrary", "arbitrary"),
    )(*operands)


def attn_bwd(q, k, v, ck, o, lse, do, causal, name, carry=None):
    h_, sq, d_ = q.shape
    sk = k.shape[1]
    tq = _pick(sq, (1024, 512, 256, 128) if causal else (2048, 1024, 512, 256, 128))
    tk = tq if causal else _pick(sk, (512, 256, 128))
    nq, nk = sq // tq, sk // tk
    assert not causal or nq == 1 or nq % 2 == 0
    bias = ck is not None
    grid, blocks = _attn_grid(h_, nk, nq, causal, False)
    nc = len(carry[1]) if carry else 0

    def body(*refs):
        ins, c_in, outs, c_out, scratch, sems = _split_refs(
            refs, (7 if bias else 6, nc, 5 if bias else 3, nc, 3 if bias else 2, 3 if carry else 0))
        if bias:
            q_ref, k_ref, v_ref, ck_ref, o_ref, lse_ref, do_ref = ins
            dq_ref, dk_ref, dv_ref, dc_ref, dcq_ref = outs
            dk_sc, dv_sc, dc_sc = scratch
        else:
            q_ref, k_ref, v_ref, o_ref, lse_ref, do_ref = ins
            dq_ref, dk_ref, dv_ref = outs
            dk_sc, dv_sc = scratch
        j, i = blocks(pl.program_id(1), pl.program_id(2))
        if carry:
            step_no, last_no = _grid_step(grid)

            @pl.when(step_no == 0)
            def _():
                for cp in _direct_copies(carry[0], c_in, c_out, *sems):
                    cp.start()

        @pl.when((pl.program_id(1) == 0) & (pl.program_id(2) == 0))
        def _():
            dq_ref[...] = jnp.zeros(dq_ref.shape, F32)
            if bias:
                dcq_ref[...] = jnp.zeros(dcq_ref.shape, F32)

        @pl.when(i == (j if causal else 0))
        def _():
            dk_sc[...] = jnp.zeros(dk_sc.shape, F32)
            dv_sc[...] = jnp.zeros(dv_sc.shape, F32)
            if bias:
                dc_sc[...] = jnp.zeros(dc_sc.shape, F32)

        def step(masked):
            qb, kb = q_ref[0], k_ref[0]
            dof = do_ref[0]
            dob = dof.astype(BF16)
            s = _scores(qb, kb, ck_ref[0] if bias else None, masked)
            p = jnp.exp(s - lse_ref[0])
            dp = lax.dot_general(dob, v_ref[0].astype(BF16), CONTRACT_1, preferred_element_type=F32)
            delta = jnp.sum(dof * o_ref[0], axis=1, keepdims=True)
            ds = p * (dp - delta)
            dsb = ds.astype(BF16)
            dv_sc[...] += lax.dot_general(p.astype(BF16), dob, CONTRACT_0, preferred_element_type=F32)
            dk_sc[...] += lax.dot_general(dsb, qb, CONTRACT_0, preferred_element_type=F32)
            rows = pl.ds(pl.multiple_of(i * tq, tq), tq)
            dq_ref[0, rows, :] += jnp.dot(dsb, kb, preferred_element_type=F32)
            if bias:
                dc_sc[...] -= jnp.sum(ds, axis=0, keepdims=True)
                dcq_ref[0, rows, :] += jnp.sum(ds, axis=1, keepdims=True)

        if causal:
            pl.when(i > j)(functools.partial(step, False))
            pl.when(i == j)(functools.partial(step, True))
        else:
            step(False)

        @pl.when(i == nq - 1)
        def _():
            dk_ref[0] = dk_sc[...]
            dv_ref[0] = dv_sc[...]
            if bias:
                dc_ref[0] = dc_sc[...]

        if carry:
            @pl.when(step_no == last_no)
            def _():
                for cp in _direct_copies(carry[0], c_in, c_out, *sems):
                    cp.wait()

    q_spec = pl.BlockSpec((1, tq, d_), lambda h, a, b: (h, blocks(a, b)[1], 0))
    q1_spec = pl.BlockSpec((1, tq, 1), lambda h, a, b: (h, blocks(a, b)[1], 0))
    k_spec = pl.BlockSpec((1, tk, d_), lambda h, a, b: (h, blocks(a, b)[0], 0))
    c_spec = pl.BlockSpec((1, 1, tk), lambda h, a, b: (h, 0, blocks(a, b)[0]))
    in_specs = [q_spec, k_spec, k_spec] + ([c_spec] if bias else []) + [q_spec, q1_spec, q_spec]
    operands = [q, k, v] + ([ck] if bias else []) + [o, lse, do]
    out_specs = [pl.BlockSpec((1, sq, d_), lambda h, a, b: (h, 0, 0)), k_spec, k_spec]
    out_shape = [jax.ShapeDtypeStruct((h_, sq, d_), F32), jax.ShapeDtypeStruct((h_, sk, d_), F32),
                 jax.ShapeDtypeStruct((h_, sk, d_), F32)]
    scratch = [pltpu.VMEM((tk, d_), F32), pltpu.VMEM((tk, d_), F32)]
    if bias:
        out_specs += [c_spec, pl.BlockSpec((1, sq, 1), lambda h, a, b: (h, 0, 0))]
        out_shape += [jax.ShapeDtypeStruct((h_, 1, sk), F32), jax.ShapeDtypeStruct((h_, sq, 1), F32)]
        scratch.append(pltpu.VMEM((1, tk), F32))
    if carry:
        c_in_specs, c_out_specs, c_shapes, c_sems = _exchange_operands(*carry)
        in_specs, out_specs, out_shape, scratch = in_specs + c_in_specs, out_specs + c_out_specs, out_shape + c_shapes, scratch + c_sems
        operands = operands + list(carry[1])
    return pl.pallas_call(
        body, name=name, grid=grid,
        in_specs=in_specs, out_specs=tuple(out_specs), out_shape=tuple(out_shape),
        scratch_shapes=scratch,
        compiler_params=_params("arbitrary" if carry else "parallel", "arbitrary", "arbitrary"),
    )(*operands)


def _tri(lower):
    row = lax.broadcasted_iota(jnp.int32, (CHUNK, CHUNK), 0)
    col = lax.broadcasted_iota(jnp.int32, (CHUNK, CHUNK), 1)
    return jnp.where((col <= row) if lower else (col >= row), 1.0, 0.0).astype(F32)


def fox_gate_fwd(f, b, name):
    s_ = f.shape[0]

    def body(f_ref, b_ref, c_ref, carry):
        @pl.when(pl.program_id(0) == 0)
        def _():
            carry[...] = jnp.zeros(carry.shape, F32)

        xv = f_ref[...] + b_ref[...]
        log_f = jnp.minimum(xv, 0.0) - jnp.log(1.0 + jnp.exp(-jnp.abs(xv)))
        c = jnp.dot(_tri(True), log_f, precision=lax.Precision.HIGHEST, preferred_element_type=F32) + carry[...]
        c_ref[...] = c
        carry[...] = c[CHUNK - 1:CHUNK, :]

    blk = pl.BlockSpec((CHUNK, LANES), lambda i: (i, 0))
    return pl.pallas_call(
        body, name=name, grid=(s_ // CHUNK,),
        in_specs=[blk, pl.BlockSpec((1, LANES), lambda i: (0, 0))], out_specs=blk,
        out_shape=jax.ShapeDtypeStruct((s_, LANES), F32),
        scratch_shapes=[pltpu.VMEM((1, LANES), F32)],
        compiler_params=_params("arbitrary"),
    )(f, b)


def fox_gate_bwd(dc, f, b, name):
    s_ = f.shape[0]
    n = s_ // CHUNK

    def body(dc_ref, f_ref, b_ref, df_ref, db_ref, carry):
        @pl.when(pl.program_id(0) == 0)
        def _():
            carry[...] = jnp.zeros(carry.shape, F32)
            db_ref[...] = jnp.zeros(db_ref.shape, F32)

        dlog = jnp.dot(_tri(False), dc_ref[...], precision=lax.Precision.HIGHEST,
                       preferred_element_type=F32) + carry[...]
        df = dlog * _sigmoid(-(f_ref[...] + b_ref[...]))
        df_ref[...] = df
        db_ref[...] += jnp.sum(df, axis=0, keepdims=True)
        carry[...] = dlog[0:1, :]

    blk = pl.BlockSpec((CHUNK, LANES), lambda i: (n - 1 - i, 0))
    vec = pl.BlockSpec((1, LANES), lambda i: (0, 0))
    return pl.pallas_call(
        body, name=name, grid=(n,),
        in_specs=[blk, blk, vec], out_specs=(blk, vec),
        out_shape=(jax.ShapeDtypeStruct((s_, LANES), F32), jax.ShapeDtypeStruct((1, LANES), F32)),
        scratch_shapes=[pltpu.VMEM((1, LANES), F32)],
        compiler_params=_params("arbitrary"),
    )(dc, f, b)


GELU_K = math.sqrt(2.0 / math.pi)
GELU_C = 0.044715


def _gelu(x):
    return 0.5 * x * (1.0 + jnp.tanh(GELU_K * (x + GELU_C * (x * x * x))))


def _gelu_grad(x):
    t = jnp.tanh(GELU_K * (x + GELU_C * (x * x * x)))
    return 0.5 * (1.0 + t) + 0.5 * x * (1.0 - t * t) * (GELU_K * (1.0 + 3.0 * GELU_C * (x * x)))


def _tril_mask():
    row = lax.broadcasted_iota(jnp.int32, (CHUNK, CHUNK), 0)
    col = lax.broadcasted_iota(jnp.int32, (CHUNK, CHUNK), 1)
    return col <= row


def _gmlp_specs(s_, ts):
    row = pl.BlockSpec((1, ts, HEAD_DIM), lambda g, i: (g, i, 0))
    gain = pl.BlockSpec((1, 1, HEAD_DIM), lambda g, i: (g, 0, 0))
    w = pl.BlockSpec((1, CHUNK, CHUNK), lambda g, i: (g, 0, 0))
    b = pl.BlockSpec((1, CHUNK, 1), lambda g, i: (g, 0, 0))
    return row, gain, w, b


def gmlp_fwd(pu, pv, gain, w, b, name):
    g_, s_, d_ = pu.shape
    ts = _pick(s_, (2048, 1024, 512, 256, 128))

    def body(pu_ref, pv_ref, g_ref, w_ref, b_ref, o_ref):
        v = _gelu(pv_ref[0])
        r = lax.rsqrt(jnp.mean(v * v, axis=-1, keepdims=True) + EPS)
        vn = (v * r * g_ref[0]).astype(BF16)
        wt = jnp.where(_tril_mask(), w_ref[0], 0.0).astype(BF16)
        for c in range(ts // CHUNK):
            rows = pl.ds(c * CHUNK, CHUNK)
            gate = jnp.dot(wt, vn[c * CHUNK:(c + 1) * CHUNK], preferred_element_type=F32) + b_ref[0]
            o_ref[0, rows, :] = _gelu(pu_ref[0, rows, :]) * gate

    row, gspec, wspec, bspec = _gmlp_specs(s_, ts)
    return pl.pallas_call(
        body, name=name, grid=(g_, s_ // ts),
        in_specs=[row, row, gspec, wspec, bspec], out_specs=row,
        out_shape=jax.ShapeDtypeStruct((g_, s_, d_), F32),
        compiler_params=_params("parallel", "parallel"),
    )(pu, pv, gain, w, b)


def gmlp_bwd(pu, pv, gain, w, b, dout, name):
    g_, s_, d_ = pu.shape
    ts = _pick(s_, (2048, 1024, 512, 256, 128))

    def body(pu_ref, pv_ref, g_ref, w_ref, b_ref, do_ref, dpu_ref, dpv_ref, dw_ref, db_ref, dg_ref):
        @pl.when(pl.program_id(1) == 0)
        def _():
            dw_ref[...] = jnp.zeros(dw_ref.shape, F32)
            db_ref[...] = jnp.zeros(db_ref.shape, F32)
            dg_ref[...] = jnp.zeros(dg_ref.shape, F32)

        gain_v = g_ref[0]
        mask = _tril_mask()
        wt = jnp.where(mask, w_ref[0], 0.0).astype(BF16)
        dw = jnp.zeros((CHUNK, CHUNK), F32)
        db = jnp.zeros((CHUNK, 1), F32)
        dg = jnp.zeros((1, d_), F32)
        for c in range(ts // CHUNK):
            rows = pl.ds(c * CHUNK, CHUNK)
            pu_c = pu_ref[0, rows, :]
            pv_c = pv_ref[0, rows, :]
            do_c = do_ref[0, rows, :]
            u = _gelu(pu_c)
            v = _gelu(pv_c)
            r = lax.rsqrt(jnp.mean(v * v, axis=-1, keepdims=True) + EPS)
            n = v * r
            vn = (n * gain_v).astype(BF16)
            gate = jnp.dot(wt, vn, preferred_element_type=F32) + b_ref[0]
            dgate = do_c * u
            dgate_b = dgate.astype(BF16)
            dpu_ref[0, rows, :] = do_c * gate * _gelu_grad(pu_c)
            db = db + jnp.sum(dgate, axis=1, keepdims=True)
            dw = dw + lax.dot_general(dgate_b, vn, CONTRACT_1, preferred_element_type=F32)
            dvn = lax.dot_general(wt, dgate_b, CONTRACT_0, preferred_element_type=F32)
            dg = dg + jnp.sum(dvn * n, axis=0, keepdims=True)
            dn = dvn * gain_v
            dv = r * (dn - n * jnp.mean(dn * n, axis=-1, keepdims=True))
            dpv_ref[0, rows, :] = dv * _gelu_grad(pv_c)
        dw_ref[0] += jnp.where(mask, dw, 0.0)
        db_ref[0] += db
        dg_ref[0] += dg

    row, gspec, wspec, bspec = _gmlp_specs(s_, ts)
    return pl.pallas_call(
        body, name=name, grid=(g_, s_ // ts),
        in_specs=[row, row, gspec, wspec, bspec, row],
        out_specs=(row, row, wspec, bspec, gspec),
        out_shape=(jax.ShapeDtypeStruct((g_, s_, d_), F32), jax.ShapeDtypeStruct((g_, s_, d_), F32),
                   jax.ShapeDtypeStruct((g_, CHUNK, CHUNK), F32), jax.ShapeDtypeStruct((g_, CHUNK, 1), F32),
                   jax.ShapeDtypeStruct((g_, 1, d_), F32)),
        compiler_params=_params("parallel", "arbitrary"),
    )(pu, pv, gain, w, b, dout)


def loss_head(y, t, name):
    s_, d_ = y.shape
    tr = _pick(s_, (1024, 512, 256, 128))

    def body(y_ref, t_ref, dy_ref, l_ref):
        err = y_ref[...] - t_ref[...]
        dy_ref[...] = err * (1.0 / d_)
        part = jnp.full(l_ref.shape, jnp.sum(err * err) * (0.5 / d_), F32)

        @pl.when(pl.program_id(0) == 0)
        def _():
            l_ref[...] = part

        @pl.when(pl.program_id(0) != 0)
        def _():
            l_ref[...] += part

    blk = pl.BlockSpec((tr, d_), lambda i: (i, 0))
    dy, l = pl.pallas_call(
        body, name=name, grid=(s_ // tr,),
        in_specs=[blk, blk], out_specs=(blk, pl.BlockSpec((8, LANES), lambda i: (0, 0))),
        out_shape=(jax.ShapeDtypeStruct((s_, d_), F32), jax.ShapeDtypeStruct((8, LANES), F32)),
        compiler_params=_params("arbitrary"),
    )(y, t)
    return dy, l[0, 0]


def adamw(w, g_slots, m, v, name):
    r_, c_ = w.shape
    tr = _pick(r_, (1024, 512, 352, 256, 224, 128, 64, 32, 16, 8))

    def body(w_ref, gs_ref, m_ref, v_ref, g_ref, d_ref, nm_ref, nv_ref):
        g = gs_ref[0].astype(F32)
        for k in range(1, N_DEV):
            g = g + gs_ref[k].astype(F32)
        m_new = ADAM_B1 * m_ref[...] + (1.0 - ADAM_B1) * g
        v_new = ADAM_B2 * v_ref[...] + (1.0 - ADAM_B2) * (g * g)
        m_hat = m_new / (1.0 - ADAM_B1 ** ADAM_STEP)
        v_hat = v_new / (1.0 - ADAM_B2 ** ADAM_STEP)
        g_ref[...] = g
        d_ref[...] = -ADAM_LR * (m_hat / (jnp.sqrt(v_hat) + ADAM_EPS) + ADAM_WD * w_ref[...])
        nm_ref[...] = m_new
        nv_ref[...] = v_new

    blk = pl.BlockSpec((tr, c_), lambda i: (i, 0))
    out = jax.ShapeDtypeStruct((r_, c_), F32)
    return pl.pallas_call(
        body, name=name, grid=(r_ // tr,),
        in_specs=[blk, pl.BlockSpec((N_DEV, tr, c_), lambda i: (0, i, 0)), blk, blk],
        out_specs=(blk, blk, blk, blk), out_shape=(out, out, out, out),
        compiler_params=_params("parallel"),
    )(w, g_slots, m, v)


def _position():
    x, y, c = lax.axis_index("x"), lax.axis_index("y"), lax.axis_index("c")
    return x, y, c


def _slot(px, py, pc):
    return 4 * px + 2 * py + pc


def all_gather_multi(blocks, name):
    n = len(blocks)

    def body(*refs):
        x_refs, out_refs = refs[:n], refs[n:2 * n]
        send_sems, recv_sems, local_sems = refs[2 * n:]
        x, y, c = _position()
        me, sibling = (x, y, c), (x, y, 1 - c)
        chips = [(1 - x, y), (x, 1 - y), (1 - x, 1 - y)]

        def copy(b, k, owner, to, src=None):
            dst = out_refs[b].at[_slot(*owner)]
            return pltpu.make_async_remote_copy(
                src_ref=dst if src is None else src, dst_ref=dst,
                send_sem=send_sems.at[b, k], recv_sem=recv_sems.at[b, k], device_id=to, device_id_type=MESH)

        mine = [pltpu.make_async_copy(x_refs[b], out_refs[b].at[_slot(*me)], local_sems.at[b]) for b in range(n)]
        for cp in mine:
            cp.start()
        first = [copy(b, 1 + j, me, (*chip, c), src=x_refs[b]) for j, chip in enumerate(chips) for b in range(n)]
        first += [copy(b, 0, me, sibling, src=x_refs[b]) for b in range(n)]
        for cp in first:
            cp.start()
        passed = []
        for j, chip in enumerate(chips):
            for b in range(n):
                copy(b, 1 + j, (*chip, c), me).wait_recv()
                cp = copy(b, 4 + j, (*chip, c), sibling)
                cp.start()
                passed.append(cp)
        for b in range(n):
            copy(b, 0, sibling, me).wait_recv()
        for j, chip in enumerate(chips):
            for b in range(n):
                copy(b, 4 + j, (*chip, 1 - c), me).wait_recv()
        for cp in first + passed:
            cp.wait_send()
        for cp in mine:
            cp.wait()

    any_spec = pl.BlockSpec(memory_space=pl.ANY)
    return pl.pallas_call(
        body, name=name,
        in_specs=[any_spec] * n, out_specs=tuple([any_spec] * n),
        out_shape=tuple(jax.ShapeDtypeStruct((N_DEV,) + blk.shape, blk.dtype) for blk in blocks),
        scratch_shapes=[pltpu.SemaphoreType.DMA((n, 7)), pltpu.SemaphoreType.DMA((n, 7)), pltpu.SemaphoreType.DMA((n,))],
    )(*blocks)


def _direct_copies(scatter, in_refs, out_refs, send_sems, recv_sems, local_sems):
    x, y, c = _position()
    me = _slot(x, y, c)
    src = (lambda ref, slot: ref.at[slot]) if scatter else (lambda ref, slot: ref)
    copies = [pltpu.make_async_copy(src(ref, me), out.at[me], local_sems.at[b])
              for b, (ref, out) in enumerate(zip(in_refs, out_refs))]
    for k in range(1, N_DEV):
        px = 1 - x if k & 4 else x
        py = 1 - y if k & 2 else y
        pc = 1 - c if k & 1 else c
        copies += [pltpu.make_async_remote_copy(
            src_ref=src(ref, _slot(px, py, pc)), dst_ref=out.at[me],
            send_sem=send_sems.at[b, k - 1], recv_sem=recv_sems.at[b, k - 1],
            device_id=(px, py, pc), device_id_type=MESH) for b, (ref, out) in enumerate(zip(in_refs, out_refs))]
    return copies


def _exchange_operands(scatter, bufs):
    n = len(bufs)
    any_spec = pl.BlockSpec(memory_space=pl.ANY)
    shapes = [jax.ShapeDtypeStruct(b.shape if scatter else (N_DEV,) + b.shape, b.dtype) for b in bufs]
    sems = [pltpu.SemaphoreType.DMA((n, 7)), pltpu.SemaphoreType.DMA((n, 7)), pltpu.SemaphoreType.DMA((n,))]
    return [any_spec] * n, [any_spec] * n, shapes, sems


def _seg_len(n):
    return -(-n // SEG_ALIGN) * SEG_ALIGN


def _pack(arrays, lead=()):
    parts, total = [], 0
    for a in arrays:
        flat = a.reshape(lead + (-1,))
        pad = _seg_len(flat.shape[-1]) - flat.shape[-1]
        parts.append(jnp.pad(flat, [(0, 0)] * len(lead) + [(0, pad)]) if pad else flat)
        total += flat.shape[-1] + pad
    tail = -(-total // PACK_ALIGN) * PACK_ALIGN - total
    if tail:
        parts.append(jnp.zeros(lead + (tail,), parts[0].dtype))
    return jnp.concatenate(parts, axis=-1).reshape(lead + (-1, LANES))


def _unpack(packed, shapes, lead=()):
    flat = packed.reshape(lead + (-1,))
    out, off = [], 0
    for shp in shapes:
        n = math.prod(shp)
        out.append(flat[..., off:off + n].reshape(lead + tuple(shp)))
        off += _seg_len(n)
    return out


def _heads(a, n_heads):
    return a.reshape(a.shape[0], n_heads, HEAD_DIM).transpose(1, 0, 2)


def _unheads(a):
    return a.transpose(1, 0, 2).reshape(a.shape[1], a.shape[0] * HEAD_DIM)


def _head_gain(g, n_heads):
    return jnp.broadcast_to(g.reshape(1, 1, HEAD_DIM), (n_heads, 1, HEAD_DIM))


def _ffn_forward(x, gain, w_in_t, w_out, tag, carry=None):
    h = rmsnorm_fwd(x[None], gain.reshape(1, 1, -1), BF16, f"{tag}_norm")[0]
    a, b, u, *carried = ffn_in(h, w_in_t, f"{tag}_in", carry)
    if carry:
        w_out = w_out(carried)
    y = matmul(u, w_out, f"{tag}_out", residual=x, scale=0.5)
    return y, (x, h, a, b, u), carried


def _ffn_backward(dy, saved, gain, w_in_t, w_out, tag, carry_of=None):
    x, h, a, b, u = saved
    dw_out = matmul_tn(u, dy, f"{tag}_dwout", scale=0.5, out_dtype=BF16)
    da, db, *carried_1 = ffn_bwd_act(dy, w_out, a, b, f"{tag}_dact", carry_of(dw_out, None) if carry_of else None)
    dw_in_t = jnp.concatenate([matmul_tn(da, h, f"{tag}_dwin_a", out_dtype=BF16),
                               matmul_tn(db, h, f"{tag}_dwin_b", out_dtype=BF16)], axis=0)
    dh, *carried_2 = matmul_pair(da, db, w_in_t, f"{tag}_dh", carry_of(None, dw_in_t) if carry_of else None)
    dx, dgain = rmsnorm_bwd(x[None], gain.reshape(1, 1, -1), dh[None], f"{tag}_dnorm", residual=dy[None])
    return dx[0], dgain.reshape(-1), dw_in_t, dw_out, carried_1 + carried_2


def _mem_forward(mq, mem_n, w_kv, g_q, g_k, tag):
    gq = _head_gain(g_q, MEM_HEADS)
    gk = _head_gain(g_k, MEM_HEADS)
    qn = rmsnorm_fwd(mq, gq * QK_SCALE, BF16, f"{tag}_qnorm")
    kv = matmul(mem_n, w_kv, f"{tag}_kv")
    k = _heads(kv[:, :MEM_WIDTH], MEM_HEADS)
    v = _heads(kv[:, MEM_WIDTH:], MEM_HEADS)
    kn = rmsnorm_fwd(k, gk, BF16, f"{tag}_knorm")
    o, lse = attn_fwd(qn, kn, _widen_v(v), None, False, f"{tag}_attn")
    return o, (mq, gq, gk, qn, k, kn, v, o, lse)


def _mem_backward(do, saved, mem_n, w_kv, tag):
    mq, gq, gk, qn, k, kn, v, o, lse = saved
    dqn, dkn, dv = attn_bwd(qn, kn, v, None, o, lse, do, False, f"{tag}_dattn")
    dmq, dgq = rmsnorm_bwd(mq, gq, dqn * QK_SCALE, f"{tag}_dqnorm")
    dk, dgk = rmsnorm_bwd(k, gk, dkn, f"{tag}_dknorm")
    dkv = jnp.concatenate([_unheads(dk), _unheads(dv)], axis=1)
    dw_kv = matmul_tn(mem_n, dkv, f"{tag}_dwkv", out_dtype=BF16)
    dmem_n = matmul(dkv, w_kv, f"{tag}_dmem", transpose_b=True)
    return dmq, dgq.sum(axis=0).reshape(-1), dgk.sum(axis=0).reshape(-1), dw_kv, dmem_n


def _fox_split(w_in):
    t3 = 3 * TOK_WIDTH
    pad = jnp.zeros(w_in.shape[:-1] + (LANES - FOX_HEADS,), w_in.dtype)
    return jnp.concatenate([w_in[..., :t3], w_in[..., t3 + FOX_HEADS:], w_in[..., t3:t3 + FOX_HEADS], pad], axis=-1)


def _fox_unsplit(w):
    t3 = 3 * TOK_WIDTH
    return jnp.concatenate([w[..., :t3], w[..., t3 + MEM_WIDTH:t3 + MEM_WIDTH + FOX_HEADS], w[..., t3:t3 + MEM_WIDTH]],
                           axis=-1)


def _fox_forward(h, w_split, b_f, g_q, g_k, tag, carry=None):
    t3 = 3 * TOK_WIDTH
    proj = matmul(h, w_split, f"{tag}_proj")
    qkv = _heads(proj[:, :t3], 3 * FOX_HEADS)
    mq = _heads(proj[:, t3:t3 + MEM_WIDTH], MEM_HEADS)
    f_pad = proj[:, t3 + MEM_WIDTH:]
    b_pad = jnp.pad(b_f.reshape(1, -1), ((0, 0), (0, LANES - FOX_HEADS)))
    gains = jnp.concatenate([_head_gain(g_q, FOX_HEADS), _head_gain(g_k, FOX_HEADS)], axis=0)
    qk = qkv[:2 * FOX_HEADS]
    scaled = jnp.concatenate([gains[:FOX_HEADS] * QK_SCALE, gains[FOX_HEADS:]], axis=0)
    qkn = rmsnorm_fwd(qk, scaled, BF16, f"{tag}_qknorm")
    v = qkv[2 * FOX_HEADS:]
    c = fox_gate_fwd(f_pad, b_pad, f"{tag}_gate")
    ck = c[:, :FOX_HEADS].T[:, None, :]
    qn, kn = qkn[:FOX_HEADS], qkn[FOX_HEADS:]
    o, lse, *carried = attn_fwd(qn, kn, _widen_v(v), ck, True, f"{tag}_attn", carry)
    return o, mq, (qk, gains, qn, kn, v, ck, o, lse, f_pad, b_pad), carried


def _fox_backward(do, dmq, saved, tag, carry=None):
    qk, gains, qn, kn, v, ck, o, lse, f_pad, b_pad = saved
    dqn, dkn, dv, dck, dcq, *carried = attn_bwd(qn, kn, v, ck, o, lse, do, True, f"{tag}_dattn", carry)
    dqk, dgains = rmsnorm_bwd(qk, gains, jnp.concatenate([dqn * QK_SCALE, dkn], axis=0), f"{tag}_dqknorm")
    dc = jnp.pad((dck[:, 0, :] + dcq[:, :, 0]).T, ((0, 0), (0, LANES - FOX_HEADS)))
    df, db = fox_gate_bwd(dc, f_pad, b_pad, f"{tag}_dgate")
    dproj = jnp.concatenate([_unheads(dqk), _unheads(dv), _unheads(dmq), df], axis=1)
    dgains = dgains.reshape(2, FOX_HEADS, HEAD_DIM).sum(axis=1)
    return dproj, db[0, :FOX_HEADS], dgains[0], dgains[1], carried


def _gmlp_forward(h, w_in_t, v_gain, w_s, b_s, tag):
    proj = matmul(h, w_in_t, f"{tag}_proj", transpose_b=True)
    pu = _heads(proj[:, :TOK_WIDTH], FOX_HEADS)
    pv = _heads(proj[:, TOK_WIDTH:2 * TOK_WIDTH], FOX_HEADS)
    mq = _heads(proj[:, 2 * TOK_WIDTH:], MEM_HEADS)
    gain = v_gain.reshape(FOX_HEADS, 1, HEAD_DIM)
    b = b_s[:, :, None]
    o = gmlp_fwd(pu, pv, gain, w_s, b, f"{tag}_sgu")
    return o, mq, (pu, pv, gain, w_s, b)


def _gmlp_backward(do, dmq, saved, tag):
    pu, pv, gain, w_s, b = saved
    dpu, dpv, dw, db, dg = gmlp_bwd(pu, pv, gain, w_s, b, do, f"{tag}_dsgu")
    dproj = jnp.concatenate([_unheads(dpu), _unheads(dpv), _unheads(dmq)], axis=1)
    return dproj, dg.reshape(-1), dw, db[:, :, 0]


BIG = ("ffn1_w_in", "ffn1_w_out", "ffn2_w_in", "ffn2_w_out", "w_out", "mem_w_kv", "fox_w_in", "gmlp_w_in")
COLUMN_SHARDED = ("ffn1_w_in", "ffn2_w_in", "gmlp_w_in")
REPLICATED =("norm_ffn1", "norm_mix", "norm_ffn2", "mem_norm", "mem_q_norm", "mem_k_norm", "fox_b_f",
              "fox_q_norm", "fox_k_norm", "gmlp_w_s", "gmlp_b_s")
WEIGHTS = ("norm_ffn1", "ffn1_w_in", "ffn1_w_out", "norm_mix", "norm_ffn2", "ffn2_w_in", "ffn2_w_out", "w_out",
           "mem_norm", "mem_w_kv", "mem_q_norm", "mem_k_norm", "fox_w_in", "fox_b_f", "fox_q_norm", "fox_k_norm",
           "gmlp_w_in", "gmlp_v_norm", "gmlp_w_s", "gmlp_b_s")


def _to_transport(name, a):
    if name in COLUMN_SHARDED:
        return jnp.swapaxes(a, -1, -2)
    return _fox_split(a) if name == "fox_w_in" else a


def _from_transport(name, a):
    if name in COLUMN_SHARDED:
        return jnp.swapaxes(a, -1, -2)
    return _fox_unsplit(a) if name == "fox_w_in" else a


def kernel(x, mem, norm_ffn1, ffn1_w_in, ffn1_w_out, norm_mix, norm_ffn2, ffn2_w_in, ffn2_w_out, w_out, mem_norm, mem_w_kv, mem_q_norm, mem_k_norm, fox_w_in, fox_b_f, fox_q_norm, fox_k_norm, gmlp_w_in, gmlp_v_norm, gmlp_w_s, gmlp_b_s, loss_target, m_norm_ffn1, m_ffn1_w_in, m_ffn1_w_out, m_norm_mix, m_norm_ffn2, m_ffn2_w_in, m_ffn2_w_out, m_w_out, m_mem_norm, m_mem_w_kv, m_mem_q_norm, m_mem_k_norm, m_fox_w_in, m_fox_b_f, m_fox_q_norm, m_fox_k_norm, m_gmlp_w_in, m_gmlp_v_norm, m_gmlp_w_s, m_gmlp_b_s, v_norm_ffn1, v_ffn1_w_in, v_ffn1_w_out, v_norm_mix, v_norm_ffn2, v_ffn2_w_in, v_ffn2_w_out, v_w_out, v_mem_norm, v_mem_w_kv, v_mem_q_norm, v_mem_k_norm, v_fox_w_in, v_fox_b_f, v_fox_q_norm, v_fox_k_norm, v_gmlp_w_in, v_gmlp_v_norm, v_gmlp_w_s, v_gmlp_b_s):
    w = dict(norm_ffn1=norm_ffn1, ffn1_w_in=ffn1_w_in, ffn1_w_out=ffn1_w_out, norm_mix=norm_mix, norm_ffn2=norm_ffn2, ffn2_w_in=ffn2_w_in, ffn2_w_out=ffn2_w_out, w_out=w_out, mem_norm=mem_norm, mem_w_kv=mem_w_kv, mem_q_norm=mem_q_norm, mem_k_norm=mem_k_norm, fox_w_in=fox_w_in, fox_b_f=fox_b_f, fox_q_norm=fox_q_norm, fox_k_norm=fox_k_norm, gmlp_w_in=gmlp_w_in, gmlp_v_norm=gmlp_v_norm, gmlp_w_s=gmlp_w_s, gmlp_b_s=gmlp_b_s)
    m = dict(norm_ffn1=m_norm_ffn1, ffn1_w_in=m_ffn1_w_in, ffn1_w_out=m_ffn1_w_out, norm_mix=m_norm_mix, norm_ffn2=m_norm_ffn2, ffn2_w_in=m_ffn2_w_in, ffn2_w_out=m_ffn2_w_out, w_out=m_w_out, mem_norm=m_mem_norm, mem_w_kv=m_mem_w_kv, mem_q_norm=m_mem_q_norm, mem_k_norm=m_mem_k_norm, fox_w_in=m_fox_w_in, fox_b_f=m_fox_b_f, fox_q_norm=m_fox_q_norm, fox_k_norm=m_fox_k_norm, gmlp_w_in=m_gmlp_w_in, gmlp_v_norm=m_gmlp_v_norm, gmlp_w_s=m_gmlp_w_s, gmlp_b_s=m_gmlp_b_s)
    v = dict(norm_ffn1=v_norm_ffn1, ffn1_w_in=v_ffn1_w_in, ffn1_w_out=v_ffn1_w_out, norm_mix=v_norm_mix, norm_ffn2=v_norm_ffn2, ffn2_w_in=v_ffn2_w_in, ffn2_w_out=v_ffn2_w_out, w_out=v_w_out, mem_norm=v_mem_norm, mem_w_kv=v_mem_w_kv, mem_q_norm=v_mem_q_norm, mem_k_norm=v_mem_k_norm, fox_w_in=v_fox_w_in, fox_b_f=v_fox_b_f, fox_q_norm=v_fox_q_norm, fox_k_norm=v_fox_k_norm, gmlp_w_in=v_gmlp_w_in, gmlp_v_norm=v_gmlp_v_norm, gmlp_w_s=v_gmlp_w_s, gmlp_b_s=v_gmlp_b_s)

    depth = norm_ffn1.shape[0]
    x0 = x[0]
    mem0 = mem[0]
    target = loss_target[0]
    me = _slot(*_position())

    keys = [(n, i) for n in BIG for i in range(w[n].shape[0])]
    local = {k: _to_transport(k[0], w[k[0]][k[1]]) for k in keys}
    n_gain, gain_len = gmlp_v_norm.shape
    pad_gain = lambda a: jnp.pad(a, ((0, 8 - n_gain), (0, LANES - gain_len)))
    first = [("ffn1_w_in", 0)]
    early = [("ffn1_w_out", 0), ("fox_w_in", 0)]
    inner = [k for k in keys if k not in first + early]
    stack = lambda g: g.reshape(-1, g.shape[-1])
    full = {k: stack(g) for k, g in zip(first, all_gather_multi([local[k].astype(BF16) for k in first], "gather_weights"))}
    gather_early = (False, [local[k].astype(BF16) for k in early])
    gather_inner = (False, [local[k].astype(BF16) for k in inner] + [pad_gain(gmlp_v_norm)])

    mem_n = rmsnorm_fwd(mem0[None], mem_norm.reshape(1, 1, -1), BF16, "mem_norm")[0]
    saved = []
    xi = x0
    for i in range(depth):
        kind, j = i % 2, i // 2
        if i == 0:
            x1, ffn1_saved, arrived = _ffn_forward(xi, norm_ffn1[i], full["ffn1_w_in", i], lambda got: stack(got[0]),
                                                   f"l{i}_ffn1", gather_early)
            full.update({k: stack(g) for k, g in zip(early, arrived)})
        else:
            x1, ffn1_saved, _ = _ffn_forward(xi, norm_ffn1[i], full["ffn1_w_in", i], full["ffn1_w_out", i], f"l{i}_ffn1")
        h = rmsnorm_fwd(x1[None], norm_mix[i].reshape(1, 1, -1), BF16, f"l{i}_mixnorm")[0]
        if kind == 0:
            tok, mq, mix_saved, arrived = _fox_forward(h, full["fox_w_in", j], fox_b_f[j], fox_q_norm[j], fox_k_norm[j],
                                                       f"l{i}_fox", gather_inner if i == 0 else None)
            if i == 0:
                full.update({k: stack(g) for k, g in zip(inner, arrived)})
                v_gain_full = arrived[-1][:, :n_gain, :gain_len]
        else:
            tok, mq, mix_saved = _gmlp_forward(h, full["gmlp_w_in", j], v_gain_full[:, j, :].reshape(-1), gmlp_w_s[j],
                                               gmlp_b_s[j], f"l{i}_gmlp")
        mo, mem_saved = _mem_forward(mq, mem_n, full["mem_w_kv", i], mem_q_norm[i], mem_k_norm[i], f"l{i}_mem")
        cat = _unheads(jnp.concatenate([tok, mo], axis=0)).astype(BF16)
        x2 = matmul(cat, full["w_out", i], f"l{i}_wout", residual=x1)
        x3, ffn2_saved, _ = _ffn_forward(x2, norm_ffn2[i], full["ffn2_w_in", i], full["ffn2_w_out", i], f"l{i}_ffn2")
        saved.append((ffn1_saved, x1, h, mix_saved, mem_saved, cat, ffn2_saved))
        xi = x3

    dy, loss_part = loss_head(xi, target, "loss_head")
    loss = lax.psum(loss_part, ("x", "y", "c"))

    small = {n: [None] * w[n].shape[0] for n in REPLICATED + ("gmlp_v_norm",) if n != "mem_norm"}
    big = {}
    slots_of = lambda g: g.reshape((N_DEV, -1, g.shape[-1]))
    dmem_n = None
    for i in reversed(range(depth)):
        kind, j = i % 2, i // 2
        ffn1_saved, x1, h, mix_saved, mem_saved, cat, ffn2_saved = saved[i]
        dy, small["norm_ffn2"][i], big["ffn2_w_in", i], big["ffn2_w_out", i], _ = _ffn_backward(
            dy, ffn2_saved, norm_ffn2[i], full["ffn2_w_in", i], full["ffn2_w_out", i], f"l{i}_ffn2")
        big["w_out", i] = matmul_tn(cat, dy, f"l{i}_dwout", out_dtype=BF16)
        dcat = _heads(matmul(dy, full["w_out", i], f"l{i}_dcat", transpose_b=True), FOX_HEADS + MEM_HEADS)
        dmq, small["mem_q_norm"][i], small["mem_k_norm"][i], big["mem_w_kv", i], dmem_i = _mem_backward(
            dcat[FOX_HEADS:], mem_saved, mem_n, full["mem_w_kv", i], f"l{i}_mem")
        dmem_n = dmem_i if dmem_n is None else dmem_n + dmem_i
        if kind == 0:
            scatter_inner = (True, [slots_of(big[k]) for k in inner]) if i == 0 else None
            dproj, small["fox_b_f"][j], small["fox_q_norm"][j], small["fox_k_norm"][j], arrived = _fox_backward(
                dcat[:FOX_HEADS], dmq, mix_saved, f"l{i}_fox", scatter_inner)
            if i == 0:
                got = dict(zip(inner, arrived))
            big["fox_w_in", j] = matmul_tn(h, dproj, f"l{i}_fox_dwin", out_dtype=BF16)
            dh = matmul(dproj, full["fox_w_in", j], f"l{i}_fox_dh", transpose_b=True)
        else:
            dproj, small["gmlp_v_norm"][j], small["gmlp_w_s"][j], small["gmlp_b_s"][j] = _gmlp_backward(
                dcat[:FOX_HEADS], dmq, mix_saved, f"l{i}_gmlp")
            big["gmlp_w_in", j] = matmul_tn(dproj, h, f"l{i}_gmlp_dwin", out_dtype=BF16)
            dh = matmul(dproj, full["gmlp_w_in", j], f"l{i}_gmlp_dh")
        dy, dg_mix = rmsnorm_bwd(x1[None], norm_mix[i].reshape(1, 1, -1), dh[None], f"l{i}_dmixnorm", residual=dy[None])
        dy, small["norm_mix"][i] = dy[0], dg_mix.reshape(-1)
        def scatter_last(dw_out, dw_in_t):
            return (True, [slots_of(dw_in_t)] if dw_out is None else [slots_of(dw_out), slots_of(big["fox_w_in", 0])])

        dy, small["norm_ffn1"][i], big["ffn1_w_in", i], big["ffn1_w_out", i], arrived = _ffn_backward(
            dy, ffn1_saved, norm_ffn1[i], full["ffn1_w_in", i], full["ffn1_w_out", i], f"l{i}_ffn1",
            scatter_last if i == 0 else None)
        if i == 0:
            got.update(zip(early + first, arrived))
    grad_x = dy[None]
    _, dg_mem = rmsnorm_bwd(mem0[None], mem_norm.reshape(1, 1, -1), dmem_n[None], "dmem_norm")
    small = {n: jnp.stack(g) for n, g in small.items()}
    small["mem_norm"] = dg_mem.reshape(-1)

    results ={n: [[None] * w[n].shape[0] for _ in range(4)] for n in BIG}
    for k in keys:
        n, i = k
        outs = adamw(local[k], got[k], _to_transport(n, m[n][i]), _to_transport(n, v[n][i]), f"adamw_{n}_{i}")
        for q in range(4):
            results[n][q][i] = _from_transport(n, outs[q])
    sharded = {n: [jnp.stack(r) for r in results[n]] for n in BIG}

    small_names = REPLICATED + ("gmlp_v_norm",)
    (small_got,) = all_gather_multi([_pack([small[n] for n in small_names])], "gather_small_grads")
    rep_shapes = [w[n].shape for n in REPLICATED]
    gain_seg = jnp.zeros((n_gain, N_DEV * gain_len), F32)
    pack_rep = lambda d: _pack([d[n] for n in REPLICATED] + [gain_seg])
    outs = adamw(pack_rep(w), small_got, pack_rep(m), pack_rep(v), "adamw_replicated")
    replicated = [dict(zip(REPLICATED, _unpack(o, rep_shapes))) for o in outs]
    gain_parts = _unpack(small_got, rep_shapes + [(n_gain, N_DEV * gain_len)], lead=(N_DEV,))[-1]
    gain_slots = lax.dynamic_slice_in_dim(gain_parts, me * gain_len, gain_len, axis=2)
    gain_slots = jnp.pad(gain_slots, ((0, 0), (0, 8 - n_gain), (0, LANES - gain_len)))
    outs = adamw(pad_gain(gmlp_v_norm), gain_slots, pad_gain(m["gmlp_v_norm"]), pad_gain(v["gmlp_v_norm"]),
                 "adamw_gmlp_v_norm")
    sharded["gmlp_v_norm"] = [o[:n_gain, :gain_len] for o in outs]

    out = [loss, grad_x]
    for q in range(4):
        out += [(replicated[q][n] if n in REPLICATED else sharded[n][q]) for n in WEIGHTS]
    return tuple(out)
```
